```python
import jax, jax.numpy as jnp
from jax import lax
import numpy as np

D_MODEL = 2048
BATCH = 8
SEQ = 8192
DEPTH = 2

D_MIX = D_MODEL
D_ATTN = D_MIX // 2
D_CONV = D_MIX // 4
D_POOL = D_MIX // 4
HEAD_DIM = 64
ATTN_HEADS = D_ATTN // HEAD_DIM
CONV_GROUPS = 8
CONV_WIDTH = 3
POOL_WINDOWS = (2, 4, 8, 16)
POOL_GROUP = D_POOL // len(POOL_WINDOWS)
DILATED_PATTERNS = ((128, 1), (512, 4), (2048, 16))
BLK = 128
D_IN = 3 * D_ATTN + 3 * D_CONV + D_POOL
D_FF = 5632
FFN_RESIDUAL = 0.5
RMS_EPS = 1e-6
NEG_INF = -1e30

kernel_name = "hybrid_dilated_conv_pool_macaron"


def rmsnorm(x, g):
    xf = x.astype(jnp.float32)
    y = xf * lax.rsqrt(jnp.mean(xf * xf, axis=-1, keepdims=True) + RMS_EPS)
    return (y * g.astype(jnp.float32)).astype(x.dtype)


def swiglu(h, w_gate, w_up, w_down):
    return (jax.nn.silu(h @ w_gate) * (h @ w_up)) @ w_down


def dilated_band_attention(q, k, v, window, dilation):
    B, S, H, Dh = q.shape
    d = dilation
    L = S // d
    span = window // d
    assert span <= BLK
    nb = -(-L // BLK)
    Lp = nb * BLK

    def to_sub(t):
        t = t.reshape(B, L, d, H, Dh).transpose(0, 2, 1, 3, 4).reshape(B * d, L, H, Dh)
        t = jnp.pad(t, ((0, 0), (0, Lp - L), (0, 0), (0, 0)))
        return t.reshape(B * d, nb, BLK, H, Dh)

    def with_prev(t):
        prev = jnp.pad(t, ((0, 0), (1, 0), (0, 0), (0, 0), (0, 0)))[:, :-1]
        return jnp.concatenate([prev, t], axis=2)

    qb = to_sub(q)
    kc = with_prev(to_sub(k))
    vc = with_prev(to_sub(v))
    s = jnp.einsum('znqhd,znkhd->znhqk', qb, kc).astype(jnp.float32) * (Dh ** -0.5)
    qi = jnp.arange(BLK)[:, None]
    kj = jnp.arange(2 * BLK)[None, :]
    dist = qi + BLK - kj
    blk = jnp.arange(nb)[:, None, None]
    valid = (dist >= 0) & (dist <= span) & (blk * BLK + kj - BLK >= 0)
    s = jnp.where(valid[None, :, None], s, NEG_INF)
    m = jnp.max(s, axis=-1, keepdims=True)
    p = jnp.exp(s - m)
    l = jnp.sum(p, axis=-1)
    num = jnp.einsum('znhqk,znkhd->znqhd', p.astype(v.dtype), vc).astype(jnp.float32)
    l_q = l.transpose(0, 1, 3, 2)
    o = num / l_q[..., None]
    lse = m[..., 0].transpose(0, 1, 3, 2) + jnp.log(l_q)

    def from_sub(t):
        rest = t.shape[3:]
        t = t.reshape((B * d, Lp) + rest)[:, :L]
        t = t.reshape((B, d, L) + rest)
        t = jnp.moveaxis(t, 1, 2)
        return t.reshape((B, S) + rest)

    return from_sub(o), from_sub(lse)


def dilated_attention(q, k, v):
    outs, lses = [], []
    for window, dilation in DILATED_PATTERNS:
        o, lse = dilated_band_attention(q, k, v, window, dilation)
        outs.append(o)
        lses.append(lse)
    w = jax.nn.softmax(jnp.stack(lses, axis=0), axis=0)
    o = jnp.sum(w[..., None] * jnp.stack(outs, axis=0), axis=0)
    return o.astype(q.dtype)


def causal_short_conv(u, conv_w):
    S = u.shape[1]
    up = jnp.pad(u, ((0, 0), (CONV_WIDTH - 1, 0), (0, 0)))
    return conv_w[0] * up[:, 0:S] + conv_w[1] * up[:, 1:S + 1] + conv_w[2] * up[:, 2:S + 2]


def multiscale_pool(u, pool_w, pool_scale):
    B, S, C = u.shape
    uf = u.astype(jnp.float32)
    csz = jnp.pad(jnp.cumsum(uf, axis=1), ((0, 0), (1, 0), (0, 0)))
    pos = jnp.arange(S)
    outs = []
    for g, w in enumerate(POOL_WINDOWS):
        sl = slice(g * POOL_GROUP, (g + 1) * POOL_GROUP)
        P = jnp.pad(csz[..., sl], ((0, 0), (w - 1, 0), (0, 0)))
        win_sum = P[:, w:w + S] - P[:, 0:S]
        count = jnp.minimum(pos + 1, w).astype(jnp.float32)[None, :, None]
        outs.append(win_sum / count - uf[..., sl])
    pooled = jnp.stack(outs, axis=2).astype(u.dtype)
    y = jnp.einsum('bsgc,gcd->bsgd', pooled, pool_w).reshape(B, S, C)
    return y * pool_scale


def hybrid_mixer(h, w_in, conv_w, pool_w, pool_scale, w_out):
    B, S, _ = h.shape
    z = h @ w_in
    cuts = np.cumsum([D_ATTN, D_ATTN, D_ATTN, D_CONV, D_CONV, D_CONV])
    q, k, v, gate_b, gate_c, conv_in, pool_in = jnp.split(z, cuts, axis=-1)
    heads = lambda t: t.reshape(B, S, ATTN_HEADS, HEAD_DIM)
    y_attn = dilated_attention(heads(q), heads(k), heads(v)).reshape(B, S, D_ATTN)
    y_conv = gate_b * causal_short_conv(gate_c * conv_in, conv_w)
    y_pool = multiscale_pool(pool_in, pool_w, pool_scale)
    return jnp.concatenate([y_attn, y_conv, y_pool], axis=-1) @ w_out


def _fwd_setup_inputs(seed: int = 0) -> dict:
    key = jax.random.key(seed)
    ks = jax.random.split(key, 16)
    f32 = jnp.float32

    def lin(k, shape, fan_in):
        return jax.random.normal(k, shape, f32) * (fan_in ** -0.5)

    def gain(k, shape, noise=0.02):
        return 1.0 + noise * jax.random.normal(k, shape, f32)

    return {
        "x": jax.random.normal(ks[0], (BATCH, SEQ, D_MODEL), f32),
        "ffn1_norm": gain(ks[1], (DEPTH, D_MODEL)),
        "ffn1_w_gate": lin(ks[2], (DEPTH, D_MODEL, D_FF), D_MODEL),
        "ffn1_w_up": lin(ks[3], (DEPTH, D_MODEL, D_FF), D_MODEL),
        "ffn1_w_down": lin(ks[4], (DEPTH, D_FF, D_MODEL), D_FF),
        "mix_norm": gain(ks[5], (DEPTH, D_MODEL)),
        "w_in": lin(ks[6], (DEPTH, D_MODEL, D_IN), D_MODEL),
        "conv_w": lin(ks[7], (DEPTH, CONV_WIDTH, D_CONV), CONV_WIDTH),
        "pool_w": lin(ks[8], (DEPTH, len(POOL_WINDOWS), POOL_GROUP, POOL_GROUP), POOL_GROUP),
        "pool_scale": gain(ks[9], (DEPTH, D_POOL), 0.1),
        "w_out": lin(ks[10], (DEPTH, D_MIX, D_MODEL), D_MIX),
        "ffn2_norm": gain(ks[11], (DEPTH, D_MODEL)),
        "ffn2_w_gate": lin(ks[12], (DEPTH, D_MODEL, D_FF), D_MODEL),
        "ffn2_w_up": lin(ks[13], (DEPTH, D_MODEL, D_FF), D_MODEL),
        "ffn2_w_down": lin(ks[14], (DEPTH, D_FF, D_MODEL), D_FF),
        "final_norm": gain(ks[15], (D_MODEL,)),
    }


def _fwd_reference(x, ffn1_norm, ffn1_w_gate, ffn1_w_up, ffn1_w_down, mix_norm, w_in, conv_w,
              pool_w, pool_scale, w_out, ffn2_norm, ffn2_w_gate, ffn2_w_up, ffn2_w_down,
              final_norm):
    for l in range(DEPTH):
        x = x + FFN_RESIDUAL * swiglu(rmsnorm(x, ffn1_norm[l]), ffn1_w_gate[l], ffn1_w_up[l], ffn1_w_down[l])
        x = x + hybrid_mixer(rmsnorm(x, mix_norm[l]), w_in[l], conv_w[l], pool_w[l], pool_scale[l], w_out[l])
        x = x + FFN_RESIDUAL * swiglu(rmsnorm(x, ffn2_norm[l]), ffn2_w_gate[l], ffn2_w_up[l], ffn2_w_down[l])
    return rmsnorm(x, final_norm)


import jax as _jax
import jax.numpy as _jnp

TWIN_FORMAT = 'train_step'
FWD_PARAMS = ['x', 'ffn1_norm', 'ffn1_w_gate', 'ffn1_w_up', 'ffn1_w_down', 'mix_norm', 'w_in', 'conv_w', 'pool_w', 'pool_scale', 'w_out', 'ffn2_norm', 'ffn2_w_gate', 'ffn2_w_up', 'ffn2_w_down', 'final_norm']
TWIN_WEIGHTS = ['ffn1_norm', 'ffn1_w_gate', 'ffn1_w_up', 'ffn1_w_down', 'mix_norm', 'w_in', 'conv_w', 'pool_w', 'pool_scale', 'w_out', 'ffn2_norm', 'ffn2_w_gate', 'ffn2_w_up', 'ffn2_w_down', 'final_norm']
TWIN_DIFF_INPUT = 'x'
TWIN_INPUTS = ['x', 'ffn1_norm', 'ffn1_w_gate', 'ffn1_w_up', 'ffn1_w_down', 'mix_norm', 'w_in', 'conv_w', 'pool_w', 'pool_scale', 'w_out', 'ffn2_norm', 'ffn2_w_gate', 'ffn2_w_up', 'ffn2_w_down', 'final_norm', 'loss_target', 'm_ffn1_norm', 'm_ffn1_w_gate', 'm_ffn1_w_up', 'm_ffn1_w_down', 'm_mix_norm', 'm_w_in', 'm_conv_w', 'm_pool_w', 'm_pool_scale', 'm_w_out', 'm_ffn2_norm', 'm_ffn2_w_gate', 'm_ffn2_w_up', 'm_ffn2_w_down', 'm_final_norm', 'v_ffn1_norm', 'v_ffn1_w_gate', 'v_ffn1_w_up', 'v_ffn1_w_down', 'v_mix_norm', 'v_w_in', 'v_conv_w', 'v_pool_w', 'v_pool_scale', 'v_w_out', 'v_ffn2_norm', 'v_ffn2_w_gate', 'v_ffn2_w_up', 'v_ffn2_w_down', 'v_final_norm']
TWIN_OUTPUTS = ['loss', 'grad_x', 'grad_ffn1_norm', 'grad_ffn1_w_gate', 'grad_ffn1_w_up', 'grad_ffn1_w_down', 'grad_mix_norm', 'grad_w_in', 'grad_conv_w', 'grad_pool_w', 'grad_pool_scale', 'grad_w_out', 'grad_ffn2_norm', 'grad_ffn2_w_gate', 'grad_ffn2_w_up', 'grad_ffn2_w_down', 'grad_final_norm', 'delta_ffn1_norm', 'delta_ffn1_w_gate', 'delta_ffn1_w_up', 'delta_ffn1_w_down', 'delta_mix_norm', 'delta_w_in', 'delta_conv_w', 'delta_pool_w', 'delta_pool_scale', 'delta_w_out', 'delta_ffn2_norm', 'delta_ffn2_w_gate', 'delta_ffn2_w_up', 'delta_ffn2_w_down', 'delta_final_norm', 'new_m_ffn1_norm', 'new_m_ffn1_w_gate', 'new_m_ffn1_w_up', 'new_m_ffn1_w_down', 'new_m_mix_norm', 'new_m_w_in', 'new_m_conv_w', 'new_m_pool_w', 'new_m_pool_scale', 'new_m_w_out', 'new_m_ffn2_norm', 'new_m_ffn2_w_gate', 'new_m_ffn2_w_up', 'new_m_ffn2_w_down', 'new_m_final_norm', 'new_v_ffn1_norm', 'new_v_ffn1_w_gate', 'new_v_ffn1_w_up', 'new_v_ffn1_w_down', 'new_v_mix_norm', 'new_v_w_in', 'new_v_conv_w', 'new_v_pool_w', 'new_v_pool_scale', 'new_v_w_out', 'new_v_ffn2_norm', 'new_v_ffn2_w_gate', 'new_v_ffn2_w_up', 'new_v_ffn2_w_down', 'new_v_final_norm']
TWIN_LEAF_KINDS = {'loss': 'loss', 'grad_x': 'grad_x', 'grad_ffn1_norm': 'grad_w', 'grad_ffn1_w_gate': 'grad_w', 'grad_ffn1_w_up': 'grad_w', 'grad_ffn1_w_down': 'grad_w', 'grad_mix_norm': 'grad_w', 'grad_w_in': 'grad_w', 'grad_conv_w': 'grad_w', 'grad_pool_w': 'grad_w', 'grad_pool_scale': 'grad_w', 'grad_w_out': 'grad_w', 'grad_ffn2_norm': 'grad_w', 'grad_ffn2_w_gate': 'grad_w', 'grad_ffn2_w_up': 'grad_w', 'grad_ffn2_w_down': 'grad_w', 'grad_final_norm': 'grad_w', 'delta_ffn1_norm': 'delta_w', 'delta_ffn1_w_gate': 'delta_w', 'delta_ffn1_w_up': 'delta_w', 'delta_ffn1_w_down': 'delta_w', 'delta_mix_norm': 'delta_w', 'delta_w_in': 'delta_w', 'delta_conv_w': 'delta_w', 'delta_pool_w': 'delta_w', 'delta_pool_scale': 'delta_w', 'delta_w_out': 'delta_w', 'delta_ffn2_norm': 'delta_w', 'delta_ffn2_w_gate': 'delta_w', 'delta_ffn2_w_up': 'delta_w', 'delta_ffn2_w_down': 'delta_w', 'delta_final_norm': 'delta_w', 'new_m_ffn1_norm': 'new_m', 'new_m_ffn1_w_gate': 'new_m', 'new_m_ffn1_w_up': 'new_m', 'new_m_ffn1_w_down': 'new_m', 'new_m_mix_norm': 'new_m', 'new_m_w_in': 'new_m', 'new_m_conv_w': 'new_m', 'new_m_pool_w': 'new_m', 'new_m_pool_scale': 'new_m', 'new_m_w_out': 'new_m', 'new_m_ffn2_norm': 'new_m', 'new_m_ffn2_w_gate': 'new_m', 'new_m_ffn2_w_up': 'new_m', 'new_m_ffn2_w_down': 'new_m', 'new_m_final_norm': 'new_m', 'new_v_ffn1_norm': 'new_v', 'new_v_ffn1_w_gate': 'new_v', 'new_v_ffn1_w_up': 'new_v', 'new_v_ffn1_w_down': 'new_v', 'new_v_mix_norm': 'new_v', 'new_v_w_in': 'new_v', 'new_v_conv_w': 'new_v', 'new_v_pool_w': 'new_v', 'new_v_pool_scale': 'new_v', 'new_v_w_out': 'new_v', 'new_v_ffn2_norm': 'new_v', 'new_v_ffn2_w_gate': 'new_v', 'new_v_ffn2_w_up': 'new_v', 'new_v_ffn2_w_down': 'new_v', 'new_v_final_norm': 'new_v'}


def _forward(args):
    return _fwd_reference(*[args[k] for k in FWD_PARAMS])


def _output_shape():
    def fwd():
        inp = _fwd_setup_inputs(0)
        return _fwd_reference(*[inp[k] for k in FWD_PARAMS])
    out = _jax.eval_shape(fwd)
    return out.shape, out.dtype

N_MICROBATCH = 1
ADAM_LR = 0.001
ADAM_B1 = 0.9
ADAM_B2 = 0.999
ADAM_EPS = 1e-08
ADAM_WD = 0.01
ADAM_STEP = 10
PER_EXAMPLE_BATCH_AXIS = {'x': 0, 'loss_target': 0}
SHARED_INPUTS = []
_WEIGHT_DTYPES = {'ffn1_norm': _jnp.float32, 'ffn1_w_gate': _jnp.float32, 'ffn1_w_up': _jnp.float32, 'ffn1_w_down': _jnp.float32, 'mix_norm': _jnp.float32, 'w_in': _jnp.float32, 'conv_w': _jnp.float32, 'pool_w': _jnp.float32, 'pool_scale': _jnp.float32, 'w_out': _jnp.float32, 'ffn2_norm': _jnp.float32, 'ffn2_w_gate': _jnp.float32, 'ffn2_w_up': _jnp.float32, 'ffn2_w_down': _jnp.float32, 'final_norm': _jnp.float32}
MOMENT_SCALE = {'ffn1_norm': 5.999717e-02, 'ffn1_w_gate': 2.606944e-02, 'ffn1_w_up': 2.525278e-02, 'ffn1_w_down': 4.190475e-02, 'mix_norm': 1.043273e-01, 'w_in': 6.601633e-02, 'conv_w': 1.064517e-01, 'pool_w': 9.353297e-02, 'pool_scale': 9.606970e-02, 'w_out': 7.139933e-02, 'ffn2_norm': 4.440438e-02, 'ffn2_w_gate': 1.873327e-02, 'ffn2_w_up': 1.813901e-02, 'ffn2_w_down': 3.011508e-02, 'final_norm': 3.197153e+01}


def _to_microbatches(a, axis):
    t = _jnp.moveaxis(a, axis, 0)
    t = t.reshape((N_MICROBATCH, t.shape[0] // N_MICROBATCH) + t.shape[1:])
    return _jnp.moveaxis(t, 1, axis + 1)


def setup_inputs(seed: int = 0) -> dict:
    inp = _fwd_setup_inputs(seed)
    key = _jax.random.fold_in(_jax.random.key(seed), 7919)
    shape, _ = _output_shape()
    out = dict(inp)
    out["loss_target"] = _jax.random.normal(_jax.random.fold_in(key, 0), shape, _jnp.float32)
    for i, name in enumerate(TWIN_WEIGHTS):
        w = inp[name].astype(_jnp.float32)
        if MOMENT_SCALE is None:
            s = _jnp.sqrt(_jnp.mean(_jnp.square(w)) + 1e-30)
        else:
            s = MOMENT_SCALE[name]
        km, kv = _jax.random.split(_jax.random.fold_in(key, i + 1))
        out[name] = w
        out["m_" + name] = s * _jax.random.normal(km, w.shape, _jnp.float32)
        out["v_" + name] = (s * s) * _jax.random.uniform(kv, w.shape, _jnp.float32, 0.5, 1.5)
    if N_MICROBATCH > 1:
        for name, axis in PER_EXAMPLE_BATCH_AXIS.items():
            out[name] = _to_microbatches(out[name], axis)
    return {'x': out['x'], 'ffn1_norm': out['ffn1_norm'], 'ffn1_w_gate': out['ffn1_w_gate'], 'ffn1_w_up': out['ffn1_w_up'], 'ffn1_w_down': out['ffn1_w_down'], 'mix_norm': out['mix_norm'], 'w_in': out['w_in'], 'conv_w': out['conv_w'], 'pool_w': out['pool_w'], 'pool_scale': out['pool_scale'], 'w_out': out['w_out'], 'ffn2_norm': out['ffn2_norm'], 'ffn2_w_gate': out['ffn2_w_gate'], 'ffn2_w_up': out['ffn2_w_up'], 'ffn2_w_down': out['ffn2_w_down'], 'final_norm': out['final_norm'], 'loss_target': out['loss_target'], 'm_ffn1_norm': out['m_ffn1_norm'], 'm_ffn1_w_gate': out['m_ffn1_w_gate'], 'm_ffn1_w_up': out['m_ffn1_w_up'], 'm_ffn1_w_down': out['m_ffn1_w_down'], 'm_mix_norm': out['m_mix_norm'], 'm_w_in': out['m_w_in'], 'm_conv_w': out['m_conv_w'], 'm_pool_w': out['m_pool_w'], 'm_pool_scale': out['m_pool_scale'], 'm_w_out': out['m_w_out'], 'm_ffn2_norm': out['m_ffn2_norm'], 'm_ffn2_w_gate': out['m_ffn2_w_gate'], 'm_ffn2_w_up': out['m_ffn2_w_up'], 'm_ffn2_w_down': out['m_ffn2_w_down'], 'm_final_norm': out['m_final_norm'], 'v_ffn1_norm': out['v_ffn1_norm'], 'v_ffn1_w_gate': out['v_ffn1_w_gate'], 'v_ffn1_w_up': out['v_ffn1_w_up'], 'v_ffn1_w_down': out['v_ffn1_w_down'], 'v_mix_norm': out['v_mix_norm'], 'v_w_in': out['v_w_in'], 'v_conv_w': out['v_conv_w'], 'v_pool_w': out['v_pool_w'], 'v_pool_scale': out['v_pool_scale'], 'v_w_out': out['v_w_out'], 'v_ffn2_norm': out['v_ffn2_norm'], 'v_ffn2_w_gate': out['v_ffn2_w_gate'], 'v_ffn2_w_up': out['v_ffn2_w_up'], 'v_ffn2_w_down': out['v_ffn2_w_down'], 'v_final_norm': out['v_final_norm']}


def _loss(weights, diff, rest, loss_target):
    with _jax.named_scope("forward"):
        args = {**rest, TWIN_DIFF_INPUT: diff, **{k: w.astype(_WEIGHT_DTYPES[k]) for k, w in weights.items()}}
        y = _forward(args)
    with _jax.named_scope("loss_head"):
        err = _jnp.square(y.astype(_jnp.float32) - loss_target)
        return 0.5 * _jnp.sum(_jnp.mean(err, axis=-1)) if err.ndim else 0.5 * err


def _adamw(w, g, m, v):
    m = ADAM_B1 * m + (1.0 - ADAM_B1) * g
    v = ADAM_B2 * v + (1.0 - ADAM_B2) * _jnp.square(g)
    m_hat = m / (1.0 - ADAM_B1 ** ADAM_STEP)
    v_hat = v / (1.0 - ADAM_B2 ** ADAM_STEP)
    delta = -ADAM_LR * (m_hat / (_jnp.sqrt(v_hat) + ADAM_EPS) + ADAM_WD * w)
    return delta, m, v


def reference(x, ffn1_norm, ffn1_w_gate, ffn1_w_up, ffn1_w_down, mix_norm, w_in, conv_w, pool_w, pool_scale, w_out, ffn2_norm, ffn2_w_gate, ffn2_w_up, ffn2_w_down, final_norm, loss_target, m_ffn1_norm, m_ffn1_w_gate, m_ffn1_w_up, m_ffn1_w_down, m_mix_norm, m_w_in, m_conv_w, m_pool_w, m_pool_scale, m_w_out, m_ffn2_norm, m_ffn2_w_gate, m_ffn2_w_up, m_ffn2_w_down, m_final_norm, v_ffn1_norm, v_ffn1_w_gate, v_ffn1_w_up, v_ffn1_w_down, v_mix_norm, v_w_in, v_conv_w, v_pool_w, v_pool_scale, v_w_out, v_ffn2_norm, v_ffn2_w_gate, v_ffn2_w_up, v_ffn2_w_down, v_final_norm):
    given = dict(x=x, ffn1_norm=ffn1_norm, ffn1_w_gate=ffn1_w_gate, ffn1_w_up=ffn1_w_up, ffn1_w_down=ffn1_w_down, mix_norm=mix_norm, w_in=w_in, conv_w=conv_w, pool_w=pool_w, pool_scale=pool_scale, w_out=w_out, ffn2_norm=ffn2_norm, ffn2_w_gate=ffn2_w_gate, ffn2_w_up=ffn2_w_up, ffn2_w_down=ffn2_w_down, final_norm=final_norm, loss_target=loss_target, m_ffn1_norm=m_ffn1_norm, m_ffn1_w_gate=m_ffn1_w_gate, m_ffn1_w_up=m_ffn1_w_up, m_ffn1_w_down=m_ffn1_w_down, m_mix_norm=m_mix_norm, m_w_in=m_w_in, m_conv_w=m_conv_w, m_pool_w=m_pool_w, m_pool_scale=m_pool_scale, m_w_out=m_w_out, m_ffn2_norm=m_ffn2_norm, m_ffn2_w_gate=m_ffn2_w_gate, m_ffn2_w_up=m_ffn2_w_up, m_ffn2_w_down=m_ffn2_w_down, m_final_norm=m_final_norm, v_ffn1_norm=v_ffn1_norm, v_ffn1_w_gate=v_ffn1_w_gate, v_ffn1_w_up=v_ffn1_w_up, v_ffn1_w_down=v_ffn1_w_down, v_mix_norm=v_mix_norm, v_w_in=v_w_in, v_conv_w=v_conv_w, v_pool_w=v_pool_w, v_pool_scale=v_pool_scale, v_w_out=v_w_out, v_ffn2_norm=v_ffn2_norm, v_ffn2_w_gate=v_ffn2_w_gate, v_ffn2_w_up=v_ffn2_w_up, v_ffn2_w_down=v_ffn2_w_down, v_final_norm=v_final_norm)
    weights = {n: given[n] for n in TWIN_WEIGHTS}
    shared = {n: given[n] for n in SHARED_INPUTS}
    per_example = {n: given[n] for n in ['x']}
    grad_fn = _jax.value_and_grad(_loss, argnums=(0, 1))

    def one_microbatch(ex, loss_target):
        ex = dict(ex)
        diff = ex.pop(TWIN_DIFF_INPUT)
        return grad_fn(weights, diff, {**shared, **ex}, loss_target)

    if N_MICROBATCH == 1:
        loss, (grad_w, grad_x) = one_microbatch(per_example, given["loss_target"])
    else:
        def body(carry, xs):
            loss_sum, grad_sum = carry
            l_k, (gw_k, gx_k) = one_microbatch(xs[0], xs[1])
            with _jax.named_scope("update"):
                return (loss_sum + l_k, _jax.tree.map(_jnp.add, grad_sum, gw_k)), gx_k

        init = (_jnp.zeros((), _jnp.float32), _jax.tree.map(_jnp.zeros_like, weights))
        (loss, grad_w), grad_x = _jax.lax.scan(body, init, (per_example, given["loss_target"]))
    with _jax.named_scope("update"):
        delta_w, new_m, new_v = {}, {}, {}
        for n in TWIN_WEIGHTS:
            delta_w[n], new_m[n], new_v[n] = _adamw(weights[n], grad_w[n], given["m_" + n], given["v_" + n])
    return (loss, grad_x, *[grad_w[n] for n in TWIN_WEIGHTS], *[delta_w[n] for n in TWIN_WEIGHTS],
            *[new_m[n] for n in TWIN_WEIGHTS], *[new_v[n] for n in TWIN_WEIGHTS])
```

```python
import functools

import jax
import jax.numpy as jnp
from jax import lax
from jax.experimental import pallas as pl
from jax.experimental.pallas import tpu as pltpu

F32 = jnp.float32
BF16 = jnp.bfloat16
NDEV = 8
HEAD_DIM = 64
LANES = 128
BLK = 128
DILATIONS = (1, 4, 16)
POOL_WINDOWS = (2, 4, 8, 16)
POOL_GROUP = 128
HALO = 16
FFN_RESIDUAL = 0.5
RMS_EPS = 1e-6
NEG_INF = -1e30
ATTN_SCALE = HEAD_DIM ** -0.5
ADAM_LR, ADAM_B1, ADAM_B2, ADAM_EPS, ADAM_WD, ADAM_STEP = 0.001, 0.9, 0.999, 1e-08, 0.01, 10
VMEM_BYTES = 64 * 1024 * 1024
MESH = pl.DeviceIdType.MESH

NN = (((1,), (0,)), ((), ()))
NT = (((1,), (1,)), ((), ()))
TN = (((0,), (0,)), ((), ()))


def _dot(a, b, dims=NN):
    return lax.dot_general(a, b, dims, preferred_element_type=F32)


def _params(vmem_mb=48):
    return pltpu.CompilerParams(vmem_limit_bytes=min(vmem_mb * 1024 * 1024, VMEM_BYTES - 4 * 1024 * 1024))


def _tile(n, want):
    t = min(n, max(16, want // 16 * 16))
    while t > 16 and (n % t or t % 16):
        t -= 16
    return t if n % t == 0 else n


def _coords():
    return lax.axis_index("x"), lax.axis_index("y"), lax.axis_index("c")


def all_gather(arrs, name):
    n = len(arrs)

    def body(*refs):
        ins, outs = refs[:n], refs[n:2 * n]
        send_sems, recv_sems, local_sems = refs[2 * n:]
        x, y, c = _coords()
        me, sibling = (x, y, c), (x, y, 1 - c)
        chips = [(1 - x, y), (x, 1 - y), (1 - x, 1 - y)]

        def slot(out, p):
            return out.at[4 * p[0] + 2 * p[1] + p[2]]

        def copy(a, k, block, to, src=None):
            return pltpu.make_async_remote_copy(
                src_ref=slot(outs[a], block) if src is None else src, dst_ref=slot(outs[a], block),
                send_sem=send_sems.at[7 * a + k], recv_sem=recv_sems.at[7 * a + k],
                device_id=to, device_id_type=MESH)

        mine = [pltpu.make_async_copy(ins[a], slot(outs[a], me), local_sems.at[a]) for a in range(n)]
        for cp in mine:
            cp.start()
        first = []
        for a in range(n):
            first.append(copy(a, 0, me, sibling, src=ins[a]))
            first += [copy(a, 1 + j, me, (*chip, c), src=ins[a]) for j, chip in enumerate(chips)]
        for cp in first:
            cp.start()
        passed = []
        for j, chip in enumerate(chips):
            for a in range(n):
                copy(a, 1 + j, (*chip, c), me).wait_recv()
                fwd = copy(a, 4 + j, (*chip, c), sibling)
                fwd.start()
                passed.append(fwd)
        for a in range(n):
            copy(a, 0, sibling, me).wait_recv()
            for j, chip in enumerate(chips):
                copy(a, 4 + j, (*chip, 1 - c), me).wait_recv()
        for cp in first + passed:
            cp.wait_send()
        for cp in mine:
            cp.wait()

    any_spec = pl.BlockSpec(memory_space=pl.ANY)
    return pl.pallas_call(
        body, name=name,
        out_shape=[jax.ShapeDtypeStruct((NDEV,) + a.shape, a.dtype) for a in arrs],
        in_specs=[any_spec] * n, out_specs=[any_spec] * n,
        scratch_shapes=[pltpu.SemaphoreType.DMA((7 * n,)), pltpu.SemaphoreType.DMA((7 * n,)),
                        pltpu.SemaphoreType.DMA((n,))],
    )(*arrs)


def exchange(parts, name):
    n = len(parts)

    def body(*refs):
        ins, outs = refs[:n], refs[n:2 * n]
        send_sems, recv_sems, local_sems = refs[2 * n:]
        x, y, c = _coords()
        me = 4 * x + 2 * y + c

        def peer(k):
            return (1 - x if k & 4 else x, 1 - y if k & 2 else y, 1 - c if k & 1 else c)

        def copy(a, k):
            p = peer(k)
            return pltpu.make_async_remote_copy(
                src_ref=ins[a].at[4 * p[0] + 2 * p[1] + p[2]], dst_ref=outs[a].at[me],
                send_sem=send_sems.at[7 * a + k - 1], recv_sem=recv_sems.at[7 * a + k - 1],
                device_id=p, device_id_type=MESH)

        def landed(a, k):
            p = peer(k)
            return pltpu.make_async_remote_copy(
                src_ref=ins[a].at[me], dst_ref=outs[a].at[4 * p[0] + 2 * p[1] + p[2]],
                send_sem=send_sems.at[7 * a + k - 1], recv_sem=recv_sems.at[7 * a + k - 1],
                device_id=p, device_id_type=MESH)

        mine = [pltpu.make_async_copy(ins[a].at[me], outs[a].at[me], local_sems.at[a]) for a in range(n)]
        for cp in mine:
            cp.start()
        sent = [copy(a, k) for a in range(n) for k in range(1, NDEV)]
        for cp in sent:
            cp.start()
        for a in range(n):
            for k in range(1, NDEV):
                landed(a, k).wait_recv()
        for cp in sent:
            cp.wait_send()
        for cp in mine:
            cp.wait()

    any_spec = pl.BlockSpec(memory_space=pl.ANY)
    return pl.pallas_call(
        body, name=name,
        out_shape=[jax.ShapeDtypeStruct(a.shape, a.dtype) for a in parts],
        in_specs=[any_spec] * n, out_specs=[any_spec] * n,
        scratch_shapes=[pltpu.SemaphoreType.DMA((7 * n,)), pltpu.SemaphoreType.DMA((7 * n,)),
                        pltpu.SemaphoreType.DMA((n,))],
    )(*parts)


def _rstd(x):
    return lax.rsqrt(jnp.mean(x * x, axis=-1, keepdims=True) + RMS_EPS)


def rms_fwd(x, gain, name):
    S, D = x.shape
    tm = _tile(S, 512)

    def body(x_ref, g_ref, h_ref):
        xv = x_ref[...]
        h_ref[...] = (xv * _rstd(xv) * g_ref[...]).astype(BF16)

    return pl.pallas_call(
        body, name=name, grid=(S // tm,),
        in_specs=[pl.BlockSpec((tm, D), lambda i: (i, 0)), pl.BlockSpec((1, D), lambda i: (0, 0))],
        out_specs=pl.BlockSpec((tm, D), lambda i: (i, 0)),
        out_shape=jax.ShapeDtypeStruct((S, D), BF16), compiler_params=_params(32),
    )(x, gain)


def _rms_bwd_math(xv, gain, dh):
    r = _rstd(xv)
    xhat = xv * r
    dxhat = dh * gain
    dx = r * (dxhat - xhat * jnp.mean(dxhat * xhat, axis=-1, keepdims=True))
    return dx, dh * xhat


def rms_bwd(x, gain, dh, g, name):
    S, D = x.shape
    tm = _tile(S, 256)

    def body(x_ref, gain_ref, dh_ref, g_ref, go_ref, gb_ref, dg_ref):
        @pl.when(pl.program_id(0) == 0)
        def _():
            dg_ref[...] = jnp.zeros_like(dg_ref)

        dx, dgain = _rms_bwd_math(x_ref[...], gain_ref[...], dh_ref[...])
        gn = g_ref[...] + dx
        go_ref[...] = gn
        gb_ref[...] = gn.astype(BF16)
        dg_ref[...] += jnp.sum(dgain, axis=0, keepdims=True)

    row = pl.BlockSpec((tm, D), lambda i: (i, 0))
    vec = pl.BlockSpec((1, D), lambda i: (0, 0))
    return pl.pallas_call(
        body, name=name, grid=(S // tm,),
        in_specs=[row, vec, row, row], out_specs=[row, row, vec],
        out_shape=[jax.ShapeDtypeStruct((S, D), F32), jax.ShapeDtypeStruct((S, D), BF16),
                   jax.ShapeDtypeStruct((1, D), F32)],
        compiler_params=_params(48),
    )(x, gain, dh, g)


def final_loss(x, gain, target, name):
    S, D = x.shape
    tm = _tile(S, 256)

    def body(x_ref, gain_ref, t_ref, loss_ref, go_ref, gb_ref, dg_ref):
        @pl.when(pl.program_id(0) == 0)
        def _():
            dg_ref[...] = jnp.zeros_like(dg_ref)
            loss_ref[...] = jnp.zeros_like(loss_ref)

        xv, gain_v = x_ref[...], gain_ref[...]
        err = xv * _rstd(xv) * gain_v - t_ref[...]
        loss_ref[...] += jnp.sum(jnp.sum(err * err, axis=-1, keepdims=True), axis=0, keepdims=True) * (0.5 / D)
        dx, dgain = _rms_bwd_math(xv, gain_v, err * (1.0 / D))
        go_ref[...] = dx
        gb_ref[...] = dx.astype(BF16)
        dg_ref[...] += jnp.sum(dgain, axis=0, keepdims=True)

    row = pl.BlockSpec((tm, D), lambda i: (i, 0))
    vec = pl.BlockSpec((1, D), lambda i: (0, 0))
    return pl.pallas_call(
        body, name=name, grid=(S // tm,),
        in_specs=[row, vec, row],
        out_specs=[pl.BlockSpec((1, LANES), lambda i: (0, 0)), row, row, vec],
        out_shape=[jax.ShapeDtypeStruct((1, LANES), F32), jax.ShapeDtypeStruct((S, D), F32),
                   jax.ShapeDtypeStruct((S, D), BF16), jax.ShapeDtypeStruct((1, D), F32)],
        compiler_params=_params(48),
    )(x, gain, target)


def _blk(arr, width, tm, stacked):
    if stacked:
        return pl.BlockSpec((None, tm, width), lambda j, i: (j, i, 0))
    return pl.BlockSpec((tm, width), lambda j, i: (i, j))


def ffn_up(h, wg, wu, name):
    S, D = h.shape
    J, _, Fb = wg.shape
    tm = _tile(S, 1024)

    def body(h_ref, wg_ref, wu_ref, a_ref, b_ref, hid_ref):
        hv = h_ref[...]
        a = _dot(hv, wg_ref[...])
        b = _dot(hv, wu_ref[...])
        a_ref[...] = a.astype(BF16)
        b_ref[...] = b.astype(BF16)
        hid_ref[...] = (a * jax.nn.sigmoid(a) * b).astype(BF16)

    w_spec = pl.BlockSpec((None, D, Fb), lambda j, i: (j, 0, 0))
    o_spec = pl.BlockSpec((None, tm, Fb), lambda j, i: (j, i, 0))
    shp = jax.ShapeDtypeStruct((J, S, Fb), BF16)
    return pl.pallas_call(
        body, name=name, grid=(J, S // tm),
        in_specs=[pl.BlockSpec((tm, D), lambda j, i: (i, 0)), w_spec, w_spec],
        out_specs=[o_spec, o_spec, o_spec], out_shape=[shp, shp, shp], compiler_params=_params(48),
    )(h, wg, wu)


def proj_cols(h, w, name):
    S, D = h.shape
    J, _, Wb = w.shape
    tm = _tile(S, 1024)

    def body(h_ref, w_ref, z_ref):
        z_ref[...] = _dot(h_ref[...], w_ref[...]).astype(BF16)

    return pl.pallas_call(
        body, name=name, grid=(J, S // tm),
        in_specs=[pl.BlockSpec((tm, D), lambda j, i: (i, 0)), pl.BlockSpec((None, D, Wb), lambda j, i: (j, 0, 0))],
        out_specs=pl.BlockSpec((tm, Wb), lambda j, i: (i, j)),
        out_shape=jax.ShapeDtypeStruct((S, J * Wb), BF16), compiler_params=_params(48),
    )(h, w)


def ffn_down(hid, wd, x, name):
    J, S, Fb = hid.shape
    D = wd.shape[2]
    tm = _tile(S, 512)

    def body(hid_ref, wd_ref, x_ref, o_ref, acc_ref):
        j = pl.program_id(1)

        @pl.when(j == 0)
        def _():
            acc_ref[...] = jnp.zeros_like(acc_ref)

        acc_ref[...] += _dot(hid_ref[...], wd_ref[...])

        @pl.when(j == J - 1)
        def _():
            o_ref[...] = x_ref[...] + FFN_RESIDUAL * acc_ref[...]

    row = pl.BlockSpec((tm, D), lambda i, j: (i, 0))
    return pl.pallas_call(
        body, name=name, grid=(S // tm, J),
        in_specs=[pl.BlockSpec((None, tm, Fb), lambda i, j: (j, i, 0)),
                  pl.BlockSpec((None, Fb, D), lambda i, j: (j, 0, 0)), row],
        out_specs=row, out_shape=jax.ShapeDtypeStruct((S, D), F32),
        scratch_shapes=[pltpu.VMEM((tm, D), F32)], compiler_params=_params(48),
    )(hid, wd, x)


def ffn_dhid(gb, wd, a, b, name):
    S, D = gb.shape
    J, Fb, _ = wd.shape
    tm = _tile(S, 1024)

    def body(g_ref, wd_ref, a_ref, b_ref, da_ref, db_ref):
        dhid = _dot(g_ref[...], wd_ref[...], NT) * FFN_RESIDUAL
        a = a_ref[...].astype(F32)
        b = b_ref[...].astype(F32)
        sg = jax.nn.sigmoid(a)
        da_ref[...] = (dhid * b * (sg * (1.0 + a * (1.0 - sg)))).astype(BF16)
        db_ref[...] = (dhid * (a * sg)).astype(BF16)

    o_spec = pl.BlockSpec((None, tm, Fb), lambda j, i: (j, i, 0))
    shp = jax.ShapeDtypeStruct((J, S, Fb), BF16)
    return pl.pallas_call(
        body, name=name, grid=(J, S // tm),
        in_specs=[pl.BlockSpec((tm, D), lambda j, i: (i, 0)), pl.BlockSpec((None, Fb, D), lambda j, i: (j, 0, 0)),
                  o_spec, o_spec],
        out_specs=[o_spec, o_spec], out_shape=[shp, shp], compiler_params=_params(48),
    )(gb, wd, a, b)


def back_proj(ds, ws, stacked, name):
    n = len(ds)
    J, D, Wb = ws[0].shape
    S = ds[0].shape[1] if stacked else ds[0].shape[0]
    tm = _tile(S, 512)

    def body(*refs):
        d_refs, w_refs, o_ref, acc_ref = refs[:n], refs[n:2 * n], refs[2 * n], refs[2 * n + 1]
        j = pl.program_id(1)

        @pl.when(j == 0)
        def _():
            acc_ref[...] = jnp.zeros_like(acc_ref)

        for d_ref, w_ref in zip(d_refs, w_refs):
            acc_ref[...] += _dot(d_ref[...], w_ref[...], NT)

        @pl.when(j == J - 1)
        def _():
            o_ref[...] = acc_ref[...]

    if stacked:
        d_spec = pl.BlockSpec((None, tm, Wb), lambda i, j: (j, i, 0))
    else:
        d_spec = pl.BlockSpec((tm, Wb), lambda i, j: (i, j))
    w_spec = pl.BlockSpec((None, D, Wb), lambda i, j: (j, 0, 0))
    return pl.pallas_call(
        body, name=name, grid=(S // tm, J),
        in_specs=[d_spec] * n + [w_spec] * n,
        out_specs=pl.BlockSpec((tm, D), lambda i, j: (i, 0)), out_shape=jax.ShapeDtypeStruct((S, D), F32),
        scratch_shapes=[pltpu.VMEM((tm, D), F32)], compiler_params=_params(48),
    )(*ds, *ws)


def grad_shared_lhs(h, ds, stacked, name):
    n = len(ds)
    S, D = h.shape
    if stacked:
        J, _, Wb = ds[0].shape
    else:
        J, Wb = NDEV, ds[0].shape[1] // NDEV
    tk = _tile(S, 512)
    nk = S // tk

    def body(*refs):
        h_ref, d_refs, o_refs, acc_refs = refs[0], refs[1:1 + n], refs[1 + n:1 + 2 * n], refs[1 + 2 * n:]
        i = pl.program_id(1)
        hv = h_ref[...]
        for d_ref, o_ref, acc_ref in zip(d_refs, o_refs, acc_refs):
            @pl.when(i == 0)
            def _():
                acc_ref[...] = jnp.zeros_like(acc_ref)

            acc_ref[...] += _dot(hv, d_ref[...], TN)

            @pl.when(i == nk - 1)
            def _():
                o_ref[...] = acc_ref[...].astype(BF16)

    o_spec = pl.BlockSpec((None, D, Wb), lambda j, i: (j, 0, 0))
    return pl.pallas_call(
        body, name=name, grid=(J, nk),
        in_specs=[pl.BlockSpec((tk, D), lambda j, i: (i, 0))] + [_blk(d, Wb, tk, stacked) for d in ds],
        out_specs=[o_spec] * n, out_shape=[jax.ShapeDtypeStruct((J, D, Wb), BF16)] * n,
        scratch_shapes=[pltpu.VMEM((D, Wb), F32)] * n, compiler_params=_params(56),
    )(h, *ds)


def grad_shared_rhs(a, gb, stacked, scale, name):
    S, D = gb.shape
    if stacked:
        J, _, Wb = a.shape
    else:
        J, Wb = NDEV, a.shape[1] // NDEV
    tk = _tile(S, 512)
    nk = S // tk

    def body(a_ref, g_ref, o_ref, acc_ref):
        i = pl.program_id(1)

        @pl.when(i == 0)
        def _():
            acc_ref[...] = jnp.zeros_like(acc_ref)

        acc_ref[...] += _dot(a_ref[...], g_ref[...], TN)

        @pl.when(i == nk - 1)
        def _():
            o_ref[...] = (acc_ref[...] * scale).astype(BF16)

    return pl.pallas_call(
        body, name=name, grid=(J, nk),
        in_specs=[_blk(a, Wb, tk, stacked), pl.BlockSpec((tk, D), lambda j, i: (i, 0))],
        out_specs=pl.BlockSpec((None, Wb, D), lambda j, i: (j, 0, 0)),
        out_shape=jax.ShapeDtypeStruct((J, Wb, D), BF16),
        scratch_shapes=[pltpu.VMEM((Wb, D), F32)], compiler_params=_params(48),
    )(a, gb)


def out_proj(ya, ycp, wo, x, name):
    S, A = ya.shape
    Dm, D = wo.shape
    tm = _tile(S, 512)

    def body(ya_ref, ycp_ref, wo_ref, x_ref, o_ref):
        acc = _dot(ya_ref[...], wo_ref[0:A, :]) + _dot(ycp_ref[...], wo_ref[A:Dm, :])
        o_ref[...] = x_ref[...] + acc

    row = pl.BlockSpec((tm, D), lambda i: (i, 0))
    return pl.pallas_call(
        body, name=name, grid=(S // tm,),
        in_specs=[pl.BlockSpec((tm, A), lambda i: (i, 0)), pl.BlockSpec((tm, Dm - A), lambda i: (i, 0)),
                  pl.BlockSpec((Dm, D), lambda i: (0, 0)), row],
        out_specs=row, out_shape=jax.ShapeDtypeStruct((S, D), F32), compiler_params=_params(48),
    )(ya, ycp, wo, x)


def out_proj_bwd(gb, wo, A, name):
    S, D = gb.shape
    Dm = wo.shape[0]
    tm = _tile(S, 512)

    def body(g_ref, wo_ref, dya_ref, dycp_ref):
        gv = g_ref[...]
        dya_ref[...] = _dot(gv, wo_ref[0:A, :], NT).astype(BF16)
        dycp_ref[...] = _dot(gv, wo_ref[A:Dm, :], NT).astype(BF16)

    return pl.pallas_call(
        body, name=name, grid=(S // tm,),
        in_specs=[pl.BlockSpec((tm, D), lambda i: (i, 0)), pl.BlockSpec((Dm, D), lambda i: (0, 0))],
        out_specs=[pl.BlockSpec((tm, A), lambda i: (i, 0)), pl.BlockSpec((tm, Dm - A), lambda i: (i, 0))],
        out_shape=[jax.ShapeDtypeStruct((S, A), BF16), jax.ShapeDtypeStruct((S, Dm - A), BF16)],
        compiler_params=_params(48),
    )(gb, wo)


def _lane_stat(v, hi):
    lane = lax.broadcasted_iota(jnp.int32, v.shape, 1)
    sel = (lane >= HEAD_DIM) if hi else (lane < HEAD_DIM)
    return jnp.max(jnp.where(sel, v, -jnp.inf), axis=-1, keepdims=True)


def band_fwd(q, k, v, L, name):
    H, S, W = q.shape
    T = min(512, L)
    nb = T // BLK

    def body(q_ref, k_ref, kp_ref, v_ref, vp_ref, o_ref):
        i = pl.program_id(1)
        first_key = jnp.where((i * T) % L != 0, 0, BLK)
        qi = lax.broadcasted_iota(jnp.int32, (BLK, 2 * BLK), 0)
        kj = lax.broadcasted_iota(jnp.int32, (BLK, 2 * BLK), 1)
        band = (kj >= qi) & (kj <= qi + BLK)
        lane = lax.broadcasted_iota(jnp.int32, (BLK, W), 1)
        for b in range(nb):
            rows = slice(b * BLK, (b + 1) * BLK)
            if b == 0:
                kw = jnp.concatenate([kp_ref[...], k_ref[rows, :]], axis=0)
                vw = jnp.concatenate([vp_ref[...], v_ref[rows, :]], axis=0)
                mask = band & (kj >= first_key)
            else:
                kw = k_ref[(b - 1) * BLK:(b + 1) * BLK, :]
                vw = v_ref[(b - 1) * BLK:(b + 1) * BLK, :]
                mask = band
            s = jnp.where(mask, _dot(q_ref[rows, :], kw, NT) * ATTN_SCALE, NEG_INF)
            m = jnp.max(s, axis=-1, keepdims=True)
            p = jnp.exp(s - m)
            l = jnp.sum(p, axis=-1, keepdims=True)
            o = _dot(p.astype(BF16), vw)
            o_ref[rows, :] = jnp.where(lane < HEAD_DIM, o / l, m + jnp.log(l))

    cur = pl.BlockSpec((None, T, W), lambda h, i: (h, i, 0))
    prev = pl.BlockSpec((None, BLK, W), lambda h, i: (h, jnp.maximum(i * nb - 1, 0), 0))
    return pl.pallas_call(
        body, name=name, grid=(H, S // T),
        in_specs=[cur, cur, prev, cur, prev], out_specs=cur,
        out_shape=jax.ShapeDtypeStruct((H, S, W), F32), compiler_params=_params(32),
    )(q, k, k, v, v)


def band_bwd(q, k, v, do, st, L, name):
    H, S, W = q.shape
    T = min(512, L)
    nb = T // BLK
    last_blk = S // BLK - 1

    def body(q_ref, qn_ref, do_ref, don_ref, st_ref, stn_ref, k_ref, kp_ref, v_ref, vp_ref,
             dq_ref, dk_ref, dv_ref, dk_acc, dv_acc):
        i = pl.program_id(1)
        first_key = jnp.where((i * T) % L != 0, 0, BLK)
        next_off = jnp.where(((i + 1) * T) % L != 0, 0, 4 * BLK)
        qi = lax.broadcasted_iota(jnp.int32, (BLK, 2 * BLK), 0)
        kj = lax.broadcasted_iota(jnp.int32, (BLK, 2 * BLK), 1)
        band = (kj >= qi) & (kj <= qi + BLK)
        qi1 = lax.broadcasted_iota(jnp.int32, (BLK, BLK), 0)
        kj1 = lax.broadcasted_iota(jnp.int32, (BLK, BLK), 1)
        dk_acc[...] = jnp.zeros_like(dk_acc)
        dv_acc[...] = jnp.zeros_like(dv_acc)
        for b in range(nb + 1):
            rows = slice(b * BLK, (b + 1) * BLK)
            if b < nb:
                qb, dob, stb = q_ref[rows, :], do_ref[rows, :], st_ref[rows, :]
            else:
                qb, dob, stb = qn_ref[...], don_ref[...], stn_ref[...]
            if b == 0:
                kw = jnp.concatenate([kp_ref[...], k_ref[rows, :]], axis=0)
                vw = jnp.concatenate([vp_ref[...], v_ref[rows, :]], axis=0)
                mask = band & (kj >= first_key)
            elif b < nb:
                kw = k_ref[(b - 1) * BLK:(b + 1) * BLK, :]
                vw = v_ref[(b - 1) * BLK:(b + 1) * BLK, :]
                mask = band
            else:
                kw = k_ref[(nb - 1) * BLK:nb * BLK, :]
                vw = v_ref[(nb - 1) * BLK:nb * BLK, :]
                mask = kj1 >= qi1 + next_off
            s = _dot(qb, kw, NT) * ATTN_SCALE
            p = jnp.where(mask, jnp.exp(s - _lane_stat(stb, False)), 0.0)
            dp = _dot(dob, vw, NT)
            ds = (p * (dp - _lane_stat(stb, True))).astype(BF16)
            if b < nb:
                dq_ref[rows, :] = _dot(ds, kw) * ATTN_SCALE
            win = slice(b * BLK, b * BLK + kw.shape[0])
            dk_acc[win, :] += _dot(ds, qb, TN) * ATTN_SCALE
            dv_acc[win, :] += _dot(p.astype(BF16), dob, TN)
        dk_ref[...] = dk_acc[BLK:, :]
        dv_ref[...] = dv_acc[BLK:, :]

    cur = pl.BlockSpec((None, T, W), lambda h, i: (h, i, 0))
    prev = pl.BlockSpec((None, BLK, W), lambda h, i: (h, jnp.maximum(i * nb - 1, 0), 0))
    nxt = pl.BlockSpec((None, BLK, W), lambda h, i: (h, jnp.minimum((i + 1) * nb, last_blk), 0))
    shp = jax.ShapeDtypeStruct((H, S, W), F32)
    return pl.pallas_call(
        body, name=name, grid=(H, S // T),
        in_specs=[cur, nxt, cur, nxt, cur, nxt, cur, prev, cur, prev],
        out_specs=[cur, cur, cur], out_shape=[shp, shp, shp],
        scratch_shapes=[pltpu.VMEM((T + BLK, W), F32), pltpu.VMEM((T + BLK, W), F32)],
        compiler_params=_params(32),
    )(q, q, do, do, st, st, k, k, v, v)


def attn_merge(os_, name):
    H, S, W = os_[0].shape
    tm = _tile(S, 1024)

    def body(*refs):
        o_ref = refs[-1]
        vals = [r[...] for r in refs[:-1]]
        lses = [_lane_stat(v, True) for v in vals]
        m = functools.reduce(jnp.maximum, lses)
        ws = [jnp.exp(l - m) for l in lses]
        tot = functools.reduce(jnp.add, ws)
        out = functools.reduce(jnp.add, [(w / tot) * v for w, v in zip(ws, vals)])
        lane = lax.broadcasted_iota(jnp.int32, out.shape, 1)
        o_ref[...] = jnp.where(lane < HEAD_DIM, out, m + jnp.log(tot))

    spec = pl.BlockSpec((None, tm, W), lambda h, i: (h, i, 0))
    return pl.pallas_call(
        body, name=name, grid=(H, S // tm), in_specs=[spec] * len(os_), out_specs=spec,
        out_shape=jax.ShapeDtypeStruct((H, S, W), F32), compiler_params=_params(32),
    )(*os_)


def attn_bwd_stats(merged, do, name):
    H, S, W = merged.shape
    tm = _tile(S, 1024)

    def body(m_ref, do_ref, st_ref):
        mv = m_ref[...]
        delta = jnp.sum(mv * do_ref[...].astype(F32), axis=-1, keepdims=True)
        lane = lax.broadcasted_iota(jnp.int32, mv.shape, 1)
        st_ref[...] = jnp.where(lane < HEAD_DIM, _lane_stat(mv, True), delta)

    spec = pl.BlockSpec((None, tm, W), lambda h, i: (h, i, 0))
    return pl.pallas_call(
        body, name=name, grid=(H, S // tm), in_specs=[spec, spec], out_specs=spec,
        out_shape=jax.ShapeDtypeStruct((H, S, W), F32), compiler_params=_params(32),
    )(merged, do)


def sum_patterns(ts, name):
    H, S, W = ts[0].shape
    tm = _tile(S, 1024)

    def body(*refs):
        refs[-1][...] = functools.reduce(jnp.add, [r[...] for r in refs[:-1]]).astype(BF16)

    spec = pl.BlockSpec((None, tm, W), lambda h, i: (h, i, 0))
    return pl.pallas_call(
        body, name=name, grid=(H, S // tm), in_specs=[spec] * len(ts), out_specs=spec,
        out_shape=jax.ShapeDtypeStruct((H, S, W), BF16), compiler_params=_params(32),
    )(*ts)


def to_heads(t):
    S = t.shape[0]
    t = jnp.transpose(t.reshape(S, -1, HEAD_DIM), (1, 0, 2))
    return jnp.pad(t, ((0, 0), (0, 0), (0, LANES - HEAD_DIM)))


def from_heads(t):
    H, S, _ = t.shape
    return jnp.transpose(t[:, :, :HEAD_DIM], (1, 0, 2)).reshape(S, H * HEAD_DIM)


def permute(t, d):
    if d == 1:
        return t
    H, S, W = t.shape
    return jnp.transpose(t.reshape(H, S // d, d, W), (0, 2, 1, 3)).reshape(H, S, W)


def unpermute(t, d):
    if d == 1:
        return t
    H, S, W = t.shape
    return jnp.transpose(t.reshape(H, d, S // d, W), (0, 2, 1, 3)).reshape(H, S, W)


def _shift_down(e, k):
    return pltpu.roll(e, k, 0)


def _shift_up(e, k):
    return pltpu.roll(e, e.shape[0] - k, 0)


def _causal_sums(e, g):
    for lvl in range(g + 1):
        e = e + _shift_down(e, 1 << lvl)
    return e


def _anticausal_sums(e, g):
    for lvl in range(g + 1):
        e = e + _shift_up(e, 1 << lvl)
    return e


def _mixer_specs(S, tm, A, C, P):
    assert (3 * A) % C == 0 and (3 * A + 3 * C) % P == 0 and tm % HALO == 0
    cb, pb = 3 * A // C, (3 * A + 3 * C) // P
    hb = tm // HALO
    last = S // HALO - 1
    col = lambda w, j: pl.BlockSpec((tm, w), lambda i: (i, j))
    prev = lambda w, j: pl.BlockSpec((HALO, w), lambda i: (jnp.maximum(i * hb - 1, 0), j))
    nxt = lambda w, j: pl.BlockSpec((HALO, w), lambda i: (jnp.minimum((i + 1) * hb, last), j))
    return cb, pb, col, prev, nxt


def _conv_taps(gc, ci, gch, cih, keep_prev):
    u = gc * ci
    e = jnp.concatenate([gch * cih * keep_prev, u], axis=0)
    return u, _shift_down(e, 1)[HALO:], _shift_down(e, 2)[HALO:]


def _pooled(xp, xph, keep_prev, pos, g):
    cols = slice(g * POOL_GROUP, (g + 1) * POOL_GROUP)
    x = xp[:, cols]
    e = jnp.concatenate([xph[:, cols] * keep_prev, x], axis=0)
    cnt = jnp.minimum(pos + 1, POOL_WINDOWS[g]).astype(F32)
    return _causal_sums(e, g)[HALO:] / cnt - x


def convpool_fwd(z, conv_w, pool_w, pool_scale, A, C, P, name):
    S = z.shape[0]
    tm = _tile(S, 512)
    cb, pb, col, prev, _ = _mixer_specs(S, tm, A, C, P)

    def body(gb_ref, gc_ref, ci_ref, xp_ref, gch_ref, cih_ref, xph_ref, cw_ref, pw_ref, ps_ref, y_ref):
        i = pl.program_id(0)
        keep_prev = jnp.where(i == 0, 0.0, 1.0)
        f = lambda r: r[...].astype(F32)
        u, u1, u2 = _conv_taps(f(gc_ref), f(ci_ref), f(gch_ref), f(cih_ref), keep_prev)
        cw = cw_ref[...]
        y_ref[:, 0:C] = (f(gb_ref) * (cw[0:1] * u2 + cw[1:2] * u1 + cw[2:3] * u)).astype(BF16)
        xp, xph = f(xp_ref), f(xph_ref)
        pos = i * tm + lax.broadcasted_iota(jnp.int32, (tm, 1), 0)
        for g in range(len(POOL_WINDOWS)):
            cols = slice(g * POOL_GROUP, (g + 1) * POOL_GROUP)
            lin = _dot(_pooled(xp, xph, keep_prev, pos, g).astype(BF16), pw_ref[g])
            y_ref[:, C + g * POOL_GROUP:C + (g + 1) * POOL_GROUP] = (lin * ps_ref[:, cols]).astype(BF16)

    full = lambda shape: pl.BlockSpec(shape, lambda i: (0,) * len(shape))
    return pl.pallas_call(
        body, name=name, grid=(S // tm,),
        in_specs=[col(C, cb), col(C, cb + 1), col(C, cb + 2), col(P, pb),
                  prev(C, cb + 1), prev(C, cb + 2), prev(P, pb),
                  full(conv_w.shape), full(pool_w.shape), full(pool_scale.shape)],
        out_specs=pl.BlockSpec((tm, C + P), lambda i: (i, 0)),
        out_shape=jax.ShapeDtypeStruct((S, C + P), BF16), compiler_params=_params(48),
    )(z, z, z, z, z, z, z, conv_w, pool_w, pool_scale)


def convpool_bwd(z, dycp, conv_w, pool_w, pool_scale, A, C, P, name):
    S = z.shape[0]
    tm = _tile(S, 512)
    nt = S // tm
    cb, pb, col, prev, nxt = _mixer_specs(S, tm, A, C, P)
    NG = len(POOL_WINDOWS)

    def body(gb_ref, gc_ref, ci_ref, xp_ref, gch_ref, cih_ref, xph_ref, gbn_ref, dy_ref, dyn_ref,
             cw_ref, pw_ref, ps_ref, dz_ref, dcw_ref, dpw_ref, dps_ref):
        i = pl.program_id(0)

        @pl.when(i == 0)
        def _():
            dcw_ref[...] = jnp.zeros_like(dcw_ref)
            dpw_ref[...] = jnp.zeros_like(dpw_ref)
            dps_ref[...] = jnp.zeros_like(dps_ref)

        keep_prev = jnp.where(i == 0, 0.0, 1.0)
        keep_next = jnp.where(i == nt - 1, 0.0, 1.0)
        f = lambda r: r[...].astype(F32)
        gb, gc, ci = f(gb_ref), f(gc_ref), f(ci_ref)
        u, u1, u2 = _conv_taps(gc, ci, f(gch_ref), f(cih_ref), keep_prev)
        cw = cw_ref[...]
        dy, dyn = f(dy_ref), f(dyn_ref) * keep_next
        dyc = dy[:, 0:C]
        dz_ref[:, 0:C] = (dyc * (cw[0:1] * u2 + cw[1:2] * u1 + cw[2:3] * u)).astype(BF16)
        dc = dyc * gb
        e = jnp.concatenate([dc, dyn[:, 0:C] * f(gbn_ref)], axis=0)
        du = cw[2:3] * dc + cw[1:2] * _shift_up(e, 1)[:tm] + cw[0:1] * _shift_up(e, 2)[:tm]
        dz_ref[:, C:2 * C] = (du * ci).astype(BF16)
        dz_ref[:, 2 * C:3 * C] = (du * gc).astype(BF16)
        dcw_ref[0:1, :] += jnp.sum(dc * u2, axis=0, keepdims=True)
        dcw_ref[1:2, :] += jnp.sum(dc * u1, axis=0, keepdims=True)
        dcw_ref[2:3, :] += jnp.sum(dc * u, axis=0, keepdims=True)

        xp, xph = f(xp_ref), f(xph_ref)
        pos = i * tm + lax.broadcasted_iota(jnp.int32, (tm, 1), 0)
        pos_e = i * tm + lax.broadcasted_iota(jnp.int32, (tm + HALO, 1), 0)
        for g in range(NG):
            cols = slice(g * POOL_GROUP, (g + 1) * POOL_GROUP)
            ycols = slice(C + g * POOL_GROUP, C + (g + 1) * POOL_GROUP)
            pooled = _pooled(xp, xph, keep_prev, pos, g).astype(BF16)
            pw = pw_ref[g]
            dyp = dy[:, ycols]
            dps_ref[:, cols] += jnp.sum(dyp * _dot(pooled, pw), axis=0, keepdims=True)
            dlin = (jnp.concatenate([dyp, dyn[:, ycols]], axis=0) * ps_ref[:, cols]).astype(BF16)
            dpw_ref[g] += _dot(pooled, dlin[:tm], TN)
            dpool = _dot(dlin, pw, NT)
            r = dpool / jnp.minimum(pos_e + 1, POOL_WINDOWS[g]).astype(F32)
            dz_ref[:, 3 * C + g * POOL_GROUP:3 * C + (g + 1) * POOL_GROUP] = (
                _anticausal_sums(r, g)[:tm] - dpool[:tm]).astype(BF16)

    full = lambda shape: pl.BlockSpec(shape, lambda i: (0,) * len(shape))
    return pl.pallas_call(
        body, name=name, grid=(nt,),
        in_specs=[col(C, cb), col(C, cb + 1), col(C, cb + 2), col(P, pb),
                  prev(C, cb + 1), prev(C, cb + 2), prev(P, pb), nxt(C, cb),
                  pl.BlockSpec((tm, C + P), lambda i: (i, 0)),
                  pl.BlockSpec((HALO, C + P), lambda i: (jnp.minimum((i + 1) * (tm // HALO), S // HALO - 1), 0)),
                  full(conv_w.shape), full(pool_w.shape), full(pool_scale.shape)],
        out_specs=[pl.BlockSpec((tm, 3 * C + P), lambda i: (i, 0)),
                   full(conv_w.shape), full(pool_w.shape), full(pool_scale.shape)],
        out_shape=[jax.ShapeDtypeStruct((S, 3 * C + P), BF16), jax.ShapeDtypeStruct(conv_w.shape, F32),
                   jax.ShapeDtypeStruct(pool_w.shape, F32), jax.ShapeDtypeStruct(pool_scale.shape, F32)],
        compiler_params=_params(48),
    )(z, z, z, z, z, z, z, z, dycp, dycp, conv_w, pool_w, pool_scale)


def adamw(lands, w, m, v, name):
    R, C = w.shape
    nl = len(lands)
    n, Rl, _ = lands[0].shape
    assert Rl * nl == R
    tr = _tile(Rl, 128 * 1024 // C)
    nb = Rl // tr
    c1 = 1.0 - ADAM_B1 ** ADAM_STEP
    c2 = 1.0 - ADAM_B2 ** ADAM_STEP

    def body(*refs):
        land_refs = refs[:nl]
        w_ref, m_ref, v_ref, g_ref, d_ref, mo_ref, vo_ref = refs[nl:]
        i = pl.program_id(0)
        for a, land_ref in enumerate(land_refs):
            @pl.when((i >= a * nb) & (i < (a + 1) * nb))
            def _():
                g = land_ref[0].astype(F32)
                for s in range(1, n):
                    g = g + land_ref[s].astype(F32)
                g_ref[...] = g
                mn = ADAM_B1 * m_ref[...] + (1.0 - ADAM_B1) * g
                vn = ADAM_B2 * v_ref[...] + (1.0 - ADAM_B2) * (g * g)
                mo_ref[...] = mn
                vo_ref[...] = vn
                d_ref[...] = -ADAM_LR * ((mn / c1) / (jnp.sqrt(vn / c2) + ADAM_EPS) + ADAM_WD * w_ref[...])

    land_specs = [pl.BlockSpec((n, tr, C), lambda i, a=a: (0, jnp.clip(i - a * nb, 0, nb - 1), 0))
                  for a in range(nl)]
    row = pl.BlockSpec((tr, C), lambda i: (i, 0))
    shp = jax.ShapeDtypeStruct((R, C), F32)
    return pl.pallas_call(
        body, name=name, grid=(nl * nb,),
        in_specs=land_specs + [row, row, row], out_specs=[row] * 4, out_shape=[shp] * 4,
        compiler_params=_params(48),
    )(*lands, w, m, v)


def _ffn_fwd(x, gain, wg, wu, wd, tag):
    h = rms_fwd(x, gain, f"rms_{tag}")
    a, b, hid = ffn_up(h, wg, wu, f"ffn_up_{tag}")
    return ffn_down(hid, wd, x, f"ffn_down_{tag}"), (x, h, a, b, hid)


def _ffn_bwd(g, gb, saved, gain, wg, wu, wd, tag):
    x, h, a, b, hid = saved
    da, db = ffn_dhid(gb, wd, a, b, f"ffn_dhid_{tag}")
    dwd = grad_shared_rhs(hid, gb, True, FFN_RESIDUAL, f"ffn_dwd_{tag}")
    dwg, dwu = grad_shared_lhs(h, [da, db], True, f"ffn_dwgu_{tag}")
    dh = back_proj([da, db], [wg, wu], True, f"ffn_dh_{tag}")
    g, gb, dgain = rms_bwd(x, gain, dh, g, f"rms_bwd_{tag}")
    return g, gb, (dgain, dwg, dwu, dwd)


def _mixer_fwd(x, gain, w_in, conv_w, pool_w, pool_scale, w_out, dims, tag):
    A, C, P = dims
    h = rms_fwd(x, gain, f"rms_{tag}")
    z = proj_cols(h, w_in, f"in_proj_{tag}")
    qkv = [to_heads(z[:, n * A:(n + 1) * A]) for n in range(3)]
    S = x.shape[0]
    perms, outs = [], []
    for d in DILATIONS:
        qd, kd, vd = [permute(t, d) for t in qkv]
        perms.append((qd, kd, vd))
        outs.append(unpermute(band_fwd(qd, kd, vd, S // d, f"band_fwd_{tag}_d{d}"), d))
    merged = attn_merge(outs, f"attn_merge_{tag}")
    ya = from_heads(merged).astype(BF16)
    ycp = convpool_fwd(z, conv_w, pool_w, pool_scale, A, C, P, f"convpool_{tag}")
    xo = out_proj(ya, ycp, w_out, x, f"out_proj_{tag}")
    return xo, (x, h, z, perms, merged, ya, ycp)


def _mixer_bwd(g, gb, saved, gain, w_in, conv_w, pool_w, pool_scale, w_out, dims, tag):
    A, C, P = dims
    x, h, z, perms, merged, ya, ycp = saved
    S = x.shape[0]
    dya, dycp = out_proj_bwd(gb, w_out, A, f"out_proj_bwd_{tag}")
    dwo_a = grad_shared_rhs(ya, gb, False, 1.0, f"dwo_attn_{tag}")
    dwo_cp = grad_shared_rhs(ycp, gb, False, 1.0, f"dwo_cp_{tag}")
    do = to_heads(dya)
    st = attn_bwd_stats(merged, do, f"attn_stats_{tag}")
    grads = []
    for d, (qd, kd, vd) in zip(DILATIONS, perms):
        parts = band_bwd(qd, kd, vd, permute(do, d), permute(st, d), S // d, f"band_bwd_{tag}_d{d}")
        grads.append([unpermute(t, d) for t in parts])
    dqkv = [from_heads(sum_patterns([gr[n] for gr in grads], f"attn_sum_{tag}_{n}")) for n in range(3)]
    dz_cp, dcw, dpw, dps = convpool_bwd(z, dycp, conv_w, pool_w, pool_scale, A, C, P, f"convpool_bwd_{tag}")
    dz = jnp.concatenate(dqkv + [dz_cp], axis=1)
    dwin, = grad_shared_lhs(h, [dz], False, f"dwin_{tag}")
    dh = back_proj([dz], [w_in], False, f"in_proj_bwd_{tag}")
    g, gb, dgain = rms_bwd(x, gain, dh, g, f"rms_bwd_{tag}")
    return g, gb, (dgain, dwin, dcw, dpw, dps, dwo_a, dwo_cp)


def _pack(arrs):
    flat = [a.reshape(-1).astype(F32) for a in arrs]
    spans, off = [], 0
    for a in flat:
        spans.append((off, a.shape[0]))
        off += a.shape[0]
    rows = -(-off // (8 * LANES)) * 8
    buf = jnp.concatenate(flat + [jnp.zeros((rows * LANES - off,), F32)]).reshape(rows, LANES)
    return buf, spans


def kernel(x, ffn1_norm, ffn1_w_gate, ffn1_w_up, ffn1_w_down, mix_norm, w_in, conv_w, pool_w, pool_scale, w_out, ffn2_norm, ffn2_w_gate, ffn2_w_up, ffn2_w_down, final_norm, loss_target, m_ffn1_norm, m_ffn1_w_gate, m_ffn1_w_up, m_ffn1_w_down, m_mix_norm, m_w_in, m_conv_w, m_pool_w, m_pool_scale, m_w_out, m_ffn2_norm, m_ffn2_w_gate, m_ffn2_w_up, m_ffn2_w_down, m_final_norm, v_ffn1_norm, v_ffn1_w_gate, v_ffn1_w_up, v_ffn1_w_down, v_mix_norm, v_w_in, v_conv_w, v_pool_w, v_pool_scale, v_w_out, v_ffn2_norm, v_ffn2_w_gate, v_ffn2_w_up, v_ffn2_w_down, v_final_norm):
    depth = ffn1_norm.shape[0]
    S, D = x.shape[1], x.shape[2]
    Cb = conv_w.shape[2]
    C = Cb * NDEV
    P = pool_scale.shape[1]
    A = (w_in.shape[2] * NDEV - 3 * C - P) // 3
    dims = (A, C, P)
    me = 4 * lax.axis_index("x") + 2 * lax.axis_index("y") + lax.axis_index("c")

    big = dict(ffn1_w_gate=ffn1_w_gate, ffn1_w_up=ffn1_w_up, ffn1_w_down=ffn1_w_down, w_in=w_in, w_out=w_out,
               ffn2_w_gate=ffn2_w_gate, ffn2_w_up=ffn2_w_up, ffn2_w_down=ffn2_w_down)
    big_m = dict(ffn1_w_gate=m_ffn1_w_gate, ffn1_w_up=m_ffn1_w_up, ffn1_w_down=m_ffn1_w_down, w_in=m_w_in,
                 w_out=m_w_out, ffn2_w_gate=m_ffn2_w_gate, ffn2_w_up=m_ffn2_w_up, ffn2_w_down=m_ffn2_w_down)
    big_v = dict(ffn1_w_gate=v_ffn1_w_gate, ffn1_w_up=v_ffn1_w_up, ffn1_w_down=v_ffn1_w_down, w_in=v_w_in,
                 w_out=v_w_out, ffn2_w_gate=v_ffn2_w_gate, ffn2_w_up=v_ffn2_w_up, ffn2_w_down=v_ffn2_w_down)
    names = list(big)

    full = []
    for l in range(depth):
        got = all_gather([big[n][l].astype(BF16) for n in names], f"gather_weights_l{l}")
        full.append(dict(zip(names, got)))
    conv_full = all_gather([jnp.pad(conv_w.reshape(-1, Cb), ((0, 2), (0, LANES - Cb)))], "gather_conv_w")[0]
    conv_full = jnp.transpose(conv_full[:, :depth * 3, :Cb].reshape(NDEV, depth, 3, Cb), (1, 2, 0, 3)).reshape(depth, 3, C)
    pool_w_bf = pool_w.astype(BF16)

    xs = x[0]
    saved = []
    for l in range(depth):
        W = full[l]
        wo = W["w_out"].reshape(-1, D)
        xs, s1 = _ffn_fwd(xs, ffn1_norm[l:l + 1], W["ffn1_w_gate"], W["ffn1_w_up"], W["ffn1_w_down"], f"f1_l{l}")
        xs, s2 = _mixer_fwd(xs, mix_norm[l:l + 1], W["w_in"], conv_full[l], pool_w_bf[l], pool_scale[l:l + 1], wo,
                            dims, f"mix_l{l}")
        xs, s3 = _ffn_fwd(xs, ffn2_norm[l:l + 1], W["ffn2_w_gate"], W["ffn2_w_up"], W["ffn2_w_down"], f"f2_l{l}")
        saved.append((s1, s2, s3))
    loss_part, g, gb, d_final = final_loss(xs, final_norm.reshape(1, D), loss_target[0], "final_loss")

    small = [None] * depth
    lands = [None] * depth
    for l in reversed(range(depth)):
        W = full[l]
        wo = W["w_out"].reshape(-1, D)
        s1, s2, s3 = saved[l]
        g, gb, (dn2, dwg2, dwu2, dwd2) = _ffn_bwd(g, gb, s3, ffn2_norm[l:l + 1], W["ffn2_w_gate"], W["ffn2_w_up"],
                                                 W["ffn2_w_down"], f"f2_l{l}")
        g, gb, (dnm, dwin, dcw, dpw, dps, dwo_a, dwo_cp) = _mixer_bwd(
            g, gb, s2, mix_norm[l:l + 1], W["w_in"], conv_full[l], pool_w_bf[l], pool_scale[l:l + 1], wo, dims,
            f"mix_l{l}")
        g, gb, (dn1, dwg1, dwu1, dwd1) = _ffn_bwd(g, gb, s1, ffn1_norm[l:l + 1], W["ffn1_w_gate"], W["ffn1_w_up"],
                                                 W["ffn1_w_down"], f"f1_l{l}")
        dwo = jnp.concatenate([dwo_a.reshape(-1, D), dwo_cp.reshape(-1, D)], axis=0).reshape(NDEV, -1, D)
        parts = dict(ffn1_w_gate=dwg1, ffn1_w_up=dwu1, ffn1_w_down=dwd1, w_in=dwin, w_out=dwo,
                     ffn2_w_gate=dwg2, ffn2_w_up=dwu2, ffn2_w_down=dwd2)
        lands[l] = dict(zip(names, exchange([parts[n] for n in names], f"exchange_grads_l{l}")))
        small[l] = (dn1, dnm, dcw, dpw, dps, dn2)

    res = {}
    for n in names:
        shp = big[n].shape
        two = lambda t: t.reshape(-1, shp[-1])
        outs = adamw([lands[l][n] for l in range(depth)], two(big[n]), two(big_m[n]), two(big_v[n]), f"adamw_{n}")
        res[n] = [o.reshape(shp) for o in outs]

    st = lambda i: jnp.stack([small[l][i] for l in range(depth)])
    small_g = dict(ffn1_norm=st(0), mix_norm=st(1), conv_w=st(2), pool_w=st(3), pool_scale=st(4), ffn2_norm=st(5),
                   final_norm=d_final)
    small_names = list(small_g)
    zeros_conv = jnp.zeros((depth, 3, C), F32)
    place = lambda t: lax.dynamic_update_slice(zeros_conv, t, (0, 0, me * Cb))
    small_w = dict(ffn1_norm=ffn1_norm, mix_norm=mix_norm, conv_w=place(conv_w), pool_w=pool_w, pool_scale=pool_scale,
                   ffn2_norm=ffn2_norm, final_norm=final_norm)
    small_m = dict(ffn1_norm=m_ffn1_norm, mix_norm=m_mix_norm, conv_w=place(m_conv_w), pool_w=m_pool_w,
                   pool_scale=m_pool_scale, ffn2_norm=m_ffn2_norm, final_norm=m_final_norm)
    small_v = dict(ffn1_norm=v_ffn1_norm, mix_norm=v_mix_norm, conv_w=place(v_conv_w), pool_w=v_pool_w,
                   pool_scale=v_pool_scale, ffn2_norm=v_ffn2_norm, final_norm=v_final_norm)
    gbuf, spans = _pack([small_g[n] for n in small_names] + [loss_part])
    wbuf, _ = _pack([small_w[n] for n in small_names] + [jnp.zeros((1, LANES), F32)])
    mbuf, _ = _pack([small_m[n] for n in small_names] + [jnp.zeros((1, LANES), F32)])
    vbuf, _ = _pack([small_v[n] for n in small_names] + [jnp.zeros((1, LANES), F32)])
    gathered = all_gather([gbuf], "gather_small_grads")[0]
    outs = adamw([gathered], wbuf, mbuf, vbuf, "adamw_small")
    for n, (off, size) in zip(small_names, spans):
        shp = small_w[n].shape
        vals = [o.reshape(-1)[off:off + size].reshape(shp) for o in outs]
        if n == "conv_w":
            vals = [lax.dynamic_slice(t, (0, 0, me * Cb), (depth, 3, Cb)) for t in vals]
        res[n] = vals
    loss = outs[0].reshape(-1)[spans[-1][0]]

    order = ["ffn1_norm", "ffn1_w_gate", "ffn1_w_up", "ffn1_w_down", "mix_norm", "w_in", "conv_w", "pool_w",
             "pool_scale", "w_out", "ffn2_norm", "ffn2_w_gate", "ffn2_w_up", "ffn2_w_down", "final_norm"]
    return (loss, g[None], *[res[n][0] for n in order], *[res[n][1] for n in order],
            *[res[n][2] for n in order], *[res[n][3] for n in order])
```

```python
import functools

import jax
import jax.numpy as jnp
from jax import lax
from jax.experimental import pallas as pl
from jax.experimental.pallas import tpu as pltpu

F32 = jnp.float32
BF16 = jnp.bfloat16
NDEV = 8
HEAD_DIM = 64
LANES = 128
BLK = 128
DILATIONS = (1, 4, 16)
POOL_WINDOWS = (2, 4, 8, 16)
POOL_GROUP = 128
HALO = 16
FFN_RESIDUAL = 0.5
RMS_EPS = 1e-6
NEG_INF = -1e30
ATTN_SCALE = HEAD_DIM ** -0.5
ADAM_LR, ADAM_B1, ADAM_B2, ADAM_EPS, ADAM_WD, ADAM_STEP = 0.001, 0.9, 0.999, 1e-08, 0.01, 10
VMEM_BYTES = 64 * 1024 * 1024
MESH = pl.DeviceIdType.MESH

NN = (((1,), (0,)), ((), ()))
NT = (((1,), (1,)), ((), ()))
TN = (((0,), (0,)), ((), ()))


def _dot(a, b, dims=NN):
    return lax.dot_general(a, b, dims, preferred_element_type=F32)


def _params(vmem_mb=48):
    return pltpu.CompilerParams(vmem_limit_bytes=min(vmem_mb * 1024 * 1024, VMEM_BYTES - 4 * 1024 * 1024))


def _tile(n, want):
    t = min(n, max(16, want // 16 * 16))
    while t > 16 and (n % t or t % 16):
        t -= 16
    return t if n % t == 0 else n


def _coords():
    return lax.axis_index("x"), lax.axis_index("y"), lax.axis_index("c")


def _comm_sems(n):
    return [pltpu.SemaphoreType.DMA((7 * n,)), pltpu.SemaphoreType.DMA((7 * n,)), pltpu.SemaphoreType.DMA((n,))]


def _gather_ops(ins, outs, sems):
    n = len(ins)
    send_sems, recv_sems, local_sems = sems
    x, y, c = _coords()
    me, sibling = (x, y, c), (x, y, 1 - c)
    chips = [(1 - x, y), (x, 1 - y), (1 - x, 1 - y)]

    def slot(out, p):
        return out.at[4 * p[0] + 2 * p[1] + p[2]]

    def copy(a, k, block, to, src=None):
        return pltpu.make_async_remote_copy(
            src_ref=slot(outs[a], block) if src is None else src, dst_ref=slot(outs[a], block),
            send_sem=send_sems.at[7 * a + k], recv_sem=recv_sems.at[7 * a + k],
            device_id=to, device_id_type=MESH)

    def own():
        mine = [pltpu.make_async_copy(ins[a], slot(outs[a], me), local_sems.at[a]) for a in range(n)]
        first = []
        for a in range(n):
            first.append(copy(a, 0, me, sibling, src=ins[a]))
            first += [copy(a, 1 + j, me, (*chip, c), src=ins[a]) for j, chip in enumerate(chips)]
        return mine, first

    def start():
        mine, first = own()
        for cp in mine + first:
            cp.start()

    def finish():
        mine, first = own()
        passed = []
        for j, chip in enumerate(chips):
            for a in range(n):
                copy(a, 1 + j, (*chip, c), me).wait_recv()
                fwd = copy(a, 4 + j, (*chip, c), sibling)
                fwd.start()
                passed.append(fwd)
        for a in range(n):
            copy(a, 0, sibling, me).wait_recv()
            for j, chip in enumerate(chips):
                copy(a, 4 + j, (*chip, 1 - c), me).wait_recv()
        for cp in first + passed:
            cp.wait_send()
        for cp in mine:
            cp.wait()

    return start, finish


def _exchange_ops(ins, outs, sems):
    n = len(ins)
    send_sems, recv_sems, local_sems = sems
    x, y, c = _coords()
    me = 4 * x + 2 * y + c

    def peer(k):
        return (1 - x if k & 4 else x, 1 - y if k & 2 else y, 1 - c if k & 1 else c)

    def copy(a, k):
        p = peer(k)
        return pltpu.make_async_remote_copy(
            src_ref=ins[a].at[4 * p[0] + 2 * p[1] + p[2]], dst_ref=outs[a].at[me],
            send_sem=send_sems.at[7 * a + k - 1], recv_sem=recv_sems.at[7 * a + k - 1],
            device_id=p, device_id_type=MESH)

    def landed(a, k):
        p = peer(k)
        return pltpu.make_async_remote_copy(
            src_ref=ins[a].at[me], dst_ref=outs[a].at[4 * p[0] + 2 * p[1] + p[2]],
            send_sem=send_sems.at[7 * a + k - 1], recv_sem=recv_sems.at[7 * a + k - 1],
            device_id=p, device_id_type=MESH)

    def own():
        mine = [pltpu.make_async_copy(ins[a].at[me], outs[a].at[me], local_sems.at[a]) for a in range(n)]
        return mine, [copy(a, k) for a in range(n) for k in range(1, NDEV)]

    def start():
        mine, sent = own()
        for cp in mine + sent:
            cp.start()

    def finish():
        mine, sent = own()
        for a in range(n):
            for k in range(1, NDEV):
                landed(a, k).wait_recv()
        for cp in sent:
            cp.wait_send()
        for cp in mine:
            cp.wait()

    return start, finish


_COMM = {"gather": (_gather_ops, lambda a: (NDEV,) + a.shape), "exchange": (_exchange_ops, lambda a: a.shape)}


def all_gather(arrs, name):
    n = len(arrs)

    def body(*refs):
        start, finish = _gather_ops(refs[:n], refs[n:2 * n], refs[2 * n:])
        start()
        finish()

    any_spec = pl.BlockSpec(memory_space=pl.ANY)
    return pl.pallas_call(
        body, name=name, out_shape=[jax.ShapeDtypeStruct((NDEV,) + a.shape, a.dtype) for a in arrs],
        in_specs=[any_spec] * n, out_specs=[any_spec] * n, scratch_shapes=_comm_sems(n),
    )(*arrs)


def _call(body, *, name, grid, in_specs, out_specs, out_shape, args, scratch=(), vmem=48, comm=None):
    if comm is None:
        outs = pl.pallas_call(
            body, name=name, grid=grid, in_specs=list(in_specs), out_specs=list(out_specs), out_shape=list(out_shape),
            scratch_shapes=list(scratch), compiler_params=_params(vmem))(*args)
        return list(outs), []
    kind, arrs = comm
    ops, shape_of = _COMM[kind]
    n, n_in, n_out, n_scr = len(arrs), len(in_specs), len(out_specs), len(scratch)

    def carrier(*refs):
        ins, c_in = refs[:n_in], refs[n_in:n_in + n]
        o0 = n_in + n
        outs, c_out = refs[o0:o0 + n_out], refs[o0 + n_out:o0 + n_out + n]
        s0 = o0 + n_out + n
        scr, sems = refs[s0:s0 + n_scr], refs[s0 + n_scr:]
        ids = [pl.program_id(d) for d in range(len(grid))]
        first = functools.reduce(jnp.logical_and, [i == 0 for i in ids])
        last = functools.reduce(jnp.logical_and, [i == g - 1 for i, g in zip(ids, grid)])
        start, finish = ops(c_in, c_out, sems)
        pl.when(first)(start)
        body(*ins, *outs, *scr)
        pl.when(last)(finish)

    any_spec = pl.BlockSpec(memory_space=pl.ANY)
    res = pl.pallas_call(
        carrier, name=name, grid=grid, in_specs=list(in_specs) + [any_spec] * n,
        out_specs=list(out_specs) + [any_spec] * n,
        out_shape=list(out_shape) + [jax.ShapeDtypeStruct(shape_of(a), a.dtype) for a in arrs],
        scratch_shapes=list(scratch) + _comm_sems(n), compiler_params=_params(vmem))(*args, *arrs)
    return list(res[:n_out]), list(res[n_out:])


def _rstd(x):
    return lax.rsqrt(jnp.mean(x * x, axis=-1, keepdims=True) + RMS_EPS)


def rms_fwd(x, gain, name):
    S, D = x.shape
    tm = _tile(S, 512)

    def body(x_ref, g_ref, h_ref, ht_ref):
        xv = x_ref[...]
        h = xv * _rstd(xv) * g_ref[...]
        h_ref[...] = h.astype(BF16)
        ht_ref[...] = h.T.astype(BF16)

    return pl.pallas_call(
        body, name=name, grid=(S // tm,),
        in_specs=[pl.BlockSpec((tm, D), lambda i: (i, 0)), pl.BlockSpec((1, D), lambda i: (0, 0))],
        out_specs=[pl.BlockSpec((tm, D), lambda i: (i, 0)), pl.BlockSpec((D, tm), lambda i: (0, i))],
        out_shape=[jax.ShapeDtypeStruct((S, D), BF16), jax.ShapeDtypeStruct((D, S), BF16)],
        compiler_params=_params(40),
    )(x, gain)


def _rms_bwd_math(xv, gain, dh):
    r = _rstd(xv)
    xhat = xv * r
    dxhat = dh * gain
    dx = r * (dxhat - xhat * jnp.mean(dxhat * xhat, axis=-1, keepdims=True))
    return dx, dh * xhat


def rms_bwd(x, gain, dh, g, name):
    S, D = x.shape
    tm = _tile(S, 256)

    def body(x_ref, gain_ref, dh_ref, g_ref, go_ref, gb_ref, dg_ref):
        @pl.when(pl.program_id(0) == 0)
        def _():
            dg_ref[...] = jnp.zeros_like(dg_ref)

        dx, dgain = _rms_bwd_math(x_ref[...], gain_ref[...], dh_ref[...])
        gn = g_ref[...] + dx
        go_ref[...] = gn
        gb_ref[...] = gn.astype(BF16)
        dg_ref[...] += jnp.sum(dgain, axis=0, keepdims=True)

    row = pl.BlockSpec((tm, D), lambda i: (i, 0))
    vec = pl.BlockSpec((1, D), lambda i: (0, 0))
    return pl.pallas_call(
        body, name=name, grid=(S // tm,),
        in_specs=[row, vec, row, row], out_specs=[row, row, vec],
        out_shape=[jax.ShapeDtypeStruct((S, D), F32), jax.ShapeDtypeStruct((S, D), BF16),
                   jax.ShapeDtypeStruct((1, D), F32)],
        compiler_params=_params(48),
    )(x, gain, dh, g)


def final_loss(x, gain, target, name):
    S, D = x.shape
    tm = _tile(S, 256)

    def body(x_ref, gain_ref, t_ref, loss_ref, go_ref, gb_ref, dg_ref):
        @pl.when(pl.program_id(0) == 0)
        def _():
            dg_ref[...] = jnp.zeros_like(dg_ref)
            loss_ref[...] = jnp.zeros_like(loss_ref)

        xv, gain_v = x_ref[...], gain_ref[...]
        err = xv * _rstd(xv) * gain_v - t_ref[...]
        loss_ref[...] += jnp.sum(jnp.sum(err * err, axis=-1, keepdims=True), axis=0, keepdims=True) * (0.5 / D)
        dx, dgain = _rms_bwd_math(xv, gain_v, err * (1.0 / D))
        go_ref[...] = dx
        gb_ref[...] = dx.astype(BF16)
        dg_ref[...] += jnp.sum(dgain, axis=0, keepdims=True)

    row = pl.BlockSpec((tm, D), lambda i: (i, 0))
    vec = pl.BlockSpec((1, D), lambda i: (0, 0))
    return pl.pallas_call(
        body, name=name, grid=(S // tm,),
        in_specs=[row, vec, row],
        out_specs=[pl.BlockSpec((1, LANES), lambda i: (0, 0)), row, row, vec],
        out_shape=[jax.ShapeDtypeStruct((1, LANES), F32), jax.ShapeDtypeStruct((S, D), F32),
                   jax.ShapeDtypeStruct((S, D), BF16), jax.ShapeDtypeStruct((1, D), F32)],
        compiler_params=_params(48),
    )(x, gain, target)


def _blk(arr, width, tm, stacked):
    if stacked:
        return pl.BlockSpec((None, tm, width), lambda j, i: (j, i, 0))
    return pl.BlockSpec((tm, width), lambda j, i: (i, j))


def ffn_up(h, wg, wu, name, comm=None):
    S, D = h.shape
    J, _, Fb = wg.shape
    tm = _tile(S, 1024)

    def body(h_ref, wg_ref, wu_ref, a_ref, b_ref, hid_ref):
        hv = h_ref[...]
        a = _dot(hv, wg_ref[...])
        b = _dot(hv, wu_ref[...])
        a_ref[...] = a.astype(BF16)
        b_ref[...] = b.astype(BF16)
        hid_ref[...] = (a * jax.nn.sigmoid(a) * b).astype(BF16)

    w_spec = pl.BlockSpec((None, D, Fb), lambda j, i: (j, 0, 0))
    o_spec = pl.BlockSpec((None, tm, Fb), lambda j, i: (j, i, 0))
    shp = jax.ShapeDtypeStruct((J, S, Fb), BF16)
    return _call(
        body, name=name, grid=(J, S // tm),
        in_specs=[pl.BlockSpec((tm, D), lambda j, i: (i, 0)), w_spec, w_spec],
        out_specs=[o_spec, o_spec, o_spec], out_shape=[shp, shp, shp], args=(h, wg, wu), comm=comm)


def proj_cols(h, w, name):
    S, D = h.shape
    J, _, Wb = w.shape
    tm = _tile(S, 1024)

    def body(h_ref, w_ref, z_ref):
        z_ref[...] = _dot(h_ref[...], w_ref[...]).astype(BF16)

    return pl.pallas_call(
        body, name=name, grid=(J, S // tm),
        in_specs=[pl.BlockSpec((tm, D), lambda j, i: (i, 0)), pl.BlockSpec((None, D, Wb), lambda j, i: (j, 0, 0))],
        out_specs=pl.BlockSpec((tm, Wb), lambda j, i: (i, j)),
        out_shape=jax.ShapeDtypeStruct((S, J * Wb), BF16), compiler_params=_params(48),
    )(h, w)


def ffn_down(hid, wd, x, name, comm=None):
    J, S, Fb = hid.shape
    D = wd.shape[2]
    tm = _tile(S, 512)

    def body(hid_ref, wd_ref, x_ref, o_ref, acc_ref):
        j = pl.program_id(1)

        @pl.when(j == 0)
        def _():
            acc_ref[...] = jnp.zeros_like(acc_ref)

        acc_ref[...] += _dot(hid_ref[...], wd_ref[...])

        @pl.when(j == J - 1)
        def _():
            o_ref[...] = x_ref[...] + FFN_RESIDUAL * acc_ref[...]

    row = pl.BlockSpec((tm, D), lambda i, j: (i, 0))
    (xo,), got = _call(
        body, name=name, grid=(S // tm, J),
        in_specs=[pl.BlockSpec((None, tm, Fb), lambda i, j: (j, i, 0)),
                  pl.BlockSpec((None, Fb, D), lambda i, j: (j, 0, 0)), row],
        out_specs=[row], out_shape=[jax.ShapeDtypeStruct((S, D), F32)],
        scratch=[pltpu.VMEM((tm, D), F32)], args=(hid, wd, x), comm=comm)
    return xo, got


def ffn_dhid(gb, wd, a, b, name):
    S, D = gb.shape
    J, Fb, _ = wd.shape
    tm = _tile(S, 1024)

    def body(g_ref, wd_ref, a_ref, b_ref, da_ref, db_ref):
        dhid = _dot(g_ref[...], wd_ref[...], NT) * FFN_RESIDUAL
        a = a_ref[...].astype(F32)
        b = b_ref[...].astype(F32)
        sg = jax.nn.sigmoid(a)
        da_ref[...] = (dhid * b * (sg * (1.0 + a * (1.0 - sg)))).astype(BF16)
        db_ref[...] = (dhid * (a * sg)).astype(BF16)

    o_spec = pl.BlockSpec((None, tm, Fb), lambda j, i: (j, i, 0))
    shp = jax.ShapeDtypeStruct((J, S, Fb), BF16)
    return pl.pallas_call(
        body, name=name, grid=(J, S // tm),
        in_specs=[pl.BlockSpec((tm, D), lambda j, i: (i, 0)), pl.BlockSpec((None, Fb, D), lambda j, i: (j, 0, 0)),
                  o_spec, o_spec],
        out_specs=[o_spec, o_spec], out_shape=[shp, shp], compiler_params=_params(48),
    )(gb, wd, a, b)


def back_proj(ds, ws, stacked, name, comm=None):
    n = len(ds)
    J, D, Wb = ws[0].shape
    S = ds[0].shape[1] if stacked else ds[0].shape[0]
    tm = _tile(S, 512)

    def body(*refs):
        d_refs, w_refs, o_ref, acc_ref = refs[:n], refs[n:2 * n], refs[2 * n], refs[2 * n + 1]
        j = pl.program_id(1)

        @pl.when(j == 0)
        def _():
            acc_ref[...] = jnp.zeros_like(acc_ref)

        for d_ref, w_ref in zip(d_refs, w_refs):
            acc_ref[...] += _dot(d_ref[...], w_ref[...], NT)

        @pl.when(j == J - 1)
        def _():
            o_ref[...] = acc_ref[...]

    if stacked:
        d_spec = pl.BlockSpec((None, tm, Wb), lambda i, j: (j, i, 0))
    else:
        d_spec = pl.BlockSpec((tm, Wb), lambda i, j: (i, j))
    w_spec = pl.BlockSpec((None, D, Wb), lambda i, j: (j, 0, 0))
    (dh,), got = _call(
        body, name=name, grid=(S // tm, J),
        in_specs=[d_spec] * n + [w_spec] * n,
        out_specs=[pl.BlockSpec((tm, D), lambda i, j: (i, 0))], out_shape=[jax.ShapeDtypeStruct((S, D), F32)],
        scratch=[pltpu.VMEM((tm, D), F32)], args=(*ds, *ws), comm=comm)
    return dh, got


def grad_lhs(ht, d, stacked, name, comm=None):
    D, S = ht.shape
    if stacked:
        J, _, Wb = d.shape
    else:
        J, Wb = NDEV, d.shape[1] // NDEV
    tk = _tile(S, 2048)
    nk = S // tk

    def body(ht_ref, d_ref, o_ref, acc_ref):
        i = pl.program_id(1)

        @pl.when(i == 0)
        def _():
            acc_ref[...] = jnp.zeros_like(acc_ref)

        acc_ref[...] += _dot(ht_ref[...], d_ref[...])

        @pl.when(i == nk - 1)
        def _():
            o_ref[...] = acc_ref[...].astype(BF16)

    (dw,), got = _call(
        body, name=name, grid=(J, nk),
        in_specs=[pl.BlockSpec((D, tk), lambda j, i: (0, i)), _blk(d, Wb, tk, stacked)],
        out_specs=[pl.BlockSpec((None, D, Wb), lambda j, i: (j, 0, 0))],
        out_shape=[jax.ShapeDtypeStruct((J, D, Wb), BF16)],
        scratch=[pltpu.VMEM((D, Wb), F32)], args=(ht, d), vmem=56, comm=comm)
    return dw, got


def grad_shared_rhs(a, gb, nblk, scale, name):
    S, D = gb.shape
    stacked = nblk is None
    if stacked:
        J, _, Wb = a.shape
    else:
        J, Wb = nblk, a.shape[1] // nblk
    tk = _tile(S, 2048 if Wb <= 768 else 1024)
    nk = S // tk

    def body(a_ref, g_ref, o_ref, acc_ref):
        i = pl.program_id(1)

        @pl.when(i == 0)
        def _():
            acc_ref[...] = jnp.zeros_like(acc_ref)

        acc_ref[...] += _dot(a_ref[...], g_ref[...], TN)

        @pl.when(i == nk - 1)
        def _():
            o_ref[...] = (acc_ref[...] * scale).astype(BF16)

    return pl.pallas_call(
        body, name=name, grid=(J, nk),
        in_specs=[_blk(a, Wb, tk, stacked), pl.BlockSpec((tk, D), lambda j, i: (i, 0))],
        out_specs=pl.BlockSpec((None, Wb, D), lambda j, i: (j, 0, 0)),
        out_shape=jax.ShapeDtypeStruct((J, Wb, D), BF16),
        scratch_shapes=[pltpu.VMEM((Wb, D), F32)], compiler_params=_params(56),
    )(a, gb)


def out_proj(ya, ycp, wo, x, name):
    S, A = ya.shape
    Dm, D = wo.shape
    tm = _tile(S, 512)

    def body(ya_ref, ycp_ref, wo_ref, x_ref, o_ref):
        acc = _dot(ya_ref[...], wo_ref[0:A, :]) + _dot(ycp_ref[...], wo_ref[A:Dm, :])
        o_ref[...] = x_ref[...] + acc

    row = pl.BlockSpec((tm, D), lambda i: (i, 0))
    return pl.pallas_call(
        body, name=name, grid=(S // tm,),
        in_specs=[pl.BlockSpec((tm, A), lambda i: (i, 0)), pl.BlockSpec((tm, Dm - A), lambda i: (i, 0)),
                  pl.BlockSpec((Dm, D), lambda i: (0, 0)), row],
        out_specs=row, out_shape=jax.ShapeDtypeStruct((S, D), F32), compiler_params=_params(48),
    )(ya, ycp, wo, x)


def out_proj_bwd(gb, wo, A, name):
    S, D = gb.shape
    Dm = wo.shape[0]
    tm = _tile(S, 512)

    def body(g_ref, wo_ref, dya_ref, dycp_ref):
        gv = g_ref[...]
        dya_ref[...] = _dot(gv, wo_ref[0:A, :], NT).astype(BF16)
        dycp_ref[...] = _dot(gv, wo_ref[A:Dm, :], NT).astype(BF16)

    return pl.pallas_call(
        body, name=name, grid=(S // tm,),
        in_specs=[pl.BlockSpec((tm, D), lambda i: (i, 0)), pl.BlockSpec((Dm, D), lambda i: (0, 0))],
        out_specs=[pl.BlockSpec((tm, A), lambda i: (i, 0)), pl.BlockSpec((tm, Dm - A), lambda i: (i, 0))],
        out_shape=[jax.ShapeDtypeStruct((S, A), BF16), jax.ShapeDtypeStruct((S, Dm - A), BF16)],
        compiler_params=_params(48),
    )(gb, wo)


def _lane_stat(v, hi):
    lane = lax.broadcasted_iota(jnp.int32, v.shape, 1)
    sel = (lane >= HEAD_DIM) if hi else (lane < HEAD_DIM)
    return jnp.max(jnp.where(sel, v, -jnp.inf), axis=-1, keepdims=True)


def band_fwd(q, k, v, L, name):
    H, S, W = q.shape
    T = min(512, L)
    nb = T // BLK

    def body(q_ref, k_ref, kp_ref, v_ref, vp_ref, o_ref):
        i = pl.program_id(1)
        first_key = jnp.where((i * T) % L != 0, 0, BLK)
        qi = lax.broadcasted_iota(jnp.int32, (BLK, 2 * BLK), 0)
        kj = lax.broadcasted_iota(jnp.int32, (BLK, 2 * BLK), 1)
        band = (kj >= qi) & (kj <= qi + BLK)
        lane = lax.broadcasted_iota(jnp.int32, (BLK, W), 1)
        for b in range(nb):
            rows = slice(b * BLK, (b + 1) * BLK)
            if b == 0:
                kw = jnp.concatenate([kp_ref[...], k_ref[rows, :]], axis=0)
                vw = jnp.concatenate([vp_ref[...], v_ref[rows, :]], axis=0)
                mask = band & (kj >= first_key)
            else:
                kw = k_ref[(b - 1) * BLK:(b + 1) * BLK, :]
                vw = v_ref[(b - 1) * BLK:(b + 1) * BLK, :]
                mask = band
            s = jnp.where(mask, _dot(q_ref[rows, :], kw, NT) * ATTN_SCALE, NEG_INF)
            m = jnp.max(s, axis=-1, keepdims=True)
            p = jnp.exp(s - m)
            l = jnp.sum(p, axis=-1, keepdims=True)
            o = _dot(p.astype(BF16), vw)
            o_ref[rows, :] = jnp.where(lane < HEAD_DIM, o / l, m + jnp.log(l))

    cur = pl.BlockSpec((None, T, W), lambda h, i: (h, i, 0))
    prev = pl.BlockSpec((None, BLK, W), lambda h, i: (h, jnp.maximum(i * nb - 1, 0), 0))
    return pl.pallas_call(
        body, name=name, grid=(H, S // T),
        in_specs=[cur, cur, prev, cur, prev], out_specs=cur,
        out_shape=jax.ShapeDtypeStruct((H, S, W), F32), compiler_params=_params(32),
    )(q, k, k, v, v)


def band_bwd(q, k, v, do, st, L, name, comm=None):
    H, S, W = q.shape
    T = min(512, L)
    nb = T // BLK
    last_blk = S // BLK - 1

    def body(q_ref, qn_ref, do_ref, don_ref, st_ref, stn_ref, k_ref, kp_ref, v_ref, vp_ref,
             dq_ref, dk_ref, dv_ref, dk_acc, dv_acc):
        i = pl.program_id(1)
        first_key = jnp.where((i * T) % L != 0, 0, BLK)
        next_off = jnp.where(((i + 1) * T) % L != 0, 0, 4 * BLK)
        qi = lax.broadcasted_iota(jnp.int32, (BLK, 2 * BLK), 0)
        kj = lax.broadcasted_iota(jnp.int32, (BLK, 2 * BLK), 1)
        band = (kj >= qi) & (kj <= qi + BLK)
        qi1 = lax.broadcasted_iota(jnp.int32, (BLK, BLK), 0)
        kj1 = lax.broadcasted_iota(jnp.int32, (BLK, BLK), 1)
        dk_acc[...] = jnp.zeros_like(dk_acc)
        dv_acc[...] = jnp.zeros_like(dv_acc)
        for b in range(nb + 1):
            rows = slice(b * BLK, (b + 1) * BLK)
            if b < nb:
                qb, dob, stb = q_ref[rows, :], do_ref[rows, :], st_ref[rows, :]
            else:
                qb, dob, stb = qn_ref[...], don_ref[...], stn_ref[...]
            if b == 0:
                kw = jnp.concatenate([kp_ref[...], k_ref[rows, :]], axis=0)
                vw = jnp.concatenate([vp_ref[...], v_ref[rows, :]], axis=0)
                mask = band & (kj >= first_key)
            elif b < nb:
                kw = k_ref[(b - 1) * BLK:(b + 1) * BLK, :]
                vw = v_ref[(b - 1) * BLK:(b + 1) * BLK, :]
                mask = band
            else:
                kw = k_ref[(nb - 1) * BLK:nb * BLK, :]
                vw = v_ref[(nb - 1) * BLK:nb * BLK, :]
                mask = kj1 >= qi1 + next_off
            s = _dot(qb, kw, NT) * ATTN_SCALE
            p = jnp.where(mask, jnp.exp(s - _lane_stat(stb, False)), 0.0)
            dp = _dot(dob, vw, NT)
            ds = (p * (dp - _lane_stat(stb, True))).astype(BF16)
            if b < nb:
                dq_ref[rows, :] = _dot(ds, kw) * ATTN_SCALE
            win = slice(b * BLK, b * BLK + kw.shape[0])
            dk_acc[win, :] += _dot(ds, qb, TN) * ATTN_SCALE
            dv_acc[win, :] += _dot(p.astype(BF16), dob, TN)
        dk_ref[...] = dk_acc[BLK:, :]
        dv_ref[...] = dv_acc[BLK:, :]

    cur = pl.BlockSpec((None, T, W), lambda h, i: (h, i, 0))
    prev = pl.BlockSpec((None, BLK, W), lambda h, i: (h, jnp.maximum(i * nb - 1, 0), 0))
    nxt = pl.BlockSpec((None, BLK, W), lambda h, i: (h, jnp.minimum((i + 1) * nb, last_blk), 0))
    shp = jax.ShapeDtypeStruct((H, S, W), F32)
    return _call(
        body, name=name, grid=(H, S // T),
        in_specs=[cur, nxt, cur, nxt, cur, nxt, cur, prev, cur, prev],
        out_specs=[cur, cur, cur], out_shape=[shp, shp, shp],
        scratch=[pltpu.VMEM((T + BLK, W), F32), pltpu.VMEM((T + BLK, W), F32)],
        args=(q, q, do, do, st, st, k, k, v, v), vmem=32, comm=comm)


def attn_merge(os_, name):
    H, S, W = os_[0].shape
    tm = _tile(S, 1024)

    def body(*refs):
        o_ref = refs[-1]
        vals = [r[...] for r in refs[:-1]]
        lses = [_lane_stat(v, True) for v in vals]
        m = functools.reduce(jnp.maximum, lses)
        ws = [jnp.exp(l - m) for l in lses]
        tot = functools.reduce(jnp.add, ws)
        out = functools.reduce(jnp.add, [(w / tot) * v for w, v in zip(ws, vals)])
        lane = lax.broadcasted_iota(jnp.int32, out.shape, 1)
        o_ref[...] = jnp.where(lane < HEAD_DIM, out, m + jnp.log(tot))

    spec = pl.BlockSpec((None, tm, W), lambda h, i: (h, i, 0))
    return pl.pallas_call(
        body, name=name, grid=(H, S // tm), in_specs=[spec] * len(os_), out_specs=spec,
        out_shape=jax.ShapeDtypeStruct((H, S, W), F32), compiler_params=_params(32),
    )(*os_)


def attn_bwd_stats(merged, do, name):
    H, S, W = merged.shape
    tm = _tile(S, 1024)

    def body(m_ref, do_ref, st_ref):
        mv = m_ref[...]
        delta = jnp.sum(mv * do_ref[...].astype(F32), axis=-1, keepdims=True)
        lane = lax.broadcasted_iota(jnp.int32, mv.shape, 1)
        st_ref[...] = jnp.where(lane < HEAD_DIM, _lane_stat(mv, True), delta)

    spec = pl.BlockSpec((None, tm, W), lambda h, i: (h, i, 0))
    return pl.pallas_call(
        body, name=name, grid=(H, S // tm), in_specs=[spec, spec], out_specs=spec,
        out_shape=jax.ShapeDtypeStruct((H, S, W), F32), compiler_params=_params(32),
    )(merged, do)


def sum_patterns(ts, name):
    H, S, W = ts[0].shape
    tm = _tile(S, 1024)

    def body(*refs):
        refs[-1][...] = functools.reduce(jnp.add, [r[...] for r in refs[:-1]]).astype(BF16)

    spec = pl.BlockSpec((None, tm, W), lambda h, i: (h, i, 0))
    return pl.pallas_call(
        body, name=name, grid=(H, S // tm), in_specs=[spec] * len(ts), out_specs=spec,
        out_shape=jax.ShapeDtypeStruct((H, S, W), BF16), compiler_params=_params(32),
    )(*ts)


def to_heads(t):
    S = t.shape[0]
    t = jnp.transpose(t.reshape(S, -1, HEAD_DIM), (1, 0, 2))
    return jnp.pad(t, ((0, 0), (0, 0), (0, LANES - HEAD_DIM)))


def from_heads(t):
    H, S, _ = t.shape
    return jnp.transpose(t[:, :, :HEAD_DIM], (1, 0, 2)).reshape(S, H * HEAD_DIM)


def permute(t, d):
    if d == 1:
        return t
    H, S, W = t.shape
    return jnp.transpose(t.reshape(H, S // d, d, W), (0, 2, 1, 3)).reshape(H, S, W)


def unpermute(t, d):
    if d == 1:
        return t
    H, S, W = t.shape
    return jnp.transpose(t.reshape(H, d, S // d, W), (0, 2, 1, 3)).reshape(H, S, W)


def _shift_down(e, k):
    return pltpu.roll(e, k, 0)


def _shift_up(e, k):
    return pltpu.roll(e, e.shape[0] - k, 0)


def _causal_sums(e, g):
    for lvl in range(g + 1):
        e = e + _shift_down(e, 1 << lvl)
    return e


def _anticausal_sums(e, g):
    for lvl in range(g + 1):
        e = e + _shift_up(e, 1 << lvl)
    return e


def _mixer_specs(S, tm, A, C, P):
    assert (3 * A) % C == 0 and (3 * A + 3 * C) % P == 0 and tm % HALO == 0
    cb, pb = 3 * A // C, (3 * A + 3 * C) // P
    hb = tm // HALO
    last = S // HALO - 1
    col = lambda w, j: pl.BlockSpec((tm, w), lambda i: (i, j))
    prev = lambda w, j: pl.BlockSpec((HALO, w), lambda i: (jnp.maximum(i * hb - 1, 0), j))
    nxt = lambda w, j: pl.BlockSpec((HALO, w), lambda i: (jnp.minimum((i + 1) * hb, last), j))
    return cb, pb, col, prev, nxt


def _conv_taps(gc, ci, gch, cih, keep_prev):
    u = gc * ci
    e = jnp.concatenate([gch * cih * keep_prev, u], axis=0)
    return u, _shift_down(e, 1)[HALO:], _shift_down(e, 2)[HALO:]


def _pooled(xp, xph, keep_prev, pos, g):
    cols = slice(g * POOL_GROUP, (g + 1) * POOL_GROUP)
    x = xp[:, cols]
    e = jnp.concatenate([xph[:, cols] * keep_prev, x], axis=0)
    cnt = jnp.minimum(pos + 1, POOL_WINDOWS[g]).astype(F32)
    return _causal_sums(e, g)[HALO:] / cnt - x


def convpool_fwd(z, conv_w, pool_w, pool_scale, A, C, P, name):
    S = z.shape[0]
    tm = _tile(S, 512)
    cb, pb, col, prev, _ = _mixer_specs(S, tm, A, C, P)

    def body(gb_ref, gc_ref, ci_ref, xp_ref, gch_ref, cih_ref, xph_ref, cw_ref, pw_ref, ps_ref, y_ref):
        i = pl.program_id(0)
        keep_prev = jnp.where(i == 0, 0.0, 1.0)
        f = lambda r: r[...].astype(F32)
        u, u1, u2 = _conv_taps(f(gc_ref), f(ci_ref), f(gch_ref), f(cih_ref), keep_prev)
        cw = cw_ref[...]
        y_ref[:, 0:C] = (f(gb_ref) * (cw[0:1] * u2 + cw[1:2] * u1 + cw[2:3] * u)).astype(BF16)
        xp, xph = f(xp_ref), f(xph_ref)
        pos = i * tm + lax.broadcasted_iota(jnp.int32, (tm, 1), 0)
        for g in range(len(POOL_WINDOWS)):
            cols = slice(g * POOL_GROUP, (g + 1) * POOL_GROUP)
            lin = _dot(_pooled(xp, xph, keep_prev, pos, g).astype(BF16), pw_ref[g])
            y_ref[:, C + g * POOL_GROUP:C + (g + 1) * POOL_GROUP] = (lin * ps_ref[:, cols]).astype(BF16)

    full = lambda shape: pl.BlockSpec(shape, lambda i: (0,) * len(shape))
    return pl.pallas_call(
        body, name=name, grid=(S // tm,),
        in_specs=[col(C, cb), col(C, cb + 1), col(C, cb + 2), col(P, pb),
                  prev(C, cb + 1), prev(C, cb + 2), prev(P, pb),
                  full(conv_w.shape), full(pool_w.shape), full(pool_scale.shape)],
        out_specs=pl.BlockSpec((tm, C + P), lambda i: (i, 0)),
        out_shape=jax.ShapeDtypeStruct((S, C + P), BF16), compiler_params=_params(48),
    )(z, z, z, z, z, z, z, conv_w, pool_w, pool_scale)


def convpool_bwd(z, dycp, conv_w, pool_w, pool_scale, A, C, P, name):
    S = z.shape[0]
    tm = _tile(S, 512)
    nt = S // tm
    cb, pb, col, prev, nxt = _mixer_specs(S, tm, A, C, P)
    NG = len(POOL_WINDOWS)

    def body(gb_ref, gc_ref, ci_ref, xp_ref, gch_ref, cih_ref, xph_ref, gbn_ref, dy_ref, dyn_ref,
             cw_ref, pw_ref, ps_ref, dz_ref, dcw_ref, dpw_ref, dps_ref):
        i = pl.program_id(0)

        @pl.when(i == 0)
        def _():
            dcw_ref[...] = jnp.zeros_like(dcw_ref)
            dpw_ref[...] = jnp.zeros_like(dpw_ref)
            dps_ref[...] = jnp.zeros_like(dps_ref)

        keep_prev = jnp.where(i == 0, 0.0, 1.0)
        keep_next = jnp.where(i == nt - 1, 0.0, 1.0)
        f = lambda r: r[...].astype(F32)
        gb, gc, ci = f(gb_ref), f(gc_ref), f(ci_ref)
        u, u1, u2 = _conv_taps(gc, ci, f(gch_ref), f(cih_ref), keep_prev)
        cw = cw_ref[...]
        dy, dyn = f(dy_ref), f(dyn_ref) * keep_next
        dyc = dy[:, 0:C]
        dz_ref[:, 0:C] = (dyc * (cw[0:1] * u2 + cw[1:2] * u1 + cw[2:3] * u)).astype(BF16)
        dc = dyc * gb
        e = jnp.concatenate([dc, dyn[:, 0:C] * f(gbn_ref)], axis=0)
        du = cw[2:3] * dc + cw[1:2] * _shift_up(e, 1)[:tm] + cw[0:1] * _shift_up(e, 2)[:tm]
        dz_ref[:, C:2 * C] = (du * ci).astype(BF16)
        dz_ref[:, 2 * C:3 * C] = (du * gc).astype(BF16)
        dcw_ref[0:1, :] += jnp.sum(dc * u2, axis=0, keepdims=True)
        dcw_ref[1:2, :] += jnp.sum(dc * u1, axis=0, keepdims=True)
        dcw_ref[2:3, :] += jnp.sum(dc * u, axis=0, keepdims=True)

        xp, xph = f(xp_ref), f(xph_ref)
        pos = i * tm + lax.broadcasted_iota(jnp.int32, (tm, 1), 0)
        pos_e = i * tm + lax.broadcasted_iota(jnp.int32, (tm + HALO, 1), 0)
        for g in range(NG):
            cols = slice(g * POOL_GROUP, (g + 1) * POOL_GROUP)
            ycols = slice(C + g * POOL_GROUP, C + (g + 1) * POOL_GROUP)
            pooled = _pooled(xp, xph, keep_prev, pos, g).astype(BF16)
            pw = pw_ref[g]
            dyp = dy[:, ycols]
            dps_ref[:, cols] += jnp.sum(dyp * _dot(pooled, pw), axis=0, keepdims=True)
            dlin = (jnp.concatenate([dyp, dyn[:, ycols]], axis=0) * ps_ref[:, cols]).astype(BF16)
            dpw_ref[g] += _dot(pooled, dlin[:tm], TN)
            dpool = _dot(dlin, pw, NT)
            r = dpool / jnp.minimum(pos_e + 1, POOL_WINDOWS[g]).astype(F32)
            dz_ref[:, 3 * C + g * POOL_GROUP:3 * C + (g + 1) * POOL_GROUP] = (
                _anticausal_sums(r, g)[:tm] - dpool[:tm]).astype(BF16)

    full = lambda shape: pl.BlockSpec(shape, lambda i: (0,) * len(shape))
    return pl.pallas_call(
        body, name=name, grid=(nt,),
        in_specs=[col(C, cb), col(C, cb + 1), col(C, cb + 2), col(P, pb),
                  prev(C, cb + 1), prev(C, cb + 2), prev(P, pb), nxt(C, cb),
                  pl.BlockSpec((tm, C + P), lambda i: (i, 0)),
                  pl.BlockSpec((HALO, C + P), lambda i: (jnp.minimum((i + 1) * (tm // HALO), S // HALO - 1), 0)),
                  full(conv_w.shape), full(pool_w.shape), full(pool_scale.shape)],
        out_specs=[pl.BlockSpec((tm, 3 * C + P), lambda i: (i, 0)),
                   full(conv_w.shape), full(pool_w.shape), full(pool_scale.shape)],
        out_shape=[jax.ShapeDtypeStruct((S, 3 * C + P), BF16), jax.ShapeDtypeStruct(conv_w.shape, F32),
                   jax.ShapeDtypeStruct(pool_w.shape, F32), jax.ShapeDtypeStruct(pool_scale.shape, F32)],
        compiler_params=_params(48),
    )(z, z, z, z, z, z, z, z, dycp, dycp, conv_w, pool_w, pool_scale)


def adamw(lands, w, m, v, name):
    R, C = w.shape
    nl = len(lands)
    n, Rl, _ = lands[0].shape
    assert Rl * nl == R
    tr = _tile(Rl, 128 * 1024 // C)
    nb = Rl // tr
    c1 = 1.0 - ADAM_B1 ** ADAM_STEP
    c2 = 1.0 - ADAM_B2 ** ADAM_STEP

    def body(*refs):
        land_refs = refs[:nl]
        w_ref, m_ref, v_ref, g_ref, d_ref, mo_ref, vo_ref = refs[nl:]
        i = pl.program_id(0)
        for a, land_ref in enumerate(land_refs):
            @pl.when((i >= a * nb) & (i < (a + 1) * nb))
            def _():
                g = land_ref[0].astype(F32)
                for s in range(1, n):
                    g = g + land_ref[s].astype(F32)
                g_ref[...] = g
                mn = ADAM_B1 * m_ref[...] + (1.0 - ADAM_B1) * g
                vn = ADAM_B2 * v_ref[...] + (1.0 - ADAM_B2) * (g * g)
                mo_ref[...] = mn
                vo_ref[...] = vn
                d_ref[...] = -ADAM_LR * ((mn / c1) / (jnp.sqrt(vn / c2) + ADAM_EPS) + ADAM_WD * w_ref[...])

    land_specs = [pl.BlockSpec((n, tr, C), lambda i, a=a: (0, jnp.clip(i - a * nb, 0, nb - 1), 0))
                  for a in range(nl)]
    row = pl.BlockSpec((tr, C), lambda i: (i, 0))
    shp = jax.ShapeDtypeStruct((R, C), F32)
    return pl.pallas_call(
        body, name=name, grid=(nl * nb,),
        in_specs=land_specs + [row, row, row], out_specs=[row] * 4, out_shape=[shp] * 4,
        compiler_params=_params(48),
    )(*lands, w, m, v)


def _gather_comm(arrs):
    return ("gather", arrs) if arrs else None


def _ffn_fwd(x, gain, wg, wu, wd, tag, carry_up=(), carry_down=()):
    h, ht = rms_fwd(x, gain, f"rms_{tag}")
    (a, b, hid), got_up = ffn_up(h, wg, wu, f"ffn_up_{tag}", _gather_comm(list(carry_up)))
    xo, got_down = ffn_down(hid, wd, x, f"ffn_down_{tag}", _gather_comm(list(carry_down)))
    return xo, (x, ht, a, b, hid), got_up, got_down


def _ffn_bwd(g, gb, saved, gain, wg, wu, wd, tag):
    x, ht, a, b, hid = saved
    da, db = ffn_dhid(gb, wd, a, b, f"ffn_dhid_{tag}")
    dwd = grad_shared_rhs(hid, gb, None, FFN_RESIDUAL, f"ffn_dwd_{tag}")
    dwg, (land_wd,) = grad_lhs(ht, da, True, f"ffn_dwg_{tag}", ("exchange", [dwd]))
    dwu, (land_wg,) = grad_lhs(ht, db, True, f"ffn_dwu_{tag}", ("exchange", [dwg]))
    dh, (land_wu,) = back_proj([da, db], [wg, wu], True, f"ffn_dh_{tag}", ("exchange", [dwu]))
    g, gb, dgain = rms_bwd(x, gain, dh, g, f"rms_bwd_{tag}")
    return g, gb, dgain, (land_wg, land_wu, land_wd)


def _mixer_fwd(x, gain, w_in, conv_w, pool_w, pool_scale, w_out, dims, tag):
    A, C, P = dims
    h, ht = rms_fwd(x, gain, f"rms_{tag}")
    z = proj_cols(h, w_in, f"in_proj_{tag}")
    qkv = [to_heads(z[:, n * A:(n + 1) * A]) for n in range(3)]
    S = x.shape[0]
    perms, outs = [], []
    for d in DILATIONS:
        qd, kd, vd = [permute(t, d) for t in qkv]
        perms.append((qd, kd, vd))
        outs.append(unpermute(band_fwd(qd, kd, vd, S // d, f"band_fwd_{tag}_d{d}"), d))
    merged = attn_merge(outs, f"attn_merge_{tag}")
    ya = from_heads(merged).astype(BF16)
    ycp = convpool_fwd(z, conv_w, pool_w, pool_scale, A, C, P, f"convpool_{tag}")
    xo = out_proj(ya, ycp, w_out, x, f"out_proj_{tag}")
    return xo, (x, ht, z, perms, merged, ya, ycp)


def _mixer_bwd(g, gb, saved, gain, w_in, conv_w, pool_w, pool_scale, w_out, dims, tag):
    A, C, P = dims
    x, ht, z, perms, merged, ya, ycp = saved
    S, D = x.shape
    dya, dycp = out_proj_bwd(gb, w_out, A, f"out_proj_bwd_{tag}")
    dwo_a = grad_shared_rhs(ya, gb, 1, 1.0, f"dwo_attn_{tag}")
    dwo_cp = grad_shared_rhs(ycp, gb, 1, 1.0, f"dwo_cp_{tag}")
    dwo = jnp.concatenate([dwo_a[0], dwo_cp[0]], axis=0).reshape(NDEV, -1, D)
    do = to_heads(dya)
    st = attn_bwd_stats(merged, do, f"attn_stats_{tag}")
    grads, land_wo = [], None
    for d, (qd, kd, vd) in zip(DILATIONS, perms):
        parts, got = band_bwd(qd, kd, vd, permute(do, d), permute(st, d), S // d, f"band_bwd_{tag}_d{d}",
                              ("exchange", [dwo]) if land_wo is None else None)
        land_wo = got[0] if got else land_wo
        grads.append([unpermute(t, d) for t in parts])
    dqkv = [from_heads(sum_patterns([gr[n] for gr in grads], f"attn_sum_{tag}_{n}")) for n in range(3)]
    dz_cp, dcw, dpw, dps = convpool_bwd(z, dycp, conv_w, pool_w, pool_scale, A, C, P, f"convpool_bwd_{tag}")
    dz = jnp.concatenate(dqkv + [dz_cp], axis=1)
    dwin, _ = grad_lhs(ht, dz, False, f"dwin_{tag}")
    dh, (land_win,) = back_proj([dz], [w_in], False, f"in_proj_bwd_{tag}", ("exchange", [dwin]))
    g, gb, dgain = rms_bwd(x, gain, dh, g, f"rms_bwd_{tag}")
    return g, gb, (dgain, dcw, dpw, dps), (land_win, land_wo)


def _pack(arrs):
    flat = [a.reshape(-1).astype(F32) for a in arrs]
    spans, off = [], 0
    for a in flat:
        spans.append((off, a.shape[0]))
        off += a.shape[0]
    rows = -(-off // (8 * LANES)) * 8
    buf = jnp.concatenate(flat + [jnp.zeros((rows * LANES - off,), F32)]).reshape(rows, LANES)
    return buf, spans


def kernel(x, ffn1_norm, ffn1_w_gate, ffn1_w_up, ffn1_w_down, mix_norm, w_in, conv_w, pool_w, pool_scale, w_out, ffn2_norm, ffn2_w_gate, ffn2_w_up, ffn2_w_down, final_norm, loss_target, m_ffn1_norm, m_ffn1_w_gate, m_ffn1_w_up, m_ffn1_w_down, m_mix_norm, m_w_in, m_conv_w, m_pool_w, m_pool_scale, m_w_out, m_ffn2_norm, m_ffn2_w_gate, m_ffn2_w_up, m_ffn2_w_down, m_final_norm, v_ffn1_norm, v_ffn1_w_gate, v_ffn1_w_up, v_ffn1_w_down, v_mix_norm, v_w_in, v_conv_w, v_pool_w, v_pool_scale, v_w_out, v_ffn2_norm, v_ffn2_w_gate, v_ffn2_w_up, v_ffn2_w_down, v_final_norm):
    depth = ffn1_norm.shape[0]
    S, D = x.shape[1], x.shape[2]
    Cb = conv_w.shape[2]
    C = Cb * NDEV
    P = pool_scale.shape[1]
    A = (w_in.shape[2] * NDEV - 3 * C - P) // 3
    dims = (A, C, P)
    me = 4 * lax.axis_index("x") + 2 * lax.axis_index("y") + lax.axis_index("c")

    big = dict(ffn1_w_gate=ffn1_w_gate, ffn1_w_up=ffn1_w_up, ffn1_w_down=ffn1_w_down, w_in=w_in, w_out=w_out,
               ffn2_w_gate=ffn2_w_gate, ffn2_w_up=ffn2_w_up, ffn2_w_down=ffn2_w_down)
    big_m = dict(ffn1_w_gate=m_ffn1_w_gate, ffn1_w_up=m_ffn1_w_up, ffn1_w_down=m_ffn1_w_down, w_in=m_w_in,
                 w_out=m_w_out, ffn2_w_gate=m_ffn2_w_gate, ffn2_w_up=m_ffn2_w_up, ffn2_w_down=m_ffn2_w_down)
    big_v = dict(ffn1_w_gate=v_ffn1_w_gate, ffn1_w_up=v_ffn1_w_up, ffn1_w_down=v_ffn1_w_down, w_in=v_w_in,
                 w_out=v_w_out, ffn2_w_gate=v_ffn2_w_gate, ffn2_w_up=v_ffn2_w_up, ffn2_w_down=v_ffn2_w_down)
    names = list(big)

    ffn1_names = ["ffn1_w_gate", "ffn1_w_up", "ffn1_w_down"]
    mix_names = ["w_in", "w_out", "ffn2_w_gate"]
    rest_names = ["ffn2_w_up", "ffn2_w_down"]
    shards = lambda l, ns: [big[n][l].astype(BF16) for n in ns]
    full = [dict() for _ in range(depth)]
    full[0].update(zip(ffn1_names, all_gather(shards(0, ffn1_names), "gather_ffn1_l0")))
    conv_full = all_gather([jnp.pad(conv_w.reshape(-1, Cb), ((0, 2), (0, LANES - Cb)))], "gather_conv_w")[0]
    conv_full = jnp.transpose(conv_full[:, :depth * 3, :Cb].reshape(NDEV, depth, 3, Cb), (1, 2, 0, 3)).reshape(depth, 3, C)
    pool_w_bf = pool_w.astype(BF16)

    xs = x[0]
    saved = []
    for l in range(depth):
        W = full[l]
        up1, down1 = (mix_names, rest_names) if l == 0 else (rest_names, [])
        xs, s1, got_up, got_down = _ffn_fwd(xs, ffn1_norm[l:l + 1], W["ffn1_w_gate"], W["ffn1_w_up"],
                                            W["ffn1_w_down"], f"f1_l{l}", shards(l, up1), shards(l, down1))
        W.update(zip(up1, got_up))
        W.update(zip(down1, got_down))
        xs, s2 = _mixer_fwd(xs, mix_norm[l:l + 1], W["w_in"], conv_full[l], pool_w_bf[l], pool_scale[l:l + 1],
                            W["w_out"].reshape(-1, D), dims, f"mix_l{l}")
        up2, down2 = (ffn1_names, mix_names) if l + 1 < depth else ([], [])
        xs, s3, got_up, got_down = _ffn_fwd(xs, ffn2_norm[l:l + 1], W["ffn2_w_gate"], W["ffn2_w_up"],
                                            W["ffn2_w_down"], f"f2_l{l}", shards(l + 1, up2) if up2 else (),
                                            shards(l + 1, down2) if down2 else ())
        if l + 1 < depth:
            full[l + 1].update(zip(up2, got_up))
            full[l + 1].update(zip(down2, got_down))
        saved.append((s1, s2, s3))
    loss_part, g, gb, d_final = final_loss(xs, final_norm.reshape(1, D), loss_target[0], "final_loss")

    small = [None] * depth
    lands = [None] * depth
    for l in reversed(range(depth)):
        W = full[l]
        s1, s2, s3 = saved[l]
        g, gb, dn2, (lg2, lu2, ld2) = _ffn_bwd(g, gb, s3, ffn2_norm[l:l + 1], W["ffn2_w_gate"], W["ffn2_w_up"],
                                               W["ffn2_w_down"], f"f2_l{l}")
        g, gb, (dnm, dcw, dpw, dps), (lwin, lwo) = _mixer_bwd(
            g, gb, s2, mix_norm[l:l + 1], W["w_in"], conv_full[l], pool_w_bf[l], pool_scale[l:l + 1],
            W["w_out"].reshape(-1, D), dims, f"mix_l{l}")
        g, gb, dn1, (lg1, lu1, ld1) = _ffn_bwd(g, gb, s1, ffn1_norm[l:l + 1], W["ffn1_w_gate"], W["ffn1_w_up"],
                                               W["ffn1_w_down"], f"f1_l{l}")
        lands[l] = dict(ffn1_w_gate=lg1, ffn1_w_up=lu1, ffn1_w_down=ld1, w_in=lwin, w_out=lwo,
                        ffn2_w_gate=lg2, ffn2_w_up=lu2, ffn2_w_down=ld2)
        small[l] = (dn1, dnm, dcw, dpw, dps, dn2)

    res = {}
    for n in names:
        shp = big[n].shape
        two = lambda t: t.reshape(-1, shp[-1])
        outs = adamw([lands[l][n] for l in range(depth)], two(big[n]), two(big_m[n]), two(big_v[n]), f"adamw_{n}")
        res[n] = [o.reshape(shp) for o in outs]

    st = lambda i: jnp.stack([small[l][i] for l in range(depth)])
    small_g = dict(ffn1_norm=st(0), mix_norm=st(1), conv_w=st(2), pool_w=st(3), pool_scale=st(4), ffn2_norm=st(5),
                   final_norm=d_final)
    small_names = list(small_g)
    zeros_conv = jnp.zeros((depth, 3, C), F32)
    place = lambda t: lax.dynamic_update_slice(zeros_conv, t, (0, 0, me * Cb))
    small_w = dict(ffn1_norm=ffn1_norm, mix_norm=mix_norm, conv_w=place(conv_w), pool_w=pool_w, pool_scale=pool_scale,
                   ffn2_norm=ffn2_norm, final_norm=final_norm)
    small_m = dict(ffn1_norm=m_ffn1_norm, mix_norm=m_mix_norm, conv_w=place(m_conv_w), pool_w=m_pool_w,
                   pool_scale=m_pool_scale, ffn2_norm=m_ffn2_norm, final_norm=m_final_norm)
    small_v = dict(ffn1_norm=v_ffn1_norm, mix_norm=v_mix_norm, conv_w=place(v_conv_w), pool_w=v_pool_w,
                   pool_scale=v_pool_scale, ffn2_norm=v_ffn2_norm, final_norm=v_final_norm)
    gbuf, spans = _pack([small_g[n] for n in small_names] + [loss_part])
    wbuf, _ = _pack([small_w[n] for n in small_names] + [jnp.zeros((1, LANES), F32)])
    mbuf, _ = _pack([small_m[n] for n in small_names] + [jnp.zeros((1, LANES), F32)])
    vbuf, _ = _pack([small_v[n] for n in small_names] + [jnp.zeros((1, LANES), F32)])
    gathered = all_gather([gbuf], "gather_small_grads")[0]
    outs = adamw([gathered], wbuf, mbuf, vbuf, "adamw_small")
    for n, (off, size) in zip(small_names, spans):
        shp = small_w[n].shape
        vals = [o.reshape(-1)[off:off + size].reshape(shp) for o in outs]
        if n == "conv_w":
            vals = [lax.dynamic_slice(t, (0, 0, me * Cb), (depth, 3, Cb)) for t in vals]
        res[n] = vals
    loss = outs[0].reshape(-1)[spans[-1][0]]

    order = ["ffn1_norm", "ffn1_w_gate", "ffn1_w_up", "ffn1_w_down", "mix_norm", "w_in", "conv_w", "pool_w",
             "pool_scale", "w_out", "ffn2_norm", "ffn2_w_gate", "ffn2_w_up", "ffn2_w_down", "final_norm"]
    return (loss, g[None], *[res[n][0] for n in order], *[res[n][1] for n in order],
            *[res[n][2] for n in order], *[res[n][3] for n in order])
```

```python
import functools

import jax
import jax.numpy as jnp
from jax import lax
from jax.experimental import pallas as pl
from jax.experimental.pallas import tpu as pltpu

F32 = jnp.float32
BF16 = jnp.bfloat16
NDEV = 8
HEAD_DIM = 64
LANES = 128
BLK = 128
DILATIONS = (1, 4, 16)
POOL_WINDOWS = (2, 4, 8, 16)
POOL_GROUP = 128
HALO = 16
FFN_RESIDUAL = 0.5
RMS_EPS = 1e-6
NEG_INF = -1e30
ATTN_SCALE = HEAD_DIM ** -0.5
ADAM_LR, ADAM_B1, ADAM_B2, ADAM_EPS, ADAM_WD, ADAM_STEP = 0.001, 0.9, 0.999, 1e-08, 0.01, 10
VMEM_BYTES = 64 * 1024 * 1024
MESH = pl.DeviceIdType.MESH

NN = (((1,), (0,)), ((), ()))
NT = (((1,), (1,)), ((), ()))
TN = (((0,), (0,)), ((), ()))


def _dot(a, b, dims=NN):
    return lax.dot_general(a, b, dims, preferred_element_type=F32)


def _params(vmem_mb=48):
    return pltpu.CompilerParams(vmem_limit_bytes=min(vmem_mb * 1024 * 1024, VMEM_BYTES - 4 * 1024 * 1024))


def _tile(n, want):
    t = min(n, max(16, want // 16 * 16))
    while t > 16 and (n % t or t % 16):
        t -= 16
    return t if n % t == 0 else n


def _coords():
    return lax.axis_index("x"), lax.axis_index("y"), lax.axis_index("c")


def _comm_sems(n):
    return [pltpu.SemaphoreType.DMA((7 * n,)), pltpu.SemaphoreType.DMA((7 * n,)), pltpu.SemaphoreType.DMA((n,))]


def _gather_ops(ins, outs, sems):
    n = len(ins)
    send_sems, recv_sems, local_sems = sems
    x, y, c = _coords()
    me, sibling = (x, y, c), (x, y, 1 - c)
    chips = [(1 - x, y), (x, 1 - y), (1 - x, 1 - y)]

    def slot(out, p):
        return out.at[4 * p[0] + 2 * p[1] + p[2]]

    def copy(a, k, block, to, src=None):
        return pltpu.make_async_remote_copy(
            src_ref=slot(outs[a], block) if src is None else src, dst_ref=slot(outs[a], block),
            send_sem=send_sems.at[7 * a + k], recv_sem=recv_sems.at[7 * a + k],
            device_id=to, device_id_type=MESH)

    def own():
        mine = [pltpu.make_async_copy(ins[a], slot(outs[a], me), local_sems.at[a]) for a in range(n)]
        first = []
        for a in range(n):
            first.append(copy(a, 0, me, sibling, src=ins[a]))
            first += [copy(a, 1 + j, me, (*chip, c), src=ins[a]) for j, chip in enumerate(chips)]
        return mine, first

    def start():
        mine, first = own()
        for cp in mine + first:
            cp.start()

    def finish():
        mine, first = own()
        passed = []
        for j, chip in enumerate(chips):
            for a in range(n):
                copy(a, 1 + j, (*chip, c), me).wait_recv()
                fwd = copy(a, 4 + j, (*chip, c), sibling)
                fwd.start()
                passed.append(fwd)
        for a in range(n):
            copy(a, 0, sibling, me).wait_recv()
            for j, chip in enumerate(chips):
                copy(a, 4 + j, (*chip, 1 - c), me).wait_recv()
        for cp in first + passed:
            cp.wait_send()
        for cp in mine:
            cp.wait()

    return start, finish


def _exchange_ops(ins, outs, sems):
    n = len(ins)
    send_sems, recv_sems, local_sems = sems
    x, y, c = _coords()
    me = 4 * x + 2 * y + c

    def peer(k):
        return (1 - x if k & 4 else x, 1 - y if k & 2 else y, 1 - c if k & 1 else c)

    def copy(a, k):
        p = peer(k)
        return pltpu.make_async_remote_copy(
            src_ref=ins[a].at[4 * p[0] + 2 * p[1] + p[2]], dst_ref=outs[a].at[me],
            send_sem=send_sems.at[7 * a + k - 1], recv_sem=recv_sems.at[7 * a + k - 1],
            device_id=p, device_id_type=MESH)

    def landed(a, k):
        p = peer(k)
        return pltpu.make_async_remote_copy(
            src_ref=ins[a].at[me], dst_ref=outs[a].at[4 * p[0] + 2 * p[1] + p[2]],
            send_sem=send_sems.at[7 * a + k - 1], recv_sem=recv_sems.at[7 * a + k - 1],
            device_id=p, device_id_type=MESH)

    def own():
        mine = [pltpu.make_async_copy(ins[a].at[me], outs[a].at[me], local_sems.at[a]) for a in range(n)]
        return mine, [copy(a, k) for a in range(n) for k in range(1, NDEV)]

    def start():
        mine, sent = own()
        for cp in mine + sent:
            cp.start()

    def finish():
        mine, sent = own()
        for a in range(n):
            for k in range(1, NDEV):
                landed(a, k).wait_recv()
        for cp in sent:
            cp.wait_send()
        for cp in mine:
            cp.wait()

    return start, finish


_COMM = {"gather": (_gather_ops, lambda a: (NDEV,) + a.shape), "exchange": (_exchange_ops, lambda a: a.shape)}


def all_gather(arrs, name):
    n = len(arrs)

    def body(*refs):
        start, finish = _gather_ops(refs[:n], refs[n:2 * n], refs[2 * n:])
        start()
        finish()

    any_spec = pl.BlockSpec(memory_space=pl.ANY)
    return pl.pallas_call(
        body, name=name, out_shape=[jax.ShapeDtypeStruct((NDEV,) + a.shape, a.dtype) for a in arrs],
        in_specs=[any_spec] * n, out_specs=[any_spec] * n, scratch_shapes=_comm_sems(n),
    )(*arrs)


def _call(body, *, name, grid, in_specs, out_specs, out_shape, args, scratch=(), vmem=48, comm=None):
    if comm is None:
        outs = pl.pallas_call(
            body, name=name, grid=grid, in_specs=list(in_specs), out_specs=list(out_specs), out_shape=list(out_shape),
            scratch_shapes=list(scratch), compiler_params=_params(vmem))(*args)
        return list(outs), []
    kind, arrs = comm
    ops, shape_of = _COMM[kind]
    n, n_in, n_out, n_scr = len(arrs), len(in_specs), len(out_specs), len(scratch)

    def carrier(*refs):
        ins, c_in = refs[:n_in], refs[n_in:n_in + n]
        o0 = n_in + n
        outs, c_out = refs[o0:o0 + n_out], refs[o0 + n_out:o0 + n_out + n]
        s0 = o0 + n_out + n
        scr, sems = refs[s0:s0 + n_scr], refs[s0 + n_scr:]
        ids = [pl.program_id(d) for d in range(len(grid))]
        first = functools.reduce(jnp.logical_and, [i == 0 for i in ids])
        last = functools.reduce(jnp.logical_and, [i == g - 1 for i, g in zip(ids, grid)])
        start, finish = ops(c_in, c_out, sems)
        pl.when(first)(start)
        body(*ins, *outs, *scr)
        pl.when(last)(finish)

    any_spec = pl.BlockSpec(memory_space=pl.ANY)
    res = pl.pallas_call(
        carrier, name=name, grid=grid, in_specs=list(in_specs) + [any_spec] * n,
        out_specs=list(out_specs) + [any_spec] * n,
        out_shape=list(out_shape) + [jax.ShapeDtypeStruct(shape_of(a), a.dtype) for a in arrs],
        scratch_shapes=list(scratch) + _comm_sems(n), compiler_params=_params(vmem))(*args, *arrs)
    return list(res[:n_out]), list(res[n_out:])


def _rstd(x):
    return lax.rsqrt(jnp.mean(x * x, axis=-1, keepdims=True) + RMS_EPS)


def rms_fwd(x, gain, name):
    S, D = x.shape
    tm = _tile(S, 512)

    def body(x_ref, g_ref, h_ref, ht_ref):
        xv = x_ref[...]
        h = xv * _rstd(xv) * g_ref[...]
        h_ref[...] = h.astype(BF16)
        ht_ref[...] = h.T.astype(BF16)

    return pl.pallas_call(
        body, name=name, grid=(S // tm,),
        in_specs=[pl.BlockSpec((tm, D), lambda i: (i, 0)), pl.BlockSpec((1, D), lambda i: (0, 0))],
        out_specs=[pl.BlockSpec((tm, D), lambda i: (i, 0)), pl.BlockSpec((D, tm), lambda i: (0, i))],
        out_shape=[jax.ShapeDtypeStruct((S, D), BF16), jax.ShapeDtypeStruct((D, S), BF16)],
        compiler_params=_params(40),
    )(x, gain)


def _rms_bwd_math(xv, gain, dh):
    r = _rstd(xv)
    xhat = xv * r
    dxhat = dh * gain
    dx = r * (dxhat - xhat * jnp.mean(dxhat * xhat, axis=-1, keepdims=True))
    return dx, dh * xhat


def rms_bwd(x, gain, dh, g, name):
    S, D = x.shape
    tm = _tile(S, 256)

    def body(x_ref, gain_ref, dh_ref, g_ref, go_ref, gb_ref, dg_ref):
        @pl.when(pl.program_id(0) == 0)
        def _():
            dg_ref[...] = jnp.zeros_like(dg_ref)

        dx, dgain = _rms_bwd_math(x_ref[...], gain_ref[...], dh_ref[...])
        gn = g_ref[...] + dx
        go_ref[...] = gn
        gb_ref[...] = gn.astype(BF16)
        dg_ref[...] += jnp.sum(dgain, axis=0, keepdims=True)

    row = pl.BlockSpec((tm, D), lambda i: (i, 0))
    vec = pl.BlockSpec((1, D), lambda i: (0, 0))
    return pl.pallas_call(
        body, name=name, grid=(S // tm,),
        in_specs=[row, vec, row, row], out_specs=[row, row, vec],
        out_shape=[jax.ShapeDtypeStruct((S, D), F32), jax.ShapeDtypeStruct((S, D), BF16),
                   jax.ShapeDtypeStruct((1, D), F32)],
        compiler_params=_params(48),
    )(x, gain, dh, g)


def final_loss(x, gain, target, name):
    S, D = x.shape
    tm = _tile(S, 256)

    def body(x_ref, gain_ref, t_ref, loss_ref, go_ref, gb_ref, dg_ref):
        @pl.when(pl.program_id(0) == 0)
        def _():
            dg_ref[...] = jnp.zeros_like(dg_ref)
            loss_ref[...] = jnp.zeros_like(loss_ref)

        xv, gain_v = x_ref[...], gain_ref[...]
        err = xv * _rstd(xv) * gain_v - t_ref[...]
        loss_ref[...] += jnp.sum(jnp.sum(err * err, axis=-1, keepdims=True), axis=0, keepdims=True) * (0.5 / D)
        dx, dgain = _rms_bwd_math(xv, gain_v, err * (1.0 / D))
        go_ref[...] = dx
        gb_ref[...] = dx.astype(BF16)
        dg_ref[...] += jnp.sum(dgain, axis=0, keepdims=True)

    row = pl.BlockSpec((tm, D), lambda i: (i, 0))
    vec = pl.BlockSpec((1, D), lambda i: (0, 0))
    return pl.pallas_call(
        body, name=name, grid=(S // tm,),
        in_specs=[row, vec, row],
        out_specs=[pl.BlockSpec((1, LANES), lambda i: (0, 0)), row, row, vec],
        out_shape=[jax.ShapeDtypeStruct((1, LANES), F32), jax.ShapeDtypeStruct((S, D), F32),
                   jax.ShapeDtypeStruct((S, D), BF16), jax.ShapeDtypeStruct((1, D), F32)],
        compiler_params=_params(48),
    )(x, gain, target)


def _blk(arr, width, tm, stacked):
    if stacked:
        return pl.BlockSpec((None, tm, width), lambda j, i: (j, i, 0))
    return pl.BlockSpec((tm, width), lambda j, i: (i, j))


def ffn_up(h, wg, wu, name, comm=None):
    S, D = h.shape
    J, _, Fb = wg.shape
    tm = _tile(S, 1024)

    def body(h_ref, wg_ref, wu_ref, a_ref, b_ref, hid_ref):
        hv = h_ref[...]
        a = _dot(hv, wg_ref[...])
        b = _dot(hv, wu_ref[...])
        a_ref[...] = a.astype(BF16)
        b_ref[...] = b.astype(BF16)
        hid_ref[...] = (a * jax.nn.sigmoid(a) * b).astype(BF16)

    w_spec = pl.BlockSpec((None, D, Fb), lambda j, i: (j, 0, 0))
    o_spec = pl.BlockSpec((None, tm, Fb), lambda j, i: (j, i, 0))
    shp = jax.ShapeDtypeStruct((J, S, Fb), BF16)
    return _call(
        body, name=name, grid=(J, S // tm),
        in_specs=[pl.BlockSpec((tm, D), lambda j, i: (i, 0)), w_spec, w_spec],
        out_specs=[o_spec, o_spec, o_spec], out_shape=[shp, shp, shp], args=(h, wg, wu), comm=comm)


def proj_cols(h, w, name):
    S, D = h.shape
    J, _, Wb = w.shape
    tm = _tile(S, 1024)

    def body(h_ref, w_ref, z_ref):
        z_ref[...] = _dot(h_ref[...], w_ref[...]).astype(BF16)

    return pl.pallas_call(
        body, name=name, grid=(J, S // tm),
        in_specs=[pl.BlockSpec((tm, D), lambda j, i: (i, 0)), pl.BlockSpec((None, D, Wb), lambda j, i: (j, 0, 0))],
        out_specs=pl.BlockSpec((tm, Wb), lambda j, i: (i, j)),
        out_shape=jax.ShapeDtypeStruct((S, J * Wb), BF16), compiler_params=_params(48),
    )(h, w)


def ffn_down(hid, wd, x, name, comm=None):
    J, S, Fb = hid.shape
    D = wd.shape[2]
    tm = _tile(S, 512)

    def body(hid_ref, wd_ref, x_ref, o_ref, acc_ref):
        j = pl.program_id(1)

        @pl.when(j == 0)
        def _():
            acc_ref[...] = jnp.zeros_like(acc_ref)

        acc_ref[...] += _dot(hid_ref[...], wd_ref[...])

        @pl.when(j == J - 1)
        def _():
            o_ref[...] = x_ref[...] + FFN_RESIDUAL * acc_ref[...]

    row = pl.BlockSpec((tm, D), lambda i, j: (i, 0))
    (xo,), got = _call(
        body, name=name, grid=(S // tm, J),
        in_specs=[pl.BlockSpec((None, tm, Fb), lambda i, j: (j, i, 0)),
                  pl.BlockSpec((None, Fb, D), lambda i, j: (j, 0, 0)), row],
        out_specs=[row], out_shape=[jax.ShapeDtypeStruct((S, D), F32)],
        scratch=[pltpu.VMEM((tm, D), F32)], args=(hid, wd, x), comm=comm)
    return xo, got


def ffn_dhid(gb, wd, a, b, name):
    S, D = gb.shape
    J, Fb, _ = wd.shape
    tm = _tile(S, 1024)

    def body(g_ref, wd_ref, a_ref, b_ref, da_ref, db_ref):
        dhid = _dot(g_ref[...], wd_ref[...], NT) * FFN_RESIDUAL
        a = a_ref[...].astype(F32)
        b = b_ref[...].astype(F32)
        sg = jax.nn.sigmoid(a)
        da_ref[...] = (dhid * b * (sg * (1.0 + a * (1.0 - sg)))).astype(BF16)
        db_ref[...] = (dhid * (a * sg)).astype(BF16)

    o_spec = pl.BlockSpec((None, tm, Fb), lambda j, i: (j, i, 0))
    shp = jax.ShapeDtypeStruct((J, S, Fb), BF16)
    return pl.pallas_call(
        body, name=name, grid=(J, S // tm),
        in_specs=[pl.BlockSpec((tm, D), lambda j, i: (i, 0)), pl.BlockSpec((None, Fb, D), lambda j, i: (j, 0, 0)),
                  o_spec, o_spec],
        out_specs=[o_spec, o_spec], out_shape=[shp, shp], compiler_params=_params(48),
    )(gb, wd, a, b)


def back_proj(ds, ws, stacked, name, comm=None):
    n = len(ds)
    J, D, Wb = ws[0].shape
    S = ds[0].shape[1] if stacked else ds[0].shape[0]
    tm = _tile(S, 512)

    def body(*refs):
        d_refs, w_refs, o_ref, acc_ref = refs[:n], refs[n:2 * n], refs[2 * n], refs[2 * n + 1]
        j = pl.program_id(1)

        @pl.when(j == 0)
        def _():
            acc_ref[...] = jnp.zeros_like(acc_ref)

        for d_ref, w_ref in zip(d_refs, w_refs):
            acc_ref[...] += _dot(d_ref[...], w_ref[...], NT)

        @pl.when(j == J - 1)
        def _():
            o_ref[...] = acc_ref[...]

    if stacked:
        d_spec = pl.BlockSpec((None, tm, Wb), lambda i, j: (j, i, 0))
    else:
        d_spec = pl.BlockSpec((tm, Wb), lambda i, j: (i, j))
    w_spec = pl.BlockSpec((None, D, Wb), lambda i, j: (j, 0, 0))
    (dh,), got = _call(
        body, name=name, grid=(S // tm, J),
        in_specs=[d_spec] * n + [w_spec] * n,
        out_specs=[pl.BlockSpec((tm, D), lambda i, j: (i, 0))], out_shape=[jax.ShapeDtypeStruct((S, D), F32)],
        scratch=[pltpu.VMEM((tm, D), F32)], args=(*ds, *ws), comm=comm)
    return dh, got


def grad_lhs(ht, d, stacked, name, comm=None):
    D, S = ht.shape
    if stacked:
        J, _, Wb = d.shape
    else:
        J, Wb = NDEV, d.shape[1] // NDEV
    tk = _tile(S, 2048)
    nk = S // tk

    def body(ht_ref, d_ref, o_ref, acc_ref):
        i = pl.program_id(1)

        @pl.when(i == 0)
        def _():
            acc_ref[...] = jnp.zeros_like(acc_ref)

        acc_ref[...] += _dot(ht_ref[...], d_ref[...])

        @pl.when(i == nk - 1)
        def _():
            o_ref[...] = acc_ref[...].astype(BF16)

    (dw,), got = _call(
        body, name=name, grid=(J, nk),
        in_specs=[pl.BlockSpec((D, tk), lambda j, i: (0, i)), _blk(d, Wb, tk, stacked)],
        out_specs=[pl.BlockSpec((None, D, Wb), lambda j, i: (j, 0, 0))],
        out_shape=[jax.ShapeDtypeStruct((J, D, Wb), BF16)],
        scratch=[pltpu.VMEM((D, Wb), F32)], args=(ht, d), vmem=56, comm=comm)
    return dw, got


def grad_shared_rhs(a, gb, nblk, scale, name):
    S, D = gb.shape
    stacked = nblk is None
    if stacked:
        J, _, Wb = a.shape
    else:
        J, Wb = nblk, a.shape[1] // nblk
    tk = _tile(S, 2048 if Wb <= 768 else 1024)
    nk = S // tk

    def body(a_ref, g_ref, o_ref, acc_ref):
        i = pl.program_id(1)

        @pl.when(i == 0)
        def _():
            acc_ref[...] = jnp.zeros_like(acc_ref)

        acc_ref[...] += _dot(a_ref[...], g_ref[...], TN)

        @pl.when(i == nk - 1)
        def _():
            o_ref[...] = (acc_ref[...] * scale).astype(BF16)

    return pl.pallas_call(
        body, name=name, grid=(J, nk),
        in_specs=[_blk(a, Wb, tk, stacked), pl.BlockSpec((tk, D), lambda j, i: (i, 0))],
        out_specs=pl.BlockSpec((None, Wb, D), lambda j, i: (j, 0, 0)),
        out_shape=jax.ShapeDtypeStruct((J, Wb, D), BF16),
        scratch_shapes=[pltpu.VMEM((Wb, D), F32)], compiler_params=_params(56),
    )(a, gb)


def out_proj(ya, ycp, wo, x, name):
    S, A = ya.shape
    Dm, D = wo.shape
    tm = _tile(S, 512)

    def body(ya_ref, ycp_ref, wo_ref, x_ref, o_ref):
        acc = _dot(ya_ref[...], wo_ref[0:A, :]) + _dot(ycp_ref[...], wo_ref[A:Dm, :])
        o_ref[...] = x_ref[...] + acc

    row = pl.BlockSpec((tm, D), lambda i: (i, 0))
    return pl.pallas_call(
        body, name=name, grid=(S // tm,),
        in_specs=[pl.BlockSpec((tm, A), lambda i: (i, 0)), pl.BlockSpec((tm, Dm - A), lambda i: (i, 0)),
                  pl.BlockSpec((Dm, D), lambda i: (0, 0)), row],
        out_specs=row, out_shape=jax.ShapeDtypeStruct((S, D), F32), compiler_params=_params(48),
    )(ya, ycp, wo, x)


def out_proj_bwd(gb, wo, A, name):
    S, D = gb.shape
    Dm = wo.shape[0]
    tm = _tile(S, 512)

    def body(g_ref, wo_ref, dya_ref, dycp_ref):
        gv = g_ref[...]
        dya_ref[...] = _dot(gv, wo_ref[0:A, :], NT).astype(BF16)
        dycp_ref[...] = _dot(gv, wo_ref[A:Dm, :], NT).astype(BF16)

    return pl.pallas_call(
        body, name=name, grid=(S // tm,),
        in_specs=[pl.BlockSpec((tm, D), lambda i: (i, 0)), pl.BlockSpec((Dm, D), lambda i: (0, 0))],
        out_specs=[pl.BlockSpec((tm, A), lambda i: (i, 0)), pl.BlockSpec((tm, Dm - A), lambda i: (i, 0))],
        out_shape=[jax.ShapeDtypeStruct((S, A), BF16), jax.ShapeDtypeStruct((S, Dm - A), BF16)],
        compiler_params=_params(48),
    )(gb, wo)


def _lane_stat(v, hi):
    lane = lax.broadcasted_iota(jnp.int32, v.shape, 1)
    sel = (lane >= HEAD_DIM) if hi else (lane < HEAD_DIM)
    return jnp.max(jnp.where(sel, v, -jnp.inf), axis=-1, keepdims=True)


def band_fwd(q, k, v, L, name):
    H, S, W = q.shape
    T = min(512, L)
    nb = T // BLK

    def body(q_ref, k_ref, kp_ref, v_ref, vp_ref, o_ref):
        i = pl.program_id(1)
        first_key = jnp.where((i * T) % L != 0, 0, BLK)
        qi = lax.broadcasted_iota(jnp.int32, (BLK, 2 * BLK), 0)
        kj = lax.broadcasted_iota(jnp.int32, (BLK, 2 * BLK), 1)
        band = (kj >= qi) & (kj <= qi + BLK)
        lane = lax.broadcasted_iota(jnp.int32, (BLK, W), 1)
        for b in range(nb):
            rows = slice(b * BLK, (b + 1) * BLK)
            if b == 0:
                kw = jnp.concatenate([kp_ref[...], k_ref[rows, :]], axis=0)
                vw = jnp.concatenate([vp_ref[...], v_ref[rows, :]], axis=0)
                mask = band & (kj >= first_key)
            else:
                kw = k_ref[(b - 1) * BLK:(b + 1) * BLK, :]
                vw = v_ref[(b - 1) * BLK:(b + 1) * BLK, :]
                mask = band
            s = jnp.where(mask, _dot(q_ref[rows, :], kw, NT) * ATTN_SCALE, NEG_INF)
            m = jnp.max(s, axis=-1, keepdims=True)
            p = jnp.exp(s - m)
            l = jnp.sum(p, axis=-1, keepdims=True)
            o = _dot(p.astype(BF16), vw)
            o_ref[rows, :] = jnp.where(lane < HEAD_DIM, o / l, m + jnp.log(l))

    cur = pl.BlockSpec((None, T, W), lambda h, i: (h, i, 0))
    prev = pl.BlockSpec((None, BLK, W), lambda h, i: (h, jnp.maximum(i * nb - 1, 0), 0))
    return pl.pallas_call(
        body, name=name, grid=(H, S // T),
        in_specs=[cur, cur, prev, cur, prev], out_specs=cur,
        out_shape=jax.ShapeDtypeStruct((H, S, W), F32), compiler_params=_params(32),
    )(q, k, k, v, v)


def band_bwd(q, k, v, do, st, L, name, comm=None):
    H, S, W = q.shape
    T = min(512, L)
    nb = T // BLK
    last_blk = S // BLK - 1

    def body(q_ref, qn_ref, do_ref, don_ref, st_ref, stn_ref, k_ref, kp_ref, v_ref, vp_ref,
             dq_ref, dk_ref, dv_ref, dk_acc, dv_acc):
        i = pl.program_id(1)
        first_key = jnp.where((i * T) % L != 0, 0, BLK)
        next_off = jnp.where(((i + 1) * T) % L != 0, 0, 4 * BLK)
        qi = lax.broadcasted_iota(jnp.int32, (BLK, 2 * BLK), 0)
        kj = lax.broadcasted_iota(jnp.int32, (BLK, 2 * BLK), 1)
        band = (kj >= qi) & (kj <= qi + BLK)
        qi1 = lax.broadcasted_iota(jnp.int32, (BLK, BLK), 0)
        kj1 = lax.broadcasted_iota(jnp.int32, (BLK, BLK), 1)
        dk_acc[...] = jnp.zeros_like(dk_acc)
        dv_acc[...] = jnp.zeros_like(dv_acc)
        for b in range(nb + 1):
            rows = slice(b * BLK, (b + 1) * BLK)
            if b < nb:
                qb, dob, stb = q_ref[rows, :], do_ref[rows, :], st_ref[rows, :]
            else:
                qb, dob, stb = qn_ref[...], don_ref[...], stn_ref[...]
            if b == 0:
                kw = jnp.concatenate([kp_ref[...], k_ref[rows, :]], axis=0)
                vw = jnp.concatenate([vp_ref[...], v_ref[rows, :]], axis=0)
                mask = band & (kj >= first_key)
            elif b < nb:
                kw = k_ref[(b - 1) * BLK:(b + 1) * BLK, :]
                vw = v_ref[(b - 1) * BLK:(b + 1) * BLK, :]
                mask = band
            else:
                kw = k_ref[(nb - 1) * BLK:nb * BLK, :]
                vw = v_ref[(nb - 1) * BLK:nb * BLK, :]
                mask = kj1 >= qi1 + next_off
            s = _dot(qb, kw, NT) * ATTN_SCALE
            p = jnp.where(mask, jnp.exp(s - _lane_stat(stb, False)), 0.0)
            dp = _dot(dob, vw, NT)
            ds = (p * (dp - _lane_stat(stb, True))).astype(BF16)
            if b < nb:
                dq_ref[rows, :] = _dot(ds, kw) * ATTN_SCALE
            win = slice(b * BLK, b * BLK + kw.shape[0])
            dk_acc[win, :] += _dot(ds, qb, TN) * ATTN_SCALE
            dv_acc[win, :] += _dot(p.astype(BF16), dob, TN)
        dk_ref[...] = dk_acc[BLK:, :]
        dv_ref[...] = dv_acc[BLK:, :]

    cur = pl.BlockSpec((None, T, W), lambda h, i: (h, i, 0))
    prev = pl.BlockSpec((None, BLK, W), lambda h, i: (h, jnp.maximum(i * nb - 1, 0), 0))
    nxt = pl.BlockSpec((None, BLK, W), lambda h, i: (h, jnp.minimum((i + 1) * nb, last_blk), 0))
    shp = jax.ShapeDtypeStruct((H, S, W), F32)
    return _call(
        body, name=name, grid=(H, S // T),
        in_specs=[cur, nxt, cur, nxt, cur, nxt, cur, prev, cur, prev],
        out_specs=[cur, cur, cur], out_shape=[shp, shp, shp],
        scratch=[pltpu.VMEM((T + BLK, W), F32), pltpu.VMEM((T + BLK, W), F32)],
        args=(q, q, do, do, st, st, k, k, v, v), vmem=32, comm=comm)


def attn_merge(os_, name):
    H, S, W = os_[0].shape
    tm = _tile(S, 1024)

    def body(*refs):
        o_ref = refs[-1]
        vals = [r[...] for r in refs[:-1]]
        lses = [_lane_stat(v, True) for v in vals]
        m = functools.reduce(jnp.maximum, lses)
        ws = [jnp.exp(l - m) for l in lses]
        tot = functools.reduce(jnp.add, ws)
        out = functools.reduce(jnp.add, [(w / tot) * v for w, v in zip(ws, vals)])
        lane = lax.broadcasted_iota(jnp.int32, out.shape, 1)
        o_ref[...] = jnp.where(lane < HEAD_DIM, out, m + jnp.log(tot))

    spec = pl.BlockSpec((None, tm, W), lambda h, i: (h, i, 0))
    return pl.pallas_call(
        body, name=name, grid=(H, S // tm), in_specs=[spec] * len(os_), out_specs=spec,
        out_shape=jax.ShapeDtypeStruct((H, S, W), F32), compiler_params=_params(32),
    )(*os_)


def attn_bwd_stats(merged, do, name):
    H, S, W = merged.shape
    tm = _tile(S, 1024)

    def body(m_ref, do_ref, st_ref):
        mv = m_ref[...]
        delta = jnp.sum(mv * do_ref[...].astype(F32), axis=-1, keepdims=True)
        lane = lax.broadcasted_iota(jnp.int32, mv.shape, 1)
        st_ref[...] = jnp.where(lane < HEAD_DIM, _lane_stat(mv, True), delta)

    spec = pl.BlockSpec((None, tm, W), lambda h, i: (h, i, 0))
    return pl.pallas_call(
        body, name=name, grid=(H, S // tm), in_specs=[spec, spec], out_specs=spec,
        out_shape=jax.ShapeDtypeStruct((H, S, W), F32), compiler_params=_params(32),
    )(merged, do)


def sum_patterns(ts, name):
    H, S, W = ts[0].shape
    tm = _tile(S, 1024)

    def body(*refs):
        refs[-1][...] = functools.reduce(jnp.add, [r[...] for r in refs[:-1]]).astype(BF16)

    spec = pl.BlockSpec((None, tm, W), lambda h, i: (h, i, 0))
    return pl.pallas_call(
        body, name=name, grid=(H, S // tm), in_specs=[spec] * len(ts), out_specs=spec,
        out_shape=jax.ShapeDtypeStruct((H, S, W), BF16), compiler_params=_params(32),
    )(*ts)


def to_heads(t):
    S = t.shape[0]
    t = jnp.transpose(t.reshape(S, -1, HEAD_DIM), (1, 0, 2))
    return jnp.pad(t, ((0, 0), (0, 0), (0, LANES - HEAD_DIM)))


def from_heads(t):
    H, S, _ = t.shape
    return jnp.transpose(t[:, :, :HEAD_DIM], (1, 0, 2)).reshape(S, H * HEAD_DIM)


def permute(t, d):
    if d == 1:
        return t
    H, S, W = t.shape
    return jnp.transpose(t.reshape(H, S // d, d, W), (0, 2, 1, 3)).reshape(H, S, W)


def unpermute(t, d):
    if d == 1:
        return t
    H, S, W = t.shape
    return jnp.transpose(t.reshape(H, d, S // d, W), (0, 2, 1, 3)).reshape(H, S, W)


ATT_TILE = BLK * max(DILATIONS)


def _head0(shape):
    return lax.broadcasted_iota(jnp.int32, shape, 1) < HEAD_DIM


def _pair_col(v, sel):
    return jnp.max(jnp.where(sel, v, -jnp.inf), axis=-1, keepdims=True)


def _band(lo, hi):
    qi = lax.broadcasted_iota(jnp.int32, (BLK, 2 * BLK), 0)
    kj = lax.broadcasted_iota(jnp.int32, (BLK, 2 * BLK), 1)
    return (kj >= qi) & (kj <= qi + BLK) & (kj >= lo) & (kj < hi)


def _deinterleave(dst, src, d, rows, dst_stride, dst_off=0, src_off=0, cast=None):
    for r in range(d):
        v = src[pl.ds(src_off + r, rows, stride=d), :] if d > 1 else src[pl.ds(src_off, rows), :]
        dst[pl.ds(r * dst_stride + dst_off, rows), :] = v if cast is None else v.astype(cast)


def _interleave(dst, src, d, rows, src_stride, src_off=0, add=False):
    for r in range(d):
        v = src[pl.ds(r * src_stride + src_off, rows), :]
        idx = pl.ds(r, rows, stride=d) if d > 1 else pl.ds(0, rows)
        dst[idx, :] = dst[idx, :] + v if add else v


def attn_fwd(z, A, name):
    S = z.shape[0]
    T = ATT_TILE
    Hp, nt = A // LANES, S // T
    np_ = len(DILATIONS)

    def body(q_ref, k_ref, kp_ref, v_ref, vp_ref, y_ref, lse_ref, qn, kn, vn, qp, kp, vp, accp, mp, lp, *nat):
        accs, ms, ls = nat[:np_], nat[np_:2 * np_], nat[2 * np_:]
        i = pl.program_id(1)
        qn[...] = q_ref[...].astype(F32)
        kn[0:T, :] = kp_ref[...].astype(F32)
        kn[T:2 * T, :] = k_ref[...].astype(F32)
        vn[0:T, :] = vp_ref[...].astype(F32)
        vn[T:2 * T, :] = v_ref[...].astype(F32)
        h0 = _head0((BLK, LANES))
        for pi, d in enumerate(DILATIONS):
            Ld = T // d
            nblk = Ld // BLK
            _deinterleave(qp, qn, d, Ld, Ld, cast=BF16)
            for src, dst in ((kn, kp), (vn, vp)):
                _deinterleave(dst, src, d, Ld, 2 * Ld, cast=BF16)
                _deinterleave(dst, src, d, Ld, 2 * Ld, dst_off=Ld, src_off=T, cast=BF16)

            def unit(u, carry, nblk=nblk):
                r, n = u // nblk, u % nblk
                q0 = pl.multiple_of(u * BLK, BLK)
                k0 = pl.multiple_of((2 * r * nblk + nblk + n - 1) * BLK, BLK)
                qb = qp[pl.ds(q0, BLK), :]
                kw, vw = kp[pl.ds(k0, 2 * BLK), :], vp[pl.ds(k0, 2 * BLK), :]
                mask = _band(jnp.where((i == 0) & (n == 0), BLK, 0), 2 * BLK)
                res = []
                for sel in (h0, ~h0):
                    s = jnp.where(mask, _dot(jnp.where(sel, qb, jnp.zeros_like(qb)), kw, NT) * ATTN_SCALE, NEG_INF)
                    mx = jnp.max(s, axis=-1, keepdims=True)
                    p = jnp.exp(s - mx)
                    res.append((mx, jnp.sum(p, axis=-1, keepdims=True), _dot(p.astype(BF16), vw)))
                (m0, l0, o0), (m1, l1, o1) = res
                accp[pl.ds(q0, BLK), :] = jnp.where(h0, o0, o1)
                mp[pl.ds(q0, BLK), :] = jnp.where(h0, m0, m1)
                lp[pl.ds(q0, BLK), :] = jnp.where(h0, l0, l1)
                return carry

            lax.fori_loop(0, T // BLK, unit, 0, unroll=4)
            for src, dst in ((accp, accs[pi]), (mp, ms[pi]), (lp, ls[pi])):
                _interleave(dst, src, d, Ld, Ld)
        mv = [m[...] for m in ms]
        mx = mv[0]
        for m in mv[1:]:
            mx = jnp.maximum(mx, m)
        ws = [jnp.exp(m - mx) for m in mv]
        l = sum(w * lr[...] for w, lr in zip(ws, ls))
        a = sum(w * ar[...] for w, ar in zip(ws, accs))
        y_ref[...] = (a / l).astype(BF16)
        lse_ref[...] = mx + jnp.log(l)

    kb, vb = A // LANES, 2 * A // LANES
    cur = lambda off: pl.BlockSpec((T, LANES), lambda h, i: (i, off + h))
    prev = lambda off: pl.BlockSpec((T, LANES), lambda h, i: (jnp.maximum(i - 1, 0), off + h))
    out = pl.BlockSpec((T, LANES), lambda h, i: (i, h))
    vm = lambda rows, dt: pltpu.VMEM((rows, LANES), dt)
    (y, lse), _ = _call(
        body, name=name, grid=(Hp, nt),
        in_specs=[cur(0), cur(kb), prev(kb), cur(vb), prev(vb)], out_specs=[out, out],
        out_shape=[jax.ShapeDtypeStruct((S, A), BF16), jax.ShapeDtypeStruct((S, A), F32)],
        scratch=[vm(T, F32), vm(2 * T, F32), vm(2 * T, F32), vm(T, BF16), vm(2 * T, BF16), vm(2 * T, BF16),
                 vm(T, F32), vm(T, F32), vm(T, F32)] + [vm(T, F32)] * (3 * np_),
        args=(z, z, z, z, z), vmem=48)
    return y, lse


def attn_bwd(z, dy, y, lse, A, name, comm=None):
    S = z.shape[0]
    T = ATT_TILE
    Hp, nt = A // LANES, S // T

    def body(q_ref, qx_ref, k_ref, kp_ref, v_ref, vp_ref, do_ref, dox_ref, y_ref, yx_ref, ls_ref, lsx_ref,
             dq_ref, dk_ref, dv_ref, qn, don, dln, kn, vn, qp, dop, lsp, dlp, kp, vp, dqp, dkp, dvp, dqa, dka, dva):
        i = pl.program_id(1)
        h0t = _head0((T, LANES))
        for half, (qr, dor, yr) in enumerate(((q_ref, do_ref, y_ref), (qx_ref, dox_ref, yx_ref))):
            rows = pl.ds(half * T, T)
            dof = dor[...].astype(F32)
            prod = dof * yr[...].astype(F32)
            d0 = jnp.sum(jnp.where(h0t, prod, 0.0), axis=-1, keepdims=True)
            d1 = jnp.sum(jnp.where(h0t, 0.0, prod), axis=-1, keepdims=True)
            qn[rows, :] = qr[...].astype(F32)
            don[rows, :] = dof
            dln[rows, :] = jnp.where(h0t, d0, d1)
        kn[0:T, :] = kp_ref[...].astype(F32)
        kn[T:2 * T, :] = k_ref[...].astype(F32)
        vn[0:T, :] = vp_ref[...].astype(F32)
        vn[T:2 * T, :] = v_ref[...].astype(F32)
        h0 = _head0((BLK, LANES))
        for pi, d in enumerate(DILATIONS):
            Ld = T // d
            nblk = Ld // BLK
            for src, dst, cast in ((qn, qp, BF16), (don, dop, BF16), (dln, dlp, None)):
                _deinterleave(dst, src, d, Ld, Ld, cast=cast)
                _deinterleave(dst, src, d, BLK, BLK, dst_off=T, src_off=T, cast=cast)
            _deinterleave(lsp, ls_ref, d, Ld, Ld)
            _deinterleave(lsp, lsx_ref, d, BLK, BLK, dst_off=T)
            for src, dst in ((kn, kp), (vn, vp)):
                _deinterleave(dst, src, d, Ld, 2 * Ld, cast=BF16)
                _deinterleave(dst, src, d, Ld, 2 * Ld, dst_off=Ld, src_off=T, cast=BF16)
            dkp[...] = jnp.zeros_like(dkp)
            dvp[...] = jnp.zeros_like(dvp)

            def grads(q0, k0, nkeys, mask, with_dq):
                qb, dob = qp[pl.ds(q0, BLK), :], dop[pl.ds(q0, BLK), :]
                lsb, dlb = lsp[pl.ds(q0, BLK), :], dlp[pl.ds(q0, BLK), :]
                kw, vw = kp[pl.ds(k0, nkeys), :], vp[pl.ds(k0, nkeys), :]
                dq, dk, dv = None, None, None
                for sel in (h0, ~h0):
                    qh = jnp.where(sel, qb, jnp.zeros_like(qb))
                    doh = jnp.where(sel, dob, jnp.zeros_like(dob))
                    s = _dot(qh, kw, NT) * ATTN_SCALE
                    p = jnp.where(mask, jnp.exp(s - _pair_col(lsb, sel)), 0.0)
                    ds = (p * (_dot(doh, vw, NT) - _pair_col(dlb, sel))).astype(BF16)
                    dkh = _dot(ds, qh, TN) * ATTN_SCALE
                    dvh = _dot(p.astype(BF16), doh, TN)
                    dk = dkh if dk is None else dk + dkh
                    dv = dvh if dv is None else dv + dvh
                    if with_dq:
                        dqh = _dot(ds, kw) * ATTN_SCALE
                        dq = dqh if dq is None else jnp.where(h0, dq, dqh)
                if with_dq:
                    dqp[pl.ds(q0, BLK), :] = dq
                dkp[pl.ds(k0, nkeys), :] += dk
                dvp[pl.ds(k0, nkeys), :] += dv

            def own(u, carry, nblk=nblk):
                r, b = u // nblk, u % nblk
                grads(pl.multiple_of(u * BLK, BLK), pl.multiple_of((2 * r * nblk + nblk + b - 1) * BLK, BLK),
                      2 * BLK, _band(jnp.where((i == 0) & (b == 0), BLK, 0), 2 * BLK), True)
                return carry

            def from_next(r, carry, nblk=nblk):
                qi = lax.broadcasted_iota(jnp.int32, (BLK, BLK), 0)
                kj = lax.broadcasted_iota(jnp.int32, (BLK, BLK), 1)
                grads(pl.multiple_of(T + r * BLK, BLK), pl.multiple_of((2 * r * nblk + 2 * nblk - 1) * BLK, BLK),
                      BLK, kj >= qi + jnp.where(i == nt - 1, 2 * BLK, 0), False)
                return carry

            lax.fori_loop(0, T // BLK, own, 0, unroll=4)
            lax.fori_loop(0, d, from_next, 0, unroll=min(d, 4))
            _interleave(dqa, dqp, d, Ld, Ld, add=pi > 0)
            _interleave(dka, dkp, d, Ld, 2 * Ld, src_off=Ld, add=pi > 0)
            _interleave(dva, dvp, d, Ld, 2 * Ld, src_off=Ld, add=pi > 0)
        dq_ref[...] = dqa[...].astype(BF16)
        dk_ref[...] = dka[...].astype(BF16)
        dv_ref[...] = dva[...].astype(BF16)

    kb, vb = A // LANES, 2 * A // LANES
    cur = lambda off: pl.BlockSpec((T, LANES), lambda h, i: (i, off + h))
    prev = lambda off: pl.BlockSpec((T, LANES), lambda h, i: (jnp.maximum(i - 1, 0), off + h))
    nxt = lambda off: pl.BlockSpec((T, LANES), lambda h, i: (jnp.minimum(i + 1, nt - 1), off + h))
    vm = lambda rows, dt: pltpu.VMEM((rows, LANES), dt)
    shp = jax.ShapeDtypeStruct((S, A), BF16)
    return _call(
        body, name=name, grid=(Hp, nt),
        in_specs=[cur(0), nxt(0), cur(kb), prev(kb), cur(vb), prev(vb), cur(0), nxt(0), cur(0), nxt(0), cur(0), nxt(0)],
        out_specs=[cur(0)] * 3, out_shape=[shp] * 3,
        scratch=[vm(2 * T, F32), vm(2 * T, F32), vm(2 * T, F32), vm(2 * T, F32), vm(2 * T, F32),
                 vm(2 * T, BF16), vm(2 * T, BF16), vm(2 * T, F32), vm(2 * T, F32), vm(2 * T, BF16), vm(2 * T, BF16),
                 vm(T, F32), vm(2 * T, F32), vm(2 * T, F32), vm(T, F32), vm(T, F32), vm(T, F32)],
        args=(z, z, z, z, z, z, dy, dy, y, y, lse, lse), vmem=56, comm=comm)


def _shift_down(e, k):
    return pltpu.roll(e, k, 0)


def _shift_up(e, k):
    return pltpu.roll(e, e.shape[0] - k, 0)


def _causal_sums(e, g):
    for lvl in range(g + 1):
        e = e + _shift_down(e, 1 << lvl)
    return e


def _anticausal_sums(e, g):
    for lvl in range(g + 1):
        e = e + _shift_up(e, 1 << lvl)
    return e


def _mixer_specs(S, tm, A, C, P):
    assert (3 * A) % C == 0 and (3 * A + 3 * C) % P == 0 and tm % HALO == 0
    cb, pb = 3 * A // C, (3 * A + 3 * C) // P
    hb = tm // HALO
    last = S // HALO - 1
    col = lambda w, j: pl.BlockSpec((tm, w), lambda i: (i, j))
    prev = lambda w, j: pl.BlockSpec((HALO, w), lambda i: (jnp.maximum(i * hb - 1, 0), j))
    nxt = lambda w, j: pl.BlockSpec((HALO, w), lambda i: (jnp.minimum((i + 1) * hb, last), j))
    return cb, pb, col, prev, nxt


def _conv_taps(gc, ci, gch, cih, keep_prev):
    u = gc * ci
    e = jnp.concatenate([gch * cih * keep_prev, u], axis=0)
    return u, _shift_down(e, 1)[HALO:], _shift_down(e, 2)[HALO:]


def _pooled(xp, xph, keep_prev, pos, g):
    cols = slice(g * POOL_GROUP, (g + 1) * POOL_GROUP)
    x = xp[:, cols]
    e = jnp.concatenate([xph[:, cols] * keep_prev, x], axis=0)
    cnt = jnp.minimum(pos + 1, POOL_WINDOWS[g]).astype(F32)
    return _causal_sums(e, g)[HALO:] / cnt - x


def convpool_fwd(z, conv_w, pool_w, pool_scale, A, C, P, name):
    S = z.shape[0]
    tm = _tile(S, 512)
    cb, pb, col, prev, _ = _mixer_specs(S, tm, A, C, P)

    def body(gb_ref, gc_ref, ci_ref, xp_ref, gch_ref, cih_ref, xph_ref, cw_ref, pw_ref, ps_ref, y_ref):
        i = pl.program_id(0)
        keep_prev = jnp.where(i == 0, 0.0, 1.0)
        f = lambda r: r[...].astype(F32)
        u, u1, u2 = _conv_taps(f(gc_ref), f(ci_ref), f(gch_ref), f(cih_ref), keep_prev)
        cw = cw_ref[...]
        y_ref[:, 0:C] = (f(gb_ref) * (cw[0:1] * u2 + cw[1:2] * u1 + cw[2:3] * u)).astype(BF16)
        xp, xph = f(xp_ref), f(xph_ref)
        pos = i * tm + lax.broadcasted_iota(jnp.int32, (tm, 1), 0)
        for g in range(len(POOL_WINDOWS)):
            cols = slice(g * POOL_GROUP, (g + 1) * POOL_GROUP)
            lin = _dot(_pooled(xp, xph, keep_prev, pos, g).astype(BF16), pw_ref[g])
            y_ref[:, C + g * POOL_GROUP:C + (g + 1) * POOL_GROUP] = (lin * ps_ref[:, cols]).astype(BF16)

    full = lambda shape: pl.BlockSpec(shape, lambda i: (0,) * len(shape))
    return pl.pallas_call(
        body, name=name, grid=(S // tm,),
        in_specs=[col(C, cb), col(C, cb + 1), col(C, cb + 2), col(P, pb),
                  prev(C, cb + 1), prev(C, cb + 2), prev(P, pb),
                  full(conv_w.shape), full(pool_w.shape), full(pool_scale.shape)],
        out_specs=pl.BlockSpec((tm, C + P), lambda i: (i, 0)),
        out_shape=jax.ShapeDtypeStruct((S, C + P), BF16), compiler_params=_params(48),
    )(z, z, z, z, z, z, z, conv_w, pool_w, pool_scale)


def convpool_bwd(z, dycp, conv_w, pool_w, pool_scale, A, C, P, name):
    S = z.shape[0]
    tm = _tile(S, 512)
    nt = S // tm
    cb, pb, col, prev, nxt = _mixer_specs(S, tm, A, C, P)
    NG = len(POOL_WINDOWS)

    def body(gb_ref, gc_ref, ci_ref, xp_ref, gch_ref, cih_ref, xph_ref, gbn_ref, dy_ref, dyn_ref,
             cw_ref, pw_ref, ps_ref, dz_ref, dcw_ref, dpw_ref, dps_ref):
        i = pl.program_id(0)

        @pl.when(i == 0)
        def _():
            dcw_ref[...] = jnp.zeros_like(dcw_ref)
            dpw_ref[...] = jnp.zeros_like(dpw_ref)
            dps_ref[...] = jnp.zeros_like(dps_ref)

        keep_prev = jnp.where(i == 0, 0.0, 1.0)
        keep_next = jnp.where(i == nt - 1, 0.0, 1.0)
        f = lambda r: r[...].astype(F32)
        gb, gc, ci = f(gb_ref), f(gc_ref), f(ci_ref)
        u, u1, u2 = _conv_taps(gc, ci, f(gch_ref), f(cih_ref), keep_prev)
        cw = cw_ref[...]
        dy, dyn = f(dy_ref), f(dyn_ref) * keep_next
        dyc = dy[:, 0:C]
        dz_ref[:, 0:C] = (dyc * (cw[0:1] * u2 + cw[1:2] * u1 + cw[2:3] * u)).astype(BF16)
        dc = dyc * gb
        e = jnp.concatenate([dc, dyn[:, 0:C] * f(gbn_ref)], axis=0)
        du = cw[2:3] * dc + cw[1:2] * _shift_up(e, 1)[:tm] + cw[0:1] * _shift_up(e, 2)[:tm]
        dz_ref[:, C:2 * C] = (du * ci).astype(BF16)
        dz_ref[:, 2 * C:3 * C] = (du * gc).astype(BF16)
        dcw_ref[0:1, :] += jnp.sum(dc * u2, axis=0, keepdims=True)
        dcw_ref[1:2, :] += jnp.sum(dc * u1, axis=0, keepdims=True)
        dcw_ref[2:3, :] += jnp.sum(dc * u, axis=0, keepdims=True)

        xp, xph = f(xp_ref), f(xph_ref)
        pos = i * tm + lax.broadcasted_iota(jnp.int32, (tm, 1), 0)
        pos_e = i * tm + lax.broadcasted_iota(jnp.int32, (tm + HALO, 1), 0)
        for g in range(NG):
            cols = slice(g * POOL_GROUP, (g + 1) * POOL_GROUP)
            ycols = slice(C + g * POOL_GROUP, C + (g + 1) * POOL_GROUP)
            pooled = _pooled(xp, xph, keep_prev, pos, g).astype(BF16)
            pw = pw_ref[g]
            dyp = dy[:, ycols]
            dps_ref[:, cols] += jnp.sum(dyp * _dot(pooled, pw), axis=0, keepdims=True)
            dlin = (jnp.concatenate([dyp, dyn[:, ycols]], axis=0) * ps_ref[:, cols]).astype(BF16)
            dpw_ref[g] += _dot(pooled, dlin[:tm], TN)
            dpool = _dot(dlin, pw, NT)
            r = dpool / jnp.minimum(pos_e + 1, POOL_WINDOWS[g]).astype(F32)
            dz_ref[:, 3 * C + g * POOL_GROUP:3 * C + (g + 1) * POOL_GROUP] = (
                _anticausal_sums(r, g)[:tm] - dpool[:tm]).astype(BF16)

    full = lambda shape: pl.BlockSpec(shape, lambda i: (0,) * len(shape))
    return pl.pallas_call(
        body, name=name, grid=(nt,),
        in_specs=[col(C, cb), col(C, cb + 1), col(C, cb + 2), col(P, pb),
                  prev(C, cb + 1), prev(C, cb + 2), prev(P, pb), nxt(C, cb),
                  pl.BlockSpec((tm, C + P), lambda i: (i, 0)),
                  pl.BlockSpec((HALO, C + P), lambda i: (jnp.minimum((i + 1) * (tm // HALO), S // HALO - 1), 0)),
                  full(conv_w.shape), full(pool_w.shape), full(pool_scale.shape)],
        out_specs=[pl.BlockSpec((tm, 3 * C + P), lambda i: (i, 0)),
                   full(conv_w.shape), full(pool_w.shape), full(pool_scale.shape)],
        out_shape=[jax.ShapeDtypeStruct((S, 3 * C + P), BF16), jax.ShapeDtypeStruct(conv_w.shape, F32),
                   jax.ShapeDtypeStruct(pool_w.shape, F32), jax.ShapeDtypeStruct(pool_scale.shape, F32)],
        compiler_params=_params(48),
    )(z, z, z, z, z, z, z, z, dycp, dycp, conv_w, pool_w, pool_scale)


def adamw(lands, w, m, v, name):
    R, C = w.shape
    nl = len(lands)
    n, Rl, _ = lands[0].shape
    assert Rl * nl == R
    tr = _tile(Rl, 128 * 1024 // C)
    nb = Rl // tr
    c1 = 1.0 - ADAM_B1 ** ADAM_STEP
    c2 = 1.0 - ADAM_B2 ** ADAM_STEP

    def body(*refs):
        land_refs = refs[:nl]
        w_ref, m_ref, v_ref, g_ref, d_ref, mo_ref, vo_ref = refs[nl:]
        i = pl.program_id(0)
        for a, land_ref in enumerate(land_refs):
            @pl.when((i >= a * nb) & (i < (a + 1) * nb))
            def _():
                g = land_ref[0].astype(F32)
                for s in range(1, n):
                    g = g + land_ref[s].astype(F32)
                g_ref[...] = g
                mn = ADAM_B1 * m_ref[...] + (1.0 - ADAM_B1) * g
                vn = ADAM_B2 * v_ref[...] + (1.0 - ADAM_B2) * (g * g)
                mo_ref[...] = mn
                vo_ref[...] = vn
                d_ref[...] = -ADAM_LR * ((mn / c1) / (jnp.sqrt(vn / c2) + ADAM_EPS) + ADAM_WD * w_ref[...])

    land_specs = [pl.BlockSpec((n, tr, C), lambda i, a=a: (0, jnp.clip(i - a * nb, 0, nb - 1), 0))
                  for a in range(nl)]
    row = pl.BlockSpec((tr, C), lambda i: (i, 0))
    shp = jax.ShapeDtypeStruct((R, C), F32)
    return pl.pallas_call(
        body, name=name, grid=(nl * nb,),
        in_specs=land_specs + [row, row, row], out_specs=[row] * 4, out_shape=[shp] * 4,
        compiler_params=_params(48),
    )(*lands, w, m, v)


def _gather_comm(arrs):
    return ("gather", arrs) if arrs else None


def _ffn_fwd(x, gain, wg, wu, wd, tag, carry_up=(), carry_down=()):
    h, ht = rms_fwd(x, gain, f"rms_{tag}")
    (a, b, hid), got_up = ffn_up(h, wg, wu, f"ffn_up_{tag}", _gather_comm(list(carry_up)))
    xo, got_down = ffn_down(hid, wd, x, f"ffn_down_{tag}", _gather_comm(list(carry_down)))
    return xo, (x, ht, a, b, hid), got_up, got_down


def _ffn_bwd(g, gb, saved, gain, wg, wu, wd, tag):
    x, ht, a, b, hid = saved
    da, db = ffn_dhid(gb, wd, a, b, f"ffn_dhid_{tag}")
    dwd = grad_shared_rhs(hid, gb, None, FFN_RESIDUAL, f"ffn_dwd_{tag}")
    dwg, (land_wd,) = grad_lhs(ht, da, True, f"ffn_dwg_{tag}", ("exchange", [dwd]))
    dwu, (land_wg,) = grad_lhs(ht, db, True, f"ffn_dwu_{tag}", ("exchange", [dwg]))
    dh, (land_wu,) = back_proj([da, db], [wg, wu], True, f"ffn_dh_{tag}", ("exchange", [dwu]))
    g, gb, dgain = rms_bwd(x, gain, dh, g, f"rms_bwd_{tag}")
    return g, gb, dgain, (land_wg, land_wu, land_wd)


def _mixer_fwd(x, gain, w_in, conv_w, pool_w, pool_scale, w_out, dims, tag):
    A, C, P = dims
    h, ht = rms_fwd(x, gain, f"rms_{tag}")
    z = proj_cols(h, w_in, f"in_proj_{tag}")
    ya, lse = attn_fwd(z, A, f"attn_fwd_{tag}")
    ycp = convpool_fwd(z, conv_w, pool_w, pool_scale, A, C, P, f"convpool_{tag}")
    xo = out_proj(ya, ycp, w_out, x, f"out_proj_{tag}")
    return xo, (x, ht, z, lse, ya, ycp)


def _mixer_bwd(g, gb, saved, gain, w_in, conv_w, pool_w, pool_scale, w_out, dims, tag):
    A, C, P = dims
    x, ht, z, lse, ya, ycp = saved
    S, D = x.shape
    dya, dycp = out_proj_bwd(gb, w_out, A, f"out_proj_bwd_{tag}")
    dwo_a = grad_shared_rhs(ya, gb, 1, 1.0, f"dwo_attn_{tag}")
    dwo_cp = grad_shared_rhs(ycp, gb, 1, 1.0, f"dwo_cp_{tag}")
    dwo = jnp.concatenate([dwo_a[0], dwo_cp[0]], axis=0).reshape(NDEV, -1, D)
    dqkv, (land_wo,) = attn_bwd(z, dya, ya, lse, A, f"attn_bwd_{tag}", ("exchange", [dwo]))
    dz_cp, dcw, dpw, dps = convpool_bwd(z, dycp, conv_w, pool_w, pool_scale, A, C, P, f"convpool_bwd_{tag}")
    dz = jnp.concatenate(dqkv + [dz_cp], axis=1)
    dwin, _ = grad_lhs(ht, dz, False, f"dwin_{tag}")
    dh, (land_win,) = back_proj([dz], [w_in], False, f"in_proj_bwd_{tag}", ("exchange", [dwin]))
    g, gb, dgain = rms_bwd(x, gain, dh, g, f"rms_bwd_{tag}")
    return g, gb, (dgain, dcw, dpw, dps), (land_win, land_wo)


def _pack(arrs):
    flat = [a.reshape(-1).astype(F32) for a in arrs]
    spans, off = [], 0
    for a in flat:
        spans.append((off, a.shape[0]))
        off += a.shape[0]
    rows = -(-off // (8 * LANES)) * 8
    buf = jnp.concatenate(flat + [jnp.zeros((rows * LANES - off,), F32)]).reshape(rows, LANES)
    return buf, spans


def kernel(x, ffn1_norm, ffn1_w_gate, ffn1_w_up, ffn1_w_down, mix_norm, w_in, conv_w, pool_w, pool_scale, w_out, ffn2_norm, ffn2_w_gate, ffn2_w_up, ffn2_w_down, final_norm, loss_target, m_ffn1_norm, m_ffn1_w_gate, m_ffn1_w_up, m_ffn1_w_down, m_mix_norm, m_w_in, m_conv_w, m_pool_w, m_pool_scale, m_w_out, m_ffn2_norm, m_ffn2_w_gate, m_ffn2_w_up, m_ffn2_w_down, m_final_norm, v_ffn1_norm, v_ffn1_w_gate, v_ffn1_w_up, v_ffn1_w_down, v_mix_norm, v_w_in, v_conv_w, v_pool_w, v_pool_scale, v_w_out, v_ffn2_norm, v_ffn2_w_gate, v_ffn2_w_up, v_ffn2_w_down, v_final_norm):
    depth = ffn1_norm.shape[0]
    S, D = x.shape[1], x.shape[2]
    Cb = conv_w.shape[2]
    C = Cb * NDEV
    P = pool_scale.shape[1]
    A = (w_in.shape[2] * NDEV - 3 * C - P) // 3
    dims = (A, C, P)
    me = 4 * lax.axis_index("x") + 2 * lax.axis_index("y") + lax.axis_index("c")

    big = dict(ffn1_w_gate=ffn1_w_gate, ffn1_w_up=ffn1_w_up, ffn1_w_down=ffn1_w_down, w_in=w_in, w_out=w_out,
               ffn2_w_gate=ffn2_w_gate, ffn2_w_up=ffn2_w_up, ffn2_w_down=ffn2_w_down)
    big_m = dict(ffn1_w_gate=m_ffn1_w_gate, ffn1_w_up=m_ffn1_w_up, ffn1_w_down=m_ffn1_w_down, w_in=m_w_in,
                 w_out=m_w_out, ffn2_w_gate=m_ffn2_w_gate, ffn2_w_up=m_ffn2_w_up, ffn2_w_down=m_ffn2_w_down)
    big_v = dict(ffn1_w_gate=v_ffn1_w_gate, ffn1_w_up=v_ffn1_w_up, ffn1_w_down=v_ffn1_w_down, w_in=v_w_in,
                 w_out=v_w_out, ffn2_w_gate=v_ffn2_w_gate, ffn2_w_up=v_ffn2_w_up, ffn2_w_down=v_ffn2_w_down)
    names = list(big)

    ffn1_names = ["ffn1_w_gate", "ffn1_w_up", "ffn1_w_down"]
    mix_names = ["w_in", "w_out", "ffn2_w_gate"]
    rest_names = ["ffn2_w_up", "ffn2_w_down"]
    shards = lambda l, ns: [big[n][l].astype(BF16) for n in ns]
    full = [dict() for _ in range(depth)]
    full[0].update(zip(ffn1_names, all_gather(shards(0, ffn1_names), "gather_ffn1_l0")))
    conv_full = all_gather([jnp.pad(conv_w.reshape(-1, Cb), ((0, 2), (0, LANES - Cb)))], "gather_conv_w")[0]
    conv_full = jnp.transpose(conv_full[:, :depth * 3, :Cb].reshape(NDEV, depth, 3, Cb), (1, 2, 0, 3)).reshape(depth, 3, C)
    pool_w_bf = pool_w.astype(BF16)

    xs = x[0]
    saved = []
    for l in range(depth):
        W = full[l]
        up1, down1 = (mix_names, rest_names) if l == 0 else (rest_names, [])
        xs, s1, got_up, got_down = _ffn_fwd(xs, ffn1_norm[l:l + 1], W["ffn1_w_gate"], W["ffn1_w_up"],
                                            W["ffn1_w_down"], f"f1_l{l}", shards(l, up1), shards(l, down1))
        W.update(zip(up1, got_up))
        W.update(zip(down1, got_down))
        xs, s2 = _mixer_fwd(xs, mix_norm[l:l + 1], W["w_in"], conv_full[l], pool_w_bf[l], pool_scale[l:l + 1],
                            W["w_out"].reshape(-1, D), dims, f"mix_l{l}")
        up2, down2 = (ffn1_names, mix_names) if l + 1 < depth else ([], [])
        xs, s3, got_up, got_down = _ffn_fwd(xs, ffn2_norm[l:l + 1], W["ffn2_w_gate"], W["ffn2_w_up"],
                                            W["ffn2_w_down"], f"f2_l{l}", shards(l + 1, up2) if up2 else (),
                                            shards(l + 1, down2) if down2 else ())
        if l + 1 < depth:
            full[l + 1].update(zip(up2, got_up))
            full[l + 1].update(zip(down2, got_down))
        saved.append((s1, s2, s3))
    loss_part, g, gb, d_final = final_loss(xs, final_norm.reshape(1, D), loss_target[0], "final_loss")

    small = [None] * depth
    lands = [None] * depth
    for l in reversed(range(depth)):
        W = full[l]
        s1, s2, s3 = saved[l]
        g, gb, dn2, (lg2, lu2, ld2) = _ffn_bwd(g, gb, s3, ffn2_norm[l:l + 1], W["ffn2_w_gate"], W["ffn2_w_up"],
                                               W["ffn2_w_down"], f"f2_l{l}")
        g, gb, (dnm, dcw, dpw, dps), (lwin, lwo) = _mixer_bwd(
            g, gb, s2, mix_norm[l:l + 1], W["w_in"], conv_full[l], pool_w_bf[l], pool_scale[l:l + 1],
            W["w_out"].reshape(-1, D), dims, f"mix_l{l}")
        g, gb, dn1, (lg1, lu1, ld1) = _ffn_bwd(g, gb, s1, ffn1_norm[l:l + 1], W["ffn1_w_gate"], W["ffn1_w_up"],
                                               W["ffn1_w_down"], f"f1_l{l}")
        lands[l] = dict(ffn1_w_gate=lg1, ffn1_w_up=lu1, ffn1_w_down=ld1, w_in=lwin, w_out=lwo,
                        ffn2_w_gate=lg2, ffn2_w_up=lu2, ffn2_w_down=ld2)
        small[l] = (dn1, dnm, dcw, dpw, dps, dn2)

    res = {}
    for n in names:
        shp = big[n].shape
        two = lambda t: t.reshape(-1, shp[-1])
        outs = adamw([lands[l][n] for l in range(depth)], two(big[n]), two(big_m[n]), two(big_v[n]), f"adamw_{n}")
        res[n] = [o.reshape(shp) for o in outs]

    st = lambda i: jnp.stack([small[l][i] for l in range(depth)])
    small_g = dict(ffn1_norm=st(0), mix_norm=st(1), conv_w=st(2), pool_w=st(3), pool_scale=st(4), ffn2_norm=st(5),
                   final_norm=d_final)
    small_names = list(small_g)
    zeros_conv = jnp.zeros((depth, 3, C), F32)
    place = lambda t: lax.dynamic_update_slice(zeros_conv, t, (0, 0, me * Cb))
    small_w = dict(ffn1_norm=ffn1_norm, mix_norm=mix_norm, conv_w=place(conv_w), pool_w=pool_w, pool_scale=pool_scale,
                   ffn2_norm=ffn2_norm, final_norm=final_norm)
    small_m = dict(ffn1_norm=m_ffn1_norm, mix_norm=m_mix_norm, conv_w=place(m_conv_w), pool_w=m_pool_w,
                   pool_scale=m_pool_scale, ffn2_norm=m_ffn2_norm, final_norm=m_final_norm)
    small_v = dict(ffn1_norm=v_ffn1_norm, mix_norm=v_mix_norm, conv_w=place(v_conv_w), pool_w=v_pool_w,
                   pool_scale=v_pool_scale, ffn2_norm=v_ffn2_norm, final_norm=v_final_norm)
    gbuf, spans = _pack([small_g[n] for n in small_names] + [loss_part])
    wbuf, _ = _pack([small_w[n] for n in small_names] + [jnp.zeros((1, LANES), F32)])
    mbuf, _ = _pack([small_m[n] for n in small_names] + [jnp.zeros((1, LANES), F32)])
    vbuf, _ = _pack([small_v[n] for n in small_names] + [jnp.zeros((1, LANES), F32)])
    gathered = all_gather([gbuf], "gather_small_grads")[0]
    outs = adamw([gathered], wbuf, mbuf, vbuf, "adamw_small")
    for n, (off, size) in zip(small_names, spans):
        shp = small_w[n].shape
        vals = [o.reshape(-1)[off:off + size].reshape(shp) for o in outs]
        if n == "conv_w":
            vals = [lax.dynamic_slice(t, (0, 0, me * Cb), (depth, 3, Cb)) for t in vals]
        res[n] = vals
    loss = outs[0].reshape(-1)[spans[-1][0]]

    order = ["ffn1_norm", "ffn1_w_gate", "ffn1_w_up", "ffn1_w_down", "mix_norm", "w_in", "conv_w", "pool_w",
             "pool_scale", "w_out", "ffn2_norm", "ffn2_w_gate", "ffn2_w_up", "ffn2_w_down", "final_norm"]
    return (loss, g[None], *[res[n][0] for n in order], *[res[n][1] for n in order],
            *[res[n][2] for n in order], *[res[n][3] for n in order])
```

```python
import functools

import jax
import jax.numpy as jnp
from jax import lax
from jax.experimental import pallas as pl
from jax.experimental.pallas import tpu as pltpu

F32 = jnp.float32
BF16 = jnp.bfloat16
NDEV = 8
HEAD_DIM = 64
LANES = 128
BLK = 128
DILATIONS = (1, 4, 16)
POOL_WINDOWS = (2, 4, 8, 16)
POOL_GROUP = 128
HALO = 16
FFN_RESIDUAL = 0.5
RMS_EPS = 1e-6
NEG_INF = -1e30
ATTN_SCALE = HEAD_DIM ** -0.5
ADAM_LR, ADAM_B1, ADAM_B2, ADAM_EPS, ADAM_WD, ADAM_STEP = 0.001, 0.9, 0.999, 1e-08, 0.01, 10
VMEM_BYTES = 64 * 1024 * 1024
MESH = pl.DeviceIdType.MESH

NN = (((1,), (0,)), ((), ()))
NT = (((1,), (1,)), ((), ()))
TN = (((0,), (0,)), ((), ()))


def _dot(a, b, dims=NN):
    return lax.dot_general(a, b, dims, preferred_element_type=F32)


def _params(vmem_mb=48):
    return pltpu.CompilerParams(vmem_limit_bytes=min(vmem_mb * 1024 * 1024, VMEM_BYTES - 4 * 1024 * 1024))


def _tile(n, want):
    t = min(n, max(16, want // 16 * 16))
    while t > 16 and (n % t or t % 16):
        t -= 16
    return t if n % t == 0 else n


def _coords():
    return lax.axis_index("x"), lax.axis_index("y"), lax.axis_index("c")


def _comm_sems(n):
    return [pltpu.SemaphoreType.DMA((7 * n,)), pltpu.SemaphoreType.DMA((7 * n,)), pltpu.SemaphoreType.DMA((n,))]


def _gather_ops(ins, outs, sems):
    n = len(ins)
    send_sems, recv_sems, local_sems = sems
    x, y, c = _coords()
    me, sibling = (x, y, c), (x, y, 1 - c)
    chips = [(1 - x, y), (x, 1 - y), (1 - x, 1 - y)]

    def slot(out, p):
        return out.at[4 * p[0] + 2 * p[1] + p[2]]

    def copy(a, k, block, to, src=None):
        return pltpu.make_async_remote_copy(
            src_ref=slot(outs[a], block) if src is None else src, dst_ref=slot(outs[a], block),
            send_sem=send_sems.at[7 * a + k], recv_sem=recv_sems.at[7 * a + k],
            device_id=to, device_id_type=MESH)

    def own():
        mine = [pltpu.make_async_copy(ins[a], slot(outs[a], me), local_sems.at[a]) for a in range(n)]
        first = []
        for a in range(n):
            first.append(copy(a, 0, me, sibling, src=ins[a]))
            first += [copy(a, 1 + j, me, (*chip, c), src=ins[a]) for j, chip in enumerate(chips)]
        return mine, first

    def start():
        mine, first = own()
        for cp in mine + first:
            cp.start()

    def finish():
        mine, first = own()
        passed = []
        for j, chip in enumerate(chips):
            for a in range(n):
                copy(a, 1 + j, (*chip, c), me).wait_recv()
                fwd = copy(a, 4 + j, (*chip, c), sibling)
                fwd.start()
                passed.append(fwd)
        for a in range(n):
            copy(a, 0, sibling, me).wait_recv()
            for j, chip in enumerate(chips):
                copy(a, 4 + j, (*chip, 1 - c), me).wait_recv()
        for cp in first + passed:
            cp.wait_send()
        for cp in mine:
            cp.wait()

    return start, finish


def _exchange_ops(ins, outs, sems):
    n = len(ins)
    send_sems, recv_sems, local_sems = sems
    x, y, c = _coords()
    me = 4 * x + 2 * y + c

    def peer(k):
        return (1 - x if k & 4 else x, 1 - y if k & 2 else y, 1 - c if k & 1 else c)

    def copy(a, k):
        p = peer(k)
        return pltpu.make_async_remote_copy(
            src_ref=ins[a].at[4 * p[0] + 2 * p[1] + p[2]], dst_ref=outs[a].at[me],
            send_sem=send_sems.at[7 * a + k - 1], recv_sem=recv_sems.at[7 * a + k - 1],
            device_id=p, device_id_type=MESH)

    def landed(a, k):
        p = peer(k)
        return pltpu.make_async_remote_copy(
            src_ref=ins[a].at[me], dst_ref=outs[a].at[4 * p[0] + 2 * p[1] + p[2]],
            send_sem=send_sems.at[7 * a + k - 1], recv_sem=recv_sems.at[7 * a + k - 1],
            device_id=p, device_id_type=MESH)

    def own():
        mine = [pltpu.make_async_copy(ins[a].at[me], outs[a].at[me], local_sems.at[a]) for a in range(n)]
        return mine, [copy(a, k) for a in range(n) for k in range(1, NDEV)]

    def start():
        mine, sent = own()
        for cp in mine + sent:
            cp.start()

    def finish():
        mine, sent = own()
        for a in range(n):
            for k in range(1, NDEV):
                landed(a, k).wait_recv()
        for cp in sent:
            cp.wait_send()
        for cp in mine:
            cp.wait()

    return start, finish


_COMM = {"gather": (_gather_ops, lambda a: (NDEV,) + a.shape), "exchange": (_exchange_ops, lambda a: a.shape)}


def all_gather(arrs, name):
    n = len(arrs)

    def body(*refs):
        start, finish = _gather_ops(refs[:n], refs[n:2 * n], refs[2 * n:])
        start()
        finish()

    any_spec = pl.BlockSpec(memory_space=pl.ANY)
    return pl.pallas_call(
        body, name=name, out_shape=[jax.ShapeDtypeStruct((NDEV,) + a.shape, a.dtype) for a in arrs],
        in_specs=[any_spec] * n, out_specs=[any_spec] * n, scratch_shapes=_comm_sems(n),
    )(*arrs)


def _call(body, *, name, grid, in_specs, out_specs, out_shape, args, scratch=(), vmem=48, comm=None):
    if comm is None:
        outs = pl.pallas_call(
            body, name=name, grid=grid, in_specs=list(in_specs), out_specs=list(out_specs), out_shape=list(out_shape),
            scratch_shapes=list(scratch), compiler_params=_params(vmem))(*args)
        return list(outs), []
    kind, arrs = comm
    ops, shape_of = _COMM[kind]
    n, n_in, n_out, n_scr = len(arrs), len(in_specs), len(out_specs), len(scratch)

    def carrier(*refs):
        ins, c_in = refs[:n_in], refs[n_in:n_in + n]
        o0 = n_in + n
        outs, c_out = refs[o0:o0 + n_out], refs[o0 + n_out:o0 + n_out + n]
        s0 = o0 + n_out + n
        scr, sems = refs[s0:s0 + n_scr], refs[s0 + n_scr:]
        ids = [pl.program_id(d) for d in range(len(grid))]
        first = functools.reduce(jnp.logical_and, [i == 0 for i in ids])
        last = functools.reduce(jnp.logical_and, [i == g - 1 for i, g in zip(ids, grid)])
        start, finish = ops(c_in, c_out, sems)
        pl.when(first)(start)
        body(*ins, *outs, *scr)
        pl.when(last)(finish)

    any_spec = pl.BlockSpec(memory_space=pl.ANY)
    res = pl.pallas_call(
        carrier, name=name, grid=grid, in_specs=list(in_specs) + [any_spec] * n,
        out_specs=list(out_specs) + [any_spec] * n,
        out_shape=list(out_shape) + [jax.ShapeDtypeStruct(shape_of(a), a.dtype) for a in arrs],
        scratch_shapes=list(scratch) + _comm_sems(n), compiler_params=_params(vmem))(*args, *arrs)
    return list(res[:n_out]), list(res[n_out:])


def _rstd(x):
    return lax.rsqrt(jnp.mean(x * x, axis=-1, keepdims=True) + RMS_EPS)


def rms_fwd(x, gain, name):
    S, D = x.shape
    tm = _tile(S, 512)

    def body(x_ref, g_ref, h_ref, ht_ref):
        xv = x_ref[...]
        h = xv * _rstd(xv) * g_ref[...]
        h_ref[...] = h.astype(BF16)
        ht_ref[...] = h.T.astype(BF16)

    return pl.pallas_call(
        body, name=name, grid=(S // tm,),
        in_specs=[pl.BlockSpec((tm, D), lambda i: (i, 0)), pl.BlockSpec((1, D), lambda i: (0, 0))],
        out_specs=[pl.BlockSpec((tm, D), lambda i: (i, 0)), pl.BlockSpec((D, tm), lambda i: (0, i))],
        out_shape=[jax.ShapeDtypeStruct((S, D), BF16), jax.ShapeDtypeStruct((D, S), BF16)],
        compiler_params=_params(40),
    )(x, gain)


def _rms_bwd_math(xv, gain, dh):
    r = _rstd(xv)
    xhat = xv * r
    dxhat = dh * gain
    dx = r * (dxhat - xhat * jnp.mean(dxhat * xhat, axis=-1, keepdims=True))
    return dx, dh * xhat


def rms_bwd(x, gain, dh, g, name):
    S, D = x.shape
    tm = _tile(S, 256)

    def body(x_ref, gain_ref, dh_ref, g_ref, go_ref, gb_ref, dg_ref):
        @pl.when(pl.program_id(0) == 0)
        def _():
            dg_ref[...] = jnp.zeros_like(dg_ref)

        dx, dgain = _rms_bwd_math(x_ref[...], gain_ref[...], dh_ref[...])
        gn = g_ref[...] + dx
        go_ref[...] = gn
        gb_ref[...] = gn.astype(BF16)
        dg_ref[...] += jnp.sum(dgain, axis=0, keepdims=True)

    row = pl.BlockSpec((tm, D), lambda i: (i, 0))
    vec = pl.BlockSpec((1, D), lambda i: (0, 0))
    return pl.pallas_call(
        body, name=name, grid=(S // tm,),
        in_specs=[row, vec, row, row], out_specs=[row, row, vec],
        out_shape=[jax.ShapeDtypeStruct((S, D), F32), jax.ShapeDtypeStruct((S, D), BF16),
                   jax.ShapeDtypeStruct((1, D), F32)],
        compiler_params=_params(48),
    )(x, gain, dh, g)


def final_loss(x, gain, target, name):
    S, D = x.shape
    tm = _tile(S, 256)

    def body(x_ref, gain_ref, t_ref, loss_ref, go_ref, gb_ref, dg_ref):
        @pl.when(pl.program_id(0) == 0)
        def _():
            dg_ref[...] = jnp.zeros_like(dg_ref)
            loss_ref[...] = jnp.zeros_like(loss_ref)

        xv, gain_v = x_ref[...], gain_ref[...]
        err = xv * _rstd(xv) * gain_v - t_ref[...]
        loss_ref[...] += jnp.sum(jnp.sum(err * err, axis=-1, keepdims=True), axis=0, keepdims=True) * (0.5 / D)
        dx, dgain = _rms_bwd_math(xv, gain_v, err * (1.0 / D))
        go_ref[...] = dx
        gb_ref[...] = dx.astype(BF16)
        dg_ref[...] += jnp.sum(dgain, axis=0, keepdims=True)

    row = pl.BlockSpec((tm, D), lambda i: (i, 0))
    vec = pl.BlockSpec((1, D), lambda i: (0, 0))
    return pl.pallas_call(
        body, name=name, grid=(S // tm,),
        in_specs=[row, vec, row],
        out_specs=[pl.BlockSpec((1, LANES), lambda i: (0, 0)), row, row, vec],
        out_shape=[jax.ShapeDtypeStruct((1, LANES), F32), jax.ShapeDtypeStruct((S, D), F32),
                   jax.ShapeDtypeStruct((S, D), BF16), jax.ShapeDtypeStruct((1, D), F32)],
        compiler_params=_params(48),
    )(x, gain, target)


def _blk(arr, width, tm, stacked):
    if stacked:
        return pl.BlockSpec((None, tm, width), lambda j, i: (j, i, 0))
    return pl.BlockSpec((tm, width), lambda j, i: (i, j))


def ffn_up(h, wg, wu, name, comm=None):
    S, D = h.shape
    J, _, Fb = wg.shape
    tm = _tile(S, 1024)

    def body(h_ref, wg_ref, wu_ref, a_ref, b_ref, hid_ref):
        hv = h_ref[...]
        a = _dot(hv, wg_ref[...])
        b = _dot(hv, wu_ref[...])
        a_ref[...] = a.astype(BF16)
        b_ref[...] = b.astype(BF16)
        hid_ref[...] = (a * jax.nn.sigmoid(a) * b).astype(BF16)

    w_spec = pl.BlockSpec((None, D, Fb), lambda j, i: (j, 0, 0))
    o_spec = pl.BlockSpec((None, tm, Fb), lambda j, i: (j, i, 0))
    shp = jax.ShapeDtypeStruct((J, S, Fb), BF16)
    return _call(
        body, name=name, grid=(J, S // tm),
        in_specs=[pl.BlockSpec((tm, D), lambda j, i: (i, 0)), w_spec, w_spec],
        out_specs=[o_spec, o_spec, o_spec], out_shape=[shp, shp, shp], args=(h, wg, wu), comm=comm)


def proj_cols(h, w, name):
    S, D = h.shape
    J, _, Wb = w.shape
    tm = _tile(S, 1024)

    def body(h_ref, w_ref, z_ref):
        z_ref[...] = _dot(h_ref[...], w_ref[...]).astype(BF16)

    return pl.pallas_call(
        body, name=name, grid=(J, S // tm),
        in_specs=[pl.BlockSpec((tm, D), lambda j, i: (i, 0)), pl.BlockSpec((None, D, Wb), lambda j, i: (j, 0, 0))],
        out_specs=pl.BlockSpec((tm, Wb), lambda j, i: (i, j)),
        out_shape=jax.ShapeDtypeStruct((S, J * Wb), BF16), compiler_params=_params(48),
    )(h, w)


def ffn_down(hid, wd, x, name, comm=None):
    J, S, Fb = hid.shape
    D = wd.shape[2]
    tm = _tile(S, 256)

    def body(hid_ref, wd_ref, x_ref, o_ref):
        acc = _dot(hid_ref[0], wd_ref[0])
        for j in range(1, J):
            acc = acc + _dot(hid_ref[j], wd_ref[j])
        o_ref[...] = x_ref[...] + FFN_RESIDUAL * acc

    row = pl.BlockSpec((tm, D), lambda i: (i, 0))
    (xo,), got = _call(
        body, name=name, grid=(S // tm,),
        in_specs=[pl.BlockSpec((J, tm, Fb), lambda i: (0, i, 0)),
                  pl.BlockSpec((J, Fb, D), lambda i: (0, 0, 0), pipeline_mode=pl.Buffered(1)), row],
        out_specs=[row], out_shape=[jax.ShapeDtypeStruct((S, D), F32)], args=(hid, wd, x), vmem=56, comm=comm)
    return xo, got


def ffn_dhid(gb, wd, a, b, name):
    S, D = gb.shape
    J, Fb, _ = wd.shape
    tm = _tile(S, 1024)

    parts = 4 if tm % 64 == 0 else 1

    def body(g_ref, wd_ref, a_ref, b_ref, da_ref, db_ref):
        wdv = wd_ref[...]
        for part in range(parts):
            rows = pl.ds(part * (tm // parts), tm // parts)
            dhid = _dot(g_ref[rows, :], wdv, NT) * FFN_RESIDUAL
            a = a_ref[rows, :].astype(F32)
            b = b_ref[rows, :].astype(F32)
            sg = jax.nn.sigmoid(a)
            da_ref[rows, :] = (dhid * b * (sg * (1.0 + a * (1.0 - sg)))).astype(BF16)
            db_ref[rows, :] = (dhid * (a * sg)).astype(BF16)

    o_spec = pl.BlockSpec((None, tm, Fb), lambda j, i: (j, i, 0))
    shp = jax.ShapeDtypeStruct((J, S, Fb), BF16)
    return pl.pallas_call(
        body, name=name, grid=(J, S // tm),
        in_specs=[pl.BlockSpec((tm, D), lambda j, i: (i, 0)), pl.BlockSpec((None, Fb, D), lambda j, i: (j, 0, 0)),
                  o_spec, o_spec],
        out_specs=[o_spec, o_spec], out_shape=[shp, shp], compiler_params=_params(48),
    )(gb, wd, a, b)


def back_proj(ds, ws, stacked, name, comm=None):
    n = len(ds)
    J, D, Wb = ws[0].shape
    S = ds[0].shape[1] if stacked else ds[0].shape[0]
    kb = J if n == 1 else 2
    nj = J // kb
    tm = _tile(S, 256 if n == 1 else 512)

    def body(*refs):
        d_refs, w_refs, o_ref = refs[:n], refs[n:2 * n], refs[2 * n]
        acc = None
        for d_ref, w_ref in zip(d_refs, w_refs):
            for k in range(kb):
                dk = d_ref[k] if stacked else d_ref[:, k * Wb:(k + 1) * Wb]
                t = _dot(dk, w_ref[k], NT)
                acc = t if acc is None else acc + t
        if nj == 1:
            o_ref[...] = acc
        else:
            acc_ref = refs[2 * n + 1]
            j = pl.program_id(1)

            @pl.when(j == 0)
            def _():
                acc_ref[...] = acc

            @pl.when((j > 0) & (j < nj - 1))
            def _():
                acc_ref[...] += acc

            @pl.when(j == nj - 1)
            def _():
                o_ref[...] = acc_ref[...] + acc

    if stacked:
        d_spec = pl.BlockSpec((kb, tm, Wb), lambda i, j: (j, i, 0))
    else:
        d_spec = pl.BlockSpec((tm, kb * Wb), lambda i, j: (i, j))
    w_spec = pl.BlockSpec((kb, D, Wb), lambda i, j: (j, 0, 0), pipeline_mode=pl.Buffered(1) if nj == 1 else None)
    (dh,), got = _call(
        body, name=name, grid=(S // tm, nj),
        in_specs=[d_spec] * n + [w_spec] * n,
        out_specs=[pl.BlockSpec((tm, D), lambda i, j: (i, 0))], out_shape=[jax.ShapeDtypeStruct((S, D), F32)],
        scratch=[] if nj == 1 else [pltpu.VMEM((tm, D), F32)], args=(*ds, *ws), vmem=56, comm=comm)
    return dh, got


def grad_lhs(ht, d, stacked, name, comm=None):
    D, S = ht.shape
    if stacked:
        J, _, Wb = d.shape
    else:
        J, Wb = NDEV, d.shape[1] // NDEV
    tk = _tile(S, 2048)
    nk = S // tk

    def body(ht_ref, d_ref, o_ref, acc_ref):
        i = pl.program_id(1)

        @pl.when(i == 0)
        def _():
            acc_ref[...] = jnp.zeros_like(acc_ref)

        acc_ref[...] += _dot(ht_ref[...], d_ref[...])

        @pl.when(i == nk - 1)
        def _():
            o_ref[...] = acc_ref[...].astype(BF16)

    (dw,), got = _call(
        body, name=name, grid=(J, nk),
        in_specs=[pl.BlockSpec((D, tk), lambda j, i: (0, i)), _blk(d, Wb, tk, stacked)],
        out_specs=[pl.BlockSpec((None, D, Wb), lambda j, i: (j, 0, 0))],
        out_shape=[jax.ShapeDtypeStruct((J, D, Wb), BF16)],
        scratch=[pltpu.VMEM((D, Wb), F32)], args=(ht, d), vmem=56, comm=comm)
    return dw, got


def grad_shared_rhs(a, gb, nblk, scale, name):
    S, D = gb.shape
    stacked = nblk is None
    if stacked:
        J, _, Wb = a.shape
    else:
        J, Wb = nblk, a.shape[1] // nblk
    tk = _tile(S, 2048 if Wb <= 768 else 1024)
    nk = S // tk

    def body(a_ref, g_ref, o_ref, acc_ref):
        i = pl.program_id(1)

        @pl.when(i == 0)
        def _():
            acc_ref[...] = jnp.zeros_like(acc_ref)

        acc_ref[...] += _dot(a_ref[...], g_ref[...], TN)

        @pl.when(i == nk - 1)
        def _():
            o_ref[...] = (acc_ref[...] * scale).astype(BF16)

    return pl.pallas_call(
        body, name=name, grid=(J, nk),
        in_specs=[_blk(a, Wb, tk, stacked), pl.BlockSpec((tk, D), lambda j, i: (i, 0))],
        out_specs=pl.BlockSpec((None, Wb, D), lambda j, i: (j, 0, 0)),
        out_shape=jax.ShapeDtypeStruct((J, Wb, D), BF16),
        scratch_shapes=[pltpu.VMEM((Wb, D), F32)], compiler_params=_params(56),
    )(a, gb)


def out_proj(ya, ycp, wo, x, name):
    S, A = ya.shape
    Dm, D = wo.shape
    tm = _tile(S, 512)

    def body(ya_ref, ycp_ref, wo_ref, x_ref, o_ref):
        acc = _dot(ya_ref[...], wo_ref[0:A, :]) + _dot(ycp_ref[...], wo_ref[A:Dm, :])
        o_ref[...] = x_ref[...] + acc

    row = pl.BlockSpec((tm, D), lambda i: (i, 0))
    return pl.pallas_call(
        body, name=name, grid=(S // tm,),
        in_specs=[pl.BlockSpec((tm, A), lambda i: (i, 0)), pl.BlockSpec((tm, Dm - A), lambda i: (i, 0)),
                  pl.BlockSpec((Dm, D), lambda i: (0, 0)), row],
        out_specs=row, out_shape=jax.ShapeDtypeStruct((S, D), F32), compiler_params=_params(48),
    )(ya, ycp, wo, x)


def out_proj_bwd(gb, wo, A, name):
    S, D = gb.shape
    Dm = wo.shape[0]
    tm = _tile(S, 512)

    def body(g_ref, wo_ref, dya_ref, dycp_ref):
        gv = g_ref[...]
        dya_ref[...] = _dot(gv, wo_ref[0:A, :], NT).astype(BF16)
        dycp_ref[...] = _dot(gv, wo_ref[A:Dm, :], NT).astype(BF16)

    return pl.pallas_call(
        body, name=name, grid=(S // tm,),
        in_specs=[pl.BlockSpec((tm, D), lambda i: (i, 0)), pl.BlockSpec((Dm, D), lambda i: (0, 0))],
        out_specs=[pl.BlockSpec((tm, A), lambda i: (i, 0)), pl.BlockSpec((tm, Dm - A), lambda i: (i, 0))],
        out_shape=[jax.ShapeDtypeStruct((S, A), BF16), jax.ShapeDtypeStruct((S, Dm - A), BF16)],
        compiler_params=_params(48),
    )(gb, wo)


def _lane_stat(v, hi):
    lane = lax.broadcasted_iota(jnp.int32, v.shape, 1)
    sel = (lane >= HEAD_DIM) if hi else (lane < HEAD_DIM)
    return jnp.max(jnp.where(sel, v, -jnp.inf), axis=-1, keepdims=True)


def band_fwd(q, k, v, L, name):
    H, S, W = q.shape
    T = min(512, L)
    nb = T // BLK

    def body(q_ref, k_ref, kp_ref, v_ref, vp_ref, o_ref):
        i = pl.program_id(1)
        first_key = jnp.where((i * T) % L != 0, 0, BLK)
        qi = lax.broadcasted_iota(jnp.int32, (BLK, 2 * BLK), 0)
        kj = lax.broadcasted_iota(jnp.int32, (BLK, 2 * BLK), 1)
        band = (kj >= qi) & (kj <= qi + BLK)
        lane = lax.broadcasted_iota(jnp.int32, (BLK, W), 1)
        for b in range(nb):
            rows = slice(b * BLK, (b + 1) * BLK)
            if b == 0:
                kw = jnp.concatenate([kp_ref[...], k_ref[rows, :]], axis=0)
                vw = jnp.concatenate([vp_ref[...], v_ref[rows, :]], axis=0)
                mask = band & (kj >= first_key)
            else:
                kw = k_ref[(b - 1) * BLK:(b + 1) * BLK, :]
                vw = v_ref[(b - 1) * BLK:(b + 1) * BLK, :]
                mask = band
            s = jnp.where(mask, _dot(q_ref[rows, :], kw, NT) * ATTN_SCALE, NEG_INF)
            m = jnp.max(s, axis=-1, keepdims=True)
            p = jnp.exp(s - m)
            l = jnp.sum(p, axis=-1, keepdims=True)
            o = _dot(p.astype(BF16), vw)
            o_ref[rows, :] = jnp.where(lane < HEAD_DIM, o / l, m + jnp.log(l))

    cur = pl.BlockSpec((None, T, W), lambda h, i: (h, i, 0))
    prev = pl.BlockSpec((None, BLK, W), lambda h, i: (h, jnp.maximum(i * nb - 1, 0), 0))
    return pl.pallas_call(
        body, name=name, grid=(H, S // T),
        in_specs=[cur, cur, prev, cur, prev], out_specs=cur,
        out_shape=jax.ShapeDtypeStruct((H, S, W), F32), compiler_params=_params(32),
    )(q, k, k, v, v)


def band_bwd(q, k, v, do, st, L, name, comm=None):
    H, S, W = q.shape
    T = min(512, L)
    nb = T // BLK
    last_blk = S // BLK - 1

    def body(q_ref, qn_ref, do_ref, don_ref, st_ref, stn_ref, k_ref, kp_ref, v_ref, vp_ref,
             dq_ref, dk_ref, dv_ref, dk_acc, dv_acc):
        i = pl.program_id(1)
        first_key = jnp.where((i * T) % L != 0, 0, BLK)
        next_off = jnp.where(((i + 1) * T) % L != 0, 0, 4 * BLK)
        qi = lax.broadcasted_iota(jnp.int32, (BLK, 2 * BLK), 0)
        kj = lax.broadcasted_iota(jnp.int32, (BLK, 2 * BLK), 1)
        band = (kj >= qi) & (kj <= qi + BLK)
        qi1 = lax.broadcasted_iota(jnp.int32, (BLK, BLK), 0)
        kj1 = lax.broadcasted_iota(jnp.int32, (BLK, BLK), 1)
        dk_acc[...] = jnp.zeros_like(dk_acc)
        dv_acc[...] = jnp.zeros_like(dv_acc)
        for b in range(nb + 1):
            rows = slice(b * BLK, (b + 1) * BLK)
            if b < nb:
                qb, dob, stb = q_ref[rows, :], do_ref[rows, :], st_ref[rows, :]
            else:
                qb, dob, stb = qn_ref[...], don_ref[...], stn_ref[...]
            if b == 0:
                kw = jnp.concatenate([kp_ref[...], k_ref[rows, :]], axis=0)
                vw = jnp.concatenate([vp_ref[...], v_ref[rows, :]], axis=0)
                mask = band & (kj >= first_key)
            elif b < nb:
                kw = k_ref[(b - 1) * BLK:(b + 1) * BLK, :]
                vw = v_ref[(b - 1) * BLK:(b + 1) * BLK, :]
                mask = band
            else:
                kw = k_ref[(nb - 1) * BLK:nb * BLK, :]
                vw = v_ref[(nb - 1) * BLK:nb * BLK, :]
                mask = kj1 >= qi1 + next_off
            s = _dot(qb, kw, NT) * ATTN_SCALE
            p = jnp.where(mask, jnp.exp(s - _lane_stat(stb, False)), 0.0)
            dp = _dot(dob, vw, NT)
            ds = (p * (dp - _lane_stat(stb, True))).astype(BF16)
            if b < nb:
                dq_ref[rows, :] = _dot(ds, kw) * ATTN_SCALE
            win = slice(b * BLK, b * BLK + kw.shape[0])
            dk_acc[win, :] += _dot(ds, qb, TN) * ATTN_SCALE
            dv_acc[win, :] += _dot(p.astype(BF16), dob, TN)
        dk_ref[...] = dk_acc[BLK:, :]
        dv_ref[...] = dv_acc[BLK:, :]

    cur = pl.BlockSpec((None, T, W), lambda h, i: (h, i, 0))
    prev = pl.BlockSpec((None, BLK, W), lambda h, i: (h, jnp.maximum(i * nb - 1, 0), 0))
    nxt = pl.BlockSpec((None, BLK, W), lambda h, i: (h, jnp.minimum((i + 1) * nb, last_blk), 0))
    shp = jax.ShapeDtypeStruct((H, S, W), F32)
    return _call(
        body, name=name, grid=(H, S // T),
        in_specs=[cur, nxt, cur, nxt, cur, nxt, cur, prev, cur, prev],
        out_specs=[cur, cur, cur], out_shape=[shp, shp, shp],
        scratch=[pltpu.VMEM((T + BLK, W), F32), pltpu.VMEM((T + BLK, W), F32)],
        args=(q, q, do, do, st, st, k, k, v, v), vmem=32, comm=comm)


def attn_merge(os_, name):
    H, S, W = os_[0].shape
    tm = _tile(S, 1024)

    def body(*refs):
        o_ref = refs[-1]
        vals = [r[...] for r in refs[:-1]]
        lses = [_lane_stat(v, True) for v in vals]
        m = functools.reduce(jnp.maximum, lses)
        ws = [jnp.exp(l - m) for l in lses]
        tot = functools.reduce(jnp.add, ws)
        out = functools.reduce(jnp.add, [(w / tot) * v for w, v in zip(ws, vals)])
        lane = lax.broadcasted_iota(jnp.int32, out.shape, 1)
        o_ref[...] = jnp.where(lane < HEAD_DIM, out, m + jnp.log(tot))

    spec = pl.BlockSpec((None, tm, W), lambda h, i: (h, i, 0))
    return pl.pallas_call(
        body, name=name, grid=(H, S // tm), in_specs=[spec] * len(os_), out_specs=spec,
        out_shape=jax.ShapeDtypeStruct((H, S, W), F32), compiler_params=_params(32),
    )(*os_)


def attn_bwd_stats(merged, do, name):
    H, S, W = merged.shape
    tm = _tile(S, 1024)

    def body(m_ref, do_ref, st_ref):
        mv = m_ref[...]
        delta = jnp.sum(mv * do_ref[...].astype(F32), axis=-1, keepdims=True)
        lane = lax.broadcasted_iota(jnp.int32, mv.shape, 1)
        st_ref[...] = jnp.where(lane < HEAD_DIM, _lane_stat(mv, True), delta)

    spec = pl.BlockSpec((None, tm, W), lambda h, i: (h, i, 0))
    return pl.pallas_call(
        body, name=name, grid=(H, S // tm), in_specs=[spec, spec], out_specs=spec,
        out_shape=jax.ShapeDtypeStruct((H, S, W), F32), compiler_params=_params(32),
    )(merged, do)


def sum_patterns(ts, name):
    H, S, W = ts[0].shape
    tm = _tile(S, 1024)

    def body(*refs):
        refs[-1][...] = functools.reduce(jnp.add, [r[...] for r in refs[:-1]]).astype(BF16)

    spec = pl.BlockSpec((None, tm, W), lambda h, i: (h, i, 0))
    return pl.pallas_call(
        body, name=name, grid=(H, S // tm), in_specs=[spec] * len(ts), out_specs=spec,
        out_shape=jax.ShapeDtypeStruct((H, S, W), BF16), compiler_params=_params(32),
    )(*ts)


def to_heads(t):
    S = t.shape[0]
    t = jnp.transpose(t.reshape(S, -1, HEAD_DIM), (1, 0, 2))
    return jnp.pad(t, ((0, 0), (0, 0), (0, LANES - HEAD_DIM)))


def from_heads(t):
    H, S, _ = t.shape
    return jnp.transpose(t[:, :, :HEAD_DIM], (1, 0, 2)).reshape(S, H * HEAD_DIM)


def permute(t, d):
    if d == 1:
        return t
    H, S, W = t.shape
    return jnp.transpose(t.reshape(H, S // d, d, W), (0, 2, 1, 3)).reshape(H, S, W)


def unpermute(t, d):
    if d == 1:
        return t
    H, S, W = t.shape
    return jnp.transpose(t.reshape(H, d, S // d, W), (0, 2, 1, 3)).reshape(H, S, W)


ATT_TILE = BLK * max(DILATIONS)


def _head0(shape):
    return lax.broadcasted_iota(jnp.int32, shape, 1) < HEAD_DIM


def _pair_col(v, sel):
    return jnp.max(jnp.where(sel, v, -jnp.inf), axis=-1, keepdims=True)


def _band(lo, hi):
    qi = lax.broadcasted_iota(jnp.int32, (BLK, 2 * BLK), 0)
    kj = lax.broadcasted_iota(jnp.int32, (BLK, 2 * BLK), 1)
    return (kj >= qi) & (kj <= qi + BLK) & (kj >= lo) & (kj < hi)


def _deinterleave(dst, src, d, rows, dst_stride, dst_off=0, src_off=0, cast=None):
    for r in range(d):
        v = src[pl.ds(src_off + r, rows, stride=d), :] if d > 1 else src[pl.ds(src_off, rows), :]
        dst[pl.ds(r * dst_stride + dst_off, rows), :] = v if cast is None else v.astype(cast)


def _interleave(dst, src, d, rows, src_stride, src_off=0, add=False):
    for r in range(d):
        v = src[pl.ds(r * src_stride + src_off, rows), :]
        idx = pl.ds(r, rows, stride=d) if d > 1 else pl.ds(0, rows)
        dst[idx, :] = dst[idx, :] + v if add else v


def attn_fwd(z, A, name, comm=None):
    S = z.shape[0]
    T = ATT_TILE
    Hp, nt = A // LANES, S // T
    np_ = len(DILATIONS)

    def body(q_ref, k_ref, kp_ref, v_ref, vp_ref, y_ref, lse_ref, qn, kn, vn, qp, kp, vp, accp, mp, lp, *nat):
        accs, ms, ls = nat[:np_], nat[np_:2 * np_], nat[2 * np_:]
        i = pl.program_id(1)
        qn[...] = q_ref[...].astype(F32)
        kn[0:T, :] = kp_ref[...].astype(F32)
        kn[T:2 * T, :] = k_ref[...].astype(F32)
        vn[0:T, :] = vp_ref[...].astype(F32)
        vn[T:2 * T, :] = v_ref[...].astype(F32)
        h0 = _head0((BLK, LANES))
        for pi, d in enumerate(DILATIONS):
            Ld = T // d
            nblk = Ld // BLK
            _deinterleave(qp, qn, d, Ld, Ld, cast=BF16)
            for src, dst in ((kn, kp), (vn, vp)):
                _deinterleave(dst, src, d, Ld, 2 * Ld, cast=BF16)
                _deinterleave(dst, src, d, Ld, 2 * Ld, dst_off=Ld, src_off=T, cast=BF16)

            def unit(u, carry, nblk=nblk):
                r, n = u // nblk, u % nblk
                q0 = pl.multiple_of(u * BLK, BLK)
                k0 = pl.multiple_of((2 * r * nblk + nblk + n - 1) * BLK, BLK)
                qb = qp[pl.ds(q0, BLK), :]
                kw, vw = kp[pl.ds(k0, 2 * BLK), :], vp[pl.ds(k0, 2 * BLK), :]
                mask = _band(jnp.where((i == 0) & (n == 0), BLK, 0), 2 * BLK)
                res = []
                for sel in (h0, ~h0):
                    s = jnp.where(mask, _dot(jnp.where(sel, qb, jnp.zeros_like(qb)), kw, NT) * ATTN_SCALE, NEG_INF)
                    mx = jnp.max(s, axis=-1, keepdims=True)
                    p = jnp.exp(s - mx)
                    res.append((mx, jnp.sum(p, axis=-1, keepdims=True), _dot(p.astype(BF16), vw)))
                (m0, l0, o0), (m1, l1, o1) = res
                accp[pl.ds(q0, BLK), :] = jnp.where(h0, o0, o1)
                mp[pl.ds(q0, BLK), :] = jnp.where(h0, m0, m1)
                lp[pl.ds(q0, BLK), :] = jnp.where(h0, l0, l1)
                return carry

            lax.fori_loop(0, T // BLK, unit, 0, unroll=4)
            for src, dst in ((accp, accs[pi]), (mp, ms[pi]), (lp, ls[pi])):
                _interleave(dst, src, d, Ld, Ld)
        mv = [m[...] for m in ms]
        mx = mv[0]
        for m in mv[1:]:
            mx = jnp.maximum(mx, m)
        ws = [jnp.exp(m - mx) for m in mv]
        l = sum(w * lr[...] for w, lr in zip(ws, ls))
        a = sum(w * ar[...] for w, ar in zip(ws, accs))
        y_ref[...] = (a / l).astype(BF16)
        lse_ref[...] = mx + jnp.log(l)

    kb, vb = A // LANES, 2 * A // LANES
    cur = lambda off: pl.BlockSpec((T, LANES), lambda h, i: (i, off + h))
    prev = lambda off: pl.BlockSpec((T, LANES), lambda h, i: (jnp.maximum(i - 1, 0), off + h))
    out = pl.BlockSpec((T, LANES), lambda h, i: (i, h))
    vm = lambda rows, dt: pltpu.VMEM((rows, LANES), dt)
    (y, lse), got = _call(
        body, name=name, grid=(Hp, nt),
        in_specs=[cur(0), cur(kb), prev(kb), cur(vb), prev(vb)], out_specs=[out, out],
        out_shape=[jax.ShapeDtypeStruct((S, A), BF16), jax.ShapeDtypeStruct((S, A), F32)],
        scratch=[vm(T, F32), vm(2 * T, F32), vm(2 * T, F32), vm(T, BF16), vm(2 * T, BF16), vm(2 * T, BF16),
                 vm(T, F32), vm(T, F32), vm(T, F32)] + [vm(T, F32)] * (3 * np_),
        args=(z, z, z, z, z), vmem=48, comm=comm)
    return y, lse, got


def attn_bwd(z, dy, y, lse, A, name, comm=None):
    S = z.shape[0]
    T = ATT_TILE
    Hp, nt = A // LANES, S // T

    def body(q_ref, qx_ref, k_ref, kp_ref, v_ref, vp_ref, do_ref, dox_ref, y_ref, yx_ref, ls_ref, lsx_ref,
             dq_ref, dk_ref, dv_ref, qn, don, dln, kn, vn, qp, dop, lsp, dlp, kp, vp, dqp, dkp, dvp, dqa, dka, dva):
        i = pl.program_id(1)
        h0t = _head0((T, LANES))
        for half, (qr, dor, yr) in enumerate(((q_ref, do_ref, y_ref), (qx_ref, dox_ref, yx_ref))):
            rows = pl.ds(half * T, T)
            dof = dor[...].astype(F32)
            prod = dof * yr[...].astype(F32)
            d0 = jnp.sum(jnp.where(h0t, prod, 0.0), axis=-1, keepdims=True)
            d1 = jnp.sum(jnp.where(h0t, 0.0, prod), axis=-1, keepdims=True)
            qn[rows, :] = qr[...].astype(F32)
            don[rows, :] = dof
            dln[rows, :] = jnp.where(h0t, d0, d1)
        kn[0:T, :] = kp_ref[...].astype(F32)
        kn[T:2 * T, :] = k_ref[...].astype(F32)
        vn[0:T, :] = vp_ref[...].astype(F32)
        vn[T:2 * T, :] = v_ref[...].astype(F32)
        h0 = _head0((BLK, LANES))
        for pi, d in enumerate(DILATIONS):
            Ld = T // d
            nblk = Ld // BLK
            for src, dst, cast in ((qn, qp, BF16), (don, dop, BF16), (dln, dlp, None)):
                _deinterleave(dst, src, d, Ld, Ld, cast=cast)
                _deinterleave(dst, src, d, BLK, BLK, dst_off=T, src_off=T, cast=cast)
            _deinterleave(lsp, ls_ref, d, Ld, Ld)
            _deinterleave(lsp, lsx_ref, d, BLK, BLK, dst_off=T)
            for src, dst in ((kn, kp), (vn, vp)):
                _deinterleave(dst, src, d, Ld, 2 * Ld, cast=BF16)
                _deinterleave(dst, src, d, Ld, 2 * Ld, dst_off=Ld, src_off=T, cast=BF16)
            dkp[...] = jnp.zeros_like(dkp)
            dvp[...] = jnp.zeros_like(dvp)

            def grads(q0, k0, nkeys, mask, with_dq):
                qb, dob = qp[pl.ds(q0, BLK), :], dop[pl.ds(q0, BLK), :]
                lsb, dlb = lsp[pl.ds(q0, BLK), :], dlp[pl.ds(q0, BLK), :]
                kw, vw = kp[pl.ds(k0, nkeys), :], vp[pl.ds(k0, nkeys), :]
                dq, dk, dv = None, None, None
                for sel in (h0, ~h0):
                    qh = jnp.where(sel, qb, jnp.zeros_like(qb))
                    doh = jnp.where(sel, dob, jnp.zeros_like(dob))
                    s = _dot(qh, kw, NT) * ATTN_SCALE
                    p = jnp.where(mask, jnp.exp(s - _pair_col(lsb, sel)), 0.0)
                    ds = (p * (_dot(doh, vw, NT) - _pair_col(dlb, sel))).astype(BF16)
                    dkh = _dot(ds, qh, TN) * ATTN_SCALE
                    dvh = _dot(p.astype(BF16), doh, TN)
                    dk = dkh if dk is None else dk + dkh
                    dv = dvh if dv is None else dv + dvh
                    if with_dq:
                        dqh = _dot(ds, kw) * ATTN_SCALE
                        dq = dqh if dq is None else jnp.where(h0, dq, dqh)
                if with_dq:
                    dqp[pl.ds(q0, BLK), :] = dq
                dkp[pl.ds(k0, nkeys), :] += dk
                dvp[pl.ds(k0, nkeys), :] += dv

            def own(u, carry, nblk=nblk):
                r, b = u // nblk, u % nblk
                grads(pl.multiple_of(u * BLK, BLK), pl.multiple_of((2 * r * nblk + nblk + b - 1) * BLK, BLK),
                      2 * BLK, _band(jnp.where((i == 0) & (b == 0), BLK, 0), 2 * BLK), True)
                return carry

            def from_next(r, carry, nblk=nblk):
                qi = lax.broadcasted_iota(jnp.int32, (BLK, BLK), 0)
                kj = lax.broadcasted_iota(jnp.int32, (BLK, BLK), 1)
                grads(pl.multiple_of(T + r * BLK, BLK), pl.multiple_of((2 * r * nblk + 2 * nblk - 1) * BLK, BLK),
                      BLK, kj >= qi + jnp.where(i == nt - 1, 2 * BLK, 0), False)
                return carry

            lax.fori_loop(0, T // BLK, own, 0, unroll=4)
            lax.fori_loop(0, d, from_next, 0, unroll=min(d, 4))
            _interleave(dqa, dqp, d, Ld, Ld, add=pi > 0)
            _interleave(dka, dkp, d, Ld, 2 * Ld, src_off=Ld, add=pi > 0)
            _interleave(dva, dvp, d, Ld, 2 * Ld, src_off=Ld, add=pi > 0)
        dq_ref[...] = dqa[...].astype(BF16)
        dk_ref[...] = dka[...].astype(BF16)
        dv_ref[...] = dva[...].astype(BF16)

    kb, vb = A // LANES, 2 * A // LANES
    cur = lambda off: pl.BlockSpec((T, LANES), lambda h, i: (i, off + h))
    prev = lambda off: pl.BlockSpec((T, LANES), lambda h, i: (jnp.maximum(i - 1, 0), off + h))
    nxt = lambda off: pl.BlockSpec((T, LANES), lambda h, i: (jnp.minimum(i + 1, nt - 1), off + h))
    vm = lambda rows, dt: pltpu.VMEM((rows, LANES), dt)
    shp = jax.ShapeDtypeStruct((S, A), BF16)
    return _call(
        body, name=name, grid=(Hp, nt),
        in_specs=[cur(0), nxt(0), cur(kb), prev(kb), cur(vb), prev(vb), cur(0), nxt(0), cur(0), nxt(0), cur(0), nxt(0)],
        out_specs=[cur(0)] * 3, out_shape=[shp] * 3,
        scratch=[vm(2 * T, F32), vm(2 * T, F32), vm(2 * T, F32), vm(2 * T, F32), vm(2 * T, F32),
                 vm(2 * T, BF16), vm(2 * T, BF16), vm(2 * T, F32), vm(2 * T, F32), vm(2 * T, BF16), vm(2 * T, BF16),
                 vm(T, F32), vm(2 * T, F32), vm(2 * T, F32), vm(T, F32), vm(T, F32), vm(T, F32)],
        args=(z, z, z, z, z, z, dy, dy, y, y, lse, lse), vmem=56, comm=comm)


def _shift_down(e, k):
    return pltpu.roll(e, k, 0)


def _shift_up(e, k):
    return pltpu.roll(e, e.shape[0] - k, 0)


def _causal_sums(e, g):
    for lvl in range(g + 1):
        e = e + _shift_down(e, 1 << lvl)
    return e


def _anticausal_sums(e, g):
    for lvl in range(g + 1):
        e = e + _shift_up(e, 1 << lvl)
    return e


def _mixer_specs(S, tm, A, C, P):
    assert (3 * A) % C == 0 and (3 * A + 3 * C) % P == 0 and tm % HALO == 0
    cb, pb = 3 * A // C, (3 * A + 3 * C) // P
    hb = tm // HALO
    last = S // HALO - 1
    col = lambda w, j: pl.BlockSpec((tm, w), lambda i: (i, j))
    prev = lambda w, j: pl.BlockSpec((HALO, w), lambda i: (jnp.maximum(i * hb - 1, 0), j))
    nxt = lambda w, j: pl.BlockSpec((HALO, w), lambda i: (jnp.minimum((i + 1) * hb, last), j))
    return cb, pb, col, prev, nxt


def _conv_taps(gc, ci, gch, cih, keep_prev):
    u = gc * ci
    e = jnp.concatenate([gch * cih * keep_prev, u], axis=0)
    return u, _shift_down(e, 1)[HALO:], _shift_down(e, 2)[HALO:]


def _pooled(xp, xph, keep_prev, pos, g):
    cols = slice(g * POOL_GROUP, (g + 1) * POOL_GROUP)
    x = xp[:, cols]
    e = jnp.concatenate([xph[:, cols] * keep_prev, x], axis=0)
    cnt = jnp.minimum(pos + 1, POOL_WINDOWS[g]).astype(F32)
    return _causal_sums(e, g)[HALO:] / cnt - x


def convpool_fwd(z, conv_w, pool_w, pool_scale, A, C, P, name):
    S = z.shape[0]
    tm = _tile(S, 512)
    cb, pb, col, prev, _ = _mixer_specs(S, tm, A, C, P)

    def body(gb_ref, gc_ref, ci_ref, xp_ref, gch_ref, cih_ref, xph_ref, cw_ref, pw_ref, ps_ref, y_ref):
        i = pl.program_id(0)
        keep_prev = jnp.where(i == 0, 0.0, 1.0)
        f = lambda r: r[...].astype(F32)
        u, u1, u2 = _conv_taps(f(gc_ref), f(ci_ref), f(gch_ref), f(cih_ref), keep_prev)
        cw = cw_ref[...]
        y_ref[:, 0:C] = (f(gb_ref) * (cw[0:1] * u2 + cw[1:2] * u1 + cw[2:3] * u)).astype(BF16)
        xp, xph = f(xp_ref), f(xph_ref)
        pos = i * tm + lax.broadcasted_iota(jnp.int32, (tm, 1), 0)
        for g in range(len(POOL_WINDOWS)):
            cols = slice(g * POOL_GROUP, (g + 1) * POOL_GROUP)
            lin = _dot(_pooled(xp, xph, keep_prev, pos, g).astype(BF16), pw_ref[g])
            y_ref[:, C + g * POOL_GROUP:C + (g + 1) * POOL_GROUP] = (lin * ps_ref[:, cols]).astype(BF16)

    full = lambda shape: pl.BlockSpec(shape, lambda i: (0,) * len(shape))
    return pl.pallas_call(
        body, name=name, grid=(S // tm,),
        in_specs=[col(C, cb), col(C, cb + 1), col(C, cb + 2), col(P, pb),
                  prev(C, cb + 1), prev(C, cb + 2), prev(P, pb),
                  full(conv_w.shape), full(pool_w.shape), full(pool_scale.shape)],
        out_specs=pl.BlockSpec((tm, C + P), lambda i: (i, 0)),
        out_shape=jax.ShapeDtypeStruct((S, C + P), BF16), compiler_params=_params(48),
    )(z, z, z, z, z, z, z, conv_w, pool_w, pool_scale)


def convpool_bwd(z, dycp, conv_w, pool_w, pool_scale, A, C, P, name):
    S = z.shape[0]
    tm = _tile(S, 512)
    nt = S // tm
    cb, pb, col, prev, nxt = _mixer_specs(S, tm, A, C, P)
    NG = len(POOL_WINDOWS)

    def body(gb_ref, gc_ref, ci_ref, xp_ref, gch_ref, cih_ref, xph_ref, gbn_ref, dy_ref, dyn_ref,
             cw_ref, pw_ref, ps_ref, dz_ref, dcw_ref, dpw_ref, dps_ref):
        i = pl.program_id(0)

        @pl.when(i == 0)
        def _():
            dcw_ref[...] = jnp.zeros_like(dcw_ref)
            dpw_ref[...] = jnp.zeros_like(dpw_ref)
            dps_ref[...] = jnp.zeros_like(dps_ref)

        keep_prev = jnp.where(i == 0, 0.0, 1.0)
        keep_next = jnp.where(i == nt - 1, 0.0, 1.0)
        f = lambda r: r[...].astype(F32)
        gb, gc, ci = f(gb_ref), f(gc_ref), f(ci_ref)
        u, u1, u2 = _conv_taps(gc, ci, f(gch_ref), f(cih_ref), keep_prev)
        cw = cw_ref[...]
        dy, dyn = f(dy_ref), f(dyn_ref) * keep_next
        dyc = dy[:, 0:C]
        dz_ref[:, 0:C] = (dyc * (cw[0:1] * u2 + cw[1:2] * u1 + cw[2:3] * u)).astype(BF16)
        dc = dyc * gb
        e = jnp.concatenate([dc, dyn[:, 0:C] * f(gbn_ref)], axis=0)
        du = cw[2:3] * dc + cw[1:2] * _shift_up(e, 1)[:tm] + cw[0:1] * _shift_up(e, 2)[:tm]
        dz_ref[:, C:2 * C] = (du * ci).astype(BF16)
        dz_ref[:, 2 * C:3 * C] = (du * gc).astype(BF16)
        dcw_ref[0:1, :] += jnp.sum(dc * u2, axis=0, keepdims=True)
        dcw_ref[1:2, :] += jnp.sum(dc * u1, axis=0, keepdims=True)
        dcw_ref[2:3, :] += jnp.sum(dc * u, axis=0, keepdims=True)

        xp, xph = f(xp_ref), f(xph_ref)
        pos = i * tm + lax.broadcasted_iota(jnp.int32, (tm, 1), 0)
        pos_e = i * tm + lax.broadcasted_iota(jnp.int32, (tm + HALO, 1), 0)
        for g in range(NG):
            cols = slice(g * POOL_GROUP, (g + 1) * POOL_GROUP)
            ycols = slice(C + g * POOL_GROUP, C + (g + 1) * POOL_GROUP)
            pooled = _pooled(xp, xph, keep_prev, pos, g).astype(BF16)
            pw = pw_ref[g]
            dyp = dy[:, ycols]
            dps_ref[:, cols] += jnp.sum(dyp * _dot(pooled, pw), axis=0, keepdims=True)
            dlin = (jnp.concatenate([dyp, dyn[:, ycols]], axis=0) * ps_ref[:, cols]).astype(BF16)
            dpw_ref[g] += _dot(pooled, dlin[:tm], TN)
            dpool = _dot(dlin, pw, NT)
            r = dpool / jnp.minimum(pos_e + 1, POOL_WINDOWS[g]).astype(F32)
            dz_ref[:, 3 * C + g * POOL_GROUP:3 * C + (g + 1) * POOL_GROUP] = (
                _anticausal_sums(r, g)[:tm] - dpool[:tm]).astype(BF16)

    full = lambda shape: pl.BlockSpec(shape, lambda i: (0,) * len(shape))
    return pl.pallas_call(
        body, name=name, grid=(nt,),
        in_specs=[col(C, cb), col(C, cb + 1), col(C, cb + 2), col(P, pb),
                  prev(C, cb + 1), prev(C, cb + 2), prev(P, pb), nxt(C, cb),
                  pl.BlockSpec((tm, C + P), lambda i: (i, 0)),
                  pl.BlockSpec((HALO, C + P), lambda i: (jnp.minimum((i + 1) * (tm // HALO), S // HALO - 1), 0)),
                  full(conv_w.shape), full(pool_w.shape), full(pool_scale.shape)],
        out_specs=[pl.BlockSpec((tm, 3 * C + P), lambda i: (i, 0)),
                   full(conv_w.shape), full(pool_w.shape), full(pool_scale.shape)],
        out_shape=[jax.ShapeDtypeStruct((S, 3 * C + P), BF16), jax.ShapeDtypeStruct(conv_w.shape, F32),
                   jax.ShapeDtypeStruct(pool_w.shape, F32), jax.ShapeDtypeStruct(pool_scale.shape, F32)],
        compiler_params=_params(48),
    )(z, z, z, z, z, z, z, z, dycp, dycp, conv_w, pool_w, pool_scale)


def adamw(lands, w, m, v, name):
    R, C = w.shape
    nl = len(lands)
    n, Rl, _ = lands[0].shape
    assert Rl * nl == R
    tr = _tile(Rl, 128 * 1024 // C)
    nb = Rl // tr
    c1 = 1.0 - ADAM_B1 ** ADAM_STEP
    c2 = 1.0 - ADAM_B2 ** ADAM_STEP

    def body(*refs):
        land_refs = refs[:nl]
        w_ref, m_ref, v_ref, g_ref, d_ref, mo_ref, vo_ref = refs[nl:]
        i = pl.program_id(0)
        for a, land_ref in enumerate(land_refs):
            @pl.when((i >= a * nb) & (i < (a + 1) * nb))
            def _():
                g = land_ref[0].astype(F32)
                for s in range(1, n):
                    g = g + land_ref[s].astype(F32)
                g_ref[...] = g
                mn = ADAM_B1 * m_ref[...] + (1.0 - ADAM_B1) * g
                vn = ADAM_B2 * v_ref[...] + (1.0 - ADAM_B2) * (g * g)
                mo_ref[...] = mn
                vo_ref[...] = vn
                d_ref[...] = -ADAM_LR * ((mn / c1) / (jnp.sqrt(vn / c2) + ADAM_EPS) + ADAM_WD * w_ref[...])

    land_specs = [pl.BlockSpec((n, tr, C), lambda i, a=a: (0, jnp.clip(i - a * nb, 0, nb - 1), 0))
                  for a in range(nl)]
    row = pl.BlockSpec((tr, C), lambda i: (i, 0))
    shp = jax.ShapeDtypeStruct((R, C), F32)
    return pl.pallas_call(
        body, name=name, grid=(nl * nb,),
        in_specs=land_specs + [row, row, row], out_specs=[row] * 4, out_shape=[shp] * 4,
        compiler_params=_params(48),
    )(*lands, w, m, v)


def _gather_comm(arrs):
    return ("gather", arrs) if arrs else None


def _ffn_fwd(x, gain, wg, wu, wd, tag, carry_up=(), carry_down=()):
    h, ht = rms_fwd(x, gain, f"rms_{tag}")
    (a, b, hid), got_up = ffn_up(h, wg, wu, f"ffn_up_{tag}", _gather_comm(list(carry_up)))
    xo, got_down = ffn_down(hid, wd, x, f"ffn_down_{tag}", _gather_comm(list(carry_down)))
    return xo, (x, ht, a, b, hid), got_up, got_down


def _ffn_bwd(g, gb, saved, gain, wg, wu, wd, tag):
    x, ht, a, b, hid = saved
    da, db = ffn_dhid(gb, wd, a, b, f"ffn_dhid_{tag}")
    dwd = grad_shared_rhs(hid, gb, None, FFN_RESIDUAL, f"ffn_dwd_{tag}")
    dwg, (land_wd,) = grad_lhs(ht, da, True, f"ffn_dwg_{tag}", ("exchange", [dwd]))
    dwu, (land_wg,) = grad_lhs(ht, db, True, f"ffn_dwu_{tag}", ("exchange", [dwg]))
    dh, (land_wu,) = back_proj([da, db], [wg, wu], True, f"ffn_dh_{tag}", ("exchange", [dwu]))
    g, gb, dgain = rms_bwd(x, gain, dh, g, f"rms_bwd_{tag}")
    return g, gb, dgain, (land_wg, land_wu, land_wd)


def _mixer_fwd(x, gain, w_in, conv_w, pool_w, pool_scale, w_out, dims, tag, carry=()):
    A, C, P = dims
    h, ht = rms_fwd(x, gain, f"rms_{tag}")
    z = proj_cols(h, w_in, f"in_proj_{tag}")
    ya, lse, got = attn_fwd(z, A, f"attn_fwd_{tag}", _gather_comm(list(carry)))
    ycp = convpool_fwd(z, conv_w, pool_w, pool_scale, A, C, P, f"convpool_{tag}")
    xo = out_proj(ya, ycp, w_out, x, f"out_proj_{tag}")
    return xo, (x, ht, z, lse, ya, ycp), got


def _mixer_bwd(g, gb, saved, gain, w_in, conv_w, pool_w, pool_scale, w_out, dims, tag):
    A, C, P = dims
    x, ht, z, lse, ya, ycp = saved
    S, D = x.shape
    dya, dycp = out_proj_bwd(gb, w_out, A, f"out_proj_bwd_{tag}")
    dwo_a = grad_shared_rhs(ya, gb, 1, 1.0, f"dwo_attn_{tag}")
    dwo_cp = grad_shared_rhs(ycp, gb, 1, 1.0, f"dwo_cp_{tag}")
    dwo = jnp.concatenate([dwo_a[0], dwo_cp[0]], axis=0).reshape(NDEV, -1, D)
    dqkv, (land_wo,) = attn_bwd(z, dya, ya, lse, A, f"attn_bwd_{tag}", ("exchange", [dwo]))
    dz_cp, dcw, dpw, dps = convpool_bwd(z, dycp, conv_w, pool_w, pool_scale, A, C, P, f"convpool_bwd_{tag}")
    dz = jnp.concatenate(dqkv + [dz_cp], axis=1)
    dwin, _ = grad_lhs(ht, dz, False, f"dwin_{tag}")
    dh, (land_win,) = back_proj([dz], [w_in], False, f"in_proj_bwd_{tag}", ("exchange", [dwin]))
    g, gb, dgain = rms_bwd(x, gain, dh, g, f"rms_bwd_{tag}")
    return g, gb, (dgain, dcw, dpw, dps), (land_win, land_wo)


def _pack(arrs):
    flat = [a.reshape(-1).astype(F32) for a in arrs]
    spans, off = [], 0
    for a in flat:
        spans.append((off, a.shape[0]))
        off += a.shape[0]
    rows = -(-off // (8 * LANES)) * 8
    buf = jnp.concatenate(flat + [jnp.zeros((rows * LANES - off,), F32)]).reshape(rows, LANES)
    return buf, spans


def kernel(x, ffn1_norm, ffn1_w_gate, ffn1_w_up, ffn1_w_down, mix_norm, w_in, conv_w, pool_w, pool_scale, w_out, ffn2_norm, ffn2_w_gate, ffn2_w_up, ffn2_w_down, final_norm, loss_target, m_ffn1_norm, m_ffn1_w_gate, m_ffn1_w_up, m_ffn1_w_down, m_mix_norm, m_w_in, m_conv_w, m_pool_w, m_pool_scale, m_w_out, m_ffn2_norm, m_ffn2_w_gate, m_ffn2_w_up, m_ffn2_w_down, m_final_norm, v_ffn1_norm, v_ffn1_w_gate, v_ffn1_w_up, v_ffn1_w_down, v_mix_norm, v_w_in, v_conv_w, v_pool_w, v_pool_scale, v_w_out, v_ffn2_norm, v_ffn2_w_gate, v_ffn2_w_up, v_ffn2_w_down, v_final_norm):
    depth = ffn1_norm.shape[0]
    S, D = x.shape[1], x.shape[2]
    Cb = conv_w.shape[2]
    C = Cb * NDEV
    P = pool_scale.shape[1]
    A = (w_in.shape[2] * NDEV - 3 * C - P) // 3
    dims = (A, C, P)
    me = 4 * lax.axis_index("x") + 2 * lax.axis_index("y") + lax.axis_index("c")

    big = dict(ffn1_w_gate=ffn1_w_gate, ffn1_w_up=ffn1_w_up, ffn1_w_down=ffn1_w_down, w_in=w_in, w_out=w_out,
               ffn2_w_gate=ffn2_w_gate, ffn2_w_up=ffn2_w_up, ffn2_w_down=ffn2_w_down)
    big_m = dict(ffn1_w_gate=m_ffn1_w_gate, ffn1_w_up=m_ffn1_w_up, ffn1_w_down=m_ffn1_w_down, w_in=m_w_in,
                 w_out=m_w_out, ffn2_w_gate=m_ffn2_w_gate, ffn2_w_up=m_ffn2_w_up, ffn2_w_down=m_ffn2_w_down)
    big_v = dict(ffn1_w_gate=v_ffn1_w_gate, ffn1_w_up=v_ffn1_w_up, ffn1_w_down=v_ffn1_w_down, w_in=v_w_in,
                 w_out=v_w_out, ffn2_w_gate=v_ffn2_w_gate, ffn2_w_up=v_ffn2_w_up, ffn2_w_down=v_ffn2_w_down)
    names = list(big)

    first_names = ["ffn1_w_gate", "ffn1_w_up"]
    ffn1_names = first_names + ["ffn1_w_down"]
    mix_names = ["w_in", "w_out"]
    ffn2_names = ["ffn2_w_gate", "ffn2_w_up", "ffn2_w_down"]
    shards = lambda l, ns: [big[n][l].astype(BF16) for n in ns]
    full = [dict() for _ in range(depth)]
    full[0].update(zip(first_names, all_gather(shards(0, first_names), "gather_first")))
    conv_full = all_gather([jnp.pad(conv_w.reshape(-1, Cb), ((0, 2), (0, LANES - Cb)))], "gather_conv_w")[0]
    conv_full = jnp.transpose(conv_full[:, :depth * 3, :Cb].reshape(NDEV, depth, 3, Cb), (1, 2, 0, 3)).reshape(depth, 3, C)
    pool_w_bf = pool_w.astype(BF16)

    xs = x[0]
    saved = []
    for l in range(depth):
        W = full[l]
        up1 = ["ffn1_w_down"] + mix_names if l == 0 else []
        h, ht = rms_fwd(xs, ffn1_norm[l:l + 1], f"rms_f1_l{l}")
        (a, b, hid), got_up = ffn_up(h, W["ffn1_w_gate"], W["ffn1_w_up"], f"ffn_up_f1_l{l}",
                                     _gather_comm(shards(l, up1)))
        W.update(zip(up1, got_up))
        xo, _ = ffn_down(hid, W["ffn1_w_down"], xs, f"ffn_down_f1_l{l}")
        s1, xs = (xs, ht, a, b, hid), xo
        xs, s2, got = _mixer_fwd(xs, mix_norm[l:l + 1], W["w_in"], conv_full[l], pool_w_bf[l], pool_scale[l:l + 1],
                                 W["w_out"].reshape(-1, D), dims, f"mix_l{l}", shards(l, ffn2_names))
        W.update(zip(ffn2_names, got))
        up2, down2 = (ffn1_names, mix_names) if l + 1 < depth else ([], [])
        xs, s3, got_up, got_down = _ffn_fwd(xs, ffn2_norm[l:l + 1], W["ffn2_w_gate"], W["ffn2_w_up"],
                                            W["ffn2_w_down"], f"f2_l{l}", shards(l + 1, up2) if up2 else (),
                                            shards(l + 1, down2) if down2 else ())
        if l + 1 < depth:
            full[l + 1].update(zip(up2, got_up))
            full[l + 1].update(zip(down2, got_down))
        saved.append((s1, s2, s3))
    loss_part, g, gb, d_final = final_loss(xs, final_norm.reshape(1, D), loss_target[0], "final_loss")

    small = [None] * depth
    lands = [None] * depth
    for l in reversed(range(depth)):
        W = full[l]
        s1, s2, s3 = saved[l]
        g, gb, dn2, (lg2, lu2, ld2) = _ffn_bwd(g, gb, s3, ffn2_norm[l:l + 1], W["ffn2_w_gate"], W["ffn2_w_up"],
                                               W["ffn2_w_down"], f"f2_l{l}")
        g, gb, (dnm, dcw, dpw, dps), (lwin, lwo) = _mixer_bwd(
            g, gb, s2, mix_norm[l:l + 1], W["w_in"], conv_full[l], pool_w_bf[l], pool_scale[l:l + 1],
            W["w_out"].reshape(-1, D), dims, f"mix_l{l}")
        g, gb, dn1, (lg1, lu1, ld1) = _ffn_bwd(g, gb, s1, ffn1_norm[l:l + 1], W["ffn1_w_gate"], W["ffn1_w_up"],
                                               W["ffn1_w_down"], f"f1_l{l}")
        lands[l] = dict(ffn1_w_gate=lg1, ffn1_w_up=lu1, ffn1_w_down=ld1, w_in=lwin, w_out=lwo,
                        ffn2_w_gate=lg2, ffn2_w_up=lu2, ffn2_w_down=ld2)
        small[l] = (dn1, dnm, dcw, dpw, dps, dn2)

    res = {}
    for n in names:
        shp = big[n].shape
        two = lambda t: t.reshape(-1, shp[-1])
        outs = adamw([lands[l][n] for l in range(depth)], two(big[n]), two(big_m[n]), two(big_v[n]), f"adamw_{n}")
        res[n] = [o.reshape(shp) for o in outs]

    st = lambda i: jnp.stack([small[l][i] for l in range(depth)])
    small_g = dict(ffn1_norm=st(0), mix_norm=st(1), conv_w=st(2), pool_w=st(3), pool_scale=st(4), ffn2_norm=st(5),
                   final_norm=d_final)
    small_names = list(small_g)
    zeros_conv = jnp.zeros((depth, 3, C), F32)
    place = lambda t: lax.dynamic_update_slice(zeros_conv, t, (0, 0, me * Cb))
    small_w = dict(ffn1_norm=ffn1_norm, mix_norm=mix_norm, conv_w=place(conv_w), pool_w=pool_w, pool_scale=pool_scale,
                   ffn2_norm=ffn2_norm, final_norm=final_norm)
    small_m = dict(ffn1_norm=m_ffn1_norm, mix_norm=m_mix_norm, conv_w=place(m_conv_w), pool_w=m_pool_w,
                   pool_scale=m_pool_scale, ffn2_norm=m_ffn2_norm, final_norm=m_final_norm)
    small_v = dict(ffn1_norm=v_ffn1_norm, mix_norm=v_mix_norm, conv_w=place(v_conv_w), pool_w=v_pool_w,
                   pool_scale=v_pool_scale, ffn2_norm=v_ffn2_norm, final_norm=v_final_norm)
    gbuf, spans = _pack([small_g[n] for n in small_names] + [loss_part])
    wbuf, _ = _pack([small_w[n] for n in small_names] + [jnp.zeros((1, LANES), F32)])
    mbuf, _ = _pack([small_m[n] for n in small_names] + [jnp.zeros((1, LANES), F32)])
    vbuf, _ = _pack([small_v[n] for n in small_names] + [jnp.zeros((1, LANES), F32)])
    gathered = all_gather([gbuf], "gather_small_grads")[0]
    outs = adamw([gathered], wbuf, mbuf, vbuf, "adamw_small")
    for n, (off, size) in zip(small_names, spans):
        shp = small_w[n].shape
        vals = [o.reshape(-1)[off:off + size].reshape(shp) for o in outs]
        if n == "conv_w":
            vals = [lax.dynamic_slice(t, (0, 0, me * Cb), (depth, 3, Cb)) for t in vals]
        res[n] = vals
    loss = outs[0].reshape(-1)[spans[-1][0]]

    order = ["ffn1_norm", "ffn1_w_gate", "ffn1_w_up", "ffn1_w_down", "mix_norm", "w_in", "conv_w", "pool_w",
             "pool_scale", "w_out", "ffn2_norm", "ffn2_w_gate", "ffn2_w_up", "ffn2_w_down", "final_norm"]
    return (loss, g[None], *[res[n][0] for n in order], *[res[n][1] for n in order],
            *[res[n][2] for n in order], *[res[n][3] for n in order])
```

```python
import functools

import jax
import jax.numpy as jnp
from jax import lax
from jax.experimental import pallas as pl
from jax.experimental.pallas import tpu as pltpu

F32 = jnp.float32
BF16 = jnp.bfloat16
NDEV = 8
HEAD_DIM = 64
LANES = 128
BLK = 128
DILATIONS = (1, 4, 16)
POOL_WINDOWS = (2, 4, 8, 16)
POOL_GROUP = 128
HALO = 16
FFN_RESIDUAL = 0.5
RMS_EPS = 1e-6
NEG_INF = -1e30
ATTN_SCALE = HEAD_DIM ** -0.5
ADAM_LR, ADAM_B1, ADAM_B2, ADAM_EPS, ADAM_WD, ADAM_STEP = 0.001, 0.9, 0.999, 1e-08, 0.01, 10
VMEM_BYTES = 64 * 1024 * 1024
MESH = pl.DeviceIdType.MESH

NN = (((1,), (0,)), ((), ()))
NT = (((1,), (1,)), ((), ()))
TN = (((0,), (0,)), ((), ()))


def _dot(a, b, dims=NN):
    return lax.dot_general(a, b, dims, preferred_element_type=F32)


def _params(vmem_mb=48):
    return pltpu.CompilerParams(vmem_limit_bytes=min(vmem_mb * 1024 * 1024, VMEM_BYTES - 4 * 1024 * 1024))


def _tile(n, want):
    t = min(n, max(16, want // 16 * 16))
    while t > 16 and (n % t or t % 16):
        t -= 16
    return t if n % t == 0 else n


def _coords():
    return lax.axis_index("x"), lax.axis_index("y"), lax.axis_index("c")


def _comm_sems(n):
    return [pltpu.SemaphoreType.DMA((7 * n,)), pltpu.SemaphoreType.DMA((7 * n,)), pltpu.SemaphoreType.DMA((n,))]


def _gather_ops(ins, outs, sems):
    n = len(ins)
    send_sems, recv_sems, local_sems = sems
    x, y, c = _coords()
    me, sibling = (x, y, c), (x, y, 1 - c)
    chips = [(1 - x, y), (x, 1 - y), (1 - x, 1 - y)]

    def slot(out, p):
        return out.at[4 * p[0] + 2 * p[1] + p[2]]

    def copy(a, k, block, to, src=None):
        return pltpu.make_async_remote_copy(
            src_ref=slot(outs[a], block) if src is None else src, dst_ref=slot(outs[a], block),
            send_sem=send_sems.at[7 * a + k], recv_sem=recv_sems.at[7 * a + k],
            device_id=to, device_id_type=MESH)

    def own():
        mine = [pltpu.make_async_copy(ins[a], slot(outs[a], me), local_sems.at[a]) for a in range(n)]
        first = []
        for a in range(n):
            first.append(copy(a, 0, me, sibling, src=ins[a]))
            first += [copy(a, 1 + j, me, (*chip, c), src=ins[a]) for j, chip in enumerate(chips)]
        return mine, first

    def start():
        mine, first = own()
        for cp in mine + first:
            cp.start()

    def finish():
        mine, first = own()
        passed = []
        for j, chip in enumerate(chips):
            for a in range(n):
                copy(a, 1 + j, (*chip, c), me).wait_recv()
                fwd = copy(a, 4 + j, (*chip, c), sibling)
                fwd.start()
                passed.append(fwd)
        for a in range(n):
            copy(a, 0, sibling, me).wait_recv()
            for j, chip in enumerate(chips):
                copy(a, 4 + j, (*chip, 1 - c), me).wait_recv()
        for cp in first + passed:
            cp.wait_send()
        for cp in mine:
            cp.wait()

    return start, finish


def _exchange_ops(ins, outs, sems):
    n = len(ins)
    send_sems, recv_sems, local_sems = sems
    x, y, c = _coords()
    me = 4 * x + 2 * y + c

    def peer(k):
        return (1 - x if k & 4 else x, 1 - y if k & 2 else y, 1 - c if k & 1 else c)

    def copy(a, k):
        p = peer(k)
        return pltpu.make_async_remote_copy(
            src_ref=ins[a].at[4 * p[0] + 2 * p[1] + p[2]], dst_ref=outs[a].at[me],
            send_sem=send_sems.at[7 * a + k - 1], recv_sem=recv_sems.at[7 * a + k - 1],
            device_id=p, device_id_type=MESH)

    def landed(a, k):
        p = peer(k)
        return pltpu.make_async_remote_copy(
            src_ref=ins[a].at[me], dst_ref=outs[a].at[4 * p[0] + 2 * p[1] + p[2]],
            send_sem=send_sems.at[7 * a + k - 1], recv_sem=recv_sems.at[7 * a + k - 1],
            device_id=p, device_id_type=MESH)

    def own():
        mine = [pltpu.make_async_copy(ins[a].at[me], outs[a].at[me], local_sems.at[a]) for a in range(n)]
        return mine, [copy(a, k) for a in range(n) for k in range(1, NDEV)]

    def start():
        mine, sent = own()
        for cp in mine + sent:
            cp.start()

    def finish():
        mine, sent = own()
        for a in range(n):
            for k in range(1, NDEV):
                landed(a, k).wait_recv()
        for cp in sent:
            cp.wait_send()
        for cp in mine:
            cp.wait()

    return start, finish


_COMM = {"gather": (_gather_ops, lambda a: (NDEV,) + a.shape), "exchange": (_exchange_ops, lambda a: a.shape)}


def all_gather(arrs, name):
    n = len(arrs)

    def body(*refs):
        start, finish = _gather_ops(refs[:n], refs[n:2 * n], refs[2 * n:])
        start()
        finish()

    any_spec = pl.BlockSpec(memory_space=pl.ANY)
    return pl.pallas_call(
        body, name=name, out_shape=[jax.ShapeDtypeStruct((NDEV,) + a.shape, a.dtype) for a in arrs],
        in_specs=[any_spec] * n, out_specs=[any_spec] * n, scratch_shapes=_comm_sems(n),
    )(*arrs)


def _call(body, *, name, grid, in_specs, out_specs, out_shape, args, scratch=(), vmem=48, comm=None):
    if comm is None:
        outs = pl.pallas_call(
            body, name=name, grid=grid, in_specs=list(in_specs), out_specs=list(out_specs), out_shape=list(out_shape),
            scratch_shapes=list(scratch), compiler_params=_params(vmem))(*args)
        return list(outs), []
    kind, arrs = comm
    ops, shape_of = _COMM[kind]
    n, n_in, n_out, n_scr = len(arrs), len(in_specs), len(out_specs), len(scratch)

    def carrier(*refs):
        ins, c_in = refs[:n_in], refs[n_in:n_in + n]
        o0 = n_in + n
        outs, c_out = refs[o0:o0 + n_out], refs[o0 + n_out:o0 + n_out + n]
        s0 = o0 + n_out + n
        scr, sems = refs[s0:s0 + n_scr], refs[s0 + n_scr:]
        ids = [pl.program_id(d) for d in range(len(grid))]
        first = functools.reduce(jnp.logical_and, [i == 0 for i in ids])
        last = functools.reduce(jnp.logical_and, [i == g - 1 for i, g in zip(ids, grid)])
        start, finish = ops(c_in, c_out, sems)
        pl.when(first)(start)
        body(*ins, *outs, *scr)
        pl.when(last)(finish)

    any_spec = pl.BlockSpec(memory_space=pl.ANY)
    res = pl.pallas_call(
        carrier, name=name, grid=grid, in_specs=list(in_specs) + [any_spec] * n,
        out_specs=list(out_specs) + [any_spec] * n,
        out_shape=list(out_shape) + [jax.ShapeDtypeStruct(shape_of(a), a.dtype) for a in arrs],
        scratch_shapes=list(scratch) + _comm_sems(n), compiler_params=_params(vmem))(*args, *arrs)
    return list(res[:n_out]), list(res[n_out:])


def _rstd(x):
    return lax.rsqrt(jnp.mean(x * x, axis=-1, keepdims=True) + RMS_EPS)


def rms_fwd(x, gain, name):
    S, D = x.shape
    tm = _tile(S, 512)

    def body(x_ref, g_ref, h_ref, ht_ref):
        xv = x_ref[...]
        h = xv * _rstd(xv) * g_ref[...]
        h_ref[...] = h.astype(BF16)
        ht_ref[...] = h.T.astype(BF16)

    return pl.pallas_call(
        body, name=name, grid=(S // tm,),
        in_specs=[pl.BlockSpec((tm, D), lambda i: (i, 0)), pl.BlockSpec((1, D), lambda i: (0, 0))],
        out_specs=[pl.BlockSpec((tm, D), lambda i: (i, 0)), pl.BlockSpec((D, tm), lambda i: (0, i))],
        out_shape=[jax.ShapeDtypeStruct((S, D), BF16), jax.ShapeDtypeStruct((D, S), BF16)],
        compiler_params=_params(40),
    )(x, gain)


def _rms_bwd_math(xv, gain, dh):
    r = _rstd(xv)
    xhat = xv * r
    dxhat = dh * gain
    dx = r * (dxhat - xhat * jnp.mean(dxhat * xhat, axis=-1, keepdims=True))
    return dx, dh * xhat


def rms_bwd(x, gain, dh, g, name):
    S, D = x.shape
    tm = _tile(S, 256)

    def body(x_ref, gain_ref, dh_ref, g_ref, go_ref, gb_ref, dg_ref):
        @pl.when(pl.program_id(0) == 0)
        def _():
            dg_ref[...] = jnp.zeros_like(dg_ref)

        dx, dgain = _rms_bwd_math(x_ref[...], gain_ref[...], dh_ref[...])
        gn = g_ref[...] + dx
        go_ref[...] = gn
        gb_ref[...] = gn.astype(BF16)
        dg_ref[...] += jnp.sum(dgain, axis=0, keepdims=True)

    row = pl.BlockSpec((tm, D), lambda i: (i, 0))
    vec = pl.BlockSpec((1, D), lambda i: (0, 0))
    return pl.pallas_call(
        body, name=name, grid=(S // tm,),
        in_specs=[row, vec, row, row], out_specs=[row, row, vec],
        out_shape=[jax.ShapeDtypeStruct((S, D), F32), jax.ShapeDtypeStruct((S, D), BF16),
                   jax.ShapeDtypeStruct((1, D), F32)],
        compiler_params=_params(48),
    )(x, gain, dh, g)


def final_loss(x, gain, target, name):
    S, D = x.shape
    tm = _tile(S, 256)

    def body(x_ref, gain_ref, t_ref, loss_ref, go_ref, gb_ref, dg_ref):
        @pl.when(pl.program_id(0) == 0)
        def _():
            dg_ref[...] = jnp.zeros_like(dg_ref)
            loss_ref[...] = jnp.zeros_like(loss_ref)

        xv, gain_v = x_ref[...], gain_ref[...]
        err = xv * _rstd(xv) * gain_v - t_ref[...]
        loss_ref[...] += jnp.sum(jnp.sum(err * err, axis=-1, keepdims=True), axis=0, keepdims=True) * (0.5 / D)
        dx, dgain = _rms_bwd_math(xv, gain_v, err * (1.0 / D))
        go_ref[...] = dx
        gb_ref[...] = dx.astype(BF16)
        dg_ref[...] += jnp.sum(dgain, axis=0, keepdims=True)

    row = pl.BlockSpec((tm, D), lambda i: (i, 0))
    vec = pl.BlockSpec((1, D), lambda i: (0, 0))
    return pl.pallas_call(
        body, name=name, grid=(S // tm,),
        in_specs=[row, vec, row],
        out_specs=[pl.BlockSpec((1, LANES), lambda i: (0, 0)), row, row, vec],
        out_shape=[jax.ShapeDtypeStruct((1, LANES), F32), jax.ShapeDtypeStruct((S, D), F32),
                   jax.ShapeDtypeStruct((S, D), BF16), jax.ShapeDtypeStruct((1, D), F32)],
        compiler_params=_params(48),
    )(x, gain, target)


def _blk(arr, width, tm, stacked):
    if stacked:
        return pl.BlockSpec((None, tm, width), lambda j, i: (j, i, 0))
    return pl.BlockSpec((tm, width), lambda j, i: (i, j))


def ffn_up(h, wg, wu, name, comm=None):
    S, D = h.shape
    J, _, Fb = wg.shape
    tm = _tile(S, 1024)

    def body(h_ref, wg_ref, wu_ref, a_ref, b_ref, hid_ref):
        hv = h_ref[...]
        a = _dot(hv, wg_ref[...])
        b = _dot(hv, wu_ref[...])
        a_ref[...] = a.astype(BF16)
        b_ref[...] = b.astype(BF16)
        hid_ref[...] = (a * jax.nn.sigmoid(a) * b).astype(BF16)

    w_spec = pl.BlockSpec((None, D, Fb), lambda j, i: (j, 0, 0))
    o_spec = pl.BlockSpec((None, tm, Fb), lambda j, i: (j, i, 0))
    shp = jax.ShapeDtypeStruct((J, S, Fb), BF16)
    return _call(
        body, name=name, grid=(J, S // tm),
        in_specs=[pl.BlockSpec((tm, D), lambda j, i: (i, 0)), w_spec, w_spec],
        out_specs=[o_spec, o_spec, o_spec], out_shape=[shp, shp, shp], args=(h, wg, wu), comm=comm)


def proj_cols(h, w, name):
    S, D = h.shape
    J, _, Wb = w.shape
    tm = _tile(S, 1024)

    def body(h_ref, w_ref, z_ref):
        z_ref[...] = _dot(h_ref[...], w_ref[...]).astype(BF16)

    return pl.pallas_call(
        body, name=name, grid=(J, S // tm),
        in_specs=[pl.BlockSpec((tm, D), lambda j, i: (i, 0)), pl.BlockSpec((None, D, Wb), lambda j, i: (j, 0, 0))],
        out_specs=pl.BlockSpec((tm, Wb), lambda j, i: (i, j)),
        out_shape=jax.ShapeDtypeStruct((S, J * Wb), BF16), compiler_params=_params(48),
    )(h, w)


def ffn_down(hid, wd, x, name, comm=None):
    J, S, Fb = hid.shape
    D = wd.shape[2]
    tm = _tile(S, 256)

    def body(hid_ref, wd_ref, x_ref, o_ref):
        acc = _dot(hid_ref[0], wd_ref[0])
        for j in range(1, J):
            acc = acc + _dot(hid_ref[j], wd_ref[j])
        o_ref[...] = x_ref[...] + FFN_RESIDUAL * acc

    row = pl.BlockSpec((tm, D), lambda i: (i, 0))
    (xo,), got = _call(
        body, name=name, grid=(S // tm,),
        in_specs=[pl.BlockSpec((J, tm, Fb), lambda i: (0, i, 0)),
                  pl.BlockSpec((J, Fb, D), lambda i: (0, 0, 0), pipeline_mode=pl.Buffered(1)), row],
        out_specs=[row], out_shape=[jax.ShapeDtypeStruct((S, D), F32)], args=(hid, wd, x), vmem=56, comm=comm)
    return xo, got


def ffn_dhid(gb, wd, a, b, name):
    S, D = gb.shape
    J, Fb, _ = wd.shape
    tm = _tile(S, 1024)

    parts = 4 if tm % 64 == 0 else 1

    def body(g_ref, wd_ref, a_ref, b_ref, da_ref, db_ref):
        wdv = wd_ref[...]
        for part in range(parts):
            rows = pl.ds(part * (tm // parts), tm // parts)
            dhid = _dot(g_ref[rows, :], wdv, NT) * FFN_RESIDUAL
            a = a_ref[rows, :].astype(F32)
            b = b_ref[rows, :].astype(F32)
            sg = jax.nn.sigmoid(a)
            da_ref[rows, :] = (dhid * b * (sg * (1.0 + a * (1.0 - sg)))).astype(BF16)
            db_ref[rows, :] = (dhid * (a * sg)).astype(BF16)

    o_spec = pl.BlockSpec((None, tm, Fb), lambda j, i: (j, i, 0))
    shp = jax.ShapeDtypeStruct((J, S, Fb), BF16)
    return pl.pallas_call(
        body, name=name, grid=(J, S // tm),
        in_specs=[pl.BlockSpec((tm, D), lambda j, i: (i, 0)), pl.BlockSpec((None, Fb, D), lambda j, i: (j, 0, 0)),
                  o_spec, o_spec],
        out_specs=[o_spec, o_spec], out_shape=[shp, shp], compiler_params=_params(48),
    )(gb, wd, a, b)


def back_proj(ds, ws, stacked, name, comm=None):
    n = len(ds)
    J, D, Wb = ws[0].shape
    S = ds[0].shape[1] if stacked else ds[0].shape[0]
    kb = J if n == 1 else 2
    nj = J // kb
    tm = _tile(S, 256 if n == 1 else 512)

    def body(*refs):
        d_refs, w_refs, o_ref = refs[:n], refs[n:2 * n], refs[2 * n]
        acc = None
        for d_ref, w_ref in zip(d_refs, w_refs):
            for k in range(kb):
                dk = d_ref[k] if stacked else d_ref[:, k * Wb:(k + 1) * Wb]
                t = _dot(dk, w_ref[k], NT)
                acc = t if acc is None else acc + t
        if nj == 1:
            o_ref[...] = acc
        else:
            acc_ref = refs[2 * n + 1]
            j = pl.program_id(1)

            @pl.when(j == 0)
            def _():
                acc_ref[...] = acc

            @pl.when((j > 0) & (j < nj - 1))
            def _():
                acc_ref[...] += acc

            @pl.when(j == nj - 1)
            def _():
                o_ref[...] = acc_ref[...] + acc

    if stacked:
        d_spec = pl.BlockSpec((kb, tm, Wb), lambda i, j: (j, i, 0))
    else:
        d_spec = pl.BlockSpec((tm, kb * Wb), lambda i, j: (i, j))
    w_spec = pl.BlockSpec((kb, D, Wb), lambda i, j: (j, 0, 0), pipeline_mode=pl.Buffered(1) if nj == 1 else None)
    (dh,), got = _call(
        body, name=name, grid=(S // tm, nj),
        in_specs=[d_spec] * n + [w_spec] * n,
        out_specs=[pl.BlockSpec((tm, D), lambda i, j: (i, 0))], out_shape=[jax.ShapeDtypeStruct((S, D), F32)],
        scratch=[] if nj == 1 else [pltpu.VMEM((tm, D), F32)], args=(*ds, *ws), vmem=56, comm=comm)
    return dh, got


def grad_lhs(ht, d, stacked, name, comm=None):
    D, S = ht.shape
    if stacked:
        J, _, Wb = d.shape
    else:
        J, Wb = NDEV, d.shape[1] // NDEV
    tk = _tile(S, 2048)
    nk = S // tk

    def body(ht_ref, d_ref, o_ref, acc_ref):
        i = pl.program_id(1)

        @pl.when(i == 0)
        def _():
            acc_ref[...] = jnp.zeros_like(acc_ref)

        acc_ref[...] += _dot(ht_ref[...], d_ref[...])

        @pl.when(i == nk - 1)
        def _():
            o_ref[...] = acc_ref[...].astype(BF16)

    (dw,), got = _call(
        body, name=name, grid=(J, nk),
        in_specs=[pl.BlockSpec((D, tk), lambda j, i: (0, i)), _blk(d, Wb, tk, stacked)],
        out_specs=[pl.BlockSpec((None, D, Wb), lambda j, i: (j, 0, 0))],
        out_shape=[jax.ShapeDtypeStruct((J, D, Wb), BF16)],
        scratch=[pltpu.VMEM((D, Wb), F32)], args=(ht, d), vmem=56, comm=comm)
    return dw, got


def grad_shared_rhs(a, gb, nblk, scale, name):
    S, D = gb.shape
    stacked = nblk is None
    if stacked:
        J, _, Wb = a.shape
    else:
        J, Wb = nblk, a.shape[1] // nblk
    tk = _tile(S, 2048 if Wb <= 768 else 1024)
    nk = S // tk

    def body(a_ref, g_ref, o_ref, acc_ref):
        i = pl.program_id(1)

        @pl.when(i == 0)
        def _():
            acc_ref[...] = jnp.zeros_like(acc_ref)

        acc_ref[...] += _dot(a_ref[...], g_ref[...], TN)

        @pl.when(i == nk - 1)
        def _():
            o_ref[...] = (acc_ref[...] * scale).astype(BF16)

    return pl.pallas_call(
        body, name=name, grid=(J, nk),
        in_specs=[_blk(a, Wb, tk, stacked), pl.BlockSpec((tk, D), lambda j, i: (i, 0))],
        out_specs=pl.BlockSpec((None, Wb, D), lambda j, i: (j, 0, 0)),
        out_shape=jax.ShapeDtypeStruct((J, Wb, D), BF16),
        scratch_shapes=[pltpu.VMEM((Wb, D), F32)], compiler_params=_params(56),
    )(a, gb)


def out_proj(ya, ycp, wo, x, name):
    S, A = ya.shape
    Dm, D = wo.shape
    tm = _tile(S, 512)

    def body(ya_ref, ycp_ref, wo_ref, x_ref, o_ref):
        acc = _dot(ya_ref[...], wo_ref[0:A, :]) + _dot(ycp_ref[...], wo_ref[A:Dm, :])
        o_ref[...] = x_ref[...] + acc

    row = pl.BlockSpec((tm, D), lambda i: (i, 0))
    return pl.pallas_call(
        body, name=name, grid=(S // tm,),
        in_specs=[pl.BlockSpec((tm, A), lambda i: (i, 0)), pl.BlockSpec((tm, Dm - A), lambda i: (i, 0)),
                  pl.BlockSpec((Dm, D), lambda i: (0, 0)), row],
        out_specs=row, out_shape=jax.ShapeDtypeStruct((S, D), F32), compiler_params=_params(48),
    )(ya, ycp, wo, x)


def out_proj_bwd(gb, wo, A, name):
    S, D = gb.shape
    Dm = wo.shape[0]
    tm = _tile(S, 512)

    def body(g_ref, wo_ref, dya_ref, dycp_ref):
        gv = g_ref[...]
        dya_ref[...] = _dot(gv, wo_ref[0:A, :], NT).astype(BF16)
        dycp_ref[...] = _dot(gv, wo_ref[A:Dm, :], NT).astype(BF16)

    return pl.pallas_call(
        body, name=name, grid=(S // tm,),
        in_specs=[pl.BlockSpec((tm, D), lambda i: (i, 0)), pl.BlockSpec((Dm, D), lambda i: (0, 0))],
        out_specs=[pl.BlockSpec((tm, A), lambda i: (i, 0)), pl.BlockSpec((tm, Dm - A), lambda i: (i, 0))],
        out_shape=[jax.ShapeDtypeStruct((S, A), BF16), jax.ShapeDtypeStruct((S, Dm - A), BF16)],
        compiler_params=_params(48),
    )(gb, wo)


def _lane_stat(v, hi):
    lane = lax.broadcasted_iota(jnp.int32, v.shape, 1)
    sel = (lane >= HEAD_DIM) if hi else (lane < HEAD_DIM)
    return jnp.max(jnp.where(sel, v, -jnp.inf), axis=-1, keepdims=True)


def band_fwd(q, k, v, L, name):
    H, S, W = q.shape
    T = min(512, L)
    nb = T // BLK

    def body(q_ref, k_ref, kp_ref, v_ref, vp_ref, o_ref):
        i = pl.program_id(1)
        first_key = jnp.where((i * T) % L != 0, 0, BLK)
        qi = lax.broadcasted_iota(jnp.int32, (BLK, 2 * BLK), 0)
        kj = lax.broadcasted_iota(jnp.int32, (BLK, 2 * BLK), 1)
        band = (kj >= qi) & (kj <= qi + BLK)
        lane = lax.broadcasted_iota(jnp.int32, (BLK, W), 1)
        for b in range(nb):
            rows = slice(b * BLK, (b + 1) * BLK)
            if b == 0:
                kw = jnp.concatenate([kp_ref[...], k_ref[rows, :]], axis=0)
                vw = jnp.concatenate([vp_ref[...], v_ref[rows, :]], axis=0)
                mask = band & (kj >= first_key)
            else:
                kw = k_ref[(b - 1) * BLK:(b + 1) * BLK, :]
                vw = v_ref[(b - 1) * BLK:(b + 1) * BLK, :]
                mask = band
            s = jnp.where(mask, _dot(q_ref[rows, :], kw, NT) * ATTN_SCALE, NEG_INF)
            m = jnp.max(s, axis=-1, keepdims=True)
            p = jnp.exp(s - m)
            l = jnp.sum(p, axis=-1, keepdims=True)
            o = _dot(p.astype(BF16), vw)
            o_ref[rows, :] = jnp.where(lane < HEAD_DIM, o / l, m + jnp.log(l))

    cur = pl.BlockSpec((None, T, W), lambda h, i: (h, i, 0))
    prev = pl.BlockSpec((None, BLK, W), lambda h, i: (h, jnp.maximum(i * nb - 1, 0), 0))
    return pl.pallas_call(
        body, name=name, grid=(H, S // T),
        in_specs=[cur, cur, prev, cur, prev], out_specs=cur,
        out_shape=jax.ShapeDtypeStruct((H, S, W), F32), compiler_params=_params(32),
    )(q, k, k, v, v)


def band_bwd(q, k, v, do, st, L, name, comm=None):
    H, S, W = q.shape
    T = min(512, L)
    nb = T // BLK
    last_blk = S // BLK - 1

    def body(q_ref, qn_ref, do_ref, don_ref, st_ref, stn_ref, k_ref, kp_ref, v_ref, vp_ref,
             dq_ref, dk_ref, dv_ref, dk_acc, dv_acc):
        i = pl.program_id(1)
        first_key = jnp.where((i * T) % L != 0, 0, BLK)
        next_off = jnp.where(((i + 1) * T) % L != 0, 0, 4 * BLK)
        qi = lax.broadcasted_iota(jnp.int32, (BLK, 2 * BLK), 0)
        kj = lax.broadcasted_iota(jnp.int32, (BLK, 2 * BLK), 1)
        band = (kj >= qi) & (kj <= qi + BLK)
        qi1 = lax.broadcasted_iota(jnp.int32, (BLK, BLK), 0)
        kj1 = lax.broadcasted_iota(jnp.int32, (BLK, BLK), 1)
        dk_acc[...] = jnp.zeros_like(dk_acc)
        dv_acc[...] = jnp.zeros_like(dv_acc)
        for b in range(nb + 1):
            rows = slice(b * BLK, (b + 1) * BLK)
            if b < nb:
                qb, dob, stb = q_ref[rows, :], do_ref[rows, :], st_ref[rows, :]
            else:
                qb, dob, stb = qn_ref[...], don_ref[...], stn_ref[...]
            if b == 0:
                kw = jnp.concatenate([kp_ref[...], k_ref[rows, :]], axis=0)
                vw = jnp.concatenate([vp_ref[...], v_ref[rows, :]], axis=0)
                mask = band & (kj >= first_key)
            elif b < nb:
                kw = k_ref[(b - 1) * BLK:(b + 1) * BLK, :]
                vw = v_ref[(b - 1) * BLK:(b + 1) * BLK, :]
                mask = band
            else:
                kw = k_ref[(nb - 1) * BLK:nb * BLK, :]
                vw = v_ref[(nb - 1) * BLK:nb * BLK, :]
                mask = kj1 >= qi1 + next_off
            s = _dot(qb, kw, NT) * ATTN_SCALE
            p = jnp.where(mask, jnp.exp(s - _lane_stat(stb, False)), 0.0)
            dp = _dot(dob, vw, NT)
            ds = (p * (dp - _lane_stat(stb, True))).astype(BF16)
            if b < nb:
                dq_ref[rows, :] = _dot(ds, kw) * ATTN_SCALE
            win = slice(b * BLK, b * BLK + kw.shape[0])
            dk_acc[win, :] += _dot(ds, qb, TN) * ATTN_SCALE
            dv_acc[win, :] += _dot(p.astype(BF16), dob, TN)
        dk_ref[...] = dk_acc[BLK:, :]
        dv_ref[...] = dv_acc[BLK:, :]

    cur = pl.BlockSpec((None, T, W), lambda h, i: (h, i, 0))
    prev = pl.BlockSpec((None, BLK, W), lambda h, i: (h, jnp.maximum(i * nb - 1, 0), 0))
    nxt = pl.BlockSpec((None, BLK, W), lambda h, i: (h, jnp.minimum((i + 1) * nb, last_blk), 0))
    shp = jax.ShapeDtypeStruct((H, S, W), F32)
    return _call(
        body, name=name, grid=(H, S // T),
        in_specs=[cur, nxt, cur, nxt, cur, nxt, cur, prev, cur, prev],
        out_specs=[cur, cur, cur], out_shape=[shp, shp, shp],
        scratch=[pltpu.VMEM((T + BLK, W), F32), pltpu.VMEM((T + BLK, W), F32)],
        args=(q, q, do, do, st, st, k, k, v, v), vmem=32, comm=comm)


def attn_merge(os_, name):
    H, S, W = os_[0].shape
    tm = _tile(S, 1024)

    def body(*refs):
        o_ref = refs[-1]
        vals = [r[...] for r in refs[:-1]]
        lses = [_lane_stat(v, True) for v in vals]
        m = functools.reduce(jnp.maximum, lses)
        ws = [jnp.exp(l - m) for l in lses]
        tot = functools.reduce(jnp.add, ws)
        out = functools.reduce(jnp.add, [(w / tot) * v for w, v in zip(ws, vals)])
        lane = lax.broadcasted_iota(jnp.int32, out.shape, 1)
        o_ref[...] = jnp.where(lane < HEAD_DIM, out, m + jnp.log(tot))

    spec = pl.BlockSpec((None, tm, W), lambda h, i: (h, i, 0))
    return pl.pallas_call(
        body, name=name, grid=(H, S // tm), in_specs=[spec] * len(os_), out_specs=spec,
        out_shape=jax.ShapeDtypeStruct((H, S, W), F32), compiler_params=_params(32),
    )(*os_)


def attn_bwd_stats(merged, do, name):
    H, S, W = merged.shape
    tm = _tile(S, 1024)

    def body(m_ref, do_ref, st_ref):
        mv = m_ref[...]
        delta = jnp.sum(mv * do_ref[...].astype(F32), axis=-1, keepdims=True)
        lane = lax.broadcasted_iota(jnp.int32, mv.shape, 1)
        st_ref[...] = jnp.where(lane < HEAD_DIM, _lane_stat(mv, True), delta)

    spec = pl.BlockSpec((None, tm, W), lambda h, i: (h, i, 0))
    return pl.pallas_call(
        body, name=name, grid=(H, S // tm), in_specs=[spec, spec], out_specs=spec,
        out_shape=jax.ShapeDtypeStruct((H, S, W), F32), compiler_params=_params(32),
    )(merged, do)


def sum_patterns(ts, name):
    H, S, W = ts[0].shape
    tm = _tile(S, 1024)

    def body(*refs):
        refs[-1][...] = functools.reduce(jnp.add, [r[...] for r in refs[:-1]]).astype(BF16)

    spec = pl.BlockSpec((None, tm, W), lambda h, i: (h, i, 0))
    return pl.pallas_call(
        body, name=name, grid=(H, S // tm), in_specs=[spec] * len(ts), out_specs=spec,
        out_shape=jax.ShapeDtypeStruct((H, S, W), BF16), compiler_params=_params(32),
    )(*ts)


def to_heads(t):
    S = t.shape[0]
    t = jnp.transpose(t.reshape(S, -1, HEAD_DIM), (1, 0, 2))
    return jnp.pad(t, ((0, 0), (0, 0), (0, LANES - HEAD_DIM)))


def from_heads(t):
    H, S, _ = t.shape
    return jnp.transpose(t[:, :, :HEAD_DIM], (1, 0, 2)).reshape(S, H * HEAD_DIM)


def permute(t, d):
    if d == 1:
        return t
    H, S, W = t.shape
    return jnp.transpose(t.reshape(H, S // d, d, W), (0, 2, 1, 3)).reshape(H, S, W)


def unpermute(t, d):
    if d == 1:
        return t
    H, S, W = t.shape
    return jnp.transpose(t.reshape(H, d, S // d, W), (0, 2, 1, 3)).reshape(H, S, W)


ATT_TILE = BLK * max(DILATIONS)


def _head0(shape):
    return lax.broadcasted_iota(jnp.int32, shape, 1) < HEAD_DIM


def _pair_col(v, sel):
    return jnp.max(jnp.where(sel, v, -jnp.inf), axis=-1, keepdims=True)


def _band(lo, hi):
    qi = lax.broadcasted_iota(jnp.int32, (BLK, 2 * BLK), 0)
    kj = lax.broadcasted_iota(jnp.int32, (BLK, 2 * BLK), 1)
    return (kj >= qi) & (kj <= qi + BLK) & (kj >= lo) & (kj < hi)


def _deinterleave(dst, src, d, rows, dst_stride, dst_off=0, src_off=0, cast=None):
    for r in range(d):
        v = src[pl.ds(src_off + r, rows, stride=d), :] if d > 1 else src[pl.ds(src_off, rows), :]
        dst[pl.ds(r * dst_stride + dst_off, rows), :] = v if cast is None else v.astype(cast)


def _interleave(dst, src, d, rows, src_stride, src_off=0, add=False):
    for r in range(d):
        v = src[pl.ds(r * src_stride + src_off, rows), :]
        idx = pl.ds(r, rows, stride=d) if d > 1 else pl.ds(0, rows)
        dst[idx, :] = dst[idx, :] + v if add else v


def attn_fwd(z, A, name, comm=None):
    S = z.shape[0]
    T = ATT_TILE
    Hp, nt = A // LANES, S // T
    np_ = len(DILATIONS)

    def body(q_ref, k_ref, kp_ref, v_ref, vp_ref, y_ref, lse_ref, qn, kn, vn, qp, kp, vp, accp, mp, lp, *nat):
        accs, ms, ls = nat[:np_], nat[np_:2 * np_], nat[2 * np_:]
        i = pl.program_id(1)
        qn[...] = q_ref[...].astype(F32)
        kn[0:T, :] = kp_ref[...].astype(F32)
        kn[T:2 * T, :] = k_ref[...].astype(F32)
        vn[0:T, :] = vp_ref[...].astype(F32)
        vn[T:2 * T, :] = v_ref[...].astype(F32)
        h0 = _head0((BLK, LANES))
        for pi, d in enumerate(DILATIONS):
            Ld = T // d
            nblk = Ld // BLK
            _deinterleave(qp, qn, d, Ld, Ld, cast=BF16)
            for src, dst in ((kn, kp), (vn, vp)):
                _deinterleave(dst, src, d, Ld, 2 * Ld, cast=BF16)
                _deinterleave(dst, src, d, Ld, 2 * Ld, dst_off=Ld, src_off=T, cast=BF16)

            def unit(u, carry, nblk=nblk):
                r, n = u // nblk, u % nblk
                q0 = pl.multiple_of(u * BLK, BLK)
                k0 = pl.multiple_of((2 * r * nblk + nblk + n - 1) * BLK, BLK)
                qb = qp[pl.ds(q0, BLK), :]
                kw, vw = kp[pl.ds(k0, 2 * BLK), :], vp[pl.ds(k0, 2 * BLK), :]
                mask = _band(jnp.where((i == 0) & (n == 0), BLK, 0), 2 * BLK)
                res = []
                for sel in (h0, ~h0):
                    s = jnp.where(mask, _dot(jnp.where(sel, qb, jnp.zeros_like(qb)), kw, NT) * ATTN_SCALE, NEG_INF)
                    mx = jnp.max(s, axis=-1, keepdims=True)
                    p = jnp.exp(s - mx)
                    res.append((mx, jnp.sum(p, axis=-1, keepdims=True), _dot(p.astype(BF16), vw)))
                (m0, l0, o0), (m1, l1, o1) = res
                accp[pl.ds(q0, BLK), :] = jnp.where(h0, o0, o1)
                mp[pl.ds(q0, BLK), :] = jnp.where(h0, m0, m1)
                lp[pl.ds(q0, BLK), :] = jnp.where(h0, l0, l1)
                return carry

            lax.fori_loop(0, T // BLK, unit, 0, unroll=4)
            for src, dst in ((accp, accs[pi]), (mp, ms[pi]), (lp, ls[pi])):
                _interleave(dst, src, d, Ld, Ld)
        mv = [m[...] for m in ms]
        mx = mv[0]
        for m in mv[1:]:
            mx = jnp.maximum(mx, m)
        ws = [jnp.exp(m - mx) for m in mv]
        l = sum(w * lr[...] for w, lr in zip(ws, ls))
        a = sum(w * ar[...] for w, ar in zip(ws, accs))
        y_ref[...] = (a / l).astype(BF16)
        lse_ref[...] = mx + jnp.log(l)

    kb, vb = A // LANES, 2 * A // LANES
    cur = lambda off: pl.BlockSpec((T, LANES), lambda h, i: (i, off + h))
    prev = lambda off: pl.BlockSpec((T, LANES), lambda h, i: (jnp.maximum(i - 1, 0), off + h))
    out = pl.BlockSpec((T, LANES), lambda h, i: (i, h))
    vm = lambda rows, dt: pltpu.VMEM((rows, LANES), dt)
    (y, lse), got = _call(
        body, name=name, grid=(Hp, nt),
        in_specs=[cur(0), cur(kb), prev(kb), cur(vb), prev(vb)], out_specs=[out, out],
        out_shape=[jax.ShapeDtypeStruct((S, A), BF16), jax.ShapeDtypeStruct((S, A), F32)],
        scratch=[vm(T, F32), vm(2 * T, F32), vm(2 * T, F32), vm(T, BF16), vm(2 * T, BF16), vm(2 * T, BF16),
                 vm(T, F32), vm(T, F32), vm(T, F32)] + [vm(T, F32)] * (3 * np_),
        args=(z, z, z, z, z), vmem=48, comm=comm)
    return y, lse, got


def attn_bwd(z, dy, y, lse, A, name, comm=None):
    S = z.shape[0]
    T = ATT_TILE
    Hp, nt = A // LANES, S // T

    def body(q_ref, k_ref, kp_ref, v_ref, vp_ref, do_ref, y_ref, ls_ref, dq_ref, dk_ref, dv_ref,
             qn, don, dln, kn, vn, qp, dop, lsp, dlp, kp, vp, dqp, dkp, dvp, dqa, dka, dva, dkc, dvc):
        step = pl.program_id(1)
        i = nt - 1 - step
        h0t = _head0((T, LANES))
        dof = do_ref[...].astype(F32)
        prod = dof * y_ref[...].astype(F32)
        d0 = jnp.sum(jnp.where(h0t, prod, 0.0), axis=-1, keepdims=True)
        d1 = jnp.sum(jnp.where(h0t, 0.0, prod), axis=-1, keepdims=True)
        qn[...] = q_ref[...].astype(F32)
        don[...] = dof
        dln[...] = jnp.where(h0t, d0, d1)
        kn[0:T, :] = kp_ref[...].astype(F32)
        kn[T:2 * T, :] = k_ref[...].astype(F32)
        vn[0:T, :] = vp_ref[...].astype(F32)
        vn[T:2 * T, :] = v_ref[...].astype(F32)

        @pl.when(step == 0)
        def _():
            dka[...] = jnp.zeros_like(dka)
            dva[...] = jnp.zeros_like(dva)

        @pl.when(step > 0)
        def _():
            dka[...] = dkc[...]
            dva[...] = dvc[...]

        h0 = _head0((BLK, LANES))
        for pi, d in enumerate(DILATIONS):
            Ld = T // d
            nblk = Ld // BLK
            for src, dst, cast in ((qn, qp, BF16), (don, dop, BF16), (dln, dlp, None)):
                _deinterleave(dst, src, d, Ld, Ld, cast=cast)
            _deinterleave(lsp, ls_ref, d, Ld, Ld)
            for src, dst in ((kn, kp), (vn, vp)):
                _deinterleave(dst, src, d, Ld, 2 * Ld, cast=BF16)
                _deinterleave(dst, src, d, Ld, 2 * Ld, dst_off=Ld, src_off=T, cast=BF16)
            dkp[...] = jnp.zeros_like(dkp)
            dvp[...] = jnp.zeros_like(dvp)

            def unit(u, carry, nblk=nblk):
                r, b = u // nblk, u % nblk
                q0 = pl.multiple_of(u * BLK, BLK)
                k0 = pl.multiple_of((2 * r * nblk + nblk + b - 1) * BLK, BLK)
                mask = _band(jnp.where((i == 0) & (b == 0), BLK, 0), 2 * BLK)
                qb, dob = qp[pl.ds(q0, BLK), :], dop[pl.ds(q0, BLK), :]
                lsb, dlb = lsp[pl.ds(q0, BLK), :], dlp[pl.ds(q0, BLK), :]
                kw, vw = kp[pl.ds(k0, 2 * BLK), :], vp[pl.ds(k0, 2 * BLK), :]
                dq, dk, dv = None, None, None
                for sel in (h0, ~h0):
                    qh = jnp.where(sel, qb, jnp.zeros_like(qb))
                    doh = jnp.where(sel, dob, jnp.zeros_like(dob))
                    s = _dot(qh, kw, NT) * ATTN_SCALE
                    p = jnp.where(mask, jnp.exp(s - _pair_col(lsb, sel)), 0.0)
                    ds = (p * (_dot(doh, vw, NT) - _pair_col(dlb, sel))).astype(BF16)
                    dqh = _dot(ds, kw) * ATTN_SCALE
                    dkh = _dot(ds, qh, TN) * ATTN_SCALE
                    dvh = _dot(p.astype(BF16), doh, TN)
                    dq = dqh if dq is None else jnp.where(h0, dq, dqh)
                    dk = dkh if dk is None else dk + dkh
                    dv = dvh if dv is None else dv + dvh
                dqp[pl.ds(q0, BLK), :] = dq
                dkp[pl.ds(k0, 2 * BLK), :] += dk
                dvp[pl.ds(k0, 2 * BLK), :] += dv
                return carry

            lax.fori_loop(0, T // BLK, unit, 0, unroll=4)
            _interleave(dqa, dqp, d, Ld, Ld, add=pi > 0)
            for acc, nxt, part in ((dka, dkc, dkp), (dva, dvc, dvp)):
                _interleave(acc, part, d, Ld, 2 * Ld, src_off=Ld, add=True)
                _interleave(nxt, part, d, Ld, 2 * Ld, add=pi > 0)
        dq_ref[...] = dqa[...].astype(BF16)
        dk_ref[...] = dka[...].astype(BF16)
        dv_ref[...] = dva[...].astype(BF16)

    kb, vb = A // LANES, 2 * A // LANES
    cur = lambda off: pl.BlockSpec((T, LANES), lambda h, s: (nt - 1 - s, off + h))
    prev = lambda off: pl.BlockSpec((T, LANES), lambda h, s: (jnp.maximum(nt - 2 - s, 0), off + h))
    vm = lambda rows, dt: pltpu.VMEM((rows, LANES), dt)
    shp = jax.ShapeDtypeStruct((S, A), BF16)
    return _call(
        body, name=name, grid=(Hp, nt),
        in_specs=[cur(0), cur(kb), prev(kb), cur(vb), prev(vb), cur(0), cur(0), cur(0)],
        out_specs=[cur(0)] * 3, out_shape=[shp] * 3,
        scratch=[vm(T, F32), vm(T, F32), vm(T, F32), vm(2 * T, F32), vm(2 * T, F32),
                 vm(T, BF16), vm(T, BF16), vm(T, F32), vm(T, F32), vm(2 * T, BF16), vm(2 * T, BF16),
                 vm(T, F32), vm(2 * T, F32), vm(2 * T, F32)] + [vm(T, F32)] * 5,
        args=(z, z, z, z, z, dy, y, lse), vmem=56, comm=comm)


def _shift_down(e, k):
    return pltpu.roll(e, k, 0)


def _shift_up(e, k):
    return pltpu.roll(e, e.shape[0] - k, 0)


def _causal_sums(e, g):
    for lvl in range(g + 1):
        e = e + _shift_down(e, 1 << lvl)
    return e


def _anticausal_sums(e, g):
    for lvl in range(g + 1):
        e = e + _shift_up(e, 1 << lvl)
    return e


def _mixer_specs(S, tm, A, C, P):
    assert (3 * A) % C == 0 and (3 * A + 3 * C) % P == 0 and tm % HALO == 0
    cb, pb = 3 * A // C, (3 * A + 3 * C) // P
    hb = tm // HALO
    last = S // HALO - 1
    col = lambda w, j: pl.BlockSpec((tm, w), lambda i: (i, j))
    prev = lambda w, j: pl.BlockSpec((HALO, w), lambda i: (jnp.maximum(i * hb - 1, 0), j))
    nxt = lambda w, j: pl.BlockSpec((HALO, w), lambda i: (jnp.minimum((i + 1) * hb, last), j))
    return cb, pb, col, prev, nxt


def _conv_taps(gc, ci, gch, cih, keep_prev):
    u = gc * ci
    e = jnp.concatenate([gch * cih * keep_prev, u], axis=0)
    return u, _shift_down(e, 1)[HALO:], _shift_down(e, 2)[HALO:]


def _pooled(xp, xph, keep_prev, pos, g):
    cols = slice(g * POOL_GROUP, (g + 1) * POOL_GROUP)
    x = xp[:, cols]
    e = jnp.concatenate([xph[:, cols] * keep_prev, x], axis=0)
    cnt = jnp.minimum(pos + 1, POOL_WINDOWS[g]).astype(F32)
    return _causal_sums(e, g)[HALO:] / cnt - x


def convpool_fwd(z, conv_w, pool_w, pool_scale, A, C, P, name):
    S = z.shape[0]
    tm = _tile(S, 512)
    cb, pb, col, prev, _ = _mixer_specs(S, tm, A, C, P)

    def body(gb_ref, gc_ref, ci_ref, xp_ref, gch_ref, cih_ref, xph_ref, cw_ref, pw_ref, ps_ref, y_ref):
        i = pl.program_id(0)
        keep_prev = jnp.where(i == 0, 0.0, 1.0)
        f = lambda r: r[...].astype(F32)
        u, u1, u2 = _conv_taps(f(gc_ref), f(ci_ref), f(gch_ref), f(cih_ref), keep_prev)
        cw = cw_ref[...]
        y_ref[:, 0:C] = (f(gb_ref) * (cw[0:1] * u2 + cw[1:2] * u1 + cw[2:3] * u)).astype(BF16)
        xp, xph = f(xp_ref), f(xph_ref)
        pos = i * tm + lax.broadcasted_iota(jnp.int32, (tm, 1), 0)
        for g in range(len(POOL_WINDOWS)):
            cols = slice(g * POOL_GROUP, (g + 1) * POOL_GROUP)
            lin = _dot(_pooled(xp, xph, keep_prev, pos, g).astype(BF16), pw_ref[g])
            y_ref[:, C + g * POOL_GROUP:C + (g + 1) * POOL_GROUP] = (lin * ps_ref[:, cols]).astype(BF16)

    full = lambda shape: pl.BlockSpec(shape, lambda i: (0,) * len(shape))
    return pl.pallas_call(
        body, name=name, grid=(S // tm,),
        in_specs=[col(C, cb), col(C, cb + 1), col(C, cb + 2), col(P, pb),
                  prev(C, cb + 1), prev(C, cb + 2), prev(P, pb),
                  full(conv_w.shape), full(pool_w.shape), full(pool_scale.shape)],
        out_specs=pl.BlockSpec((tm, C + P), lambda i: (i, 0)),
        out_shape=jax.ShapeDtypeStruct((S, C + P), BF16), compiler_params=_params(48),
    )(z, z, z, z, z, z, z, conv_w, pool_w, pool_scale)


def convpool_bwd(z, dycp, conv_w, pool_w, pool_scale, A, C, P, name):
    S = z.shape[0]
    tm = _tile(S, 512)
    nt = S // tm
    cb, pb, col, prev, nxt = _mixer_specs(S, tm, A, C, P)
    NG = len(POOL_WINDOWS)

    def body(gb_ref, gc_ref, ci_ref, xp_ref, gch_ref, cih_ref, xph_ref, gbn_ref, dy_ref, dyn_ref,
             cw_ref, pw_ref, ps_ref, dz_ref, dcw_ref, dpw_ref, dps_ref):
        i = pl.program_id(0)

        @pl.when(i == 0)
        def _():
            dcw_ref[...] = jnp.zeros_like(dcw_ref)
            dpw_ref[...] = jnp.zeros_like(dpw_ref)
            dps_ref[...] = jnp.zeros_like(dps_ref)

        keep_prev = jnp.where(i == 0, 0.0, 1.0)
        keep_next = jnp.where(i == nt - 1, 0.0, 1.0)
        f = lambda r: r[...].astype(F32)
        gb, gc, ci = f(gb_ref), f(gc_ref), f(ci_ref)
        u, u1, u2 = _conv_taps(gc, ci, f(gch_ref), f(cih_ref), keep_prev)
        cw = cw_ref[...]
        dy, dyn = f(dy_ref), f(dyn_ref) * keep_next
        dyc = dy[:, 0:C]
        dz_ref[:, 0:C] = (dyc * (cw[0:1] * u2 + cw[1:2] * u1 + cw[2:3] * u)).astype(BF16)
        dc = dyc * gb
        e = jnp.concatenate([dc, dyn[:, 0:C] * f(gbn_ref)], axis=0)
        du = cw[2:3] * dc + cw[1:2] * _shift_up(e, 1)[:tm] + cw[0:1] * _shift_up(e, 2)[:tm]
        dz_ref[:, C:2 * C] = (du * ci).astype(BF16)
        dz_ref[:, 2 * C:3 * C] = (du * gc).astype(BF16)
        dcw_ref[0:1, :] += jnp.sum(dc * u2, axis=0, keepdims=True)
        dcw_ref[1:2, :] += jnp.sum(dc * u1, axis=0, keepdims=True)
        dcw_ref[2:3, :] += jnp.sum(dc * u, axis=0, keepdims=True)

        xp, xph = f(xp_ref), f(xph_ref)
        pos = i * tm + lax.broadcasted_iota(jnp.int32, (tm, 1), 0)
        pos_e = i * tm + lax.broadcasted_iota(jnp.int32, (tm + HALO, 1), 0)
        for g in range(NG):
            cols = slice(g * POOL_GROUP, (g + 1) * POOL_GROUP)
            ycols = slice(C + g * POOL_GROUP, C + (g + 1) * POOL_GROUP)
            pooled = _pooled(xp, xph, keep_prev, pos, g).astype(BF16)
            pw = pw_ref[g]
            dyp = dy[:, ycols]
            dps_ref[:, cols] += jnp.sum(dyp * _dot(pooled, pw), axis=0, keepdims=True)
            dlin = (jnp.concatenate([dyp, dyn[:, ycols]], axis=0) * ps_ref[:, cols]).astype(BF16)
            dpw_ref[g] += _dot(pooled, dlin[:tm], TN)
            dpool = _dot(dlin, pw, NT)
            r = dpool / jnp.minimum(pos_e + 1, POOL_WINDOWS[g]).astype(F32)
            dz_ref[:, 3 * C + g * POOL_GROUP:3 * C + (g + 1) * POOL_GROUP] = (
                _anticausal_sums(r, g)[:tm] - dpool[:tm]).astype(BF16)

    full = lambda shape: pl.BlockSpec(shape, lambda i: (0,) * len(shape))
    return pl.pallas_call(
        body, name=name, grid=(nt,),
        in_specs=[col(C, cb), col(C, cb + 1), col(C, cb + 2), col(P, pb),
                  prev(C, cb + 1), prev(C, cb + 2), prev(P, pb), nxt(C, cb),
                  pl.BlockSpec((tm, C + P), lambda i: (i, 0)),
                  pl.BlockSpec((HALO, C + P), lambda i: (jnp.minimum((i + 1) * (tm // HALO), S // HALO - 1), 0)),
                  full(conv_w.shape), full(pool_w.shape), full(pool_scale.shape)],
        out_specs=[pl.BlockSpec((tm, 3 * C + P), lambda i: (i, 0)),
                   full(conv_w.shape), full(pool_w.shape), full(pool_scale.shape)],
        out_shape=[jax.ShapeDtypeStruct((S, 3 * C + P), BF16), jax.ShapeDtypeStruct(conv_w.shape, F32),
                   jax.ShapeDtypeStruct(pool_w.shape, F32), jax.ShapeDtypeStruct(pool_scale.shape, F32)],
        compiler_params=_params(48),
    )(z, z, z, z, z, z, z, z, dycp, dycp, conv_w, pool_w, pool_scale)


def adamw(lands, w, m, v, name):
    R, C = w.shape
    nl = len(lands)
    n, Rl, _ = lands[0].shape
    assert Rl * nl == R
    tr = _tile(Rl, 128 * 1024 // C)
    nb = Rl // tr
    c1 = 1.0 - ADAM_B1 ** ADAM_STEP
    c2 = 1.0 - ADAM_B2 ** ADAM_STEP

    def body(*refs):
        land_refs = refs[:nl]
        w_ref, m_ref, v_ref, g_ref, d_ref, mo_ref, vo_ref = refs[nl:]
        i = pl.program_id(0)
        for a, land_ref in enumerate(land_refs):
            @pl.when((i >= a * nb) & (i < (a + 1) * nb))
            def _():
                g = land_ref[0].astype(F32)
                for s in range(1, n):
                    g = g + land_ref[s].astype(F32)
                g_ref[...] = g
                mn = ADAM_B1 * m_ref[...] + (1.0 - ADAM_B1) * g
                vn = ADAM_B2 * v_ref[...] + (1.0 - ADAM_B2) * (g * g)
                mo_ref[...] = mn
                vo_ref[...] = vn
                d_ref[...] = -ADAM_LR * ((mn / c1) / (jnp.sqrt(vn / c2) + ADAM_EPS) + ADAM_WD * w_ref[...])

    land_specs = [pl.BlockSpec((n, tr, C), lambda i, a=a: (0, jnp.clip(i - a * nb, 0, nb - 1), 0))
                  for a in range(nl)]
    row = pl.BlockSpec((tr, C), lambda i: (i, 0))
    shp = jax.ShapeDtypeStruct((R, C), F32)
    return pl.pallas_call(
        body, name=name, grid=(nl * nb,),
        in_specs=land_specs + [row, row, row], out_specs=[row] * 4, out_shape=[shp] * 4,
        compiler_params=_params(48),
    )(*lands, w, m, v)


def _gather_comm(arrs):
    return ("gather", arrs) if arrs else None


def _ffn_fwd(x, gain, wg, wu, wd, tag, carry_up=(), carry_down=()):
    h, ht = rms_fwd(x, gain, f"rms_{tag}")
    (a, b, hid), got_up = ffn_up(h, wg, wu, f"ffn_up_{tag}", _gather_comm(list(carry_up)))
    xo, got_down = ffn_down(hid, wd, x, f"ffn_down_{tag}", _gather_comm(list(carry_down)))
    return xo, (x, ht, a, b, hid), got_up, got_down


def _ffn_bwd(g, gb, saved, gain, wg, wu, wd, tag):
    x, ht, a, b, hid = saved
    da, db = ffn_dhid(gb, wd, a, b, f"ffn_dhid_{tag}")
    dwd = grad_shared_rhs(hid, gb, None, FFN_RESIDUAL, f"ffn_dwd_{tag}")
    dwg, (land_wd,) = grad_lhs(ht, da, True, f"ffn_dwg_{tag}", ("exchange", [dwd]))
    dwu, (land_wg,) = grad_lhs(ht, db, True, f"ffn_dwu_{tag}", ("exchange", [dwg]))
    dh, (land_wu,) = back_proj([da, db], [wg, wu], True, f"ffn_dh_{tag}", ("exchange", [dwu]))
    g, gb, dgain = rms_bwd(x, gain, dh, g, f"rms_bwd_{tag}")
    return g, gb, dgain, (land_wg, land_wu, land_wd)


def _mixer_fwd(x, gain, w_in, conv_w, pool_w, pool_scale, w_out, dims, tag, carry=()):
    A, C, P = dims
    h, ht = rms_fwd(x, gain, f"rms_{tag}")
    z = proj_cols(h, w_in, f"in_proj_{tag}")
    ya, lse, got = attn_fwd(z, A, f"attn_fwd_{tag}", _gather_comm(list(carry)))
    ycp = convpool_fwd(z, conv_w, pool_w, pool_scale, A, C, P, f"convpool_{tag}")
    xo = out_proj(ya, ycp, w_out, x, f"out_proj_{tag}")
    return xo, (x, ht, z, lse, ya, ycp), got


def _mixer_bwd(g, gb, saved, gain, w_in, conv_w, pool_w, pool_scale, w_out, dims, tag):
    A, C, P = dims
    x, ht, z, lse, ya, ycp = saved
    S, D = x.shape
    dya, dycp = out_proj_bwd(gb, w_out, A, f"out_proj_bwd_{tag}")
    dwo_a = grad_shared_rhs(ya, gb, 1, 1.0, f"dwo_attn_{tag}")
    dwo_cp = grad_shared_rhs(ycp, gb, 1, 1.0, f"dwo_cp_{tag}")
    dwo = jnp.concatenate([dwo_a[0], dwo_cp[0]], axis=0).reshape(NDEV, -1, D)
    dqkv, (land_wo,) = attn_bwd(z, dya, ya, lse, A, f"attn_bwd_{tag}", ("exchange", [dwo]))
    dz_cp, dcw, dpw, dps = convpool_bwd(z, dycp, conv_w, pool_w, pool_scale, A, C, P, f"convpool_bwd_{tag}")
    dz = jnp.concatenate(dqkv + [dz_cp], axis=1)
    dwin, _ = grad_lhs(ht, dz, False, f"dwin_{tag}")
    dh, (land_win,) = back_proj([dz], [w_in], False, f"in_proj_bwd_{tag}", ("exchange", [dwin]))
    g, gb, dgain = rms_bwd(x, gain, dh, g, f"rms_bwd_{tag}")
    return g, gb, (dgain, dcw, dpw, dps), (land_win, land_wo)


def _pack(arrs):
    flat = [a.reshape(-1).astype(F32) for a in arrs]
    spans, off = [], 0
    for a in flat:
        spans.append((off, a.shape[0]))
        off += a.shape[0]
    rows = -(-off // (8 * LANES)) * 8
    buf = jnp.concatenate(flat + [jnp.zeros((rows * LANES - off,), F32)]).reshape(rows, LANES)
    return buf, spans


def kernel(x, ffn1_norm, ffn1_w_gate, ffn1_w_up, ffn1_w_down, mix_norm, w_in, conv_w, pool_w, pool_scale, w_out, ffn2_norm, ffn2_w_gate, ffn2_w_up, ffn2_w_down, final_norm, loss_target, m_ffn1_norm, m_ffn1_w_gate, m_ffn1_w_up, m_ffn1_w_down, m_mix_norm, m_w_in, m_conv_w, m_pool_w, m_pool_scale, m_w_out, m_ffn2_norm, m_ffn2_w_gate, m_ffn2_w_up, m_ffn2_w_down, m_final_norm, v_ffn1_norm, v_ffn1_w_gate, v_ffn1_w_up, v_ffn1_w_down, v_mix_norm, v_w_in, v_conv_w, v_pool_w, v_pool_scale, v_w_out, v_ffn2_norm, v_ffn2_w_gate, v_ffn2_w_up, v_ffn2_w_down, v_final_norm):
    depth = ffn1_norm.shape[0]
    S, D = x.shape[1], x.shape[2]
    Cb = conv_w.shape[2]
    C = Cb * NDEV
    P = pool_scale.shape[1]
    A = (w_in.shape[2] * NDEV - 3 * C - P) // 3
    dims = (A, C, P)
    me = 4 * lax.axis_index("x") + 2 * lax.axis_index("y") + lax.axis_index("c")

    big = dict(ffn1_w_gate=ffn1_w_gate, ffn1_w_up=ffn1_w_up, ffn1_w_down=ffn1_w_down, w_in=w_in, w_out=w_out,
               ffn2_w_gate=ffn2_w_gate, ffn2_w_up=ffn2_w_up, ffn2_w_down=ffn2_w_down)
    big_m = dict(ffn1_w_gate=m_ffn1_w_gate, ffn1_w_up=m_ffn1_w_up, ffn1_w_down=m_ffn1_w_down, w_in=m_w_in,
                 w_out=m_w_out, ffn2_w_gate=m_ffn2_w_gate, ffn2_w_up=m_ffn2_w_up, ffn2_w_down=m_ffn2_w_down)
    big_v = dict(ffn1_w_gate=v_ffn1_w_gate, ffn1_w_up=v_ffn1_w_up, ffn1_w_down=v_ffn1_w_down, w_in=v_w_in,
                 w_out=v_w_out, ffn2_w_gate=v_ffn2_w_gate, ffn2_w_up=v_ffn2_w_up, ffn2_w_down=v_ffn2_w_down)
    names = list(big)

    first_names = ["ffn1_w_gate", "ffn1_w_up"]
    ffn1_names = first_names + ["ffn1_w_down"]
    mix_names = ["w_in", "w_out"]
    ffn2_names = ["ffn2_w_gate", "ffn2_w_up", "ffn2_w_down"]
    shards = lambda l, ns: [big[n][l].astype(BF16) for n in ns]
    full = [dict() for _ in range(depth)]
    full[0].update(zip(first_names, all_gather(shards(0, first_names), "gather_first")))
    conv_full = all_gather([jnp.pad(conv_w.reshape(-1, Cb), ((0, 2), (0, LANES - Cb)))], "gather_conv_w")[0]
    conv_full = jnp.transpose(conv_full[:, :depth * 3, :Cb].reshape(NDEV, depth, 3, Cb), (1, 2, 0, 3)).reshape(depth, 3, C)
    pool_w_bf = pool_w.astype(BF16)

    xs = x[0]
    saved = []
    for l in range(depth):
        W = full[l]
        up1 = ["ffn1_w_down"] + mix_names if l == 0 else []
        h, ht = rms_fwd(xs, ffn1_norm[l:l + 1], f"rms_f1_l{l}")
        (a, b, hid), got_up = ffn_up(h, W["ffn1_w_gate"], W["ffn1_w_up"], f"ffn_up_f1_l{l}",
                                     _gather_comm(shards(l, up1)))
        W.update(zip(up1, got_up))
        xo, _ = ffn_down(hid, W["ffn1_w_down"], xs, f"ffn_down_f1_l{l}")
        s1, xs = (xs, ht, a, b, hid), xo
        xs, s2, got = _mixer_fwd(xs, mix_norm[l:l + 1], W["w_in"], conv_full[l], pool_w_bf[l], pool_scale[l:l + 1],
                                 W["w_out"].reshape(-1, D), dims, f"mix_l{l}", shards(l, ffn2_names))
        W.update(zip(ffn2_names, got))
        up2, down2 = (ffn1_names, mix_names) if l + 1 < depth else ([], [])
        xs, s3, got_up, got_down = _ffn_fwd(xs, ffn2_norm[l:l + 1], W["ffn2_w_gate"], W["ffn2_w_up"],
                                            W["ffn2_w_down"], f"f2_l{l}", shards(l + 1, up2) if up2 else (),
                                            shards(l + 1, down2) if down2 else ())
        if l + 1 < depth:
            full[l + 1].update(zip(up2, got_up))
            full[l + 1].update(zip(down2, got_down))
        saved.append((s1, s2, s3))
    loss_part, g, gb, d_final = final_loss(xs, final_norm.reshape(1, D), loss_target[0], "final_loss")

    small = [None] * depth
    lands = [None] * depth
    for l in reversed(range(depth)):
        W = full[l]
        s1, s2, s3 = saved[l]
        g, gb, dn2, (lg2, lu2, ld2) = _ffn_bwd(g, gb, s3, ffn2_norm[l:l + 1], W["ffn2_w_gate"], W["ffn2_w_up"],
                                               W["ffn2_w_down"], f"f2_l{l}")
        g, gb, (dnm, dcw, dpw, dps), (lwin, lwo) = _mixer_bwd(
            g, gb, s2, mix_norm[l:l + 1], W["w_in"], conv_full[l], pool_w_bf[l], pool_scale[l:l + 1],
            W["w_out"].reshape(-1, D), dims, f"mix_l{l}")
        g, gb, dn1, (lg1, lu1, ld1) = _ffn_bwd(g, gb, s1, ffn1_norm[l:l + 1], W["ffn1_w_gate"], W["ffn1_w_up"],
                                               W["ffn1_w_down"], f"f1_l{l}")
        lands[l] = dict(ffn1_w_gate=lg1, ffn1_w_up=lu1, ffn1_w_down=ld1, w_in=lwin, w_out=lwo,
                        ffn2_w_gate=lg2, ffn2_w_up=lu2, ffn2_w_down=ld2)
        small[l] = (dn1, dnm, dcw, dpw, dps, dn2)

    res = {}
    for n in names:
        shp = big[n].shape
        two = lambda t: t.reshape(-1, shp[-1])
        outs = adamw([lands[l][n] for l in range(depth)], two(big[n]), two(big_m[n]), two(big_v[n]), f"adamw_{n}")
        res[n] = [o.reshape(shp) for o in outs]

    st = lambda i: jnp.stack([small[l][i] for l in range(depth)])
    small_g = dict(ffn1_norm=st(0), mix_norm=st(1), conv_w=st(2), pool_w=st(3), pool_scale=st(4), ffn2_norm=st(5),
                   final_norm=d_final)
    small_names = list(small_g)
    zeros_conv = jnp.zeros((depth, 3, C), F32)
    place = lambda t: lax.dynamic_update_slice(zeros_conv, t, (0, 0, me * Cb))
    small_w = dict(ffn1_norm=ffn1_norm, mix_norm=mix_norm, conv_w=place(conv_w), pool_w=pool_w, pool_scale=pool_scale,
                   ffn2_norm=ffn2_norm, final_norm=final_norm)
    small_m = dict(ffn1_norm=m_ffn1_norm, mix_norm=m_mix_norm, conv_w=place(m_conv_w), pool_w=m_pool_w,
                   pool_scale=m_pool_scale, ffn2_norm=m_ffn2_norm, final_norm=m_final_norm)
    small_v = dict(ffn1_norm=v_ffn1_norm, mix_norm=v_mix_norm, conv_w=place(v_conv_w), pool_w=v_pool_w,
                   pool_scale=v_pool_scale, ffn2_norm=v_ffn2_norm, final_norm=v_final_norm)
    gbuf, spans = _pack([small_g[n] for n in small_names] + [loss_part])
    wbuf, _ = _pack([small_w[n] for n in small_names] + [jnp.zeros((1, LANES), F32)])
    mbuf, _ = _pack([small_m[n] for n in small_names] + [jnp.zeros((1, LANES), F32)])
    vbuf, _ = _pack([small_v[n] for n in small_names] + [jnp.zeros((1, LANES), F32)])
    gathered = all_gather([gbuf], "gather_small_grads")[0]
    outs = adamw([gathered], wbuf, mbuf, vbuf, "adamw_small")
    for n, (off, size) in zip(small_names, spans):
        shp = small_w[n].shape
        vals = [o.reshape(-1)[off:off + size].reshape(shp) for o in outs]
        if n == "conv_w":
            vals = [lax.dynamic_slice(t, (0, 0, me * Cb), (depth, 3, Cb)) for t in vals]
        res[n] = vals
    loss = outs[0].reshape(-1)[spans[-1][0]]

    order = ["ffn1_norm", "ffn1_w_gate", "ffn1_w_up", "ffn1_w_down", "mix_norm", "w_in", "conv_w", "pool_w",
             "pool_scale", "w_out", "ffn2_norm", "ffn2_w_gate", "ffn2_w_up", "ffn2_w_down", "final_norm"]
    return (loss, g[None], *[res[n][0] for n in order], *[res[n][1] for n in order],
            *[res[n][2] for n in order], *[res[n][3] for n in order])
```

```python
import functools

import jax
import jax.numpy as jnp
from jax import lax
from jax.experimental import pallas as pl
from jax.experimental.pallas import tpu as pltpu

F32 = jnp.float32
BF16 = jnp.bfloat16
NDEV = 8
HEAD_DIM = 64
LANES = 128
BLK = 128
DILATIONS = (1, 4, 16)
POOL_WINDOWS = (2, 4, 8, 16)
POOL_GROUP = 128
HALO = 16
FFN_RESIDUAL = 0.5
RMS_EPS = 1e-6
NEG_INF = -1e30
ATTN_SCALE = HEAD_DIM ** -0.5
ADAM_LR, ADAM_B1, ADAM_B2, ADAM_EPS, ADAM_WD, ADAM_STEP = 0.001, 0.9, 0.999, 1e-08, 0.01, 10
VMEM_BYTES = 64 * 1024 * 1024
MESH = pl.DeviceIdType.MESH

NN = (((1,), (0,)), ((), ()))
NT = (((1,), (1,)), ((), ()))
TN = (((0,), (0,)), ((), ()))


def _dot(a, b, dims=NN):
    return lax.dot_general(a, b, dims, preferred_element_type=F32)


def _params(vmem_mb=48):
    return pltpu.CompilerParams(vmem_limit_bytes=min(vmem_mb * 1024 * 1024, VMEM_BYTES - 4 * 1024 * 1024))


def _tile(n, want):
    t = min(n, max(16, want // 16 * 16))
    while t > 16 and (n % t or t % 16):
        t -= 16
    return t if n % t == 0 else n


def _coords():
    return lax.axis_index("x"), lax.axis_index("y"), lax.axis_index("c")


def _comm_sems(n):
    return [pltpu.SemaphoreType.DMA((7 * n,)), pltpu.SemaphoreType.DMA((7 * n,)), pltpu.SemaphoreType.DMA((n,))]


def _gather_ops(ins, outs, sems):
    n = len(ins)
    send_sems, recv_sems, local_sems = sems
    x, y, c = _coords()
    me, sibling = (x, y, c), (x, y, 1 - c)
    chips = [(1 - x, y), (x, 1 - y), (1 - x, 1 - y)]

    def slot(out, p):
        return out.at[4 * p[0] + 2 * p[1] + p[2]]

    def copy(a, k, block, to, src=None):
        return pltpu.make_async_remote_copy(
            src_ref=slot(outs[a], block) if src is None else src, dst_ref=slot(outs[a], block),
            send_sem=send_sems.at[7 * a + k], recv_sem=recv_sems.at[7 * a + k],
            device_id=to, device_id_type=MESH)

    def own():
        mine = [pltpu.make_async_copy(ins[a], slot(outs[a], me), local_sems.at[a]) for a in range(n)]
        first = []
        for a in range(n):
            first.append(copy(a, 0, me, sibling, src=ins[a]))
            first += [copy(a, 1 + j, me, (*chip, c), src=ins[a]) for j, chip in enumerate(chips)]
        return mine, first

    def start():
        mine, first = own()
        for cp in mine + first:
            cp.start()

    def finish():
        mine, first = own()
        passed = []
        for j, chip in enumerate(chips):
            for a in range(n):
                copy(a, 1 + j, (*chip, c), me).wait_recv()
                fwd = copy(a, 4 + j, (*chip, c), sibling)
                fwd.start()
                passed.append(fwd)
        for a in range(n):
            copy(a, 0, sibling, me).wait_recv()
            for j, chip in enumerate(chips):
                copy(a, 4 + j, (*chip, 1 - c), me).wait_recv()
        for cp in first + passed:
            cp.wait_send()
        for cp in mine:
            cp.wait()

    return start, finish


def _exchange_ops(ins, outs, sems):
    n = len(ins)
    send_sems, recv_sems, local_sems = sems
    x, y, c = _coords()
    me = 4 * x + 2 * y + c

    def peer(k):
        return (1 - x if k & 4 else x, 1 - y if k & 2 else y, 1 - c if k & 1 else c)

    def copy(a, k):
        p = peer(k)
        return pltpu.make_async_remote_copy(
            src_ref=ins[a].at[4 * p[0] + 2 * p[1] + p[2]], dst_ref=outs[a].at[me],
            send_sem=send_sems.at[7 * a + k - 1], recv_sem=recv_sems.at[7 * a + k - 1],
            device_id=p, device_id_type=MESH)

    def landed(a, k):
        p = peer(k)
        return pltpu.make_async_remote_copy(
            src_ref=ins[a].at[me], dst_ref=outs[a].at[4 * p[0] + 2 * p[1] + p[2]],
            send_sem=send_sems.at[7 * a + k - 1], recv_sem=recv_sems.at[7 * a + k - 1],
            device_id=p, device_id_type=MESH)

    def own():
        mine = [pltpu.make_async_copy(ins[a].at[me], outs[a].at[me], local_sems.at[a]) for a in range(n)]
        return mine, [copy(a, k) for a in range(n) for k in range(1, NDEV)]

    def start():
        mine, sent = own()
        for cp in mine + sent:
            cp.start()

    def finish():
        mine, sent = own()
        for a in range(n):
            for k in range(1, NDEV):
                landed(a, k).wait_recv()
        for cp in sent:
            cp.wait_send()
        for cp in mine:
            cp.wait()

    return start, finish


_COMM = {"gather": (_gather_ops, lambda a: (NDEV,) + a.shape), "exchange": (_exchange_ops, lambda a: a.shape)}


def all_gather(arrs, name):
    n = len(arrs)

    def body(*refs):
        start, finish = _gather_ops(refs[:n], refs[n:2 * n], refs[2 * n:])
        start()
        finish()

    any_spec = pl.BlockSpec(memory_space=pl.ANY)
    return pl.pallas_call(
        body, name=name, out_shape=[jax.ShapeDtypeStruct((NDEV,) + a.shape, a.dtype) for a in arrs],
        in_specs=[any_spec] * n, out_specs=[any_spec] * n, scratch_shapes=_comm_sems(n),
    )(*arrs)


def _call(body, *, name, grid, in_specs, out_specs, out_shape, args, scratch=(), vmem=48, comm=None):
    if comm is None:
        outs = pl.pallas_call(
            body, name=name, grid=grid, in_specs=list(in_specs), out_specs=list(out_specs), out_shape=list(out_shape),
            scratch_shapes=list(scratch), compiler_params=_params(vmem))(*args)
        return list(outs), []
    kind, arrs = comm
    ops, shape_of = _COMM[kind]
    n, n_in, n_out, n_scr = len(arrs), len(in_specs), len(out_specs), len(scratch)

    def carrier(*refs):
        ins, c_in = refs[:n_in], refs[n_in:n_in + n]
        o0 = n_in + n
        outs, c_out = refs[o0:o0 + n_out], refs[o0 + n_out:o0 + n_out + n]
        s0 = o0 + n_out + n
        scr, sems = refs[s0:s0 + n_scr], refs[s0 + n_scr:]
        ids = [pl.program_id(d) for d in range(len(grid))]
        first = functools.reduce(jnp.logical_and, [i == 0 for i in ids])
        last = functools.reduce(jnp.logical_and, [i == g - 1 for i, g in zip(ids, grid)])
        start, finish = ops(c_in, c_out, sems)
        pl.when(first)(start)
        body(*ins, *outs, *scr)
        pl.when(last)(finish)

    any_spec = pl.BlockSpec(memory_space=pl.ANY)
    res = pl.pallas_call(
        carrier, name=name, grid=grid, in_specs=list(in_specs) + [any_spec] * n,
        out_specs=list(out_specs) + [any_spec] * n,
        out_shape=list(out_shape) + [jax.ShapeDtypeStruct(shape_of(a), a.dtype) for a in arrs],
        scratch_shapes=list(scratch) + _comm_sems(n), compiler_params=_params(vmem))(*args, *arrs)
    return list(res[:n_out]), list(res[n_out:])


def _rstd(x):
    return lax.rsqrt(jnp.mean(x * x, axis=-1, keepdims=True) + RMS_EPS)


def rms_fwd(x, gain, name):
    S, D = x.shape
    tm = _tile(S, 512)

    def body(x_ref, g_ref, h_ref, ht_ref):
        xv = x_ref[...]
        h = xv * _rstd(xv) * g_ref[...]
        h_ref[...] = h.astype(BF16)
        ht_ref[...] = h.T.astype(BF16)

    return pl.pallas_call(
        body, name=name, grid=(S // tm,),
        in_specs=[pl.BlockSpec((tm, D), lambda i: (i, 0)), pl.BlockSpec((1, D), lambda i: (0, 0))],
        out_specs=[pl.BlockSpec((tm, D), lambda i: (i, 0)), pl.BlockSpec((D, tm), lambda i: (0, i))],
        out_shape=[jax.ShapeDtypeStruct((S, D), BF16), jax.ShapeDtypeStruct((D, S), BF16)],
        compiler_params=_params(40),
    )(x, gain)


def _rms_bwd_math(xv, gain, dh):
    r = _rstd(xv)
    xhat = xv * r
    dxhat = dh * gain
    dx = r * (dxhat - xhat * jnp.mean(dxhat * xhat, axis=-1, keepdims=True))
    return dx, dh * xhat


def rms_bwd(x, gain, dh, g, name):
    S, D = x.shape
    tm = _tile(S, 256)

    def body(x_ref, gain_ref, dh_ref, g_ref, go_ref, gb_ref, dg_ref):
        @pl.when(pl.program_id(0) == 0)
        def _():
            dg_ref[...] = jnp.zeros_like(dg_ref)

        dx, dgain = _rms_bwd_math(x_ref[...], gain_ref[...], dh_ref[...])
        gn = g_ref[...] + dx
        go_ref[...] = gn
        gb_ref[...] = gn.astype(BF16)
        dg_ref[...] += jnp.sum(dgain, axis=0, keepdims=True)

    row = pl.BlockSpec((tm, D), lambda i: (i, 0))
    vec = pl.BlockSpec((1, D), lambda i: (0, 0))
    return pl.pallas_call(
        body, name=name, grid=(S // tm,),
        in_specs=[row, vec, row, row], out_specs=[row, row, vec],
        out_shape=[jax.ShapeDtypeStruct((S, D), F32), jax.ShapeDtypeStruct((S, D), BF16),
                   jax.ShapeDtypeStruct((1, D), F32)],
        compiler_params=_params(48),
    )(x, gain, dh, g)


def final_loss(x, gain, target, name):
    S, D = x.shape
    tm = _tile(S, 256)

    def body(x_ref, gain_ref, t_ref, loss_ref, go_ref, gb_ref, dg_ref):
        @pl.when(pl.program_id(0) == 0)
        def _():
            dg_ref[...] = jnp.zeros_like(dg_ref)
            loss_ref[...] = jnp.zeros_like(loss_ref)

        xv, gain_v = x_ref[...], gain_ref[...]
        err = xv * _rstd(xv) * gain_v - t_ref[...]
        loss_ref[...] += jnp.sum(jnp.sum(err * err, axis=-1, keepdims=True), axis=0, keepdims=True) * (0.5 / D)
        dx, dgain = _rms_bwd_math(xv, gain_v, err * (1.0 / D))
        go_ref[...] = dx
        gb_ref[...] = dx.astype(BF16)
        dg_ref[...] += jnp.sum(dgain, axis=0, keepdims=True)

    row = pl.BlockSpec((tm, D), lambda i: (i, 0))
    vec = pl.BlockSpec((1, D), lambda i: (0, 0))
    return pl.pallas_call(
        body, name=name, grid=(S // tm,),
        in_specs=[row, vec, row],
        out_specs=[pl.BlockSpec((1, LANES), lambda i: (0, 0)), row, row, vec],
        out_shape=[jax.ShapeDtypeStruct((1, LANES), F32), jax.ShapeDtypeStruct((S, D), F32),
                   jax.ShapeDtypeStruct((S, D), BF16), jax.ShapeDtypeStruct((1, D), F32)],
        compiler_params=_params(48),
    )(x, gain, target)


def _blk(arr, width, tm, stacked):
    if stacked:
        return pl.BlockSpec((None, tm, width), lambda j, i: (j, i, 0))
    return pl.BlockSpec((tm, width), lambda j, i: (i, j))


def ffn_up(h, wg, wu, name, comm=None):
    S, D = h.shape
    J, _, Fb = wg.shape
    tm = _tile(S, 1024)

    def body(h_ref, wg_ref, wu_ref, a_ref, b_ref, hid_ref):
        hv = h_ref[...]
        a = _dot(hv, wg_ref[...])
        b = _dot(hv, wu_ref[...])
        a_ref[...] = a.astype(BF16)
        b_ref[...] = b.astype(BF16)
        hid_ref[...] = (a * jax.nn.sigmoid(a) * b).astype(BF16)

    w_spec = pl.BlockSpec((None, D, Fb), lambda j, i: (j, 0, 0))
    o_spec = pl.BlockSpec((None, tm, Fb), lambda j, i: (j, i, 0))
    shp = jax.ShapeDtypeStruct((J, S, Fb), BF16)
    return _call(
        body, name=name, grid=(J, S // tm),
        in_specs=[pl.BlockSpec((tm, D), lambda j, i: (i, 0)), w_spec, w_spec],
        out_specs=[o_spec, o_spec, o_spec], out_shape=[shp, shp, shp], args=(h, wg, wu), comm=comm)


def proj_cols(h, w, name):
    S, D = h.shape
    J, _, Wb = w.shape
    tm = _tile(S, 1024)

    def body(h_ref, w_ref, z_ref):
        z_ref[...] = _dot(h_ref[...], w_ref[...]).astype(BF16)

    return pl.pallas_call(
        body, name=name, grid=(J, S // tm),
        in_specs=[pl.BlockSpec((tm, D), lambda j, i: (i, 0)), pl.BlockSpec((None, D, Wb), lambda j, i: (j, 0, 0))],
        out_specs=pl.BlockSpec((tm, Wb), lambda j, i: (i, j)),
        out_shape=jax.ShapeDtypeStruct((S, J * Wb), BF16), compiler_params=_params(48),
    )(h, w)


def ffn_down(hid, wd, x, name, comm=None):
    J, S, Fb = hid.shape
    D = wd.shape[2]
    tm = _tile(S, 256)

    def body(hid_ref, wd_ref, x_ref, o_ref):
        acc = _dot(hid_ref[0], wd_ref[0])
        for j in range(1, J):
            acc = acc + _dot(hid_ref[j], wd_ref[j])
        o_ref[...] = x_ref[...] + FFN_RESIDUAL * acc

    row = pl.BlockSpec((tm, D), lambda i: (i, 0))
    (xo,), got = _call(
        body, name=name, grid=(S // tm,),
        in_specs=[pl.BlockSpec((J, tm, Fb), lambda i: (0, i, 0)),
                  pl.BlockSpec((J, Fb, D), lambda i: (0, 0, 0), pipeline_mode=pl.Buffered(1)), row],
        out_specs=[row], out_shape=[jax.ShapeDtypeStruct((S, D), F32)], args=(hid, wd, x), vmem=56, comm=comm)
    return xo, got


def ffn_dhid(gb, wd, a, b, name):
    S, D = gb.shape
    J, Fb, _ = wd.shape
    tm = _tile(S, 1024)

    parts = 4 if tm % 64 == 0 else 1

    def body(g_ref, wd_ref, a_ref, b_ref, da_ref, db_ref):
        wdv = wd_ref[...]
        for part in range(parts):
            rows = pl.ds(part * (tm // parts), tm // parts)
            dhid = _dot(g_ref[rows, :], wdv, NT) * FFN_RESIDUAL
            a = a_ref[rows, :].astype(F32)
            b = b_ref[rows, :].astype(F32)
            sg = jax.nn.sigmoid(a)
            da_ref[rows, :] = (dhid * b * (sg * (1.0 + a * (1.0 - sg)))).astype(BF16)
            db_ref[rows, :] = (dhid * (a * sg)).astype(BF16)

    o_spec = pl.BlockSpec((None, tm, Fb), lambda j, i: (j, i, 0))
    shp = jax.ShapeDtypeStruct((J, S, Fb), BF16)
    return pl.pallas_call(
        body, name=name, grid=(J, S // tm),
        in_specs=[pl.BlockSpec((tm, D), lambda j, i: (i, 0)), pl.BlockSpec((None, Fb, D), lambda j, i: (j, 0, 0)),
                  o_spec, o_spec],
        out_specs=[o_spec, o_spec], out_shape=[shp, shp], compiler_params=_params(48),
    )(gb, wd, a, b)


def back_proj(ds, ws, stacked, name, comm=None):
    n = len(ds)
    J, D, Wb = ws[0].shape
    S = ds[0].shape[1] if stacked else ds[0].shape[0]
    kb = J if n == 1 else 2
    nj = J // kb
    tm = _tile(S, 256 if n == 1 else 512)

    def body(*refs):
        d_refs, w_refs, o_ref = refs[:n], refs[n:2 * n], refs[2 * n]
        acc = None
        for d_ref, w_ref in zip(d_refs, w_refs):
            for k in range(kb):
                dk = d_ref[k] if stacked else d_ref[:, k * Wb:(k + 1) * Wb]
                t = _dot(dk, w_ref[k], NT)
                acc = t if acc is None else acc + t
        if nj == 1:
            o_ref[...] = acc
        else:
            acc_ref = refs[2 * n + 1]
            j = pl.program_id(1)

            @pl.when(j == 0)
            def _():
                acc_ref[...] = acc

            @pl.when((j > 0) & (j < nj - 1))
            def _():
                acc_ref[...] += acc

            @pl.when(j == nj - 1)
            def _():
                o_ref[...] = acc_ref[...] + acc

    if stacked:
        d_spec = pl.BlockSpec((kb, tm, Wb), lambda i, j: (j, i, 0))
    else:
        d_spec = pl.BlockSpec((tm, kb * Wb), lambda i, j: (i, j))
    w_spec = pl.BlockSpec((kb, D, Wb), lambda i, j: (j, 0, 0), pipeline_mode=pl.Buffered(1) if nj == 1 else None)
    (dh,), got = _call(
        body, name=name, grid=(S // tm, nj),
        in_specs=[d_spec] * n + [w_spec] * n,
        out_specs=[pl.BlockSpec((tm, D), lambda i, j: (i, 0))], out_shape=[jax.ShapeDtypeStruct((S, D), F32)],
        scratch=[] if nj == 1 else [pltpu.VMEM((tm, D), F32)], args=(*ds, *ws), vmem=56, comm=comm)
    return dh, got


def grad_lhs(ht, d, stacked, name, comm=None):
    D, S = ht.shape
    if stacked:
        J, _, Wb = d.shape
    else:
        J, Wb = NDEV, d.shape[1] // NDEV
    tk = _tile(S, 2048)
    nk = S // tk

    def body(ht_ref, d_ref, o_ref, acc_ref):
        i = pl.program_id(1)

        @pl.when(i == 0)
        def _():
            acc_ref[...] = jnp.zeros_like(acc_ref)

        acc_ref[...] += _dot(ht_ref[...], d_ref[...])

        @pl.when(i == nk - 1)
        def _():
            o_ref[...] = acc_ref[...].astype(BF16)

    (dw,), got = _call(
        body, name=name, grid=(J, nk),
        in_specs=[pl.BlockSpec((D, tk), lambda j, i: (0, i)), _blk(d, Wb, tk, stacked)],
        out_specs=[pl.BlockSpec((None, D, Wb), lambda j, i: (j, 0, 0))],
        out_shape=[jax.ShapeDtypeStruct((J, D, Wb), BF16)],
        scratch=[pltpu.VMEM((D, Wb), F32)], args=(ht, d), vmem=56, comm=comm)
    return dw, got


def grad_shared_rhs(a, gb, nblk, scale, name):
    S, D = gb.shape
    stacked = nblk is None
    if stacked:
        J, _, Wb = a.shape
    else:
        J, Wb = nblk, a.shape[1] // nblk
    tk = _tile(S, 2048 if Wb <= 768 else 1024)
    nk = S // tk

    def body(a_ref, g_ref, o_ref, acc_ref):
        i = pl.program_id(1)

        @pl.when(i == 0)
        def _():
            acc_ref[...] = jnp.zeros_like(acc_ref)

        acc_ref[...] += _dot(a_ref[...], g_ref[...], TN)

        @pl.when(i == nk - 1)
        def _():
            o_ref[...] = (acc_ref[...] * scale).astype(BF16)

    return pl.pallas_call(
        body, name=name, grid=(J, nk),
        in_specs=[_blk(a, Wb, tk, stacked), pl.BlockSpec((tk, D), lambda j, i: (i, 0))],
        out_specs=pl.BlockSpec((None, Wb, D), lambda j, i: (j, 0, 0)),
        out_shape=jax.ShapeDtypeStruct((J, Wb, D), BF16),
        scratch_shapes=[pltpu.VMEM((Wb, D), F32)], compiler_params=_params(56),
    )(a, gb)


def out_proj(ya, ycp, wo, x, name):
    S, A = ya.shape
    Dm, D = wo.shape
    tm = _tile(S, 512)

    def body(ya_ref, ycp_ref, wo_ref, x_ref, o_ref):
        acc = _dot(ya_ref[...], wo_ref[0:A, :]) + _dot(ycp_ref[...], wo_ref[A:Dm, :])
        o_ref[...] = x_ref[...] + acc

    row = pl.BlockSpec((tm, D), lambda i: (i, 0))
    return pl.pallas_call(
        body, name=name, grid=(S // tm,),
        in_specs=[pl.BlockSpec((tm, A), lambda i: (i, 0)), pl.BlockSpec((tm, Dm - A), lambda i: (i, 0)),
                  pl.BlockSpec((Dm, D), lambda i: (0, 0)), row],
        out_specs=row, out_shape=jax.ShapeDtypeStruct((S, D), F32), compiler_params=_params(48),
    )(ya, ycp, wo, x)


def out_proj_bwd(gb, wo, A, name):
    S, D = gb.shape
    Dm = wo.shape[0]
    tm = _tile(S, 512)

    def body(g_ref, wo_ref, dya_ref, dycp_ref):
        gv = g_ref[...]
        dya_ref[...] = _dot(gv, wo_ref[0:A, :], NT).astype(BF16)
        dycp_ref[...] = _dot(gv, wo_ref[A:Dm, :], NT).astype(BF16)

    return pl.pallas_call(
        body, name=name, grid=(S // tm,),
        in_specs=[pl.BlockSpec((tm, D), lambda i: (i, 0)), pl.BlockSpec((Dm, D), lambda i: (0, 0))],
        out_specs=[pl.BlockSpec((tm, A), lambda i: (i, 0)), pl.BlockSpec((tm, Dm - A), lambda i: (i, 0))],
        out_shape=[jax.ShapeDtypeStruct((S, A), BF16), jax.ShapeDtypeStruct((S, Dm - A), BF16)],
        compiler_params=_params(48),
    )(gb, wo)


def _lane_stat(v, hi):
    lane = lax.broadcasted_iota(jnp.int32, v.shape, 1)
    sel = (lane >= HEAD_DIM) if hi else (lane < HEAD_DIM)
    return jnp.max(jnp.where(sel, v, -jnp.inf), axis=-1, keepdims=True)


def band_fwd(q, k, v, L, name):
    H, S, W = q.shape
    T = min(512, L)
    nb = T // BLK

    def body(q_ref, k_ref, kp_ref, v_ref, vp_ref, o_ref):
        i = pl.program_id(1)
        first_key = jnp.where((i * T) % L != 0, 0, BLK)
        qi = lax.broadcasted_iota(jnp.int32, (BLK, 2 * BLK), 0)
        kj = lax.broadcasted_iota(jnp.int32, (BLK, 2 * BLK), 1)
        band = (kj >= qi) & (kj <= qi + BLK)
        lane = lax.broadcasted_iota(jnp.int32, (BLK, W), 1)
        for b in range(nb):
            rows = slice(b * BLK, (b + 1) * BLK)
            if b == 0:
                kw = jnp.concatenate([kp_ref[...], k_ref[rows, :]], axis=0)
                vw = jnp.concatenate([vp_ref[...], v_ref[rows, :]], axis=0)
                mask = band & (kj >= first_key)
            else:
                kw = k_ref[(b - 1) * BLK:(b + 1) * BLK, :]
                vw = v_ref[(b - 1) * BLK:(b + 1) * BLK, :]
                mask = band
            s = jnp.where(mask, _dot(q_ref[rows, :], kw, NT) * ATTN_SCALE, NEG_INF)
            m = jnp.max(s, axis=-1, keepdims=True)
            p = jnp.exp(s - m)
            l = jnp.sum(p, axis=-1, keepdims=True)
            o = _dot(p.astype(BF16), vw)
            o_ref[rows, :] = jnp.where(lane < HEAD_DIM, o / l, m + jnp.log(l))

    cur = pl.BlockSpec((None, T, W), lambda h, i: (h, i, 0))
    prev = pl.BlockSpec((None, BLK, W), lambda h, i: (h, jnp.maximum(i * nb - 1, 0), 0))
    return pl.pallas_call(
        body, name=name, grid=(H, S // T),
        in_specs=[cur, cur, prev, cur, prev], out_specs=cur,
        out_shape=jax.ShapeDtypeStruct((H, S, W), F32), compiler_params=_params(32),
    )(q, k, k, v, v)


def band_bwd(q, k, v, do, st, L, name, comm=None):
    H, S, W = q.shape
    T = min(512, L)
    nb = T // BLK
    last_blk = S // BLK - 1

    def body(q_ref, qn_ref, do_ref, don_ref, st_ref, stn_ref, k_ref, kp_ref, v_ref, vp_ref,
             dq_ref, dk_ref, dv_ref, dk_acc, dv_acc):
        i = pl.program_id(1)
        first_key = jnp.where((i * T) % L != 0, 0, BLK)
        next_off = jnp.where(((i + 1) * T) % L != 0, 0, 4 * BLK)
        qi = lax.broadcasted_iota(jnp.int32, (BLK, 2 * BLK), 0)
        kj = lax.broadcasted_iota(jnp.int32, (BLK, 2 * BLK), 1)
        band = (kj >= qi) & (kj <= qi + BLK)
        qi1 = lax.broadcasted_iota(jnp.int32, (BLK, BLK), 0)
        kj1 = lax.broadcasted_iota(jnp.int32, (BLK, BLK), 1)
        dk_acc[...] = jnp.zeros_like(dk_acc)
        dv_acc[...] = jnp.zeros_like(dv_acc)
        for b in range(nb + 1):
            rows = slice(b * BLK, (b + 1) * BLK)
            if b < nb:
                qb, dob, stb = q_ref[rows, :], do_ref[rows, :], st_ref[rows, :]
            else:
                qb, dob, stb = qn_ref[...], don_ref[...], stn_ref[...]
            if b == 0:
                kw = jnp.concatenate([kp_ref[...], k_ref[rows, :]], axis=0)
                vw = jnp.concatenate([vp_ref[...], v_ref[rows, :]], axis=0)
                mask = band & (kj >= first_key)
            elif b < nb:
                kw = k_ref[(b - 1) * BLK:(b + 1) * BLK, :]
                vw = v_ref[(b - 1) * BLK:(b + 1) * BLK, :]
                mask = band
            else:
                kw = k_ref[(nb - 1) * BLK:nb * BLK, :]
                vw = v_ref[(nb - 1) * BLK:nb * BLK, :]
                mask = kj1 >= qi1 + next_off
            s = _dot(qb, kw, NT) * ATTN_SCALE
            p = jnp.where(mask, jnp.exp(s - _lane_stat(stb, False)), 0.0)
            dp = _dot(dob, vw, NT)
            ds = (p * (dp - _lane_stat(stb, True))).astype(BF16)
            if b < nb:
                dq_ref[rows, :] = _dot(ds, kw) * ATTN_SCALE
            win = slice(b * BLK, b * BLK + kw.shape[0])
            dk_acc[win, :] += _dot(ds, qb, TN) * ATTN_SCALE
            dv_acc[win, :] += _dot(p.astype(BF16), dob, TN)
        dk_ref[...] = dk_acc[BLK:, :]
        dv_ref[...] = dv_acc[BLK:, :]

    cur = pl.BlockSpec((None, T, W), lambda h, i: (h, i, 0))
    prev = pl.BlockSpec((None, BLK, W), lambda h, i: (h, jnp.maximum(i * nb - 1, 0), 0))
    nxt = pl.BlockSpec((None, BLK, W), lambda h, i: (h, jnp.minimum((i + 1) * nb, last_blk), 0))
    shp = jax.ShapeDtypeStruct((H, S, W), F32)
    return _call(
        body, name=name, grid=(H, S // T),
        in_specs=[cur, nxt, cur, nxt, cur, nxt, cur, prev, cur, prev],
        out_specs=[cur, cur, cur], out_shape=[shp, shp, shp],
        scratch=[pltpu.VMEM((T + BLK, W), F32), pltpu.VMEM((T + BLK, W), F32)],
        args=(q, q, do, do, st, st, k, k, v, v), vmem=32, comm=comm)


def attn_merge(os_, name):
    H, S, W = os_[0].shape
    tm = _tile(S, 1024)

    def body(*refs):
        o_ref = refs[-1]
        vals = [r[...] for r in refs[:-1]]
        lses = [_lane_stat(v, True) for v in vals]
        m = functools.reduce(jnp.maximum, lses)
        ws = [jnp.exp(l - m) for l in lses]
        tot = functools.reduce(jnp.add, ws)
        out = functools.reduce(jnp.add, [(w / tot) * v for w, v in zip(ws, vals)])
        lane = lax.broadcasted_iota(jnp.int32, out.shape, 1)
        o_ref[...] = jnp.where(lane < HEAD_DIM, out, m + jnp.log(tot))

    spec = pl.BlockSpec((None, tm, W), lambda h, i: (h, i, 0))
    return pl.pallas_call(
        body, name=name, grid=(H, S // tm), in_specs=[spec] * len(os_), out_specs=spec,
        out_shape=jax.ShapeDtypeStruct((H, S, W), F32), compiler_params=_params(32),
    )(*os_)


def attn_bwd_stats(merged, do, name):
    H, S, W = merged.shape
    tm = _tile(S, 1024)

    def body(m_ref, do_ref, st_ref):
        mv = m_ref[...]
        delta = jnp.sum(mv * do_ref[...].astype(F32), axis=-1, keepdims=True)
        lane = lax.broadcasted_iota(jnp.int32, mv.shape, 1)
        st_ref[...] = jnp.where(lane < HEAD_DIM, _lane_stat(mv, True), delta)

    spec = pl.BlockSpec((None, tm, W), lambda h, i: (h, i, 0))
    return pl.pallas_call(
        body, name=name, grid=(H, S // tm), in_specs=[spec, spec], out_specs=spec,
        out_shape=jax.ShapeDtypeStruct((H, S, W), F32), compiler_params=_params(32),
    )(merged, do)


def sum_patterns(ts, name):
    H, S, W = ts[0].shape
    tm = _tile(S, 1024)

    def body(*refs):
        refs[-1][...] = functools.reduce(jnp.add, [r[...] for r in refs[:-1]]).astype(BF16)

    spec = pl.BlockSpec((None, tm, W), lambda h, i: (h, i, 0))
    return pl.pallas_call(
        body, name=name, grid=(H, S // tm), in_specs=[spec] * len(ts), out_specs=spec,
        out_shape=jax.ShapeDtypeStruct((H, S, W), BF16), compiler_params=_params(32),
    )(*ts)


def to_heads(t):
    S = t.shape[0]
    t = jnp.transpose(t.reshape(S, -1, HEAD_DIM), (1, 0, 2))
    return jnp.pad(t, ((0, 0), (0, 0), (0, LANES - HEAD_DIM)))


def from_heads(t):
    H, S, _ = t.shape
    return jnp.transpose(t[:, :, :HEAD_DIM], (1, 0, 2)).reshape(S, H * HEAD_DIM)


def permute(t, d):
    if d == 1:
        return t
    H, S, W = t.shape
    return jnp.transpose(t.reshape(H, S // d, d, W), (0, 2, 1, 3)).reshape(H, S, W)


def unpermute(t, d):
    if d == 1:
        return t
    H, S, W = t.shape
    return jnp.transpose(t.reshape(H, d, S // d, W), (0, 2, 1, 3)).reshape(H, S, W)


ATT_TILE = BLK * max(DILATIONS)


def _head0(shape):
    return lax.broadcasted_iota(jnp.int32, shape, 1) < HEAD_DIM


def _pair_col(v, sel):
    return jnp.max(jnp.where(sel, v, -jnp.inf), axis=-1, keepdims=True)


def _stack_heads(x, h0):
    zero = jnp.zeros_like(x)
    return jnp.concatenate([jnp.where(h0, x, zero), jnp.where(h0, zero, x)], axis=0)


def _band(lo, hi):
    qi = lax.broadcasted_iota(jnp.int32, (BLK, 2 * BLK), 0)
    kj = lax.broadcasted_iota(jnp.int32, (BLK, 2 * BLK), 1)
    return (kj >= qi) & (kj <= qi + BLK) & (kj >= lo) & (kj < hi)


def _deinterleave(dst, src, d, rows, dst_stride, dst_off=0, src_off=0, cast=None):
    for r in range(d):
        v = src[pl.ds(src_off + r, rows, stride=d), :] if d > 1 else src[pl.ds(src_off, rows), :]
        dst[pl.ds(r * dst_stride + dst_off, rows), :] = v if cast is None else v.astype(cast)


def _interleave(dst, src, d, rows, src_stride, src_off=0, add=False):
    for r in range(d):
        v = src[pl.ds(r * src_stride + src_off, rows), :]
        idx = pl.ds(r, rows, stride=d) if d > 1 else pl.ds(0, rows)
        dst[idx, :] = dst[idx, :] + v if add else v


def attn_fwd(z, A, name, comm=None):
    S = z.shape[0]
    T = ATT_TILE
    Hp, nt = A // LANES, S // T
    np_ = len(DILATIONS)

    def body(q_ref, k_ref, kp_ref, v_ref, vp_ref, y_ref, lse_ref, qn, kn, vn, qp, kp, vp, accp, mp, lp, *nat):
        accs, ms, ls = nat[:np_], nat[np_:2 * np_], nat[2 * np_:]
        i = pl.program_id(1)
        qn[...] = q_ref[...].astype(F32)
        kn[0:T, :] = kp_ref[...].astype(F32)
        kn[T:2 * T, :] = k_ref[...].astype(F32)
        vn[0:T, :] = vp_ref[...].astype(F32)
        vn[T:2 * T, :] = v_ref[...].astype(F32)
        h0 = _head0((BLK, LANES))
        for pi, d in enumerate(DILATIONS):
            Ld = T // d
            nblk = Ld // BLK
            _deinterleave(qp, qn, d, Ld, Ld, cast=BF16)
            for src, dst in ((kn, kp), (vn, vp)):
                _deinterleave(dst, src, d, Ld, 2 * Ld, cast=BF16)
                _deinterleave(dst, src, d, Ld, 2 * Ld, dst_off=Ld, src_off=T, cast=BF16)

            def unit(u, carry, nblk=nblk):
                r, n = u // nblk, u % nblk
                q0 = pl.multiple_of(u * BLK, BLK)
                k0 = pl.multiple_of((2 * r * nblk + nblk + n - 1) * BLK, BLK)
                qb = qp[pl.ds(q0, BLK), :]
                kw, vw = kp[pl.ds(k0, 2 * BLK), :], vp[pl.ds(k0, 2 * BLK), :]
                mask = _band(jnp.where((i == 0) & (n == 0), BLK, 0), 2 * BLK)
                s = _dot(_stack_heads(qb, h0), kw, NT) * ATTN_SCALE
                s = jnp.where(jnp.concatenate([mask, mask], axis=0), s, NEG_INF)
                mx = jnp.max(s, axis=-1, keepdims=True)
                p = jnp.exp(s - mx)
                l = jnp.sum(p, axis=-1, keepdims=True)
                o = _dot(p.astype(BF16), vw)
                accp[pl.ds(q0, BLK), :] = jnp.where(h0, o[:BLK], o[BLK:])
                mp[pl.ds(q0, BLK), :] = jnp.where(h0, mx[:BLK], mx[BLK:])
                lp[pl.ds(q0, BLK), :] = jnp.where(h0, l[:BLK], l[BLK:])
                return carry

            lax.fori_loop(0, T // BLK, unit, 0, unroll=True)
            for src, dst in ((accp, accs[pi]), (mp, ms[pi]), (lp, ls[pi])):
                _interleave(dst, src, d, Ld, Ld)
        mv = [m[...] for m in ms]
        mx = mv[0]
        for m in mv[1:]:
            mx = jnp.maximum(mx, m)
        ws = [jnp.exp(m - mx) for m in mv]
        l = sum(w * lr[...] for w, lr in zip(ws, ls))
        a = sum(w * ar[...] for w, ar in zip(ws, accs))
        y_ref[...] = (a / l).astype(BF16)
        lse_ref[...] = mx + jnp.log(l)

    kb, vb = A // LANES, 2 * A // LANES
    cur = lambda off: pl.BlockSpec((T, LANES), lambda h, i: (i, off + h))
    prev = lambda off: pl.BlockSpec((T, LANES), lambda h, i: (jnp.maximum(i - 1, 0), off + h))
    out = pl.BlockSpec((T, LANES), lambda h, i: (i, h))
    vm = lambda rows, dt: pltpu.VMEM((rows, LANES), dt)
    (y, lse), got = _call(
        body, name=name, grid=(Hp, nt),
        in_specs=[cur(0), cur(kb), prev(kb), cur(vb), prev(vb)], out_specs=[out, out],
        out_shape=[jax.ShapeDtypeStruct((S, A), BF16), jax.ShapeDtypeStruct((S, A), F32)],
        scratch=[vm(T, F32), vm(2 * T, F32), vm(2 * T, F32), vm(T, BF16), vm(2 * T, BF16), vm(2 * T, BF16),
                 vm(T, F32), vm(T, F32), vm(T, F32)] + [vm(T, F32)] * (3 * np_),
        args=(z, z, z, z, z), vmem=48, comm=comm)
    return y, lse, got


def attn_bwd(z, dy, y, lse, A, name, comm=None):
    S = z.shape[0]
    T = ATT_TILE
    Hp, nt = A // LANES, S // T

    def body(q_ref, k_ref, kp_ref, v_ref, vp_ref, do_ref, y_ref, ls_ref, dq_ref, dk_ref, dv_ref,
             qn, don, dln, kn, vn, qp, dop, lsp, dlp, kp, vp, dqp, dkp, dvp, dqa, dka, dva, dkc, dvc):
        step = pl.program_id(1)
        i = nt - 1 - step
        h0t = _head0((T, LANES))
        dof = do_ref[...].astype(F32)
        prod = dof * y_ref[...].astype(F32)
        d0 = jnp.sum(jnp.where(h0t, prod, 0.0), axis=-1, keepdims=True)
        d1 = jnp.sum(jnp.where(h0t, 0.0, prod), axis=-1, keepdims=True)
        qn[...] = q_ref[...].astype(F32)
        don[...] = dof
        dln[...] = jnp.where(h0t, d0, d1)
        kn[0:T, :] = kp_ref[...].astype(F32)
        kn[T:2 * T, :] = k_ref[...].astype(F32)
        vn[0:T, :] = vp_ref[...].astype(F32)
        vn[T:2 * T, :] = v_ref[...].astype(F32)

        @pl.when(step == 0)
        def _():
            dka[...] = jnp.zeros_like(dka)
            dva[...] = jnp.zeros_like(dva)

        @pl.when(step > 0)
        def _():
            dka[...] = dkc[...]
            dva[...] = dvc[...]

        h0 = _head0((BLK, LANES))
        for pi, d in enumerate(DILATIONS):
            Ld = T // d
            nblk = Ld // BLK
            for src, dst, cast in ((qn, qp, BF16), (don, dop, BF16), (dln, dlp, None)):
                _deinterleave(dst, src, d, Ld, Ld, cast=cast)
            _deinterleave(lsp, ls_ref, d, Ld, Ld)
            for src, dst in ((kn, kp), (vn, vp)):
                _deinterleave(dst, src, d, Ld, 2 * Ld, cast=BF16)
                _deinterleave(dst, src, d, Ld, 2 * Ld, dst_off=Ld, src_off=T, cast=BF16)
            dkp[...] = jnp.zeros_like(dkp)
            dvp[...] = jnp.zeros_like(dvp)

            def unit(u, carry, nblk=nblk):
                r, b = u // nblk, u % nblk
                q0 = pl.multiple_of(u * BLK, BLK)
                k0 = pl.multiple_of((2 * r * nblk + nblk + b - 1) * BLK, BLK)
                mask = _band(jnp.where((i == 0) & (b == 0), BLK, 0), 2 * BLK)
                qb, dob = qp[pl.ds(q0, BLK), :], dop[pl.ds(q0, BLK), :]
                lsb, dlb = lsp[pl.ds(q0, BLK), :], dlp[pl.ds(q0, BLK), :]
                kw, vw = kp[pl.ds(k0, 2 * BLK), :], vp[pl.ds(k0, 2 * BLK), :]
                qs, dos = _stack_heads(qb, h0), _stack_heads(dob, h0)
                lse = jnp.concatenate([_pair_col(lsb, h0), _pair_col(lsb, ~h0)], axis=0)
                delta = jnp.concatenate([_pair_col(dlb, h0), _pair_col(dlb, ~h0)], axis=0)
                s = _dot(qs, kw, NT) * ATTN_SCALE
                p = jnp.where(jnp.concatenate([mask, mask], axis=0), jnp.exp(s - lse), 0.0)
                ds = (p * (_dot(dos, vw, NT) - delta)).astype(BF16)
                dq = _dot(ds, kw) * ATTN_SCALE
                dqp[pl.ds(q0, BLK), :] = jnp.where(h0, dq[:BLK], dq[BLK:])
                dkp[pl.ds(k0, 2 * BLK), :] += _dot(ds, qs, TN) * ATTN_SCALE
                dvp[pl.ds(k0, 2 * BLK), :] += _dot(p.astype(BF16), dos, TN)
                return carry

            lax.fori_loop(0, T // BLK, unit, 0, unroll=True)
            _interleave(dqa, dqp, d, Ld, Ld, add=pi > 0)
            for acc, nxt, part in ((dka, dkc, dkp), (dva, dvc, dvp)):
                _interleave(acc, part, d, Ld, 2 * Ld, src_off=Ld, add=True)
                _interleave(nxt, part, d, Ld, 2 * Ld, add=pi > 0)
        dq_ref[...] = dqa[...].astype(BF16)
        dk_ref[...] = dka[...].astype(BF16)
        dv_ref[...] = dva[...].astype(BF16)

    kb, vb = A // LANES, 2 * A // LANES
    cur = lambda off: pl.BlockSpec((T, LANES), lambda h, s: (nt - 1 - s, off + h))
    prev = lambda off: pl.BlockSpec((T, LANES), lambda h, s: (jnp.maximum(nt - 2 - s, 0), off + h))
    vm = lambda rows, dt: pltpu.VMEM((rows, LANES), dt)
    shp = jax.ShapeDtypeStruct((S, A), BF16)
    return _call(
        body, name=name, grid=(Hp, nt),
        in_specs=[cur(0), cur(kb), prev(kb), cur(vb), prev(vb), cur(0), cur(0), cur(0)],
        out_specs=[cur(0)] * 3, out_shape=[shp] * 3,
        scratch=[vm(T, F32), vm(T, F32), vm(T, F32), vm(2 * T, F32), vm(2 * T, F32),
                 vm(T, BF16), vm(T, BF16), vm(T, F32), vm(T, F32), vm(2 * T, BF16), vm(2 * T, BF16),
                 vm(T, F32), vm(2 * T, F32), vm(2 * T, F32)] + [vm(T, F32)] * 5,
        args=(z, z, z, z, z, dy, y, lse), vmem=56, comm=comm)


def _shift_down(e, k):
    return pltpu.roll(e, k, 0)


def _shift_up(e, k):
    return pltpu.roll(e, e.shape[0] - k, 0)


def _causal_sums(e, g):
    for lvl in range(g + 1):
        e = e + _shift_down(e, 1 << lvl)
    return e


def _anticausal_sums(e, g):
    for lvl in range(g + 1):
        e = e + _shift_up(e, 1 << lvl)
    return e


def _mixer_specs(S, tm, A, C, P):
    assert (3 * A) % C == 0 and (3 * A + 3 * C) % P == 0 and tm % HALO == 0
    cb, pb = 3 * A // C, (3 * A + 3 * C) // P
    hb = tm // HALO
    last = S // HALO - 1
    col = lambda w, j: pl.BlockSpec((tm, w), lambda i: (i, j))
    prev = lambda w, j: pl.BlockSpec((HALO, w), lambda i: (jnp.maximum(i * hb - 1, 0), j))
    nxt = lambda w, j: pl.BlockSpec((HALO, w), lambda i: (jnp.minimum((i + 1) * hb, last), j))
    return cb, pb, col, prev, nxt


def _conv_taps(gc, ci, gch, cih, keep_prev):
    u = gc * ci
    e = jnp.concatenate([gch * cih * keep_prev, u], axis=0)
    return u, _shift_down(e, 1)[HALO:], _shift_down(e, 2)[HALO:]


def _pooled(xp, xph, keep_prev, pos, g):
    cols = slice(g * POOL_GROUP, (g + 1) * POOL_GROUP)
    x = xp[:, cols]
    e = jnp.concatenate([xph[:, cols] * keep_prev, x], axis=0)
    cnt = jnp.minimum(pos + 1, POOL_WINDOWS[g]).astype(F32)
    return _causal_sums(e, g)[HALO:] / cnt - x


def convpool_fwd(z, conv_w, pool_w, pool_scale, A, C, P, name):
    S = z.shape[0]
    tm = _tile(S, 512)
    cb, pb, col, prev, _ = _mixer_specs(S, tm, A, C, P)

    def body(gb_ref, gc_ref, ci_ref, xp_ref, gch_ref, cih_ref, xph_ref, cw_ref, pw_ref, ps_ref, y_ref):
        i = pl.program_id(0)
        keep_prev = jnp.where(i == 0, 0.0, 1.0)
        f = lambda r: r[...].astype(F32)
        u, u1, u2 = _conv_taps(f(gc_ref), f(ci_ref), f(gch_ref), f(cih_ref), keep_prev)
        cw = cw_ref[...]
        y_ref[:, 0:C] = (f(gb_ref) * (cw[0:1] * u2 + cw[1:2] * u1 + cw[2:3] * u)).astype(BF16)
        xp, xph = f(xp_ref), f(xph_ref)
        pos = i * tm + lax.broadcasted_iota(jnp.int32, (tm, 1), 0)
        for g in range(len(POOL_WINDOWS)):
            cols = slice(g * POOL_GROUP, (g + 1) * POOL_GROUP)
            lin = _dot(_pooled(xp, xph, keep_prev, pos, g).astype(BF16), pw_ref[g])
            y_ref[:, C + g * POOL_GROUP:C + (g + 1) * POOL_GROUP] = (lin * ps_ref[:, cols]).astype(BF16)

    full = lambda shape: pl.BlockSpec(shape, lambda i: (0,) * len(shape))
    return pl.pallas_call(
        body, name=name, grid=(S // tm,),
        in_specs=[col(C, cb), col(C, cb + 1), col(C, cb + 2), col(P, pb),
                  prev(C, cb + 1), prev(C, cb + 2), prev(P, pb),
                  full(conv_w.shape), full(pool_w.shape), full(pool_scale.shape)],
        out_specs=pl.BlockSpec((tm, C + P), lambda i: (i, 0)),
        out_shape=jax.ShapeDtypeStruct((S, C + P), BF16), compiler_params=_params(48),
    )(z, z, z, z, z, z, z, conv_w, pool_w, pool_scale)


def convpool_bwd(z, dycp, conv_w, pool_w, pool_scale, A, C, P, name):
    S = z.shape[0]
    tm = _tile(S, 512)
    nt = S // tm
    cb, pb, col, prev, nxt = _mixer_specs(S, tm, A, C, P)
    NG = len(POOL_WINDOWS)

    def body(gb_ref, gc_ref, ci_ref, xp_ref, gch_ref, cih_ref, xph_ref, gbn_ref, dy_ref, dyn_ref,
             cw_ref, pw_ref, ps_ref, dz_ref, dcw_ref, dpw_ref, dps_ref):
        i = pl.program_id(0)

        @pl.when(i == 0)
        def _():
            dcw_ref[...] = jnp.zeros_like(dcw_ref)
            dpw_ref[...] = jnp.zeros_like(dpw_ref)
            dps_ref[...] = jnp.zeros_like(dps_ref)

        keep_prev = jnp.where(i == 0, 0.0, 1.0)
        keep_next = jnp.where(i == nt - 1, 0.0, 1.0)
        f = lambda r: r[...].astype(F32)
        gb, gc, ci = f(gb_ref), f(gc_ref), f(ci_ref)
        u, u1, u2 = _conv_taps(gc, ci, f(gch_ref), f(cih_ref), keep_prev)
        cw = cw_ref[...]
        dy, dyn = f(dy_ref), f(dyn_ref) * keep_next
        dyc = dy[:, 0:C]
        dz_ref[:, 0:C] = (dyc * (cw[0:1] * u2 + cw[1:2] * u1 + cw[2:3] * u)).astype(BF16)
        dc = dyc * gb
        e = jnp.concatenate([dc, dyn[:, 0:C] * f(gbn_ref)], axis=0)
        du = cw[2:3] * dc + cw[1:2] * _shift_up(e, 1)[:tm] + cw[0:1] * _shift_up(e, 2)[:tm]
        dz_ref[:, C:2 * C] = (du * ci).astype(BF16)
        dz_ref[:, 2 * C:3 * C] = (du * gc).astype(BF16)
        dcw_ref[0:1, :] += jnp.sum(dc * u2, axis=0, keepdims=True)
        dcw_ref[1:2, :] += jnp.sum(dc * u1, axis=0, keepdims=True)
        dcw_ref[2:3, :] += jnp.sum(dc * u, axis=0, keepdims=True)

        xp, xph = f(xp_ref), f(xph_ref)
        pos = i * tm + lax.broadcasted_iota(jnp.int32, (tm, 1), 0)
        pos_e = i * tm + lax.broadcasted_iota(jnp.int32, (tm + HALO, 1), 0)
        for g in range(NG):
            cols = slice(g * POOL_GROUP, (g + 1) * POOL_GROUP)
            ycols = slice(C + g * POOL_GROUP, C + (g + 1) * POOL_GROUP)
            pooled = _pooled(xp, xph, keep_prev, pos, g).astype(BF16)
            pw = pw_ref[g]
            dyp = dy[:, ycols]
            dps_ref[:, cols] += jnp.sum(dyp * _dot(pooled, pw), axis=0, keepdims=True)
            dlin = (jnp.concatenate([dyp, dyn[:, ycols]], axis=0) * ps_ref[:, cols]).astype(BF16)
            dpw_ref[g] += _dot(pooled, dlin[:tm], TN)
            dpool = _dot(dlin, pw, NT)
            r = dpool / jnp.minimum(pos_e + 1, POOL_WINDOWS[g]).astype(F32)
            dz_ref[:, 3 * C + g * POOL_GROUP:3 * C + (g + 1) * POOL_GROUP] = (
                _anticausal_sums(r, g)[:tm] - dpool[:tm]).astype(BF16)

    full = lambda shape: pl.BlockSpec(shape, lambda i: (0,) * len(shape))
    return pl.pallas_call(
        body, name=name, grid=(nt,),
        in_specs=[col(C, cb), col(C, cb + 1), col(C, cb + 2), col(P, pb),
                  prev(C, cb + 1), prev(C, cb + 2), prev(P, pb), nxt(C, cb),
                  pl.BlockSpec((tm, C + P), lambda i: (i, 0)),
                  pl.BlockSpec((HALO, C + P), lambda i: (jnp.minimum((i + 1) * (tm // HALO), S // HALO - 1), 0)),
                  full(conv_w.shape), full(pool_w.shape), full(pool_scale.shape)],
        out_specs=[pl.BlockSpec((tm, 3 * C + P), lambda i: (i, 0)),
                   full(conv_w.shape), full(pool_w.shape), full(pool_scale.shape)],
        out_shape=[jax.ShapeDtypeStruct((S, 3 * C + P), BF16), jax.ShapeDtypeStruct(conv_w.shape, F32),
                   jax.ShapeDtypeStruct(pool_w.shape, F32), jax.ShapeDtypeStruct(pool_scale.shape, F32)],
        compiler_params=_params(48),
    )(z, z, z, z, z, z, z, z, dycp, dycp, conv_w, pool_w, pool_scale)


def adamw(lands, w, m, v, name):
    R, C = w.shape
    nl = len(lands)
    n, Rl, _ = lands[0].shape
    assert Rl * nl == R
    tr = _tile(Rl, 128 * 1024 // C)
    nb = Rl // tr
    c1 = 1.0 - ADAM_B1 ** ADAM_STEP
    c2 = 1.0 - ADAM_B2 ** ADAM_STEP

    def body(*refs):
        land_refs = refs[:nl]
        w_ref, m_ref, v_ref, g_ref, d_ref, mo_ref, vo_ref = refs[nl:]
        i = pl.program_id(0)
        for a, land_ref in enumerate(land_refs):
            @pl.when((i >= a * nb) & (i < (a + 1) * nb))
            def _():
                g = land_ref[0].astype(F32)
                for s in range(1, n):
                    g = g + land_ref[s].astype(F32)
                g_ref[...] = g
                mn = ADAM_B1 * m_ref[...] + (1.0 - ADAM_B1) * g
                vn = ADAM_B2 * v_ref[...] + (1.0 - ADAM_B2) * (g * g)
                mo_ref[...] = mn
                vo_ref[...] = vn
                d_ref[...] = -ADAM_LR * ((mn / c1) / (jnp.sqrt(vn / c2) + ADAM_EPS) + ADAM_WD * w_ref[...])

    land_specs = [pl.BlockSpec((n, tr, C), lambda i, a=a: (0, jnp.clip(i - a * nb, 0, nb - 1), 0))
                  for a in range(nl)]
    row = pl.BlockSpec((tr, C), lambda i: (i, 0))
    shp = jax.ShapeDtypeStruct((R, C), F32)
    return pl.pallas_call(
        body, name=name, grid=(nl * nb,),
        in_specs=land_specs + [row, row, row], out_specs=[row] * 4, out_shape=[shp] * 4,
        compiler_params=_params(48),
    )(*lands, w, m, v)


def _gather_comm(arrs):
    return ("gather", arrs) if arrs else None


def _ffn_fwd(x, gain, wg, wu, wd, tag, carry_up=(), carry_down=()):
    h, ht = rms_fwd(x, gain, f"rms_{tag}")
    (a, b, hid), got_up = ffn_up(h, wg, wu, f"ffn_up_{tag}", _gather_comm(list(carry_up)))
    xo, got_down = ffn_down(hid, wd, x, f"ffn_down_{tag}", _gather_comm(list(carry_down)))
    return xo, (x, ht, a, b, hid), got_up, got_down


def _ffn_bwd(g, gb, saved, gain, wg, wu, wd, tag):
    x, ht, a, b, hid = saved
    da, db = ffn_dhid(gb, wd, a, b, f"ffn_dhid_{tag}")
    dwd = grad_shared_rhs(hid, gb, None, FFN_RESIDUAL, f"ffn_dwd_{tag}")
    dwg, (land_wd,) = grad_lhs(ht, da, True, f"ffn_dwg_{tag}", ("exchange", [dwd]))
    dwu, (land_wg,) = grad_lhs(ht, db, True, f"ffn_dwu_{tag}", ("exchange", [dwg]))
    dh, (land_wu,) = back_proj([da, db], [wg, wu], True, f"ffn_dh_{tag}", ("exchange", [dwu]))
    g, gb, dgain = rms_bwd(x, gain, dh, g, f"rms_bwd_{tag}")
    return g, gb, dgain, (land_wg, land_wu, land_wd)


def _mixer_fwd(x, gain, w_in, conv_w, pool_w, pool_scale, w_out, dims, tag, carry=()):
    A, C, P = dims
    h, ht = rms_fwd(x, gain, f"rms_{tag}")
    z = proj_cols(h, w_in, f"in_proj_{tag}")
    ya, lse, got = attn_fwd(z, A, f"attn_fwd_{tag}", _gather_comm(list(carry)))
    ycp = convpool_fwd(z, conv_w, pool_w, pool_scale, A, C, P, f"convpool_{tag}")
    xo = out_proj(ya, ycp, w_out, x, f"out_proj_{tag}")
    return xo, (x, ht, z, lse, ya, ycp), got


def _mixer_bwd(g, gb, saved, gain, w_in, conv_w, pool_w, pool_scale, w_out, dims, tag):
    A, C, P = dims
    x, ht, z, lse, ya, ycp = saved
    S, D = x.shape
    dya, dycp = out_proj_bwd(gb, w_out, A, f"out_proj_bwd_{tag}")
    dwo_a = grad_shared_rhs(ya, gb, 1, 1.0, f"dwo_attn_{tag}")
    dwo_cp = grad_shared_rhs(ycp, gb, 1, 1.0, f"dwo_cp_{tag}")
    dwo = jnp.concatenate([dwo_a[0], dwo_cp[0]], axis=0).reshape(NDEV, -1, D)
    dqkv, (land_wo,) = attn_bwd(z, dya, ya, lse, A, f"attn_bwd_{tag}", ("exchange", [dwo]))
    dz_cp, dcw, dpw, dps = convpool_bwd(z, dycp, conv_w, pool_w, pool_scale, A, C, P, f"convpool_bwd_{tag}")
    dz = jnp.concatenate(dqkv + [dz_cp], axis=1)
    dwin, _ = grad_lhs(ht, dz, False, f"dwin_{tag}")
    dh, (land_win,) = back_proj([dz], [w_in], False, f"in_proj_bwd_{tag}", ("exchange", [dwin]))
    g, gb, dgain = rms_bwd(x, gain, dh, g, f"rms_bwd_{tag}")
    return g, gb, (dgain, dcw, dpw, dps), (land_win, land_wo)


def _pack(arrs):
    flat = [a.reshape(-1).astype(F32) for a in arrs]
    spans, off = [], 0
    for a in flat:
        spans.append((off, a.shape[0]))
        off += a.shape[0]
    rows = -(-off // (8 * LANES)) * 8
    buf = jnp.concatenate(flat + [jnp.zeros((rows * LANES - off,), F32)]).reshape(rows, LANES)
    return buf, spans


def kernel(x, ffn1_norm, ffn1_w_gate, ffn1_w_up, ffn1_w_down, mix_norm, w_in, conv_w, pool_w, pool_scale, w_out, ffn2_norm, ffn2_w_gate, ffn2_w_up, ffn2_w_down, final_norm, loss_target, m_ffn1_norm, m_ffn1_w_gate, m_ffn1_w_up, m_ffn1_w_down, m_mix_norm, m_w_in, m_conv_w, m_pool_w, m_pool_scale, m_w_out, m_ffn2_norm, m_ffn2_w_gate, m_ffn2_w_up, m_ffn2_w_down, m_final_norm, v_ffn1_norm, v_ffn1_w_gate, v_ffn1_w_up, v_ffn1_w_down, v_mix_norm, v_w_in, v_conv_w, v_pool_w, v_pool_scale, v_w_out, v_ffn2_norm, v_ffn2_w_gate, v_ffn2_w_up, v_ffn2_w_down, v_final_norm):
    depth = ffn1_norm.shape[0]
    S, D = x.shape[1], x.shape[2]
    Cb = conv_w.shape[2]
    C = Cb * NDEV
    P = pool_scale.shape[1]
    A = (w_in.shape[2] * NDEV - 3 * C - P) // 3
    dims = (A, C, P)
    me = 4 * lax.axis_index("x") + 2 * lax.axis_index("y") + lax.axis_index("c")

    big = dict(ffn1_w_gate=ffn1_w_gate, ffn1_w_up=ffn1_w_up, ffn1_w_down=ffn1_w_down, w_in=w_in, w_out=w_out,
               ffn2_w_gate=ffn2_w_gate, ffn2_w_up=ffn2_w_up, ffn2_w_down=ffn2_w_down)
    big_m = dict(ffn1_w_gate=m_ffn1_w_gate, ffn1_w_up=m_ffn1_w_up, ffn1_w_down=m_ffn1_w_down, w_in=m_w_in,
                 w_out=m_w_out, ffn2_w_gate=m_ffn2_w_gate, ffn2_w_up=m_ffn2_w_up, ffn2_w_down=m_ffn2_w_down)
    big_v = dict(ffn1_w_gate=v_ffn1_w_gate, ffn1_w_up=v_ffn1_w_up, ffn1_w_down=v_ffn1_w_down, w_in=v_w_in,
                 w_out=v_w_out, ffn2_w_gate=v_ffn2_w_gate, ffn2_w_up=v_ffn2_w_up, ffn2_w_down=v_ffn2_w_down)
    names = list(big)

    first_names = ["ffn1_w_gate", "ffn1_w_up"]
    ffn1_names = first_names + ["ffn1_w_down"]
    mix_names = ["w_in", "w_out"]
    ffn2_names = ["ffn2_w_gate", "ffn2_w_up", "ffn2_w_down"]
    shards = lambda l, ns: [big[n][l].astype(BF16) for n in ns]
    full = [dict() for _ in range(depth)]
    full[0].update(zip(first_names, all_gather(shards(0, first_names), "gather_first")))
    conv_full = all_gather([jnp.pad(conv_w.reshape(-1, Cb), ((0, 2), (0, LANES - Cb)))], "gather_conv_w")[0]
    conv_full = jnp.transpose(conv_full[:, :depth * 3, :Cb].reshape(NDEV, depth, 3, Cb), (1, 2, 0, 3)).reshape(depth, 3, C)
    pool_w_bf = pool_w.astype(BF16)

    xs = x[0]
    saved = []
    for l in range(depth):
        W = full[l]
        up1 = ["ffn1_w_down"] + mix_names if l == 0 else []
        h, ht = rms_fwd(xs, ffn1_norm[l:l + 1], f"rms_f1_l{l}")
        (a, b, hid), got_up = ffn_up(h, W["ffn1_w_gate"], W["ffn1_w_up"], f"ffn_up_f1_l{l}",
                                     _gather_comm(shards(l, up1)))
        W.update(zip(up1, got_up))
        xo, _ = ffn_down(hid, W["ffn1_w_down"], xs, f"ffn_down_f1_l{l}")
        s1, xs = (xs, ht, a, b, hid), xo
        xs, s2, got = _mixer_fwd(xs, mix_norm[l:l + 1], W["w_in"], conv_full[l], pool_w_bf[l], pool_scale[l:l + 1],
                                 W["w_out"].reshape(-1, D), dims, f"mix_l{l}", shards(l, ffn2_names))
        W.update(zip(ffn2_names, got))
        up2, down2 = (ffn1_names, mix_names) if l + 1 < depth else ([], [])
        xs, s3, got_up, got_down = _ffn_fwd(xs, ffn2_norm[l:l + 1], W["ffn2_w_gate"], W["ffn2_w_up"],
                                            W["ffn2_w_down"], f"f2_l{l}", shards(l + 1, up2) if up2 else (),
                                            shards(l + 1, down2) if down2 else ())
        if l + 1 < depth:
            full[l + 1].update(zip(up2, got_up))
            full[l + 1].update(zip(down2, got_down))
        saved.append((s1, s2, s3))
    loss_part, g, gb, d_final = final_loss(xs, final_norm.reshape(1, D), loss_target[0], "final_loss")

    small = [None] * depth
    lands = [None] * depth
    for l in reversed(range(depth)):
        W = full[l]
        s1, s2, s3 = saved[l]
        g, gb, dn2, (lg2, lu2, ld2) = _ffn_bwd(g, gb, s3, ffn2_norm[l:l + 1], W["ffn2_w_gate"], W["ffn2_w_up"],
                                               W["ffn2_w_down"], f"f2_l{l}")
        g, gb, (dnm, dcw, dpw, dps), (lwin, lwo) = _mixer_bwd(
            g, gb, s2, mix_norm[l:l + 1], W["w_in"], conv_full[l], pool_w_bf[l], pool_scale[l:l + 1],
            W["w_out"].reshape(-1, D), dims, f"mix_l{l}")
        g, gb, dn1, (lg1, lu1, ld1) = _ffn_bwd(g, gb, s1, ffn1_norm[l:l + 1], W["ffn1_w_gate"], W["ffn1_w_up"],
                                               W["ffn1_w_down"], f"f1_l{l}")
        lands[l] = dict(ffn1_w_gate=lg1, ffn1_w_up=lu1, ffn1_w_down=ld1, w_in=lwin, w_out=lwo,
                        ffn2_w_gate=lg2, ffn2_w_up=lu2, ffn2_w_down=ld2)
        small[l] = (dn1, dnm, dcw, dpw, dps, dn2)

    res = {}
    for n in names:
        shp = big[n].shape
        two = lambda t: t.reshape(-1, shp[-1])
        outs = adamw([lands[l][n] for l in range(depth)], two(big[n]), two(big_m[n]), two(big_v[n]), f"adamw_{n}")
        res[n] = [o.reshape(shp) for o in outs]

    st = lambda i: jnp.stack([small[l][i] for l in range(depth)])
    small_g = dict(ffn1_norm=st(0), mix_norm=st(1), conv_w=st(2), pool_w=st(3), pool_scale=st(4), ffn2_norm=st(5),
                   final_norm=d_final)
    small_names = list(small_g)
    zeros_conv = jnp.zeros((depth, 3, C), F32)
    place = lambda t: lax.dynamic_update_slice(zeros_conv, t, (0, 0, me * Cb))
    small_w = dict(ffn1_norm=ffn1_norm, mix_norm=mix_norm, conv_w=place(conv_w), pool_w=pool_w, pool_scale=pool_scale,
                   ffn2_norm=ffn2_norm, final_norm=final_norm)
    small_m = dict(ffn1_norm=m_ffn1_norm, mix_norm=m_mix_norm, conv_w=place(m_conv_w), pool_w=m_pool_w,
                   pool_scale=m_pool_scale, ffn2_norm=m_ffn2_norm, final_norm=m_final_norm)
    small_v = dict(ffn1_norm=v_ffn1_norm, mix_norm=v_mix_norm, conv_w=place(v_conv_w), pool_w=v_pool_w,
                   pool_scale=v_pool_scale, ffn2_norm=v_ffn2_norm, final_norm=v_final_norm)
    gbuf, spans = _pack([small_g[n] for n in small_names] + [loss_part])
    wbuf, _ = _pack([small_w[n] for n in small_names] + [jnp.zeros((1, LANES), F32)])
    mbuf, _ = _pack([small_m[n] for n in small_names] + [jnp.zeros((1, LANES), F32)])
    vbuf, _ = _pack([small_v[n] for n in small_names] + [jnp.zeros((1, LANES), F32)])
    gathered = all_gather([gbuf], "gather_small_grads")[0]
    outs = adamw([gathered], wbuf, mbuf, vbuf, "adamw_small")
    for n, (off, size) in zip(small_names, spans):
        shp = small_w[n].shape
        vals = [o.reshape(-1)[off:off + size].reshape(shp) for o in outs]
        if n == "conv_w":
            vals = [lax.dynamic_slice(t, (0, 0, me * Cb), (depth, 3, Cb)) for t in vals]
        res[n] = vals
    loss = outs[0].reshape(-1)[spans[-1][0]]

    order = ["ffn1_norm", "ffn1_w_gate", "ffn1_w_up", "ffn1_w_down", "mix_norm", "w_in", "conv_w", "pool_w",
             "pool_scale", "w_out", "ffn2_norm", "ffn2_w_gate", "ffn2_w_up", "ffn2_w_down", "final_norm"]
    return (loss, g[None], *[res[n][0] for n in order], *[res[n][1] for n in order],
            *[res[n][2] for n in order], *[res[n][3] for n in order])
```

```python
import functools

import jax
import jax.numpy as jnp
from jax import lax
from jax.experimental import pallas as pl
from jax.experimental.pallas import tpu as pltpu

F32 = jnp.float32
BF16 = jnp.bfloat16
NDEV = 8
HEAD_DIM = 64
LANES = 128
BLK = 128
DILATIONS = (1, 4, 16)
POOL_WINDOWS = (2, 4, 8, 16)
POOL_GROUP = 128
HALO = 16
FFN_RESIDUAL = 0.5
RMS_EPS = 1e-6
NEG_INF = -1e30
ATTN_SCALE = HEAD_DIM ** -0.5
ADAM_LR, ADAM_B1, ADAM_B2, ADAM_EPS, ADAM_WD, ADAM_STEP = 0.001, 0.9, 0.999, 1e-08, 0.01, 10
VMEM_BYTES = 64 * 1024 * 1024
MESH = pl.DeviceIdType.MESH

NN = (((1,), (0,)), ((), ()))
NT = (((1,), (1,)), ((), ()))
TN = (((0,), (0,)), ((), ()))


def _dot(a, b, dims=NN):
    return lax.dot_general(a, b, dims, preferred_element_type=F32)


def _params(vmem_mb=48):
    return pltpu.CompilerParams(vmem_limit_bytes=min(vmem_mb * 1024 * 1024, VMEM_BYTES - 4 * 1024 * 1024))


def _tile(n, want):
    t = min(n, max(16, want // 16 * 16))
    while t > 16 and (n % t or t % 16):
        t -= 16
    return t if n % t == 0 else n


def _coords():
    return lax.axis_index("x"), lax.axis_index("y"), lax.axis_index("c")


def _comm_sems(n):
    return [pltpu.SemaphoreType.DMA((7 * n,)), pltpu.SemaphoreType.DMA((7 * n,)), pltpu.SemaphoreType.DMA((n,))]


def _gather_ops(ins, outs, sems):
    n = len(ins)
    send_sems, recv_sems, local_sems = sems
    x, y, c = _coords()
    me, sibling = (x, y, c), (x, y, 1 - c)
    chips = [(1 - x, y), (x, 1 - y), (1 - x, 1 - y)]

    def slot(out, p):
        return out.at[4 * p[0] + 2 * p[1] + p[2]]

    def copy(a, k, block, to, src=None):
        return pltpu.make_async_remote_copy(
            src_ref=slot(outs[a], block) if src is None else src, dst_ref=slot(outs[a], block),
            send_sem=send_sems.at[7 * a + k], recv_sem=recv_sems.at[7 * a + k],
            device_id=to, device_id_type=MESH)

    def own():
        mine = [pltpu.make_async_copy(ins[a], slot(outs[a], me), local_sems.at[a]) for a in range(n)]
        first = []
        for a in range(n):
            first.append(copy(a, 0, me, sibling, src=ins[a]))
            first += [copy(a, 1 + j, me, (*chip, c), src=ins[a]) for j, chip in enumerate(chips)]
        return mine, first

    def start():
        mine, first = own()
        for cp in mine + first:
            cp.start()

    def finish():
        mine, first = own()
        passed = []
        for j, chip in enumerate(chips):
            for a in range(n):
                copy(a, 1 + j, (*chip, c), me).wait_recv()
                fwd = copy(a, 4 + j, (*chip, c), sibling)
                fwd.start()
                passed.append(fwd)
        for a in range(n):
            copy(a, 0, sibling, me).wait_recv()
            for j, chip in enumerate(chips):
                copy(a, 4 + j, (*chip, 1 - c), me).wait_recv()
        for cp in first + passed:
            cp.wait_send()
        for cp in mine:
            cp.wait()

    return start, finish


def _exchange_ops(ins, outs, sems):
    n = len(ins)
    send_sems, recv_sems, local_sems = sems
    x, y, c = _coords()
    me = 4 * x + 2 * y + c

    def peer(k):
        return (1 - x if k & 4 else x, 1 - y if k & 2 else y, 1 - c if k & 1 else c)

    def copy(a, k):
        p = peer(k)
        return pltpu.make_async_remote_copy(
            src_ref=ins[a].at[4 * p[0] + 2 * p[1] + p[2]], dst_ref=outs[a].at[me],
            send_sem=send_sems.at[7 * a + k - 1], recv_sem=recv_sems.at[7 * a + k - 1],
            device_id=p, device_id_type=MESH)

    def landed(a, k):
        p = peer(k)
        return pltpu.make_async_remote_copy(
            src_ref=ins[a].at[me], dst_ref=outs[a].at[4 * p[0] + 2 * p[1] + p[2]],
            send_sem=send_sems.at[7 * a + k - 1], recv_sem=recv_sems.at[7 * a + k - 1],
            device_id=p, device_id_type=MESH)

    def own():
        mine = [pltpu.make_async_copy(ins[a].at[me], outs[a].at[me], local_sems.at[a]) for a in range(n)]
        return mine, [copy(a, k) for a in range(n) for k in range(1, NDEV)]

    def start():
        mine, sent = own()
        for cp in mine + sent:
            cp.start()

    def finish():
        mine, sent = own()
        for a in range(n):
            for k in range(1, NDEV):
                landed(a, k).wait_recv()
        for cp in sent:
            cp.wait_send()
        for cp in mine:
            cp.wait()

    return start, finish


_COMM = {"gather": (_gather_ops, lambda a: (NDEV,) + a.shape), "exchange": (_exchange_ops, lambda a: a.shape)}


def all_gather(arrs, name):
    n = len(arrs)

    def body(*refs):
        start, finish = _gather_ops(refs[:n], refs[n:2 * n], refs[2 * n:])
        start()
        finish()

    any_spec = pl.BlockSpec(memory_space=pl.ANY)
    return pl.pallas_call(
        body, name=name, out_shape=[jax.ShapeDtypeStruct((NDEV,) + a.shape, a.dtype) for a in arrs],
        in_specs=[any_spec] * n, out_specs=[any_spec] * n, scratch_shapes=_comm_sems(n),
    )(*arrs)


def _call(body, *, name, grid, in_specs, out_specs, out_shape, args, scratch=(), vmem=48, comm=None):
    if comm is None:
        outs = pl.pallas_call(
            body, name=name, grid=grid, in_specs=list(in_specs), out_specs=list(out_specs), out_shape=list(out_shape),
            scratch_shapes=list(scratch), compiler_params=_params(vmem))(*args)
        return list(outs), []
    kind, arrs = comm
    ops, shape_of = _COMM[kind]
    n, n_in, n_out, n_scr = len(arrs), len(in_specs), len(out_specs), len(scratch)

    def carrier(*refs):
        ins, c_in = refs[:n_in], refs[n_in:n_in + n]
        o0 = n_in + n
        outs, c_out = refs[o0:o0 + n_out], refs[o0 + n_out:o0 + n_out + n]
        s0 = o0 + n_out + n
        scr, sems = refs[s0:s0 + n_scr], refs[s0 + n_scr:]
        ids = [pl.program_id(d) for d in range(len(grid))]
        first = functools.reduce(jnp.logical_and, [i == 0 for i in ids])
        last = functools.reduce(jnp.logical_and, [i == g - 1 for i, g in zip(ids, grid)])
        start, finish = ops(c_in, c_out, sems)
        pl.when(first)(start)
        body(*ins, *outs, *scr)
        pl.when(last)(finish)

    any_spec = pl.BlockSpec(memory_space=pl.ANY)
    res = pl.pallas_call(
        carrier, name=name, grid=grid, in_specs=list(in_specs) + [any_spec] * n,
        out_specs=list(out_specs) + [any_spec] * n,
        out_shape=list(out_shape) + [jax.ShapeDtypeStruct(shape_of(a), a.dtype) for a in arrs],
        scratch_shapes=list(scratch) + _comm_sems(n), compiler_params=_params(vmem))(*args, *arrs)
    return list(res[:n_out]), list(res[n_out:])


def _rstd(x):
    return lax.rsqrt(jnp.mean(x * x, axis=-1, keepdims=True) + RMS_EPS)


def rms_fwd(x, gain, name):
    S, D = x.shape
    tm = _tile(S, 512)

    def body(x_ref, g_ref, h_ref, ht_ref):
        xv = x_ref[...]
        h = xv * _rstd(xv) * g_ref[...]
        h_ref[...] = h.astype(BF16)
        ht_ref[...] = h.T.astype(BF16)

    return pl.pallas_call(
        body, name=name, grid=(S // tm,),
        in_specs=[pl.BlockSpec((tm, D), lambda i: (i, 0)), pl.BlockSpec((1, D), lambda i: (0, 0))],
        out_specs=[pl.BlockSpec((tm, D), lambda i: (i, 0)), pl.BlockSpec((D, tm), lambda i: (0, i))],
        out_shape=[jax.ShapeDtypeStruct((S, D), BF16), jax.ShapeDtypeStruct((D, S), BF16)],
        compiler_params=_params(40),
    )(x, gain)


def _rms_bwd_math(xv, gain, dh):
    r = _rstd(xv)
    xhat = xv * r
    dxhat = dh * gain
    dx = r * (dxhat - xhat * jnp.mean(dxhat * xhat, axis=-1, keepdims=True))
    return dx, dh * xhat


def rms_bwd(x, gain, dh, g, name):
    S, D = x.shape
    tm = _tile(S, 256)

    def body(x_ref, gain_ref, dh_ref, g_ref, go_ref, gb_ref, dg_ref):
        @pl.when(pl.program_id(0) == 0)
        def _():
            dg_ref[...] = jnp.zeros_like(dg_ref)

        dx, dgain = _rms_bwd_math(x_ref[...], gain_ref[...], dh_ref[...].astype(F32))
        gn = g_ref[...] + dx
        go_ref[...] = gn
        gb_ref[...] = gn.astype(BF16)
        dg_ref[...] += jnp.sum(dgain, axis=0, keepdims=True)

    row = pl.BlockSpec((tm, D), lambda i: (i, 0))
    vec = pl.BlockSpec((1, D), lambda i: (0, 0))
    return pl.pallas_call(
        body, name=name, grid=(S // tm,),
        in_specs=[row, vec, row, row], out_specs=[row, row, vec],
        out_shape=[jax.ShapeDtypeStruct((S, D), F32), jax.ShapeDtypeStruct((S, D), BF16),
                   jax.ShapeDtypeStruct((1, D), F32)],
        compiler_params=_params(48),
    )(x, gain, dh, g)


def final_loss(x, gain, target, name):
    S, D = x.shape
    tm = _tile(S, 256)

    def body(x_ref, gain_ref, t_ref, loss_ref, go_ref, gb_ref, dg_ref):
        @pl.when(pl.program_id(0) == 0)
        def _():
            dg_ref[...] = jnp.zeros_like(dg_ref)
            loss_ref[...] = jnp.zeros_like(loss_ref)

        xv, gain_v = x_ref[...], gain_ref[...]
        err = xv * _rstd(xv) * gain_v - t_ref[...]
        loss_ref[...] += jnp.sum(jnp.sum(err * err, axis=-1, keepdims=True), axis=0, keepdims=True) * (0.5 / D)
        dx, dgain = _rms_bwd_math(xv, gain_v, err * (1.0 / D))
        go_ref[...] = dx
        gb_ref[...] = dx.astype(BF16)
        dg_ref[...] += jnp.sum(dgain, axis=0, keepdims=True)

    row = pl.BlockSpec((tm, D), lambda i: (i, 0))
    vec = pl.BlockSpec((1, D), lambda i: (0, 0))
    return pl.pallas_call(
        body, name=name, grid=(S // tm,),
        in_specs=[row, vec, row],
        out_specs=[pl.BlockSpec((1, LANES), lambda i: (0, 0)), row, row, vec],
        out_shape=[jax.ShapeDtypeStruct((1, LANES), F32), jax.ShapeDtypeStruct((S, D), F32),
                   jax.ShapeDtypeStruct((S, D), BF16), jax.ShapeDtypeStruct((1, D), F32)],
        compiler_params=_params(48),
    )(x, gain, target)


def _blk(arr, width, tm, stacked):
    if stacked:
        return pl.BlockSpec((None, tm, width), lambda j, i: (j, i, 0))
    return pl.BlockSpec((tm, width), lambda j, i: (i, j))


def ffn_up(h, wg, wu, name, comm=None):
    S, D = h.shape
    J, _, Fb = wg.shape
    tm = _tile(S, 1024)

    def body(h_ref, wg_ref, wu_ref, a_ref, b_ref, hid_ref):
        hv = h_ref[...]
        a = _dot(hv, wg_ref[...])
        b = _dot(hv, wu_ref[...])
        a_ref[...] = a.astype(BF16)
        b_ref[...] = b.astype(BF16)
        hid_ref[...] = (a * jax.nn.sigmoid(a) * b).astype(BF16)

    w_spec = pl.BlockSpec((None, D, Fb), lambda j, i: (j, 0, 0))
    o_spec = pl.BlockSpec((None, tm, Fb), lambda j, i: (j, i, 0))
    shp = jax.ShapeDtypeStruct((J, S, Fb), BF16)
    return _call(
        body, name=name, grid=(J, S // tm),
        in_specs=[pl.BlockSpec((tm, D), lambda j, i: (i, 0)), w_spec, w_spec],
        out_specs=[o_spec, o_spec, o_spec], out_shape=[shp, shp, shp], args=(h, wg, wu), comm=comm)


def proj_cols(h, w, name, comm=None):
    S, D = h.shape
    J, _, Wb = w.shape
    tm = _tile(S, 1024)

    def body(h_ref, w_ref, z_ref):
        z_ref[...] = _dot(h_ref[...], w_ref[...]).astype(BF16)

    (z,), got = _call(
        body, name=name, grid=(J, S // tm),
        in_specs=[pl.BlockSpec((tm, D), lambda j, i: (i, 0)), pl.BlockSpec((None, D, Wb), lambda j, i: (j, 0, 0))],
        out_specs=[pl.BlockSpec((tm, Wb), lambda j, i: (i, j))],
        out_shape=[jax.ShapeDtypeStruct((S, J * Wb), BF16)], args=(h, w), comm=comm)
    return z, got


def _norm_outs(S, D, tm, gain):
    if gain is None:
        return [], [], []
    return ([pl.BlockSpec((1, D), lambda i: (0, 0))],
            [pl.BlockSpec((tm, D), lambda i: (i, 0)), pl.BlockSpec((D, tm), lambda i: (0, i))],
            [jax.ShapeDtypeStruct((S, D), BF16), jax.ShapeDtypeStruct((D, S), BF16)])


def _write_norm(xo, gain_ref, h_ref, ht_ref):
    h = xo * _rstd(xo) * gain_ref[...]
    h_ref[...] = h.astype(BF16)
    ht_ref[...] = h.T.astype(BF16)


def ffn_down(hid, wd, x, name, comm=None, next_gain=None):
    J, S, Fb = hid.shape
    D = wd.shape[2]
    tm = _tile(S, 256)

    def body(hid_ref, wd_ref, x_ref, *rest):
        acc = _dot(hid_ref[0], wd_ref[0])
        for j in range(1, J):
            acc = acc + _dot(hid_ref[j], wd_ref[j])
        xo = x_ref[...] + FFN_RESIDUAL * acc
        if next_gain is None:
            rest[0][...] = xo
        else:
            gain_ref, o_ref, h_ref, ht_ref = rest
            o_ref[...] = xo
            _write_norm(xo, gain_ref, h_ref, ht_ref)

    row = pl.BlockSpec((tm, D), lambda i: (i, 0))
    g_in, n_specs, n_shapes = _norm_outs(S, D, tm, next_gain)
    outs, got = _call(
        body, name=name, grid=(S // tm,),
        in_specs=[pl.BlockSpec((J, tm, Fb), lambda i: (0, i, 0)),
                  pl.BlockSpec((J, Fb, D), lambda i: (0, 0, 0), pipeline_mode=pl.Buffered(1)), row] + g_in,
        out_specs=[row] + n_specs, out_shape=[jax.ShapeDtypeStruct((S, D), F32)] + n_shapes,
        args=(hid, wd, x) + (() if next_gain is None else (next_gain,)), vmem=56, comm=comm)
    return outs, got


def ffn_dhid(gb, wd, a, b, name):
    S, D = gb.shape
    J, Fb, _ = wd.shape
    tm = _tile(S, 1024)

    parts = 4 if tm % 64 == 0 else 1

    def body(g_ref, wd_ref, a_ref, b_ref, da_ref, db_ref):
        wdv = wd_ref[...]
        for part in range(parts):
            rows = pl.ds(part * (tm // parts), tm // parts)
            dhid = _dot(g_ref[rows, :], wdv, NT) * FFN_RESIDUAL
            a = a_ref[rows, :].astype(F32)
            b = b_ref[rows, :].astype(F32)
            sg = jax.nn.sigmoid(a)
            da_ref[rows, :] = (dhid * b * (sg * (1.0 + a * (1.0 - sg)))).astype(BF16)
            db_ref[rows, :] = (dhid * (a * sg)).astype(BF16)

    o_spec = pl.BlockSpec((None, tm, Fb), lambda j, i: (j, i, 0))
    shp = jax.ShapeDtypeStruct((J, S, Fb), BF16)
    return pl.pallas_call(
        body, name=name, grid=(J, S // tm),
        in_specs=[pl.BlockSpec((tm, D), lambda j, i: (i, 0)), pl.BlockSpec((None, Fb, D), lambda j, i: (j, 0, 0)),
                  o_spec, o_spec],
        out_specs=[o_spec, o_spec], out_shape=[shp, shp], compiler_params=_params(48),
    )(gb, wd, a, b)


def back_proj(ds, ws, stacked, name, comm=None):
    n = len(ds)
    J, D, Wb = ws[0].shape
    S = ds[0].shape[1] if stacked else ds[0].shape[0]
    kb = J if n == 1 else 2
    nj = J // kb
    tm = _tile(S, 256 if n == 1 else 512)

    def body(*refs):
        d_refs, w_refs, o_ref = refs[:n], refs[n:2 * n], refs[2 * n]
        acc = None
        for d_ref, w_ref in zip(d_refs, w_refs):
            for k in range(kb):
                dk = d_ref[k] if stacked else d_ref[:, k * Wb:(k + 1) * Wb]
                t = _dot(dk, w_ref[k], NT)
                acc = t if acc is None else acc + t
        if nj == 1:
            o_ref[...] = acc.astype(BF16)
        else:
            acc_ref = refs[2 * n + 1]
            j = pl.program_id(1)

            @pl.when(j == 0)
            def _():
                acc_ref[...] = acc

            @pl.when((j > 0) & (j < nj - 1))
            def _():
                acc_ref[...] += acc

            @pl.when(j == nj - 1)
            def _():
                o_ref[...] = (acc_ref[...] + acc).astype(BF16)

    if stacked:
        d_spec = pl.BlockSpec((kb, tm, Wb), lambda i, j: (j, i, 0))
    else:
        d_spec = pl.BlockSpec((tm, kb * Wb), lambda i, j: (i, j))
    w_spec = pl.BlockSpec((kb, D, Wb), lambda i, j: (j, 0, 0), pipeline_mode=pl.Buffered(1) if nj == 1 else None)
    (dh,), got = _call(
        body, name=name, grid=(S // tm, nj),
        in_specs=[d_spec] * n + [w_spec] * n,
        out_specs=[pl.BlockSpec((tm, D), lambda i, j: (i, 0))], out_shape=[jax.ShapeDtypeStruct((S, D), BF16)],
        scratch=[] if nj == 1 else [pltpu.VMEM((tm, D), F32)], args=(*ds, *ws), vmem=56, comm=comm)
    return dh, got


def grad_lhs(ht, d, stacked, name, comm=None):
    D, S = ht.shape
    if stacked:
        J, _, Wb = d.shape
    else:
        J, Wb = NDEV, d.shape[1] // NDEV
    tk = _tile(S, 2048)
    nk = S // tk

    def body(ht_ref, d_ref, o_ref, acc_ref):
        i = pl.program_id(1)

        @pl.when(i == 0)
        def _():
            acc_ref[...] = jnp.zeros_like(acc_ref)

        acc_ref[...] += _dot(ht_ref[...], d_ref[...])

        @pl.when(i == nk - 1)
        def _():
            o_ref[...] = acc_ref[...].astype(BF16)

    (dw,), got = _call(
        body, name=name, grid=(J, nk),
        in_specs=[pl.BlockSpec((D, tk), lambda j, i: (0, i)), _blk(d, Wb, tk, stacked)],
        out_specs=[pl.BlockSpec((None, D, Wb), lambda j, i: (j, 0, 0))],
        out_shape=[jax.ShapeDtypeStruct((J, D, Wb), BF16)],
        scratch=[pltpu.VMEM((D, Wb), F32)], args=(ht, d), vmem=56, comm=comm)
    return dw, got


def grad_shared_rhs(a, gb, nblk, scale, name):
    S, D = gb.shape
    stacked = nblk is None
    if stacked:
        J, _, Wb = a.shape
    else:
        J, Wb = nblk, a.shape[1] // nblk
    tk = _tile(S, 2048 if Wb <= 768 else 1024)
    nk = S // tk

    def body(a_ref, g_ref, o_ref, acc_ref):
        i = pl.program_id(1)

        @pl.when(i == 0)
        def _():
            acc_ref[...] = jnp.zeros_like(acc_ref)

        acc_ref[...] += _dot(a_ref[...], g_ref[...], TN)

        @pl.when(i == nk - 1)
        def _():
            o_ref[...] = (acc_ref[...] * scale).astype(BF16)

    return pl.pallas_call(
        body, name=name, grid=(J, nk),
        in_specs=[_blk(a, Wb, tk, stacked), pl.BlockSpec((tk, D), lambda j, i: (i, 0))],
        out_specs=pl.BlockSpec((None, Wb, D), lambda j, i: (j, 0, 0)),
        out_shape=jax.ShapeDtypeStruct((J, Wb, D), BF16),
        scratch_shapes=[pltpu.VMEM((Wb, D), F32)], compiler_params=_params(56),
    )(a, gb)


def out_proj(ya, ycp, wo, x, next_gain, name):
    S, A = ya.shape
    Dm, D = wo.shape
    tm = _tile(S, 256)

    def body(ya_ref, ycp_ref, wo_ref, x_ref, gain_ref, o_ref, h_ref, ht_ref):
        xo = x_ref[...] + _dot(ya_ref[...], wo_ref[0:A, :]) + _dot(ycp_ref[...], wo_ref[A:Dm, :])
        o_ref[...] = xo
        _write_norm(xo, gain_ref, h_ref, ht_ref)

    row = pl.BlockSpec((tm, D), lambda i: (i, 0))
    g_in, n_specs, n_shapes = _norm_outs(S, D, tm, next_gain)
    return pl.pallas_call(
        body, name=name, grid=(S // tm,),
        in_specs=[pl.BlockSpec((tm, A), lambda i: (i, 0)), pl.BlockSpec((tm, Dm - A), lambda i: (i, 0)),
                  pl.BlockSpec((Dm, D), lambda i: (0, 0), pipeline_mode=pl.Buffered(1)), row] + g_in,
        out_specs=[row] + n_specs, out_shape=[jax.ShapeDtypeStruct((S, D), F32)] + n_shapes,
        compiler_params=_params(48),
    )(ya, ycp, wo, x, next_gain)


def out_proj_bwd(gb, wo, A, name):
    S, D = gb.shape
    Dm = wo.shape[0]
    tm = _tile(S, 512)

    def body(g_ref, wo_ref, dya_ref, dycp_ref):
        gv = g_ref[...]
        dya_ref[...] = _dot(gv, wo_ref[0:A, :], NT).astype(BF16)
        dycp_ref[...] = _dot(gv, wo_ref[A:Dm, :], NT).astype(BF16)

    return pl.pallas_call(
        body, name=name, grid=(S // tm,),
        in_specs=[pl.BlockSpec((tm, D), lambda i: (i, 0)), pl.BlockSpec((Dm, D), lambda i: (0, 0))],
        out_specs=[pl.BlockSpec((tm, A), lambda i: (i, 0)), pl.BlockSpec((tm, Dm - A), lambda i: (i, 0))],
        out_shape=[jax.ShapeDtypeStruct((S, A), BF16), jax.ShapeDtypeStruct((S, Dm - A), BF16)],
        compiler_params=_params(48),
    )(gb, wo)


def _lane_stat(v, hi):
    lane = lax.broadcasted_iota(jnp.int32, v.shape, 1)
    sel = (lane >= HEAD_DIM) if hi else (lane < HEAD_DIM)
    return jnp.max(jnp.where(sel, v, -jnp.inf), axis=-1, keepdims=True)


def band_fwd(q, k, v, L, name):
    H, S, W = q.shape
    T = min(512, L)
    nb = T // BLK

    def body(q_ref, k_ref, kp_ref, v_ref, vp_ref, o_ref):
        i = pl.program_id(1)
        first_key = jnp.where((i * T) % L != 0, 0, BLK)
        qi = lax.broadcasted_iota(jnp.int32, (BLK, 2 * BLK), 0)
        kj = lax.broadcasted_iota(jnp.int32, (BLK, 2 * BLK), 1)
        band = (kj >= qi) & (kj <= qi + BLK)
        lane = lax.broadcasted_iota(jnp.int32, (BLK, W), 1)
        for b in range(nb):
            rows = slice(b * BLK, (b + 1) * BLK)
            if b == 0:
                kw = jnp.concatenate([kp_ref[...], k_ref[rows, :]], axis=0)
                vw = jnp.concatenate([vp_ref[...], v_ref[rows, :]], axis=0)
                mask = band & (kj >= first_key)
            else:
                kw = k_ref[(b - 1) * BLK:(b + 1) * BLK, :]
                vw = v_ref[(b - 1) * BLK:(b + 1) * BLK, :]
                mask = band
            s = jnp.where(mask, _dot(q_ref[rows, :], kw, NT) * ATTN_SCALE, NEG_INF)
            m = jnp.max(s, axis=-1, keepdims=True)
            p = jnp.exp(s - m)
            l = jnp.sum(p, axis=-1, keepdims=True)
            o = _dot(p.astype(BF16), vw)
            o_ref[rows, :] = jnp.where(lane < HEAD_DIM, o / l, m + jnp.log(l))

    cur = pl.BlockSpec((None, T, W), lambda h, i: (h, i, 0))
    prev = pl.BlockSpec((None, BLK, W), lambda h, i: (h, jnp.maximum(i * nb - 1, 0), 0))
    return pl.pallas_call(
        body, name=name, grid=(H, S // T),
        in_specs=[cur, cur, prev, cur, prev], out_specs=cur,
        out_shape=jax.ShapeDtypeStruct((H, S, W), F32), compiler_params=_params(32),
    )(q, k, k, v, v)


def band_bwd(q, k, v, do, st, L, name, comm=None):
    H, S, W = q.shape
    T = min(512, L)
    nb = T // BLK
    last_blk = S // BLK - 1

    def body(q_ref, qn_ref, do_ref, don_ref, st_ref, stn_ref, k_ref, kp_ref, v_ref, vp_ref,
             dq_ref, dk_ref, dv_ref, dk_acc, dv_acc):
        i = pl.program_id(1)
        first_key = jnp.where((i * T) % L != 0, 0, BLK)
        next_off = jnp.where(((i + 1) * T) % L != 0, 0, 4 * BLK)
        qi = lax.broadcasted_iota(jnp.int32, (BLK, 2 * BLK), 0)
        kj = lax.broadcasted_iota(jnp.int32, (BLK, 2 * BLK), 1)
        band = (kj >= qi) & (kj <= qi + BLK)
        qi1 = lax.broadcasted_iota(jnp.int32, (BLK, BLK), 0)
        kj1 = lax.broadcasted_iota(jnp.int32, (BLK, BLK), 1)
        dk_acc[...] = jnp.zeros_like(dk_acc)
        dv_acc[...] = jnp.zeros_like(dv_acc)
        for b in range(nb + 1):
            rows = slice(b * BLK, (b + 1) * BLK)
            if b < nb:
                qb, dob, stb = q_ref[rows, :], do_ref[rows, :], st_ref[rows, :]
            else:
                qb, dob, stb = qn_ref[...], don_ref[...], stn_ref[...]
            if b == 0:
                kw = jnp.concatenate([kp_ref[...], k_ref[rows, :]], axis=0)
                vw = jnp.concatenate([vp_ref[...], v_ref[rows, :]], axis=0)
                mask = band & (kj >= first_key)
            elif b < nb:
                kw = k_ref[(b - 1) * BLK:(b + 1) * BLK, :]
                vw = v_ref[(b - 1) * BLK:(b + 1) * BLK, :]
                mask = band
            else:
                kw = k_ref[(nb - 1) * BLK:nb * BLK, :]
                vw = v_ref[(nb - 1) * BLK:nb * BLK, :]
                mask = kj1 >= qi1 + next_off
            s = _dot(qb, kw, NT) * ATTN_SCALE
            p = jnp.where(mask, jnp.exp(s - _lane_stat(stb, False)), 0.0)
            dp = _dot(dob, vw, NT)
            ds = (p * (dp - _lane_stat(stb, True))).astype(BF16)
            if b < nb:
                dq_ref[rows, :] = _dot(ds, kw) * ATTN_SCALE
            win = slice(b * BLK, b * BLK + kw.shape[0])
            dk_acc[win, :] += _dot(ds, qb, TN) * ATTN_SCALE
            dv_acc[win, :] += _dot(p.astype(BF16), dob, TN)
        dk_ref[...] = dk_acc[BLK:, :]
        dv_ref[...] = dv_acc[BLK:, :]

    cur = pl.BlockSpec((None, T, W), lambda h, i: (h, i, 0))
    prev = pl.BlockSpec((None, BLK, W), lambda h, i: (h, jnp.maximum(i * nb - 1, 0), 0))
    nxt = pl.BlockSpec((None, BLK, W), lambda h, i: (h, jnp.minimum((i + 1) * nb, last_blk), 0))
    shp = jax.ShapeDtypeStruct((H, S, W), F32)
    return _call(
        body, name=name, grid=(H, S // T),
        in_specs=[cur, nxt, cur, nxt, cur, nxt, cur, prev, cur, prev],
        out_specs=[cur, cur, cur], out_shape=[shp, shp, shp],
        scratch=[pltpu.VMEM((T + BLK, W), F32), pltpu.VMEM((T + BLK, W), F32)],
        args=(q, q, do, do, st, st, k, k, v, v), vmem=32, comm=comm)


def attn_merge(os_, name):
    H, S, W = os_[0].shape
    tm = _tile(S, 1024)

    def body(*refs):
        o_ref = refs[-1]
        vals = [r[...] for r in refs[:-1]]
        lses = [_lane_stat(v, True) for v in vals]
        m = functools.reduce(jnp.maximum, lses)
        ws = [jnp.exp(l - m) for l in lses]
        tot = functools.reduce(jnp.add, ws)
        out = functools.reduce(jnp.add, [(w / tot) * v for w, v in zip(ws, vals)])
        lane = lax.broadcasted_iota(jnp.int32, out.shape, 1)
        o_ref[...] = jnp.where(lane < HEAD_DIM, out, m + jnp.log(tot))

    spec = pl.BlockSpec((None, tm, W), lambda h, i: (h, i, 0))
    return pl.pallas_call(
        body, name=name, grid=(H, S // tm), in_specs=[spec] * len(os_), out_specs=spec,
        out_shape=jax.ShapeDtypeStruct((H, S, W), F32), compiler_params=_params(32),
    )(*os_)


def attn_bwd_stats(merged, do, name):
    H, S, W = merged.shape
    tm = _tile(S, 1024)

    def body(m_ref, do_ref, st_ref):
        mv = m_ref[...]
        delta = jnp.sum(mv * do_ref[...].astype(F32), axis=-1, keepdims=True)
        lane = lax.broadcasted_iota(jnp.int32, mv.shape, 1)
        st_ref[...] = jnp.where(lane < HEAD_DIM, _lane_stat(mv, True), delta)

    spec = pl.BlockSpec((None, tm, W), lambda h, i: (h, i, 0))
    return pl.pallas_call(
        body, name=name, grid=(H, S // tm), in_specs=[spec, spec], out_specs=spec,
        out_shape=jax.ShapeDtypeStruct((H, S, W), F32), compiler_params=_params(32),
    )(merged, do)


def sum_patterns(ts, name):
    H, S, W = ts[0].shape
    tm = _tile(S, 1024)

    def body(*refs):
        refs[-1][...] = functools.reduce(jnp.add, [r[...] for r in refs[:-1]]).astype(BF16)

    spec = pl.BlockSpec((None, tm, W), lambda h, i: (h, i, 0))
    return pl.pallas_call(
        body, name=name, grid=(H, S // tm), in_specs=[spec] * len(ts), out_specs=spec,
        out_shape=jax.ShapeDtypeStruct((H, S, W), BF16), compiler_params=_params(32),
    )(*ts)


def to_heads(t):
    S = t.shape[0]
    t = jnp.transpose(t.reshape(S, -1, HEAD_DIM), (1, 0, 2))
    return jnp.pad(t, ((0, 0), (0, 0), (0, LANES - HEAD_DIM)))


def from_heads(t):
    H, S, _ = t.shape
    return jnp.transpose(t[:, :, :HEAD_DIM], (1, 0, 2)).reshape(S, H * HEAD_DIM)


def permute(t, d):
    if d == 1:
        return t
    H, S, W = t.shape
    return jnp.transpose(t.reshape(H, S // d, d, W), (0, 2, 1, 3)).reshape(H, S, W)


def unpermute(t, d):
    if d == 1:
        return t
    H, S, W = t.shape
    return jnp.transpose(t.reshape(H, d, S // d, W), (0, 2, 1, 3)).reshape(H, S, W)


ATT_TILE = BLK * max(DILATIONS)


def _head0(shape):
    return lax.broadcasted_iota(jnp.int32, shape, 1) < HEAD_DIM


def _pair_col(v, sel):
    return jnp.max(jnp.where(sel, v, -jnp.inf), axis=-1, keepdims=True)


def _stack_heads(x, h0):
    zero = jnp.zeros_like(x)
    return jnp.concatenate([jnp.where(h0, x, zero), jnp.where(h0, zero, x)], axis=0)


def _band(lo, hi):
    qi = lax.broadcasted_iota(jnp.int32, (BLK, 2 * BLK), 0)
    kj = lax.broadcasted_iota(jnp.int32, (BLK, 2 * BLK), 1)
    return (kj >= qi) & (kj <= qi + BLK) & (kj >= lo) & (kj < hi)


def _deinterleave(dst, src, d, rows, dst_stride, dst_off=0, src_off=0, cast=None):
    for r in range(d):
        v = src[pl.ds(src_off + r, rows, stride=d), :] if d > 1 else src[pl.ds(src_off, rows), :]
        dst[pl.ds(r * dst_stride + dst_off, rows), :] = v if cast is None else v.astype(cast)


def _interleave(dst, src, d, rows, src_stride, src_off=0, add=False):
    for r in range(d):
        v = src[pl.ds(r * src_stride + src_off, rows), :]
        idx = pl.ds(r, rows, stride=d) if d > 1 else pl.ds(0, rows)
        dst[idx, :] = dst[idx, :] + v if add else v


def attn_fwd(z, A, name, comm=None):
    S = z.shape[0]
    T = ATT_TILE
    Hp, nt = A // LANES, S // T
    np_ = len(DILATIONS)

    def body(q_ref, k_ref, kp_ref, v_ref, vp_ref, y_ref, lse_ref, qn, kn, vn, qp, kp, vp, accp, mp, lp, *nat):
        accs, ms, ls = nat[:np_], nat[np_:2 * np_], nat[2 * np_:]
        i = pl.program_id(1)
        qn[...] = q_ref[...].astype(F32)
        kn[0:T, :] = kp_ref[...].astype(F32)
        kn[T:2 * T, :] = k_ref[...].astype(F32)
        vn[0:T, :] = vp_ref[...].astype(F32)
        vn[T:2 * T, :] = v_ref[...].astype(F32)
        h0 = _head0((BLK, LANES))
        for pi, d in enumerate(DILATIONS):
            Ld = T // d
            nblk = Ld // BLK
            _deinterleave(qp, qn, d, Ld, Ld, cast=BF16)
            for src, dst in ((kn, kp), (vn, vp)):
                _deinterleave(dst, src, d, Ld, 2 * Ld, cast=BF16)
                _deinterleave(dst, src, d, Ld, 2 * Ld, dst_off=Ld, src_off=T, cast=BF16)

            def unit(u, carry, nblk=nblk):
                r, n = u // nblk, u % nblk
                q0 = pl.multiple_of(u * BLK, BLK)
                k0 = pl.multiple_of((2 * r * nblk + nblk + n - 1) * BLK, BLK)
                qb = qp[pl.ds(q0, BLK), :]
                kw, vw = kp[pl.ds(k0, 2 * BLK), :], vp[pl.ds(k0, 2 * BLK), :]
                mask = _band(jnp.where((i == 0) & (n == 0), BLK, 0), 2 * BLK)
                s = _dot(_stack_heads(qb, h0), kw, NT) * ATTN_SCALE
                s = jnp.where(jnp.concatenate([mask, mask], axis=0), s, NEG_INF)
                mx = jnp.max(s, axis=-1, keepdims=True)
                p = jnp.exp(s - mx)
                l = jnp.sum(p, axis=-1, keepdims=True)
                o = _dot(p.astype(BF16), vw)
                accp[pl.ds(q0, BLK), :] = jnp.where(h0, o[:BLK], o[BLK:])
                mp[pl.ds(q0, BLK), :] = jnp.where(h0, mx[:BLK], mx[BLK:])
                lp[pl.ds(q0, BLK), :] = jnp.where(h0, l[:BLK], l[BLK:])
                return carry

            lax.fori_loop(0, T // BLK, unit, 0, unroll=True)
            for src, dst in ((accp, accs[pi]), (mp, ms[pi]), (lp, ls[pi])):
                _interleave(dst, src, d, Ld, Ld)
        mv = [m[...] for m in ms]
        mx = mv[0]
        for m in mv[1:]:
            mx = jnp.maximum(mx, m)
        ws = [jnp.exp(m - mx) for m in mv]
        l = sum(w * lr[...] for w, lr in zip(ws, ls))
        a = sum(w * ar[...] for w, ar in zip(ws, accs))
        y_ref[...] = (a / l).astype(BF16)
        lse_ref[...] = mx + jnp.log(l)

    kb, vb = A // LANES, 2 * A // LANES
    cur = lambda off: pl.BlockSpec((T, LANES), lambda h, i: (i, off + h))
    prev = lambda off: pl.BlockSpec((T, LANES), lambda h, i: (jnp.maximum(i - 1, 0), off + h))
    out = pl.BlockSpec((T, LANES), lambda h, i: (i, h))
    vm = lambda rows, dt: pltpu.VMEM((rows, LANES), dt)
    (y, lse), got = _call(
        body, name=name, grid=(Hp, nt),
        in_specs=[cur(0), cur(kb), prev(kb), cur(vb), prev(vb)], out_specs=[out, out],
        out_shape=[jax.ShapeDtypeStruct((S, A), BF16), jax.ShapeDtypeStruct((S, A), F32)],
        scratch=[vm(T, F32), vm(2 * T, F32), vm(2 * T, F32), vm(T, BF16), vm(2 * T, BF16), vm(2 * T, BF16),
                 vm(T, F32), vm(T, F32), vm(T, F32)] + [vm(T, F32)] * (3 * np_),
        args=(z, z, z, z, z), vmem=48, comm=comm)
    return y, lse, got


def attn_bwd(z, dy, y, lse, A, name, comm=None):
    S = z.shape[0]
    T = ATT_TILE
    Hp, nt = A // LANES, S // T

    def body(q_ref, k_ref, kp_ref, v_ref, vp_ref, do_ref, y_ref, ls_ref, dq_ref, dk_ref, dv_ref,
             qn, don, dln, kn, vn, qp, dop, lsp, dlp, kp, vp, dqp, dkp, dvp, dqa, dka, dva, dkc, dvc):
        step = pl.program_id(1)
        i = nt - 1 - step
        h0t = _head0((T, LANES))
        dof = do_ref[...].astype(F32)
        prod = dof * y_ref[...].astype(F32)
        d0 = jnp.sum(jnp.where(h0t, prod, 0.0), axis=-1, keepdims=True)
        d1 = jnp.sum(jnp.where(h0t, 0.0, prod), axis=-1, keepdims=True)
        qn[...] = q_ref[...].astype(F32)
        don[...] = dof
        dln[...] = jnp.where(h0t, d0, d1)
        kn[0:T, :] = kp_ref[...].astype(F32)
        kn[T:2 * T, :] = k_ref[...].astype(F32)
        vn[0:T, :] = vp_ref[...].astype(F32)
        vn[T:2 * T, :] = v_ref[...].astype(F32)

        @pl.when(step == 0)
        def _():
            dka[...] = jnp.zeros_like(dka)
            dva[...] = jnp.zeros_like(dva)

        @pl.when(step > 0)
        def _():
            dka[...] = dkc[...]
            dva[...] = dvc[...]

        h0 = _head0((BLK, LANES))
        for pi, d in enumerate(DILATIONS):
            Ld = T // d
            nblk = Ld // BLK
            for src, dst, cast in ((qn, qp, BF16), (don, dop, BF16), (dln, dlp, None)):
                _deinterleave(dst, src, d, Ld, Ld, cast=cast)
            _deinterleave(lsp, ls_ref, d, Ld, Ld)
            for src, dst in ((kn, kp), (vn, vp)):
                _deinterleave(dst, src, d, Ld, 2 * Ld, cast=BF16)
                _deinterleave(dst, src, d, Ld, 2 * Ld, dst_off=Ld, src_off=T, cast=BF16)
            dkp[...] = jnp.zeros_like(dkp)
            dvp[...] = jnp.zeros_like(dvp)

            def unit(u, carry, nblk=nblk):
                r, b = u // nblk, u % nblk
                q0 = pl.multiple_of(u * BLK, BLK)
                k0 = pl.multiple_of((2 * r * nblk + nblk + b - 1) * BLK, BLK)
                mask = _band(jnp.where((i == 0) & (b == 0), BLK, 0), 2 * BLK)
                qb, dob = qp[pl.ds(q0, BLK), :], dop[pl.ds(q0, BLK), :]
                lsb, dlb = lsp[pl.ds(q0, BLK), :], dlp[pl.ds(q0, BLK), :]
                kw, vw = kp[pl.ds(k0, 2 * BLK), :], vp[pl.ds(k0, 2 * BLK), :]
                qs, dos = _stack_heads(qb, h0), _stack_heads(dob, h0)
                lse = jnp.concatenate([_pair_col(lsb, h0), _pair_col(lsb, ~h0)], axis=0)
                delta = jnp.concatenate([_pair_col(dlb, h0), _pair_col(dlb, ~h0)], axis=0)
                s = _dot(qs, kw, NT) * ATTN_SCALE
                p = jnp.where(jnp.concatenate([mask, mask], axis=0), jnp.exp(s - lse), 0.0)
                ds = (p * (_dot(dos, vw, NT) - delta)).astype(BF16)
                dq = _dot(ds, kw) * ATTN_SCALE
                dqp[pl.ds(q0, BLK), :] = jnp.where(h0, dq[:BLK], dq[BLK:])
                dkp[pl.ds(k0, 2 * BLK), :] += _dot(ds, qs, TN) * ATTN_SCALE
                dvp[pl.ds(k0, 2 * BLK), :] += _dot(p.astype(BF16), dos, TN)
                return carry

            lax.fori_loop(0, T // BLK, unit, 0, unroll=True)
            _interleave(dqa, dqp, d, Ld, Ld, add=pi > 0)
            for acc, nxt, part in ((dka, dkc, dkp), (dva, dvc, dvp)):
                _interleave(acc, part, d, Ld, 2 * Ld, src_off=Ld, add=True)
                _interleave(nxt, part, d, Ld, 2 * Ld, add=pi > 0)
        dq_ref[...] = dqa[...].astype(BF16)
        dk_ref[...] = dka[...].astype(BF16)
        dv_ref[...] = dva[...].astype(BF16)

    kb, vb = A // LANES, 2 * A // LANES
    cur = lambda off: pl.BlockSpec((T, LANES), lambda h, s: (nt - 1 - s, off + h))
    prev = lambda off: pl.BlockSpec((T, LANES), lambda h, s: (jnp.maximum(nt - 2 - s, 0), off + h))
    vm = lambda rows, dt: pltpu.VMEM((rows, LANES), dt)
    shp = jax.ShapeDtypeStruct((S, A), BF16)
    return _call(
        body, name=name, grid=(Hp, nt),
        in_specs=[cur(0), cur(kb), prev(kb), cur(vb), prev(vb), cur(0), cur(0), cur(0)],
        out_specs=[cur(0)] * 3, out_shape=[shp] * 3,
        scratch=[vm(T, F32), vm(T, F32), vm(T, F32), vm(2 * T, F32), vm(2 * T, F32),
                 vm(T, BF16), vm(T, BF16), vm(T, F32), vm(T, F32), vm(2 * T, BF16), vm(2 * T, BF16),
                 vm(T, F32), vm(2 * T, F32), vm(2 * T, F32)] + [vm(T, F32)] * 5,
        args=(z, z, z, z, z, dy, y, lse), vmem=56, comm=comm)


def _shift_down(e, k):
    return pltpu.roll(e, k, 0)


def _shift_up(e, k):
    return pltpu.roll(e, e.shape[0] - k, 0)


def _causal_sums(e, g):
    for lvl in range(g + 1):
        e = e + _shift_down(e, 1 << lvl)
    return e


def _anticausal_sums(e, g):
    for lvl in range(g + 1):
        e = e + _shift_up(e, 1 << lvl)
    return e


def _mixer_specs(S, tm, A, C, P):
    assert (3 * A) % C == 0 and (3 * A + 3 * C) % P == 0 and tm % HALO == 0
    cb, pb = 3 * A // C, (3 * A + 3 * C) // P
    hb = tm // HALO
    last = S // HALO - 1
    col = lambda w, j: pl.BlockSpec((tm, w), lambda i: (i, j))
    prev = lambda w, j: pl.BlockSpec((HALO, w), lambda i: (jnp.maximum(i * hb - 1, 0), j))
    nxt = lambda w, j: pl.BlockSpec((HALO, w), lambda i: (jnp.minimum((i + 1) * hb, last), j))
    return cb, pb, col, prev, nxt


def _conv_taps(gc, ci, gch, cih, keep_prev):
    u = gc * ci
    e = jnp.concatenate([gch * cih * keep_prev, u], axis=0)
    return u, _shift_down(e, 1)[HALO:], _shift_down(e, 2)[HALO:]


def _pooled(xp, xph, keep_prev, pos, g):
    cols = slice(g * POOL_GROUP, (g + 1) * POOL_GROUP)
    x = xp[:, cols]
    e = jnp.concatenate([xph[:, cols] * keep_prev, x], axis=0)
    cnt = jnp.minimum(pos + 1, POOL_WINDOWS[g]).astype(F32)
    return _causal_sums(e, g)[HALO:] / cnt - x


def convpool_fwd(z, conv_w, pool_w, pool_scale, A, C, P, name):
    S = z.shape[0]
    tm = _tile(S, 512)
    cb, pb, col, prev, _ = _mixer_specs(S, tm, A, C, P)

    def body(gb_ref, gc_ref, ci_ref, xp_ref, gch_ref, cih_ref, xph_ref, cw_ref, pw_ref, ps_ref, y_ref):
        i = pl.program_id(0)
        keep_prev = jnp.where(i == 0, 0.0, 1.0)
        f = lambda r: r[...].astype(F32)
        u, u1, u2 = _conv_taps(f(gc_ref), f(ci_ref), f(gch_ref), f(cih_ref), keep_prev)
        cw = cw_ref[...]
        y_ref[:, 0:C] = (f(gb_ref) * (cw[0:1] * u2 + cw[1:2] * u1 + cw[2:3] * u)).astype(BF16)
        xp, xph = f(xp_ref), f(xph_ref)
        pos = i * tm + lax.broadcasted_iota(jnp.int32, (tm, 1), 0)
        for g in range(len(POOL_WINDOWS)):
            cols = slice(g * POOL_GROUP, (g + 1) * POOL_GROUP)
            lin = _dot(_pooled(xp, xph, keep_prev, pos, g).astype(BF16), pw_ref[g])
            y_ref[:, C + g * POOL_GROUP:C + (g + 1) * POOL_GROUP] = (lin * ps_ref[:, cols]).astype(BF16)

    full = lambda shape: pl.BlockSpec(shape, lambda i: (0,) * len(shape))
    return pl.pallas_call(
        body, name=name, grid=(S // tm,),
        in_specs=[col(C, cb), col(C, cb + 1), col(C, cb + 2), col(P, pb),
                  prev(C, cb + 1), prev(C, cb + 2), prev(P, pb),
                  full(conv_w.shape), full(pool_w.shape), full(pool_scale.shape)],
        out_specs=pl.BlockSpec((tm, C + P), lambda i: (i, 0)),
        out_shape=jax.ShapeDtypeStruct((S, C + P), BF16), compiler_params=_params(48),
    )(z, z, z, z, z, z, z, conv_w, pool_w, pool_scale)


def convpool_bwd(z, dycp, conv_w, pool_w, pool_scale, A, C, P, name):
    S = z.shape[0]
    tm = _tile(S, 512)
    nt = S // tm
    cb, pb, col, prev, nxt = _mixer_specs(S, tm, A, C, P)
    NG = len(POOL_WINDOWS)

    def body(gb_ref, gc_ref, ci_ref, xp_ref, gch_ref, cih_ref, xph_ref, gbn_ref, dy_ref, dyn_ref,
             cw_ref, pw_ref, ps_ref, dz_ref, dcw_ref, dpw_ref, dps_ref):
        i = pl.program_id(0)

        @pl.when(i == 0)
        def _():
            dcw_ref[...] = jnp.zeros_like(dcw_ref)
            dpw_ref[...] = jnp.zeros_like(dpw_ref)
            dps_ref[...] = jnp.zeros_like(dps_ref)

        keep_prev = jnp.where(i == 0, 0.0, 1.0)
        keep_next = jnp.where(i == nt - 1, 0.0, 1.0)
        f = lambda r: r[...].astype(F32)
        gb, gc, ci = f(gb_ref), f(gc_ref), f(ci_ref)
        u, u1, u2 = _conv_taps(gc, ci, f(gch_ref), f(cih_ref), keep_prev)
        cw = cw_ref[...]
        dy, dyn = f(dy_ref), f(dyn_ref) * keep_next
        dyc = dy[:, 0:C]
        dz_ref[:, 0:C] = (dyc * (cw[0:1] * u2 + cw[1:2] * u1 + cw[2:3] * u)).astype(BF16)
        dc = dyc * gb
        e = jnp.concatenate([dc, dyn[:, 0:C] * f(gbn_ref)], axis=0)
        du = cw[2:3] * dc + cw[1:2] * _shift_up(e, 1)[:tm] + cw[0:1] * _shift_up(e, 2)[:tm]
        dz_ref[:, C:2 * C] = (du * ci).astype(BF16)
        dz_ref[:, 2 * C:3 * C] = (du * gc).astype(BF16)
        dcw_ref[0:1, :] += jnp.sum(dc * u2, axis=0, keepdims=True)
        dcw_ref[1:2, :] += jnp.sum(dc * u1, axis=0, keepdims=True)
        dcw_ref[2:3, :] += jnp.sum(dc * u, axis=0, keepdims=True)

        xp, xph = f(xp_ref), f(xph_ref)
        pos = i * tm + lax.broadcasted_iota(jnp.int32, (tm, 1), 0)
        pos_e = i * tm + lax.broadcasted_iota(jnp.int32, (tm + HALO, 1), 0)
        for g in range(NG):
            cols = slice(g * POOL_GROUP, (g + 1) * POOL_GROUP)
            ycols = slice(C + g * POOL_GROUP, C + (g + 1) * POOL_GROUP)
            pooled = _pooled(xp, xph, keep_prev, pos, g).astype(BF16)
            pw = pw_ref[g]
            dyp = dy[:, ycols]
            dps_ref[:, cols] += jnp.sum(dyp * _dot(pooled, pw), axis=0, keepdims=True)
            dlin = (jnp.concatenate([dyp, dyn[:, ycols]], axis=0) * ps_ref[:, cols]).astype(BF16)
            dpw_ref[g] += _dot(pooled, dlin[:tm], TN)
            dpool = _dot(dlin, pw, NT)
            r = dpool / jnp.minimum(pos_e + 1, POOL_WINDOWS[g]).astype(F32)
            dz_ref[:, 3 * C + g * POOL_GROUP:3 * C + (g + 1) * POOL_GROUP] = (
                _anticausal_sums(r, g)[:tm] - dpool[:tm]).astype(BF16)

    full = lambda shape: pl.BlockSpec(shape, lambda i: (0,) * len(shape))
    return pl.pallas_call(
        body, name=name, grid=(nt,),
        in_specs=[col(C, cb), col(C, cb + 1), col(C, cb + 2), col(P, pb),
                  prev(C, cb + 1), prev(C, cb + 2), prev(P, pb), nxt(C, cb),
                  pl.BlockSpec((tm, C + P), lambda i: (i, 0)),
                  pl.BlockSpec((HALO, C + P), lambda i: (jnp.minimum((i + 1) * (tm // HALO), S // HALO - 1), 0)),
                  full(conv_w.shape), full(pool_w.shape), full(pool_scale.shape)],
        out_specs=[pl.BlockSpec((tm, 3 * C + P), lambda i: (i, 0)),
                   full(conv_w.shape), full(pool_w.shape), full(pool_scale.shape)],
        out_shape=[jax.ShapeDtypeStruct((S, 3 * C + P), BF16), jax.ShapeDtypeStruct(conv_w.shape, F32),
                   jax.ShapeDtypeStruct(pool_w.shape, F32), jax.ShapeDtypeStruct(pool_scale.shape, F32)],
        compiler_params=_params(48),
    )(z, z, z, z, z, z, z, z, dycp, dycp, conv_w, pool_w, pool_scale)


def adamw(lands, w, m, v, name):
    R, C = w.shape
    nl = len(lands)
    n, Rl, _ = lands[0].shape
    assert Rl * nl == R
    tr = _tile(Rl, 128 * 1024 // C)
    nb = Rl // tr
    c1 = 1.0 - ADAM_B1 ** ADAM_STEP
    c2 = 1.0 - ADAM_B2 ** ADAM_STEP

    def body(*refs):
        land_refs = refs[:nl]
        w_ref, m_ref, v_ref, g_ref, d_ref, mo_ref, vo_ref = refs[nl:]
        i = pl.program_id(0)
        for a, land_ref in enumerate(land_refs):
            @pl.when((i >= a * nb) & (i < (a + 1) * nb))
            def _():
                g = land_ref[0].astype(F32)
                for s in range(1, n):
                    g = g + land_ref[s].astype(F32)
                g_ref[...] = g
                mn = ADAM_B1 * m_ref[...] + (1.0 - ADAM_B1) * g
                vn = ADAM_B2 * v_ref[...] + (1.0 - ADAM_B2) * (g * g)
                mo_ref[...] = mn
                vo_ref[...] = vn
                d_ref[...] = -ADAM_LR * ((mn / c1) / (jnp.sqrt(vn / c2) + ADAM_EPS) + ADAM_WD * w_ref[...])

    land_specs = [pl.BlockSpec((n, tr, C), lambda i, a=a: (0, jnp.clip(i - a * nb, 0, nb - 1), 0))
                  for a in range(nl)]
    row = pl.BlockSpec((tr, C), lambda i: (i, 0))
    shp = jax.ShapeDtypeStruct((R, C), F32)
    return pl.pallas_call(
        body, name=name, grid=(nl * nb,),
        in_specs=land_specs + [row, row, row], out_specs=[row] * 4, out_shape=[shp] * 4,
        compiler_params=_params(48),
    )(*lands, w, m, v)


def _gather_comm(arrs):
    return ("gather", arrs) if arrs else None


def _ffn_bwd(g, gb, saved, gain, wg, wu, wd, tag):
    x, ht, a, b, hid = saved
    da, db = ffn_dhid(gb, wd, a, b, f"ffn_dhid_{tag}")
    dwd = grad_shared_rhs(hid, gb, None, FFN_RESIDUAL, f"ffn_dwd_{tag}")
    dwg, (land_wd,) = grad_lhs(ht, da, True, f"ffn_dwg_{tag}", ("exchange", [dwd]))
    dwu, (land_wg,) = grad_lhs(ht, db, True, f"ffn_dwu_{tag}", ("exchange", [dwg]))
    dh, (land_wu,) = back_proj([da, db], [wg, wu], True, f"ffn_dh_{tag}", ("exchange", [dwu]))
    g, gb, dgain = rms_bwd(x, gain, dh, g, f"rms_bwd_{tag}")
    return g, gb, dgain, (land_wg, land_wu, land_wd)


def _mixer_fwd(x, h, ht, next_gain, w_in, conv_w, pool_w, pool_scale, w_out, dims, tag, carry):
    A, C, P = dims
    z, got_a = proj_cols(h, w_in, f"in_proj_{tag}", _gather_comm(carry[:1]))
    ya, lse, got_b = attn_fwd(z, A, f"attn_fwd_{tag}", _gather_comm(carry[1:]))
    ycp = convpool_fwd(z, conv_w, pool_w, pool_scale, A, C, P, f"convpool_{tag}")
    xo, hn, htn = out_proj(ya, ycp, w_out, x, next_gain, f"out_proj_{tag}")
    return xo, hn, htn, (x, ht, z, lse, ya, ycp), got_a + got_b


def _mixer_bwd(g, gb, saved, gain, w_in, conv_w, pool_w, pool_scale, w_out, dims, tag):
    A, C, P = dims
    x, ht, z, lse, ya, ycp = saved
    S, D = x.shape
    dya, dycp = out_proj_bwd(gb, w_out, A, f"out_proj_bwd_{tag}")
    dwo_a = grad_shared_rhs(ya, gb, 1, 1.0, f"dwo_attn_{tag}")
    dwo_cp = grad_shared_rhs(ycp, gb, 1, 1.0, f"dwo_cp_{tag}")
    dwo = jnp.concatenate([dwo_a[0], dwo_cp[0]], axis=0).reshape(NDEV, -1, D)
    dqkv, (land_wo,) = attn_bwd(z, dya, ya, lse, A, f"attn_bwd_{tag}", ("exchange", [dwo]))
    dz_cp, dcw, dpw, dps = convpool_bwd(z, dycp, conv_w, pool_w, pool_scale, A, C, P, f"convpool_bwd_{tag}")
    dz = jnp.concatenate(dqkv + [dz_cp], axis=1)
    dwin, _ = grad_lhs(ht, dz, False, f"dwin_{tag}")
    dh, (land_win,) = back_proj([dz], [w_in], False, f"in_proj_bwd_{tag}", ("exchange", [dwin]))
    g, gb, dgain = rms_bwd(x, gain, dh, g, f"rms_bwd_{tag}")
    return g, gb, (dgain, dcw, dpw, dps), (land_win, land_wo)


def _pack(arrs):
    flat = [a.reshape(-1).astype(F32) for a in arrs]
    spans, off = [], 0
    for a in flat:
        spans.append((off, a.shape[0]))
        off += a.shape[0]
    rows = -(-off // (8 * LANES)) * 8
    buf = jnp.concatenate(flat + [jnp.zeros((rows * LANES - off,), F32)]).reshape(rows, LANES)
    return buf, spans


def kernel(x, ffn1_norm, ffn1_w_gate, ffn1_w_up, ffn1_w_down, mix_norm, w_in, conv_w, pool_w, pool_scale, w_out, ffn2_norm, ffn2_w_gate, ffn2_w_up, ffn2_w_down, final_norm, loss_target, m_ffn1_norm, m_ffn1_w_gate, m_ffn1_w_up, m_ffn1_w_down, m_mix_norm, m_w_in, m_conv_w, m_pool_w, m_pool_scale, m_w_out, m_ffn2_norm, m_ffn2_w_gate, m_ffn2_w_up, m_ffn2_w_down, m_final_norm, v_ffn1_norm, v_ffn1_w_gate, v_ffn1_w_up, v_ffn1_w_down, v_mix_norm, v_w_in, v_conv_w, v_pool_w, v_pool_scale, v_w_out, v_ffn2_norm, v_ffn2_w_gate, v_ffn2_w_up, v_ffn2_w_down, v_final_norm):
    depth = ffn1_norm.shape[0]
    S, D = x.shape[1], x.shape[2]
    Cb = conv_w.shape[2]
    C = Cb * NDEV
    P = pool_scale.shape[1]
    A = (w_in.shape[2] * NDEV - 3 * C - P) // 3
    dims = (A, C, P)
    me = 4 * lax.axis_index("x") + 2 * lax.axis_index("y") + lax.axis_index("c")

    big = dict(ffn1_w_gate=ffn1_w_gate, ffn1_w_up=ffn1_w_up, ffn1_w_down=ffn1_w_down, w_in=w_in, w_out=w_out,
               ffn2_w_gate=ffn2_w_gate, ffn2_w_up=ffn2_w_up, ffn2_w_down=ffn2_w_down)
    big_m = dict(ffn1_w_gate=m_ffn1_w_gate, ffn1_w_up=m_ffn1_w_up, ffn1_w_down=m_ffn1_w_down, w_in=m_w_in,
                 w_out=m_w_out, ffn2_w_gate=m_ffn2_w_gate, ffn2_w_up=m_ffn2_w_up, ffn2_w_down=m_ffn2_w_down)
    big_v = dict(ffn1_w_gate=v_ffn1_w_gate, ffn1_w_up=v_ffn1_w_up, ffn1_w_down=v_ffn1_w_down, w_in=v_w_in,
                 w_out=v_w_out, ffn2_w_gate=v_ffn2_w_gate, ffn2_w_up=v_ffn2_w_up, ffn2_w_down=v_ffn2_w_down)
    names = list(big)

    first_names = ["ffn1_w_gate", "ffn1_w_up"]
    ffn1_names = first_names + ["ffn1_w_down"]
    mix_names = ["w_in", "w_out"]
    ffn2_names = ["ffn2_w_gate", "ffn2_w_up", "ffn2_w_down"]
    shards = lambda l, ns: [big[n][l].astype(BF16) for n in ns]
    full = [dict() for _ in range(depth)]
    full[0].update(zip(first_names, all_gather(shards(0, first_names), "gather_first")))
    conv_full = all_gather([jnp.pad(conv_w.reshape(-1, Cb), ((0, 2), (0, LANES - Cb)))], "gather_conv_w")[0]
    conv_full = jnp.transpose(conv_full[:, :depth * 3, :Cb].reshape(NDEV, depth, 3, Cb), (1, 2, 0, 3)).reshape(depth, 3, C)
    pool_w_bf = pool_w.astype(BF16)

    xs = x[0]
    saved = []
    h, ht = rms_fwd(xs, ffn1_norm[0:1], "rms_f1_l0")
    for l in range(depth):
        W = full[l]
        last = l + 1 == depth
        up1 = ["ffn1_w_down"] + mix_names if l == 0 else []
        (a, b, hid), got = ffn_up(h, W["ffn1_w_gate"], W["ffn1_w_up"], f"ffn_up_f1_l{l}", _gather_comm(shards(l, up1)))
        W.update(zip(up1, got))
        s1 = (xs, ht, a, b, hid)
        (xs, h, ht), _ = ffn_down(hid, W["ffn1_w_down"], xs, f"ffn_down_f1_l{l}", next_gain=mix_norm[l:l + 1])
        xs, h, ht, s2, got = _mixer_fwd(xs, h, ht, ffn2_norm[l:l + 1], W["w_in"], conv_full[l], pool_w_bf[l],
                                        pool_scale[l:l + 1], W["w_out"].reshape(-1, D), dims, f"mix_l{l}",
                                        shards(l, ffn2_names))
        W.update(zip(ffn2_names, got))
        up2, down2 = ([], []) if last else (ffn1_names, mix_names)
        (a, b, hid), got_up = ffn_up(h, W["ffn2_w_gate"], W["ffn2_w_up"], f"ffn_up_f2_l{l}",
                                     _gather_comm(shards(l + 1, up2) if up2 else []))
        s3 = (xs, ht, a, b, hid)
        outs, got_down = ffn_down(hid, W["ffn2_w_down"], xs, f"ffn_down_f2_l{l}",
                                  _gather_comm(shards(l + 1, down2) if down2 else []),
                                  next_gain=None if last else ffn1_norm[l + 1:l + 2])
        if last:
            xs, = outs
        else:
            xs, h, ht = outs
            full[l + 1].update(zip(up2, got_up))
            full[l + 1].update(zip(down2, got_down))
        saved.append((s1, s2, s3))
    loss_part, g, gb, d_final = final_loss(xs, final_norm.reshape(1, D), loss_target[0], "final_loss")

    small = [None] * depth
    lands = [None] * depth
    for l in reversed(range(depth)):
        W = full[l]
        s1, s2, s3 = saved[l]
        g, gb, dn2, (lg2, lu2, ld2) = _ffn_bwd(g, gb, s3, ffn2_norm[l:l + 1], W["ffn2_w_gate"], W["ffn2_w_up"],
                                               W["ffn2_w_down"], f"f2_l{l}")
        g, gb, (dnm, dcw, dpw, dps), (lwin, lwo) = _mixer_bwd(
            g, gb, s2, mix_norm[l:l + 1], W["w_in"], conv_full[l], pool_w_bf[l], pool_scale[l:l + 1],
            W["w_out"].reshape(-1, D), dims, f"mix_l{l}")
        g, gb, dn1, (lg1, lu1, ld1) = _ffn_bwd(g, gb, s1, ffn1_norm[l:l + 1], W["ffn1_w_gate"], W["ffn1_w_up"],
                                               W["ffn1_w_down"], f"f1_l{l}")
        lands[l] = dict(ffn1_w_gate=lg1, ffn1_w_up=lu1, ffn1_w_down=ld1, w_in=lwin, w_out=lwo,
                        ffn2_w_gate=lg2, ffn2_w_up=lu2, ffn2_w_down=ld2)
        small[l] = (dn1, dnm, dcw, dpw, dps, dn2)

    res = {}
    for n in names:
        shp = big[n].shape
        two = lambda t: t.reshape(-1, shp[-1])
        outs = adamw([lands[l][n] for l in range(depth)], two(big[n]), two(big_m[n]), two(big_v[n]), f"adamw_{n}")
        res[n] = [o.reshape(shp) for o in outs]

    st = lambda i: jnp.stack([small[l][i] for l in range(depth)])
    small_g = dict(ffn1_norm=st(0), mix_norm=st(1), conv_w=st(2), pool_w=st(3), pool_scale=st(4), ffn2_norm=st(5),
                   final_norm=d_final)
    small_names = list(small_g)
    zeros_conv = jnp.zeros((depth, 3, C), F32)
    place = lambda t: lax.dynamic_update_slice(zeros_conv, t, (0, 0, me * Cb))
    small_w = dict(ffn1_norm=ffn1_norm, mix_norm=mix_norm, conv_w=place(conv_w), pool_w=pool_w, pool_scale=pool_scale,
                   ffn2_norm=ffn2_norm, final_norm=final_norm)
    small_m = dict(ffn1_norm=m_ffn1_norm, mix_norm=m_mix_norm, conv_w=place(m_conv_w), pool_w=m_pool_w,
                   pool_scale=m_pool_scale, ffn2_norm=m_ffn2_norm, final_norm=m_final_norm)
    small_v = dict(ffn1_norm=v_ffn1_norm, mix_norm=v_mix_norm, conv_w=place(v_conv_w), pool_w=v_pool_w,
                   pool_scale=v_pool_scale, ffn2_norm=v_ffn2_norm, final_norm=v_final_norm)
    gbuf, spans = _pack([small_g[n] for n in small_names] + [loss_part])
    wbuf, _ = _pack([small_w[n] for n in small_names] + [jnp.zeros((1, LANES), F32)])
    mbuf, _ = _pack([small_m[n] for n in small_names] + [jnp.zeros((1, LANES), F32)])
    vbuf, _ = _pack([small_v[n] for n in small_names] + [jnp.zeros((1, LANES), F32)])
    gathered = all_gather([gbuf], "gather_small_grads")[0]
    outs = adamw([gathered], wbuf, mbuf, vbuf, "adamw_small")
    for n, (off, size) in zip(small_names, spans):
        shp = small_w[n].shape
        vals = [o.reshape(-1)[off:off + size].reshape(shp) for o in outs]
        if n == "conv_w":
            vals = [lax.dynamic_slice(t, (0, 0, me * Cb), (depth, 3, Cb)) for t in vals]
        res[n] = vals
    loss = outs[0].reshape(-1)[spans[-1][0]]

    order = ["ffn1_norm", "ffn1_w_gate", "ffn1_w_up", "ffn1_w_down", "mix_norm", "w_in", "conv_w", "pool_w",
             "pool_scale", "w_out", "ffn2_norm", "ffn2_w_gate", "ffn2_w_up", "ffn2_w_down", "final_norm"]
    return (loss, g[None], *[res[n][0] for n in order], *[res[n][1] for n in order],
            *[res[n][2] for n in order], *[res[n][3] for n in order])
```

```python
import functools

import jax
import jax.numpy as jnp
from jax import lax
from jax.experimental import pallas as pl
from jax.experimental.pallas import tpu as pltpu

F32 = jnp.float32
BF16 = jnp.bfloat16
NDEV = 8
HEAD_DIM = 64
LANES = 128
BLK = 128
DILATIONS = (1, 4, 16)
POOL_WINDOWS = (2, 4, 8, 16)
POOL_GROUP = 128
HALO = 16
FFN_RESIDUAL = 0.5
RMS_EPS = 1e-6
NEG_INF = -1e30
ATTN_SCALE = HEAD_DIM ** -0.5
ADAM_LR, ADAM_B1, ADAM_B2, ADAM_EPS, ADAM_WD, ADAM_STEP = 0.001, 0.9, 0.999, 1e-08, 0.01, 10
VMEM_BYTES = 64 * 1024 * 1024
MESH = pl.DeviceIdType.MESH

NN = (((1,), (0,)), ((), ()))
NT = (((1,), (1,)), ((), ()))
TN = (((0,), (0,)), ((), ()))


def _dot(a, b, dims=NN):
    return lax.dot_general(a, b, dims, preferred_element_type=F32)


def _params(vmem_mb=48):
    return pltpu.CompilerParams(vmem_limit_bytes=min(vmem_mb * 1024 * 1024, VMEM_BYTES - 4 * 1024 * 1024))


def _tile(n, want):
    t = min(n, max(16, want // 16 * 16))
    while t > 16 and (n % t or t % 16):
        t -= 16
    return t if n % t == 0 else n


def _coords():
    return lax.axis_index("x"), lax.axis_index("y"), lax.axis_index("c")


def _comm_sems(n):
    return [pltpu.SemaphoreType.DMA((7 * n,)), pltpu.SemaphoreType.DMA((7 * n,)), pltpu.SemaphoreType.DMA((n,))]


def _gather_ops(ins, outs, sems):
    n = len(ins)
    send_sems, recv_sems, local_sems = sems
    x, y, c = _coords()
    me, sibling = (x, y, c), (x, y, 1 - c)
    chips = [(1 - x, y), (x, 1 - y), (1 - x, 1 - y)]

    def slot(out, p):
        return out.at[4 * p[0] + 2 * p[1] + p[2]]

    def copy(a, k, block, to, src=None):
        return pltpu.make_async_remote_copy(
            src_ref=slot(outs[a], block) if src is None else src, dst_ref=slot(outs[a], block),
            send_sem=send_sems.at[7 * a + k], recv_sem=recv_sems.at[7 * a + k],
            device_id=to, device_id_type=MESH)

    def own():
        mine = [pltpu.make_async_copy(ins[a], slot(outs[a], me), local_sems.at[a]) for a in range(n)]
        first = []
        for a in range(n):
            first.append(copy(a, 0, me, sibling, src=ins[a]))
            first += [copy(a, 1 + j, me, (*chip, c), src=ins[a]) for j, chip in enumerate(chips)]
        return mine, first

    def start():
        mine, first = own()
        for cp in mine + first:
            cp.start()

    def finish():
        mine, first = own()
        passed = []
        for j, chip in enumerate(chips):
            for a in range(n):
                copy(a, 1 + j, (*chip, c), me).wait_recv()
                fwd = copy(a, 4 + j, (*chip, c), sibling)
                fwd.start()
                passed.append(fwd)
        for a in range(n):
            copy(a, 0, sibling, me).wait_recv()
            for j, chip in enumerate(chips):
                copy(a, 4 + j, (*chip, 1 - c), me).wait_recv()
        for cp in first + passed:
            cp.wait_send()
        for cp in mine:
            cp.wait()

    return start, finish


def _exchange_ops(ins, outs, sems):
    n = len(ins)
    send_sems, recv_sems, local_sems = sems
    x, y, c = _coords()
    me = 4 * x + 2 * y + c

    def peer(k):
        return (1 - x if k & 4 else x, 1 - y if k & 2 else y, 1 - c if k & 1 else c)

    def copy(a, k):
        p = peer(k)
        return pltpu.make_async_remote_copy(
            src_ref=ins[a].at[4 * p[0] + 2 * p[1] + p[2]], dst_ref=outs[a].at[me],
            send_sem=send_sems.at[7 * a + k - 1], recv_sem=recv_sems.at[7 * a + k - 1],
            device_id=p, device_id_type=MESH)

    def landed(a, k):
        p = peer(k)
        return pltpu.make_async_remote_copy(
            src_ref=ins[a].at[me], dst_ref=outs[a].at[4 * p[0] + 2 * p[1] + p[2]],
            send_sem=send_sems.at[7 * a + k - 1], recv_sem=recv_sems.at[7 * a + k - 1],
            device_id=p, device_id_type=MESH)

    def own():
        mine = [pltpu.make_async_copy(ins[a].at[me], outs[a].at[me], local_sems.at[a]) for a in range(n)]
        return mine, [copy(a, k) for a in range(n) for k in range(1, NDEV)]

    def start():
        mine, sent = own()
        for cp in mine + sent:
            cp.start()

    def finish():
        mine, sent = own()
        for a in range(n):
            for k in range(1, NDEV):
                landed(a, k).wait_recv()
        for cp in sent:
            cp.wait_send()
        for cp in mine:
            cp.wait()

    return start, finish


_COMM = {"gather": (_gather_ops, lambda a: (NDEV,) + a.shape), "exchange": (_exchange_ops, lambda a: a.shape)}


def all_gather(arrs, name):
    n = len(arrs)

    def body(*refs):
        start, finish = _gather_ops(refs[:n], refs[n:2 * n], refs[2 * n:])
        start()
        finish()

    any_spec = pl.BlockSpec(memory_space=pl.ANY)
    return pl.pallas_call(
        body, name=name, out_shape=[jax.ShapeDtypeStruct((NDEV,) + a.shape, a.dtype) for a in arrs],
        in_specs=[any_spec] * n, out_specs=[any_spec] * n, scratch_shapes=_comm_sems(n),
    )(*arrs)


def _call(body, *, name, grid, in_specs, out_specs, out_shape, args, scratch=(), vmem=48, comm=None):
    if comm is None:
        outs = pl.pallas_call(
            body, name=name, grid=grid, in_specs=list(in_specs), out_specs=list(out_specs), out_shape=list(out_shape),
            scratch_shapes=list(scratch), compiler_params=_params(vmem))(*args)
        return list(outs), []
    kind, arrs = comm
    ops, shape_of = _COMM[kind]
    n, n_in, n_out, n_scr = len(arrs), len(in_specs), len(out_specs), len(scratch)

    def carrier(*refs):
        ins, c_in = refs[:n_in], refs[n_in:n_in + n]
        o0 = n_in + n
        outs, c_out = refs[o0:o0 + n_out], refs[o0 + n_out:o0 + n_out + n]
        s0 = o0 + n_out + n
        scr, sems = refs[s0:s0 + n_scr], refs[s0 + n_scr:]
        ids = [pl.program_id(d) for d in range(len(grid))]
        first = functools.reduce(jnp.logical_and, [i == 0 for i in ids])
        last = functools.reduce(jnp.logical_and, [i == g - 1 for i, g in zip(ids, grid)])
        start, finish = ops(c_in, c_out, sems)
        pl.when(first)(start)
        body(*ins, *outs, *scr)
        pl.when(last)(finish)

    any_spec = pl.BlockSpec(memory_space=pl.ANY)
    res = pl.pallas_call(
        carrier, name=name, grid=grid, in_specs=list(in_specs) + [any_spec] * n,
        out_specs=list(out_specs) + [any_spec] * n,
        out_shape=list(out_shape) + [jax.ShapeDtypeStruct(shape_of(a), a.dtype) for a in arrs],
        scratch_shapes=list(scratch) + _comm_sems(n), compiler_params=_params(vmem))(*args, *arrs)
    return list(res[:n_out]), list(res[n_out:])


def _rstd(x):
    return lax.rsqrt(jnp.mean(x * x, axis=-1, keepdims=True) + RMS_EPS)


def rms_fwd(x, gain, name):
    S, D = x.shape
    tm = _tile(S, 512)

    def body(x_ref, g_ref, h_ref, ht_ref):
        xv = x_ref[...]
        h = xv * _rstd(xv) * g_ref[...]
        h_ref[...] = h.astype(BF16)
        ht_ref[...] = h.T.astype(BF16)

    return pl.pallas_call(
        body, name=name, grid=(S // tm,),
        in_specs=[pl.BlockSpec((tm, D), lambda i: (i, 0)), pl.BlockSpec((1, D), lambda i: (0, 0))],
        out_specs=[pl.BlockSpec((tm, D), lambda i: (i, 0)), pl.BlockSpec((D, tm), lambda i: (0, i))],
        out_shape=[jax.ShapeDtypeStruct((S, D), BF16), jax.ShapeDtypeStruct((D, S), BF16)],
        compiler_params=_params(40),
    )(x, gain)


def _rms_bwd_math(xv, gain, dh):
    r = _rstd(xv)
    xhat = xv * r
    dxhat = dh * gain
    dx = r * (dxhat - xhat * jnp.mean(dxhat * xhat, axis=-1, keepdims=True))
    return dx, dh * xhat


def rms_bwd(x, gain, dh, g, name):
    S, D = x.shape
    tm = _tile(S, 256)

    def body(x_ref, gain_ref, dh_ref, g_ref, go_ref, gb_ref, dg_ref):
        @pl.when(pl.program_id(0) == 0)
        def _():
            dg_ref[...] = jnp.zeros_like(dg_ref)

        dx, dgain = _rms_bwd_math(x_ref[...], gain_ref[...], dh_ref[...].astype(F32))
        gn = g_ref[...] + dx
        go_ref[...] = gn
        gb_ref[...] = gn.astype(BF16)
        dg_ref[...] += jnp.sum(dgain, axis=0, keepdims=True)

    row = pl.BlockSpec((tm, D), lambda i: (i, 0))
    vec = pl.BlockSpec((1, D), lambda i: (0, 0))
    return pl.pallas_call(
        body, name=name, grid=(S // tm,),
        in_specs=[row, vec, row, row], out_specs=[row, row, vec],
        out_shape=[jax.ShapeDtypeStruct((S, D), F32), jax.ShapeDtypeStruct((S, D), BF16),
                   jax.ShapeDtypeStruct((1, D), F32)],
        compiler_params=_params(48),
    )(x, gain, dh, g)


def final_loss(x, gain, target, name):
    S, D = x.shape
    tm = _tile(S, 256)

    def body(x_ref, gain_ref, t_ref, loss_ref, go_ref, gb_ref, dg_ref):
        @pl.when(pl.program_id(0) == 0)
        def _():
            dg_ref[...] = jnp.zeros_like(dg_ref)
            loss_ref[...] = jnp.zeros_like(loss_ref)

        xv, gain_v = x_ref[...], gain_ref[...]
        err = xv * _rstd(xv) * gain_v - t_ref[...]
        loss_ref[...] += jnp.sum(jnp.sum(err * err, axis=-1, keepdims=True), axis=0, keepdims=True) * (0.5 / D)
        dx, dgain = _rms_bwd_math(xv, gain_v, err * (1.0 / D))
        go_ref[...] = dx
        gb_ref[...] = dx.astype(BF16)
        dg_ref[...] += jnp.sum(dgain, axis=0, keepdims=True)

    row = pl.BlockSpec((tm, D), lambda i: (i, 0))
    vec = pl.BlockSpec((1, D), lambda i: (0, 0))
    return pl.pallas_call(
        body, name=name, grid=(S // tm,),
        in_specs=[row, vec, row],
        out_specs=[pl.BlockSpec((1, LANES), lambda i: (0, 0)), row, row, vec],
        out_shape=[jax.ShapeDtypeStruct((1, LANES), F32), jax.ShapeDtypeStruct((S, D), F32),
                   jax.ShapeDtypeStruct((S, D), BF16), jax.ShapeDtypeStruct((1, D), F32)],
        compiler_params=_params(48),
    )(x, gain, target)


def _blk(arr, width, tm, stacked):
    if stacked:
        return pl.BlockSpec((None, tm, width), lambda j, i: (j, i, 0))
    return pl.BlockSpec((tm, width), lambda j, i: (i, j))


def ffn_up(h, wg, wu, name, comm=None):
    S, D = h.shape
    J, _, Fb = wg.shape
    tm = _tile(S, 1024)

    def body(h_ref, wg_ref, wu_ref, a_ref, b_ref, hid_ref):
        hv = h_ref[...]
        a = _dot(hv, wg_ref[...])
        b = _dot(hv, wu_ref[...])
        a_ref[...] = a.astype(BF16)
        b_ref[...] = b.astype(BF16)
        hid_ref[...] = (a * jax.nn.sigmoid(a) * b).astype(BF16)

    w_spec = pl.BlockSpec((None, D, Fb), lambda j, i: (j, 0, 0))
    o_spec = pl.BlockSpec((None, tm, Fb), lambda j, i: (j, i, 0))
    shp = jax.ShapeDtypeStruct((J, S, Fb), BF16)
    return _call(
        body, name=name, grid=(J, S // tm),
        in_specs=[pl.BlockSpec((tm, D), lambda j, i: (i, 0)), w_spec, w_spec],
        out_specs=[o_spec, o_spec, o_spec], out_shape=[shp, shp, shp], args=(h, wg, wu), comm=comm)


def proj_cols(h, w, name, comm=None):
    S, D = h.shape
    J, _, Wb = w.shape
    tm = _tile(S, 1024)
    kb = 2 if Wb % LANES == 0 and J % 2 == 0 else 1

    def body(h_ref, w_ref, z_ref):
        wv = jnp.concatenate([w_ref[k] for k in range(kb)], axis=1)
        z_ref[...] = _dot(h_ref[...], wv).astype(BF16)

    (z,), got = _call(
        body, name=name, grid=(J // kb, S // tm),
        in_specs=[pl.BlockSpec((tm, D), lambda j, i: (i, 0)), pl.BlockSpec((kb, D, Wb), lambda j, i: (j, 0, 0))],
        out_specs=[pl.BlockSpec((tm, kb * Wb), lambda j, i: (i, j))],
        out_shape=[jax.ShapeDtypeStruct((S, J * Wb), BF16)], args=(h, w), comm=comm)
    return z, got


def _norm_outs(S, D, tm, gain):
    if gain is None:
        return [], [], []
    return ([pl.BlockSpec((1, D), lambda i: (0, 0))],
            [pl.BlockSpec((tm, D), lambda i: (i, 0)), pl.BlockSpec((D, tm), lambda i: (0, i))],
            [jax.ShapeDtypeStruct((S, D), BF16), jax.ShapeDtypeStruct((D, S), BF16)])


def _write_norm(xo, gain_ref, h_ref, ht_ref):
    h = xo * _rstd(xo) * gain_ref[...]
    h_ref[...] = h.astype(BF16)
    ht_ref[...] = h.T.astype(BF16)


def ffn_down(hid, wd, x, name, comm=None, next_gain=None):
    J, S, Fb = hid.shape
    D = wd.shape[2]
    tm = _tile(S, 256)

    def body(hid_ref, wd_ref, x_ref, *rest):
        acc = _dot(hid_ref[0], wd_ref[0])
        for j in range(1, J):
            acc = acc + _dot(hid_ref[j], wd_ref[j])
        xo = x_ref[...] + FFN_RESIDUAL * acc
        if next_gain is None:
            rest[0][...] = xo
        else:
            gain_ref, o_ref, h_ref, ht_ref = rest
            o_ref[...] = xo
            _write_norm(xo, gain_ref, h_ref, ht_ref)

    row = pl.BlockSpec((tm, D), lambda i: (i, 0))
    g_in, n_specs, n_shapes = _norm_outs(S, D, tm, next_gain)
    outs, got = _call(
        body, name=name, grid=(S // tm,),
        in_specs=[pl.BlockSpec((J, tm, Fb), lambda i: (0, i, 0)),
                  pl.BlockSpec((J, Fb, D), lambda i: (0, 0, 0), pipeline_mode=pl.Buffered(1)), row] + g_in,
        out_specs=[row] + n_specs, out_shape=[jax.ShapeDtypeStruct((S, D), F32)] + n_shapes,
        args=(hid, wd, x) + (() if next_gain is None else (next_gain,)), vmem=56, comm=comm)
    return outs, got


def ffn_dhid(gb, wd, a, b, name):
    S, D = gb.shape
    J, Fb, _ = wd.shape
    tm = _tile(S, 1024)

    parts = 4 if tm % 64 == 0 else 1

    def body(g_ref, wd_ref, a_ref, b_ref, da_ref, db_ref):
        wdv = wd_ref[...]
        for part in range(parts):
            rows = pl.ds(part * (tm // parts), tm // parts)
            dhid = _dot(g_ref[rows, :], wdv, NT) * FFN_RESIDUAL
            a = a_ref[rows, :].astype(F32)
            b = b_ref[rows, :].astype(F32)
            sg = jax.nn.sigmoid(a)
            da_ref[rows, :] = (dhid * b * (sg * (1.0 + a * (1.0 - sg)))).astype(BF16)
            db_ref[rows, :] = (dhid * (a * sg)).astype(BF16)

    o_spec = pl.BlockSpec((None, tm, Fb), lambda j, i: (j, i, 0))
    shp = jax.ShapeDtypeStruct((J, S, Fb), BF16)
    return pl.pallas_call(
        body, name=name, grid=(J, S // tm),
        in_specs=[pl.BlockSpec((tm, D), lambda j, i: (i, 0)), pl.BlockSpec((None, Fb, D), lambda j, i: (j, 0, 0)),
                  o_spec, o_spec],
        out_specs=[o_spec, o_spec], out_shape=[shp, shp], compiler_params=_params(48),
    )(gb, wd, a, b)


def back_proj(ds, ws, stacked, name, comm=None):
    n = len(ds)
    J, D, Wb = ws[0].shape
    S = ds[0].shape[1] if stacked else ds[0].shape[0]
    kb = J if n == 1 else 2
    nj = J // kb
    tm = _tile(S, 256 if n == 1 else 512)

    def body(*refs):
        d_refs, w_refs, o_ref = refs[:n], refs[n:2 * n], refs[2 * n]
        acc = None
        for d_ref, w_ref in zip(d_refs, w_refs):
            for k in range(kb):
                dk = d_ref[k] if stacked else d_ref[:, k * Wb:(k + 1) * Wb]
                t = _dot(dk, w_ref[k], NT)
                acc = t if acc is None else acc + t
        if nj == 1:
            o_ref[...] = acc.astype(BF16)
        else:
            acc_ref = refs[2 * n + 1]
            j = pl.program_id(1)

            @pl.when(j == 0)
            def _():
                acc_ref[...] = acc

            @pl.when((j > 0) & (j < nj - 1))
            def _():
                acc_ref[...] += acc

            @pl.when(j == nj - 1)
            def _():
                o_ref[...] = (acc_ref[...] + acc).astype(BF16)

    if stacked:
        d_spec = pl.BlockSpec((kb, tm, Wb), lambda i, j: (j, i, 0))
    else:
        d_spec = pl.BlockSpec((tm, kb * Wb), lambda i, j: (i, j))
    w_spec = pl.BlockSpec((kb, D, Wb), lambda i, j: (j, 0, 0), pipeline_mode=pl.Buffered(1) if nj == 1 else None)
    (dh,), got = _call(
        body, name=name, grid=(S // tm, nj),
        in_specs=[d_spec] * n + [w_spec] * n,
        out_specs=[pl.BlockSpec((tm, D), lambda i, j: (i, 0))], out_shape=[jax.ShapeDtypeStruct((S, D), BF16)],
        scratch=[] if nj == 1 else [pltpu.VMEM((tm, D), F32)], args=(*ds, *ws), vmem=56, comm=comm)
    return dh, got


def grad_lhs(ht, d, stacked, name, comm=None):
    D, S = ht.shape
    if stacked:
        J, _, Wb = d.shape
    else:
        J, Wb = NDEV, d.shape[1] // NDEV
    kb = 2 if not stacked and Wb % LANES == 0 and J % 2 == 0 else 1
    tk = _tile(S, 2048 // kb)
    nk = S // tk

    def body(ht_ref, d_ref, o_ref, acc_ref):
        i = pl.program_id(1)

        @pl.when(i == 0)
        def _():
            acc_ref[...] = jnp.zeros_like(acc_ref)

        acc_ref[...] += _dot(ht_ref[...], d_ref[...])

        @pl.when(i == nk - 1)
        def _():
            for k in range(kb):
                o_ref[k] = acc_ref[:, k * Wb:(k + 1) * Wb].astype(BF16)

    d_spec = pl.BlockSpec((None, tk, Wb), lambda j, i: (j, i, 0)) if stacked else pl.BlockSpec((tk, kb * Wb), lambda j, i: (i, j))
    (dw,), got = _call(
        body, name=name, grid=(J // kb, nk),
        in_specs=[pl.BlockSpec((D, tk), lambda j, i: (0, i)), d_spec],
        out_specs=[pl.BlockSpec((kb, D, Wb), lambda j, i: (j, 0, 0))],
        out_shape=[jax.ShapeDtypeStruct((J, D, Wb), BF16)],
        scratch=[pltpu.VMEM((D, kb * Wb), F32)], args=(ht, d), vmem=56, comm=comm)
    return dw, got


def grad_shared_rhs(a, gb, nblk, scale, name):
    S, D = gb.shape
    stacked = nblk is None
    if stacked:
        J, _, Wb = a.shape
    else:
        J, Wb = nblk, a.shape[1] // nblk
    tk = _tile(S, 2048 if Wb <= 768 else 1024)
    nk = S // tk

    def body(a_ref, g_ref, o_ref, acc_ref):
        i = pl.program_id(1)

        @pl.when(i == 0)
        def _():
            acc_ref[...] = jnp.zeros_like(acc_ref)

        acc_ref[...] += _dot(a_ref[...], g_ref[...], TN)

        @pl.when(i == nk - 1)
        def _():
            o_ref[...] = (acc_ref[...] * scale).astype(BF16)

    return pl.pallas_call(
        body, name=name, grid=(J, nk),
        in_specs=[_blk(a, Wb, tk, stacked), pl.BlockSpec((tk, D), lambda j, i: (i, 0))],
        out_specs=pl.BlockSpec((None, Wb, D), lambda j, i: (j, 0, 0)),
        out_shape=jax.ShapeDtypeStruct((J, Wb, D), BF16),
        scratch_shapes=[pltpu.VMEM((Wb, D), F32)], compiler_params=_params(56),
    )(a, gb)


def out_proj(ya, ycp, wo, x, next_gain, name):
    S, A = ya.shape
    Dm, D = wo.shape
    tm = _tile(S, 256)

    def body(ya_ref, ycp_ref, wo_ref, x_ref, gain_ref, o_ref, h_ref, ht_ref):
        xo = x_ref[...] + _dot(ya_ref[...], wo_ref[0:A, :]) + _dot(ycp_ref[...], wo_ref[A:Dm, :])
        o_ref[...] = xo
        _write_norm(xo, gain_ref, h_ref, ht_ref)

    row = pl.BlockSpec((tm, D), lambda i: (i, 0))
    g_in, n_specs, n_shapes = _norm_outs(S, D, tm, next_gain)
    return pl.pallas_call(
        body, name=name, grid=(S // tm,),
        in_specs=[pl.BlockSpec((tm, A), lambda i: (i, 0)), pl.BlockSpec((tm, Dm - A), lambda i: (i, 0)),
                  pl.BlockSpec((Dm, D), lambda i: (0, 0), pipeline_mode=pl.Buffered(1)), row] + g_in,
        out_specs=[row] + n_specs, out_shape=[jax.ShapeDtypeStruct((S, D), F32)] + n_shapes,
        compiler_params=_params(48),
    )(ya, ycp, wo, x, next_gain)


def out_proj_bwd(gb, wo, A, name):
    S, D = gb.shape
    Dm = wo.shape[0]
    tm = _tile(S, 512)

    def body(g_ref, wo_ref, dya_ref, dycp_ref):
        gv = g_ref[...]
        dya_ref[...] = _dot(gv, wo_ref[0:A, :], NT).astype(BF16)
        dycp_ref[...] = _dot(gv, wo_ref[A:Dm, :], NT).astype(BF16)

    return pl.pallas_call(
        body, name=name, grid=(S // tm,),
        in_specs=[pl.BlockSpec((tm, D), lambda i: (i, 0)), pl.BlockSpec((Dm, D), lambda i: (0, 0))],
        out_specs=[pl.BlockSpec((tm, A), lambda i: (i, 0)), pl.BlockSpec((tm, Dm - A), lambda i: (i, 0))],
        out_shape=[jax.ShapeDtypeStruct((S, A), BF16), jax.ShapeDtypeStruct((S, Dm - A), BF16)],
        compiler_params=_params(48),
    )(gb, wo)


ATT_TILE = BLK * max(DILATIONS)


def _head0(shape):
    return lax.broadcasted_iota(jnp.int32, shape, 1) < HEAD_DIM


def _pair_col(v, sel):
    return jnp.max(jnp.where(sel, v, -jnp.inf), axis=-1, keepdims=True)


def _stack_heads(x, h0):
    zero = jnp.zeros_like(x)
    return jnp.concatenate([jnp.where(h0, x, zero), jnp.where(h0, zero, x)], axis=0)


def _band(lo, hi):
    qi = lax.broadcasted_iota(jnp.int32, (BLK, 2 * BLK), 0)
    kj = lax.broadcasted_iota(jnp.int32, (BLK, 2 * BLK), 1)
    return (kj >= qi) & (kj <= qi + BLK) & (kj >= lo) & (kj < hi)


def _deinterleave(dst, src, d, rows, dst_stride, dst_off=0, src_off=0, cast=None):
    for r in range(d):
        v = src[pl.ds(src_off + r, rows, stride=d), :] if d > 1 else src[pl.ds(src_off, rows), :]
        dst[pl.ds(r * dst_stride + dst_off, rows), :] = v if cast is None else v.astype(cast)


def _interleave(dst, src, d, rows, src_stride, src_off=0, add=False):
    for r in range(d):
        v = src[pl.ds(r * src_stride + src_off, rows), :]
        idx = pl.ds(r, rows, stride=d) if d > 1 else pl.ds(0, rows)
        dst[idx, :] = dst[idx, :] + v if add else v


def attn_fwd(z, A, name, comm=None):
    S = z.shape[0]
    T = ATT_TILE
    Hp, nt = A // LANES, S // T
    np_ = len(DILATIONS)

    def body(q_ref, k_ref, kp_ref, v_ref, vp_ref, y_ref, lse_ref, qn, kn, vn, qp, kp, vp, accp, mp, lp, *nat):
        accs, ms, ls = nat[:np_], nat[np_:2 * np_], nat[2 * np_:]
        i = pl.program_id(1)
        qn[...] = q_ref[...].astype(F32)
        kn[0:T, :] = kp_ref[...].astype(F32)
        kn[T:2 * T, :] = k_ref[...].astype(F32)
        vn[0:T, :] = vp_ref[...].astype(F32)
        vn[T:2 * T, :] = v_ref[...].astype(F32)
        h0 = _head0((BLK, LANES))
        for pi, d in enumerate(DILATIONS):
            Ld = T // d
            nblk = Ld // BLK
            _deinterleave(qp, qn, d, Ld, Ld, cast=BF16)
            for src, dst in ((kn, kp), (vn, vp)):
                _deinterleave(dst, src, d, Ld, 2 * Ld, cast=BF16)
                _deinterleave(dst, src, d, Ld, 2 * Ld, dst_off=Ld, src_off=T, cast=BF16)

            def unit(u, carry, nblk=nblk):
                r, n = u // nblk, u % nblk
                q0 = pl.multiple_of(u * BLK, BLK)
                k0 = pl.multiple_of((2 * r * nblk + nblk + n - 1) * BLK, BLK)
                qb = qp[pl.ds(q0, BLK), :]
                kw, vw = kp[pl.ds(k0, 2 * BLK), :], vp[pl.ds(k0, 2 * BLK), :]
                mask = _band(jnp.where((i == 0) & (n == 0), BLK, 0), 2 * BLK)
                s = _dot(_stack_heads(qb, h0), kw, NT) * ATTN_SCALE
                s = jnp.where(jnp.concatenate([mask, mask], axis=0), s, NEG_INF)
                mx = jnp.max(s, axis=-1, keepdims=True)
                p = jnp.exp(s - mx)
                l = jnp.sum(p, axis=-1, keepdims=True)
                o = _dot(p.astype(BF16), vw)
                accp[pl.ds(q0, BLK), :] = jnp.where(h0, o[:BLK], o[BLK:])
                mp[pl.ds(q0, BLK), :] = jnp.where(h0, mx[:BLK], mx[BLK:])
                lp[pl.ds(q0, BLK), :] = jnp.where(h0, l[:BLK], l[BLK:])
                return carry

            lax.fori_loop(0, T // BLK, unit, 0, unroll=True)
            for src, dst in ((accp, accs[pi]), (mp, ms[pi]), (lp, ls[pi])):
                _interleave(dst, src, d, Ld, Ld)
        mv = [m[...] for m in ms]
        mx = mv[0]
        for m in mv[1:]:
            mx = jnp.maximum(mx, m)
        ws = [jnp.exp(m - mx) for m in mv]
        l = sum(w * lr[...] for w, lr in zip(ws, ls))
        a = sum(w * ar[...] for w, ar in zip(ws, accs))
        y_ref[...] = (a / l).astype(BF16)
        lse_ref[...] = mx + jnp.log(l)

    kb, vb = A // LANES, 2 * A // LANES
    cur = lambda off: pl.BlockSpec((T, LANES), lambda h, i: (i, off + h))
    prev = lambda off: pl.BlockSpec((T, LANES), lambda h, i: (jnp.maximum(i - 1, 0), off + h))
    out = pl.BlockSpec((T, LANES), lambda h, i: (i, h))
    vm = lambda rows, dt: pltpu.VMEM((rows, LANES), dt)
    (y, lse), got = _call(
        body, name=name, grid=(Hp, nt),
        in_specs=[cur(0), cur(kb), prev(kb), cur(vb), prev(vb)], out_specs=[out, out],
        out_shape=[jax.ShapeDtypeStruct((S, A), BF16), jax.ShapeDtypeStruct((S, A), F32)],
        scratch=[vm(T, F32), vm(2 * T, F32), vm(2 * T, F32), vm(T, BF16), vm(2 * T, BF16), vm(2 * T, BF16),
                 vm(T, F32), vm(T, F32), vm(T, F32)] + [vm(T, F32)] * (3 * np_),
        args=(z, z, z, z, z), vmem=48, comm=comm)
    return y, lse, got


def attn_bwd(z, dy, y, lse, A, name, comm=None):
    S = z.shape[0]
    T = ATT_TILE
    Hp, nt = A // LANES, S // T

    def body(q_ref, k_ref, kp_ref, v_ref, vp_ref, do_ref, y_ref, ls_ref, dq_ref, dk_ref, dv_ref,
             qn, don, dln, kn, vn, qp, dop, lsp, dlp, kp, vp, dqp, dkp, dvp, dqa, dka, dva, dkc, dvc):
        step = pl.program_id(1)
        i = nt - 1 - step
        h0t = _head0((T, LANES))
        dof = do_ref[...].astype(F32)
        prod = dof * y_ref[...].astype(F32)
        d0 = jnp.sum(jnp.where(h0t, prod, 0.0), axis=-1, keepdims=True)
        d1 = jnp.sum(jnp.where(h0t, 0.0, prod), axis=-1, keepdims=True)
        qn[...] = q_ref[...].astype(F32)
        don[...] = dof
        dln[...] = jnp.where(h0t, d0, d1)
        kn[0:T, :] = kp_ref[...].astype(F32)
        kn[T:2 * T, :] = k_ref[...].astype(F32)
        vn[0:T, :] = vp_ref[...].astype(F32)
        vn[T:2 * T, :] = v_ref[...].astype(F32)

        @pl.when(step == 0)
        def _():
            dka[...] = jnp.zeros_like(dka)
            dva[...] = jnp.zeros_like(dva)

        @pl.when(step > 0)
        def _():
            dka[...] = dkc[...]
            dva[...] = dvc[...]

        h0 = _head0((BLK, LANES))
        for pi, d in enumerate(DILATIONS):
            Ld = T // d
            nblk = Ld // BLK
            for src, dst, cast in ((qn, qp, BF16), (don, dop, BF16), (dln, dlp, None)):
                _deinterleave(dst, src, d, Ld, Ld, cast=cast)
            _deinterleave(lsp, ls_ref, d, Ld, Ld)
            for src, dst in ((kn, kp), (vn, vp)):
                _deinterleave(dst, src, d, Ld, 2 * Ld, cast=BF16)
                _deinterleave(dst, src, d, Ld, 2 * Ld, dst_off=Ld, src_off=T, cast=BF16)
            dkp[...] = jnp.zeros_like(dkp)
            dvp[...] = jnp.zeros_like(dvp)

            def unit(u, carry, nblk=nblk):
                r, b = u // nblk, u % nblk
                q0 = pl.multiple_of(u * BLK, BLK)
                k0 = pl.multiple_of((2 * r * nblk + nblk + b - 1) * BLK, BLK)
                mask = _band(jnp.where((i == 0) & (b == 0), BLK, 0), 2 * BLK)
                qb, dob = qp[pl.ds(q0, BLK), :], dop[pl.ds(q0, BLK), :]
                lsb, dlb = lsp[pl.ds(q0, BLK), :], dlp[pl.ds(q0, BLK), :]
                kw, vw = kp[pl.ds(k0, 2 * BLK), :], vp[pl.ds(k0, 2 * BLK), :]
                qs, dos = _stack_heads(qb, h0), _stack_heads(dob, h0)
                lse = jnp.concatenate([_pair_col(lsb, h0), _pair_col(lsb, ~h0)], axis=0)
                delta = jnp.concatenate([_pair_col(dlb, h0), _pair_col(dlb, ~h0)], axis=0)
                s = _dot(qs, kw, NT) * ATTN_SCALE
                p = jnp.where(jnp.concatenate([mask, mask], axis=0), jnp.exp(s - lse), 0.0)
                ds = (p * (_dot(dos, vw, NT) - delta)).astype(BF16)
                dq = _dot(ds, kw) * ATTN_SCALE
                dqp[pl.ds(q0, BLK), :] = jnp.where(h0, dq[:BLK], dq[BLK:])
                dkp[pl.ds(k0, 2 * BLK), :] += _dot(ds, qs, TN) * ATTN_SCALE
                dvp[pl.ds(k0, 2 * BLK), :] += _dot(p.astype(BF16), dos, TN)
                return carry

            lax.fori_loop(0, T // BLK, unit, 0, unroll=True)
            _interleave(dqa, dqp, d, Ld, Ld, add=pi > 0)
            for acc, nxt, part in ((dka, dkc, dkp), (dva, dvc, dvp)):
                _interleave(acc, part, d, Ld, 2 * Ld, src_off=Ld, add=True)
                _interleave(nxt, part, d, Ld, 2 * Ld, add=pi > 0)
        dq_ref[...] = dqa[...].astype(BF16)
        dk_ref[...] = dka[...].astype(BF16)
        dv_ref[...] = dva[...].astype(BF16)

    kb, vb = A // LANES, 2 * A // LANES
    cur = lambda off: pl.BlockSpec((T, LANES), lambda h, s: (nt - 1 - s, off + h))
    prev = lambda off: pl.BlockSpec((T, LANES), lambda h, s: (jnp.maximum(nt - 2 - s, 0), off + h))
    vm = lambda rows, dt: pltpu.VMEM((rows, LANES), dt)
    shp = jax.ShapeDtypeStruct((S, A), BF16)
    return _call(
        body, name=name, grid=(Hp, nt),
        in_specs=[cur(0), cur(kb), prev(kb), cur(vb), prev(vb), cur(0), cur(0), cur(0)],
        out_specs=[cur(0)] * 3, out_shape=[shp] * 3,
        scratch=[vm(T, F32), vm(T, F32), vm(T, F32), vm(2 * T, F32), vm(2 * T, F32),
                 vm(T, BF16), vm(T, BF16), vm(T, F32), vm(T, F32), vm(2 * T, BF16), vm(2 * T, BF16),
                 vm(T, F32), vm(2 * T, F32), vm(2 * T, F32)] + [vm(T, F32)] * 5,
        args=(z, z, z, z, z, dy, y, lse), vmem=56, comm=comm)


def _shift_down(e, k):
    return pltpu.roll(e, k, 0)


def _shift_up(e, k):
    return pltpu.roll(e, e.shape[0] - k, 0)


def _causal_sums(e, g):
    for lvl in range(g + 1):
        e = e + _shift_down(e, 1 << lvl)
    return e


def _anticausal_sums(e, g):
    for lvl in range(g + 1):
        e = e + _shift_up(e, 1 << lvl)
    return e


def _mixer_specs(S, tm, A, C, P):
    assert (3 * A) % C == 0 and (3 * A + 3 * C) % P == 0 and tm % HALO == 0
    cb, pb = 3 * A // C, (3 * A + 3 * C) // P
    hb = tm // HALO
    last = S // HALO - 1
    col = lambda w, j: pl.BlockSpec((tm, w), lambda i: (i, j))
    prev = lambda w, j: pl.BlockSpec((HALO, w), lambda i: (jnp.maximum(i * hb - 1, 0), j))
    nxt = lambda w, j: pl.BlockSpec((HALO, w), lambda i: (jnp.minimum((i + 1) * hb, last), j))
    return cb, pb, col, prev, nxt


def _conv_taps(gc, ci, gch, cih, keep_prev):
    u = gc * ci
    e = jnp.concatenate([gch * cih * keep_prev, u], axis=0)
    return u, _shift_down(e, 1)[HALO:], _shift_down(e, 2)[HALO:]


def _pooled(xp, xph, keep_prev, pos, g):
    cols = slice(g * POOL_GROUP, (g + 1) * POOL_GROUP)
    x = xp[:, cols]
    e = jnp.concatenate([xph[:, cols] * keep_prev, x], axis=0)
    cnt = jnp.minimum(pos + 1, POOL_WINDOWS[g]).astype(F32)
    return _causal_sums(e, g)[HALO:] / cnt - x


def convpool_fwd(z, conv_w, pool_w, pool_scale, A, C, P, name):
    S = z.shape[0]
    tm = _tile(S, 512)
    cb, pb, col, prev, _ = _mixer_specs(S, tm, A, C, P)

    def body(gb_ref, gc_ref, ci_ref, xp_ref, gch_ref, cih_ref, xph_ref, cw_ref, pw_ref, ps_ref, y_ref):
        i = pl.program_id(0)
        keep_prev = jnp.where(i == 0, 0.0, 1.0)
        f = lambda r: r[...].astype(F32)
        u, u1, u2 = _conv_taps(f(gc_ref), f(ci_ref), f(gch_ref), f(cih_ref), keep_prev)
        cw = cw_ref[...]
        y_ref[:, 0:C] = (f(gb_ref) * (cw[0:1] * u2 + cw[1:2] * u1 + cw[2:3] * u)).astype(BF16)
        xp, xph = f(xp_ref), f(xph_ref)
        pos = i * tm + lax.broadcasted_iota(jnp.int32, (tm, 1), 0)
        for g in range(len(POOL_WINDOWS)):
            cols = slice(g * POOL_GROUP, (g + 1) * POOL_GROUP)
            lin = _dot(_pooled(xp, xph, keep_prev, pos, g).astype(BF16), pw_ref[g])
            y_ref[:, C + g * POOL_GROUP:C + (g + 1) * POOL_GROUP] = (lin * ps_ref[:, cols]).astype(BF16)

    full = lambda shape: pl.BlockSpec(shape, lambda i: (0,) * len(shape))
    return pl.pallas_call(
        body, name=name, grid=(S // tm,),
        in_specs=[col(C, cb), col(C, cb + 1), col(C, cb + 2), col(P, pb),
                  prev(C, cb + 1), prev(C, cb + 2), prev(P, pb),
                  full(conv_w.shape), full(pool_w.shape), full(pool_scale.shape)],
        out_specs=pl.BlockSpec((tm, C + P), lambda i: (i, 0)),
        out_shape=jax.ShapeDtypeStruct((S, C + P), BF16), compiler_params=_params(48),
    )(z, z, z, z, z, z, z, conv_w, pool_w, pool_scale)


def convpool_bwd(z, dycp, conv_w, pool_w, pool_scale, A, C, P, name):
    S = z.shape[0]
    tm = _tile(S, 512)
    nt = S // tm
    cb, pb, col, prev, nxt = _mixer_specs(S, tm, A, C, P)
    NG = len(POOL_WINDOWS)

    def body(gb_ref, gc_ref, ci_ref, xp_ref, gch_ref, cih_ref, xph_ref, gbn_ref, dy_ref, dyn_ref,
             cw_ref, pw_ref, ps_ref, dz_ref, dcw_ref, dpw_ref, dps_ref):
        i = pl.program_id(0)

        @pl.when(i == 0)
        def _():
            dcw_ref[...] = jnp.zeros_like(dcw_ref)
            dpw_ref[...] = jnp.zeros_like(dpw_ref)
            dps_ref[...] = jnp.zeros_like(dps_ref)

        keep_prev = jnp.where(i == 0, 0.0, 1.0)
        keep_next = jnp.where(i == nt - 1, 0.0, 1.0)
        f = lambda r: r[...].astype(F32)
        gb, gc, ci = f(gb_ref), f(gc_ref), f(ci_ref)
        u, u1, u2 = _conv_taps(gc, ci, f(gch_ref), f(cih_ref), keep_prev)
        cw = cw_ref[...]
        dy, dyn = f(dy_ref), f(dyn_ref) * keep_next
        dyc = dy[:, 0:C]
        dz_ref[:, 0:C] = (dyc * (cw[0:1] * u2 + cw[1:2] * u1 + cw[2:3] * u)).astype(BF16)
        dc = dyc * gb
        e = jnp.concatenate([dc, dyn[:, 0:C] * f(gbn_ref)], axis=0)
        du = cw[2:3] * dc + cw[1:2] * _shift_up(e, 1)[:tm] + cw[0:1] * _shift_up(e, 2)[:tm]
        dz_ref[:, C:2 * C] = (du * ci).astype(BF16)
        dz_ref[:, 2 * C:3 * C] = (du * gc).astype(BF16)
        dcw_ref[0:1, :] += jnp.sum(dc * u2, axis=0, keepdims=True)
        dcw_ref[1:2, :] += jnp.sum(dc * u1, axis=0, keepdims=True)
        dcw_ref[2:3, :] += jnp.sum(dc * u, axis=0, keepdims=True)

        xp, xph = f(xp_ref), f(xph_ref)
        pos = i * tm + lax.broadcasted_iota(jnp.int32, (tm, 1), 0)
        pos_e = i * tm + lax.broadcasted_iota(jnp.int32, (tm + HALO, 1), 0)
        for g in range(NG):
            cols = slice(g * POOL_GROUP, (g + 1) * POOL_GROUP)
            ycols = slice(C + g * POOL_GROUP, C + (g + 1) * POOL_GROUP)
            pooled = _pooled(xp, xph, keep_prev, pos, g).astype(BF16)
            pw = pw_ref[g]
            dyp = dy[:, ycols]
            dps_ref[:, cols] += jnp.sum(dyp * _dot(pooled, pw), axis=0, keepdims=True)
            dlin = (jnp.concatenate([dyp, dyn[:, ycols]], axis=0) * ps_ref[:, cols]).astype(BF16)
            dpw_ref[g] += _dot(pooled, dlin[:tm], TN)
            dpool = _dot(dlin, pw, NT)
            r = dpool / jnp.minimum(pos_e + 1, POOL_WINDOWS[g]).astype(F32)
            dz_ref[:, 3 * C + g * POOL_GROUP:3 * C + (g + 1) * POOL_GROUP] = (
                _anticausal_sums(r, g)[:tm] - dpool[:tm]).astype(BF16)

    full = lambda shape: pl.BlockSpec(shape, lambda i: (0,) * len(shape))
    return pl.pallas_call(
        body, name=name, grid=(nt,),
        in_specs=[col(C, cb), col(C, cb + 1), col(C, cb + 2), col(P, pb),
                  prev(C, cb + 1), prev(C, cb + 2), prev(P, pb), nxt(C, cb),
                  pl.BlockSpec((tm, C + P), lambda i: (i, 0)),
                  pl.BlockSpec((HALO, C + P), lambda i: (jnp.minimum((i + 1) * (tm // HALO), S // HALO - 1), 0)),
                  full(conv_w.shape), full(pool_w.shape), full(pool_scale.shape)],
        out_specs=[pl.BlockSpec((tm, 3 * C + P), lambda i: (i, 0)),
                   full(conv_w.shape), full(pool_w.shape), full(pool_scale.shape)],
        out_shape=[jax.ShapeDtypeStruct((S, 3 * C + P), BF16), jax.ShapeDtypeStruct(conv_w.shape, F32),
                   jax.ShapeDtypeStruct(pool_w.shape, F32), jax.ShapeDtypeStruct(pool_scale.shape, F32)],
        compiler_params=_params(48),
    )(z, z, z, z, z, z, z, z, dycp, dycp, conv_w, pool_w, pool_scale)


def adamw(lands, w, m, v, name):
    R, C = w.shape
    nl = len(lands)
    n, Rl, _ = lands[0].shape
    assert Rl * nl == R
    tr = _tile(Rl, 128 * 1024 // C)
    nb = Rl // tr
    c1 = 1.0 - ADAM_B1 ** ADAM_STEP
    c2 = 1.0 - ADAM_B2 ** ADAM_STEP

    def body(*refs):
        land_refs = refs[:nl]
        w_ref, m_ref, v_ref, g_ref, d_ref, mo_ref, vo_ref = refs[nl:]
        i = pl.program_id(0)
        for a, land_ref in enumerate(land_refs):
            @pl.when((i >= a * nb) & (i < (a + 1) * nb))
            def _():
                g = land_ref[0].astype(F32)
                for s in range(1, n):
                    g = g + land_ref[s].astype(F32)
                g_ref[...] = g
                mn = ADAM_B1 * m_ref[...] + (1.0 - ADAM_B1) * g
                vn = ADAM_B2 * v_ref[...] + (1.0 - ADAM_B2) * (g * g)
                mo_ref[...] = mn
                vo_ref[...] = vn
                d_ref[...] = -ADAM_LR * ((mn / c1) / (jnp.sqrt(vn / c2) + ADAM_EPS) + ADAM_WD * w_ref[...])

    land_specs = [pl.BlockSpec((n, tr, C), lambda i, a=a: (0, jnp.clip(i - a * nb, 0, nb - 1), 0))
                  for a in range(nl)]
    row = pl.BlockSpec((tr, C), lambda i: (i, 0))
    shp = jax.ShapeDtypeStruct((R, C), F32)
    return pl.pallas_call(
        body, name=name, grid=(nl * nb,),
        in_specs=land_specs + [row, row, row], out_specs=[row] * 4, out_shape=[shp] * 4,
        compiler_params=_params(48),
    )(*lands, w, m, v)


def _gather_comm(arrs):
    return ("gather", arrs) if arrs else None


def _ffn_bwd(g, gb, saved, gain, wg, wu, wd, tag):
    x, ht, a, b, hid = saved
    da, db = ffn_dhid(gb, wd, a, b, f"ffn_dhid_{tag}")
    dwd = grad_shared_rhs(hid, gb, None, FFN_RESIDUAL, f"ffn_dwd_{tag}")
    dwg, (land_wd,) = grad_lhs(ht, da, True, f"ffn_dwg_{tag}", ("exchange", [dwd]))
    dwu, (land_wg,) = grad_lhs(ht, db, True, f"ffn_dwu_{tag}", ("exchange", [dwg]))
    dh, (land_wu,) = back_proj([da, db], [wg, wu], True, f"ffn_dh_{tag}", ("exchange", [dwu]))
    g, gb, dgain = rms_bwd(x, gain, dh, g, f"rms_bwd_{tag}")
    return g, gb, dgain, (land_wg, land_wu, land_wd)


def _mixer_fwd(x, h, ht, next_gain, w_in, conv_w, pool_w, pool_scale, w_out, dims, tag, carry):
    A, C, P = dims
    z, got_a = proj_cols(h, w_in, f"in_proj_{tag}", _gather_comm(carry[:1]))
    ya, lse, got_b = attn_fwd(z, A, f"attn_fwd_{tag}", _gather_comm(carry[1:]))
    ycp = convpool_fwd(z, conv_w, pool_w, pool_scale, A, C, P, f"convpool_{tag}")
    xo, hn, htn = out_proj(ya, ycp, w_out, x, next_gain, f"out_proj_{tag}")
    return xo, hn, htn, (x, ht, z, lse, ya, ycp), got_a + got_b


def _mixer_bwd(g, gb, saved, gain, w_in, conv_w, pool_w, pool_scale, w_out, dims, tag):
    A, C, P = dims
    x, ht, z, lse, ya, ycp = saved
    S, D = x.shape
    dya, dycp = out_proj_bwd(gb, w_out, A, f"out_proj_bwd_{tag}")
    dwo_a = grad_shared_rhs(ya, gb, 1, 1.0, f"dwo_attn_{tag}")
    dwo_cp = grad_shared_rhs(ycp, gb, 1, 1.0, f"dwo_cp_{tag}")
    dwo = jnp.concatenate([dwo_a[0], dwo_cp[0]], axis=0).reshape(NDEV, -1, D)
    dqkv, (land_wo,) = attn_bwd(z, dya, ya, lse, A, f"attn_bwd_{tag}", ("exchange", [dwo]))
    dz_cp, dcw, dpw, dps = convpool_bwd(z, dycp, conv_w, pool_w, pool_scale, A, C, P, f"convpool_bwd_{tag}")
    dz = jnp.concatenate(dqkv + [dz_cp], axis=1)
    dwin, _ = grad_lhs(ht, dz, False, f"dwin_{tag}")
    dh, (land_win,) = back_proj([dz], [w_in], False, f"in_proj_bwd_{tag}", ("exchange", [dwin]))
    g, gb, dgain = rms_bwd(x, gain, dh, g, f"rms_bwd_{tag}")
    return g, gb, (dgain, dcw, dpw, dps), (land_win, land_wo)


def _pack(arrs):
    flat = [a.reshape(-1).astype(F32) for a in arrs]
    spans, off = [], 0
    for a in flat:
        spans.append((off, a.shape[0]))
        off += a.shape[0]
    rows = -(-off // (8 * LANES)) * 8
    buf = jnp.concatenate(flat + [jnp.zeros((rows * LANES - off,), F32)]).reshape(rows, LANES)
    return buf, spans


def kernel(x, ffn1_norm, ffn1_w_gate, ffn1_w_up, ffn1_w_down, mix_norm, w_in, conv_w, pool_w, pool_scale, w_out, ffn2_norm, ffn2_w_gate, ffn2_w_up, ffn2_w_down, final_norm, loss_target, m_ffn1_norm, m_ffn1_w_gate, m_ffn1_w_up, m_ffn1_w_down, m_mix_norm, m_w_in, m_conv_w, m_pool_w, m_pool_scale, m_w_out, m_ffn2_norm, m_ffn2_w_gate, m_ffn2_w_up, m_ffn2_w_down, m_final_norm, v_ffn1_norm, v_ffn1_w_gate, v_ffn1_w_up, v_ffn1_w_down, v_mix_norm, v_w_in, v_conv_w, v_pool_w, v_pool_scale, v_w_out, v_ffn2_norm, v_ffn2_w_gate, v_ffn2_w_up, v_ffn2_w_down, v_final_norm):
    depth = ffn1_norm.shape[0]
    S, D = x.shape[1], x.shape[2]
    Cb = conv_w.shape[2]
    C = Cb * NDEV
    P = pool_scale.shape[1]
    A = (w_in.shape[2] * NDEV - 3 * C - P) // 3
    dims = (A, C, P)
    me = 4 * lax.axis_index("x") + 2 * lax.axis_index("y") + lax.axis_index("c")

    big = dict(ffn1_w_gate=ffn1_w_gate, ffn1_w_up=ffn1_w_up, ffn1_w_down=ffn1_w_down, w_in=w_in, w_out=w_out,
               ffn2_w_gate=ffn2_w_gate, ffn2_w_up=ffn2_w_up, ffn2_w_down=ffn2_w_down)
    big_m = dict(ffn1_w_gate=m_ffn1_w_gate, ffn1_w_up=m_ffn1_w_up, ffn1_w_down=m_ffn1_w_down, w_in=m_w_in,
                 w_out=m_w_out, ffn2_w_gate=m_ffn2_w_gate, ffn2_w_up=m_ffn2_w_up, ffn2_w_down=m_ffn2_w_down)
    big_v = dict(ffn1_w_gate=v_ffn1_w_gate, ffn1_w_up=v_ffn1_w_up, ffn1_w_down=v_ffn1_w_down, w_in=v_w_in,
                 w_out=v_w_out, ffn2_w_gate=v_ffn2_w_gate, ffn2_w_up=v_ffn2_w_up, ffn2_w_down=v_ffn2_w_down)
    names = list(big)

    first_names = ["ffn1_w_gate", "ffn1_w_up"]
    ffn1_names = first_names + ["ffn1_w_down"]
    mix_names = ["w_in", "w_out"]
    ffn2_names = ["ffn2_w_gate", "ffn2_w_up", "ffn2_w_down"]
    shards = lambda l, ns: [big[n][l].astype(BF16) for n in ns]
    full = [dict() for _ in range(depth)]
    full[0].update(zip(first_names, all_gather(shards(0, first_names), "gather_first")))
    conv_full = all_gather([jnp.pad(conv_w.reshape(-1, Cb), ((0, 2), (0, LANES - Cb)))], "gather_conv_w")[0]
    conv_full = jnp.transpose(conv_full[:, :depth * 3, :Cb].reshape(NDEV, depth, 3, Cb), (1, 2, 0, 3)).reshape(depth, 3, C)
    pool_w_bf = pool_w.astype(BF16)

    xs = x[0]
    saved = []
    h, ht = rms_fwd(xs, ffn1_norm[0:1], "rms_f1_l0")
    for l in range(depth):
        W = full[l]
        last = l + 1 == depth
        up1 = ["ffn1_w_down"] + mix_names if l == 0 else []
        (a, b, hid), got = ffn_up(h, W["ffn1_w_gate"], W["ffn1_w_up"], f"ffn_up_f1_l{l}", _gather_comm(shards(l, up1)))
        W.update(zip(up1, got))
        s1 = (xs, ht, a, b, hid)
        (xs, h, ht), _ = ffn_down(hid, W["ffn1_w_down"], xs, f"ffn_down_f1_l{l}", next_gain=mix_norm[l:l + 1])
        xs, h, ht, s2, got = _mixer_fwd(xs, h, ht, ffn2_norm[l:l + 1], W["w_in"], conv_full[l], pool_w_bf[l],
                                        pool_scale[l:l + 1], W["w_out"].reshape(-1, D), dims, f"mix_l{l}",
                                        shards(l, ffn2_names))
        W.update(zip(ffn2_names, got))
        up2, down2 = ([], []) if last else (ffn1_names, mix_names)
        (a, b, hid), got_up = ffn_up(h, W["ffn2_w_gate"], W["ffn2_w_up"], f"ffn_up_f2_l{l}",
                                     _gather_comm(shards(l + 1, up2) if up2 else []))
        s3 = (xs, ht, a, b, hid)
        outs, got_down = ffn_down(hid, W["ffn2_w_down"], xs, f"ffn_down_f2_l{l}",
                                  _gather_comm(shards(l + 1, down2) if down2 else []),
                                  next_gain=None if last else ffn1_norm[l + 1:l + 2])
        if last:
            xs, = outs
        else:
            xs, h, ht = outs
            full[l + 1].update(zip(up2, got_up))
            full[l + 1].update(zip(down2, got_down))
        saved.append((s1, s2, s3))
    loss_part, g, gb, d_final = final_loss(xs, final_norm.reshape(1, D), loss_target[0], "final_loss")

    small = [None] * depth
    lands = [None] * depth
    for l in reversed(range(depth)):
        W = full[l]
        s1, s2, s3 = saved[l]
        g, gb, dn2, (lg2, lu2, ld2) = _ffn_bwd(g, gb, s3, ffn2_norm[l:l + 1], W["ffn2_w_gate"], W["ffn2_w_up"],
                                               W["ffn2_w_down"], f"f2_l{l}")
        g, gb, (dnm, dcw, dpw, dps), (lwin, lwo) = _mixer_bwd(
            g, gb, s2, mix_norm[l:l + 1], W["w_in"], conv_full[l], pool_w_bf[l], pool_scale[l:l + 1],
            W["w_out"].reshape(-1, D), dims, f"mix_l{l}")
        g, gb, dn1, (lg1, lu1, ld1) = _ffn_bwd(g, gb, s1, ffn1_norm[l:l + 1], W["ffn1_w_gate"], W["ffn1_w_up"],
                                               W["ffn1_w_down"], f"f1_l{l}")
        lands[l] = dict(ffn1_w_gate=lg1, ffn1_w_up=lu1, ffn1_w_down=ld1, w_in=lwin, w_out=lwo,
                        ffn2_w_gate=lg2, ffn2_w_up=lu2, ffn2_w_down=ld2)
        small[l] = (dn1, dnm, dcw, dpw, dps, dn2)

    res = {}
    for n in names:
        shp = big[n].shape
        two = lambda t: t.reshape(-1, shp[-1])
        outs = adamw([lands[l][n] for l in range(depth)], two(big[n]), two(big_m[n]), two(big_v[n]), f"adamw_{n}")
        res[n] = [o.reshape(shp) for o in outs]

    st = lambda i: jnp.stack([small[l][i] for l in range(depth)])
    small_g = dict(ffn1_norm=st(0), mix_norm=st(1), conv_w=st(2), pool_w=st(3), pool_scale=st(4), ffn2_norm=st(5),
                   final_norm=d_final)
    small_names = list(small_g)
    zeros_conv = jnp.zeros((depth, 3, C), F32)
    place = lambda t: lax.dynamic_update_slice(zeros_conv, t, (0, 0, me * Cb))
    small_w = dict(ffn1_norm=ffn1_norm, mix_norm=mix_norm, conv_w=place(conv_w), pool_w=pool_w, pool_scale=pool_scale,
                   ffn2_norm=ffn2_norm, final_norm=final_norm)
    small_m = dict(ffn1_norm=m_ffn1_norm, mix_norm=m_mix_norm, conv_w=place(m_conv_w), pool_w=m_pool_w,
                   pool_scale=m_pool_scale, ffn2_norm=m_ffn2_norm, final_norm=m_final_norm)
    small_v = dict(ffn1_norm=v_ffn1_norm, mix_norm=v_mix_norm, conv_w=place(v_conv_w), pool_w=v_pool_w,
                   pool_scale=v_pool_scale, ffn2_norm=v_ffn2_norm, final_norm=v_final_norm)
    gbuf, spans = _pack([small_g[n] for n in small_names] + [loss_part])
    wbuf, _ = _pack([small_w[n] for n in small_names] + [jnp.zeros((1, LANES), F32)])
    mbuf, _ = _pack([small_m[n] for n in small_names] + [jnp.zeros((1, LANES), F32)])
    vbuf, _ = _pack([small_v[n] for n in small_names] + [jnp.zeros((1, LANES), F32)])
    gathered = all_gather([gbuf], "gather_small_grads")[0]
    outs = adamw([gathered], wbuf, mbuf, vbuf, "adamw_small")
    for n, (off, size) in zip(small_names, spans):
        shp = small_w[n].shape
        vals = [o.reshape(-1)[off:off + size].reshape(shp) for o in outs]
        if n == "conv_w":
            vals = [lax.dynamic_slice(t, (0, 0, me * Cb), (depth, 3, Cb)) for t in vals]
        res[n] = vals
    loss = outs[0].reshape(-1)[spans[-1][0]]

    order = ["ffn1_norm", "ffn1_w_gate", "ffn1_w_up", "ffn1_w_down", "mix_norm", "w_in", "conv_w", "pool_w",
             "pool_scale", "w_out", "ffn2_norm", "ffn2_w_gate", "ffn2_w_up", "ffn2_w_down", "final_norm"]
    return (loss, g[None], *[res[n][0] for n in order], *[res[n][1] for n in order],
            *[res[n][2] for n in order], *[res[n][3] for n in order])
```

```python
import functools

import jax
import jax.numpy as jnp
from jax import lax
from jax.experimental import pallas as pl
from jax.experimental.pallas import tpu as pltpu

F32 = jnp.float32
BF16 = jnp.bfloat16
NDEV = 8
HEAD_DIM = 64
LANES = 128
BLK = 128
DILATIONS = (1, 4, 16)
POOL_WINDOWS = (2, 4, 8, 16)
POOL_GROUP = 128
HALO = 16
FFN_RESIDUAL = 0.5
RMS_EPS = 1e-6
NEG_INF = -1e30
ATTN_SCALE = HEAD_DIM ** -0.5
ADAM_LR, ADAM_B1, ADAM_B2, ADAM_EPS, ADAM_WD, ADAM_STEP = 0.001, 0.9, 0.999, 1e-08, 0.01, 10
VMEM_BYTES = 64 * 1024 * 1024
MESH = pl.DeviceIdType.MESH

NN = (((1,), (0,)), ((), ()))
NT = (((1,), (1,)), ((), ()))
TN = (((0,), (0,)), ((), ()))


def _dot(a, b, dims=NN):
    return lax.dot_general(a, b, dims, preferred_element_type=F32)


def _params(vmem_mb=48):
    return pltpu.CompilerParams(vmem_limit_bytes=min(vmem_mb * 1024 * 1024, VMEM_BYTES - 4 * 1024 * 1024))


def _tile(n, want):
    t = min(n, max(16, want // 16 * 16))
    while t > 16 and (n % t or t % 16):
        t -= 16
    return t if n % t == 0 else n


def _coords():
    return lax.axis_index("x"), lax.axis_index("y"), lax.axis_index("c")


def _comm_sems(n):
    return [pltpu.SemaphoreType.DMA((7 * n,)), pltpu.SemaphoreType.DMA((7 * n,)), pltpu.SemaphoreType.DMA((n,))]


def _gather_ops(ins, outs, sems):
    n = len(ins)
    send_sems, recv_sems, local_sems = sems
    x, y, c = _coords()
    me, sibling = (x, y, c), (x, y, 1 - c)
    chips = [(1 - x, y), (x, 1 - y), (1 - x, 1 - y)]

    def slot(out, p):
        return out.at[4 * p[0] + 2 * p[1] + p[2]]

    def copy(a, k, block, to, src=None):
        return pltpu.make_async_remote_copy(
            src_ref=slot(outs[a], block) if src is None else src, dst_ref=slot(outs[a], block),
            send_sem=send_sems.at[7 * a + k], recv_sem=recv_sems.at[7 * a + k],
            device_id=to, device_id_type=MESH)

    def own():
        mine = [pltpu.make_async_copy(ins[a], slot(outs[a], me), local_sems.at[a]) for a in range(n)]
        first = []
        for a in range(n):
            first.append(copy(a, 0, me, sibling, src=ins[a]))
            first += [copy(a, 1 + j, me, (*chip, c), src=ins[a]) for j, chip in enumerate(chips)]
        return mine, first

    def start():
        mine, first = own()
        for cp in mine + first:
            cp.start()

    def finish():
        mine, first = own()
        passed = []
        for j, chip in enumerate(chips):
            for a in range(n):
                copy(a, 1 + j, (*chip, c), me).wait_recv()
                fwd = copy(a, 4 + j, (*chip, c), sibling)
                fwd.start()
                passed.append(fwd)
        for a in range(n):
            copy(a, 0, sibling, me).wait_recv()
            for j, chip in enumerate(chips):
                copy(a, 4 + j, (*chip, 1 - c), me).wait_recv()
        for cp in first + passed:
            cp.wait_send()
        for cp in mine:
            cp.wait()

    return start, finish


def _exchange_ops(ins, outs, sems):
    n = len(ins)
    send_sems, recv_sems, local_sems = sems
    x, y, c = _coords()
    me = 4 * x + 2 * y + c

    def peer(k):
        return (1 - x if k & 4 else x, 1 - y if k & 2 else y, 1 - c if k & 1 else c)

    def copy(a, k):
        p = peer(k)
        return pltpu.make_async_remote_copy(
            src_ref=ins[a].at[4 * p[0] + 2 * p[1] + p[2]], dst_ref=outs[a].at[me],
            send_sem=send_sems.at[7 * a + k - 1], recv_sem=recv_sems.at[7 * a + k - 1],
            device_id=p, device_id_type=MESH)

    def landed(a, k):
        p = peer(k)
        return pltpu.make_async_remote_copy(
            src_ref=ins[a].at[me], dst_ref=outs[a].at[4 * p[0] + 2 * p[1] + p[2]],
            send_sem=send_sems.at[7 * a + k - 1], recv_sem=recv_sems.at[7 * a + k - 1],
            device_id=p, device_id_type=MESH)

    def own():
        mine = [pltpu.make_async_copy(ins[a].at[me], outs[a].at[me], local_sems.at[a]) for a in range(n)]
        return mine, [copy(a, k) for a in range(n) for k in range(1, NDEV)]

    def start():
        mine, sent = own()
        for cp in mine + sent:
            cp.start()

    def finish():
        mine, sent = own()
        for a in range(n):
            for k in range(1, NDEV):
                landed(a, k).wait_recv()
        for cp in sent:
            cp.wait_send()
        for cp in mine:
            cp.wait()

    return start, finish


_COMM = {"gather": (_gather_ops, lambda a: (NDEV,) + a.shape), "exchange": (_exchange_ops, lambda a: a.shape)}


def all_gather(arrs, name):
    n = len(arrs)

    def body(*refs):
        start, finish = _gather_ops(refs[:n], refs[n:2 * n], refs[2 * n:])
        start()
        finish()

    any_spec = pl.BlockSpec(memory_space=pl.ANY)
    return pl.pallas_call(
        body, name=name, out_shape=[jax.ShapeDtypeStruct((NDEV,) + a.shape, a.dtype) for a in arrs],
        in_specs=[any_spec] * n, out_specs=[any_spec] * n, scratch_shapes=_comm_sems(n),
    )(*arrs)


def _call(body, *, name, grid, in_specs, out_specs, out_shape, args, scratch=(), vmem=48, comm=None):
    if comm is None:
        outs = pl.pallas_call(
            body, name=name, grid=grid, in_specs=list(in_specs), out_specs=list(out_specs), out_shape=list(out_shape),
            scratch_shapes=list(scratch), compiler_params=_params(vmem))(*args)
        return list(outs), []
    kind, arrs = comm
    ops, shape_of = _COMM[kind]
    n, n_in, n_out, n_scr = len(arrs), len(in_specs), len(out_specs), len(scratch)

    def carrier(*refs):
        ins, c_in = refs[:n_in], refs[n_in:n_in + n]
        o0 = n_in + n
        outs, c_out = refs[o0:o0 + n_out], refs[o0 + n_out:o0 + n_out + n]
        s0 = o0 + n_out + n
        scr, sems = refs[s0:s0 + n_scr], refs[s0 + n_scr:]
        ids = [pl.program_id(d) for d in range(len(grid))]
        first = functools.reduce(jnp.logical_and, [i == 0 for i in ids])
        last = functools.reduce(jnp.logical_and, [i == g - 1 for i, g in zip(ids, grid)])
        start, finish = ops(c_in, c_out, sems)
        pl.when(first)(start)
        body(*ins, *outs, *scr)
        pl.when(last)(finish)

    any_spec = pl.BlockSpec(memory_space=pl.ANY)
    res = pl.pallas_call(
        carrier, name=name, grid=grid, in_specs=list(in_specs) + [any_spec] * n,
        out_specs=list(out_specs) + [any_spec] * n,
        out_shape=list(out_shape) + [jax.ShapeDtypeStruct(shape_of(a), a.dtype) for a in arrs],
        scratch_shapes=list(scratch) + _comm_sems(n), compiler_params=_params(vmem))(*args, *arrs)
    return list(res[:n_out]), list(res[n_out:])


def _rstd(x):
    return lax.rsqrt(jnp.mean(x * x, axis=-1, keepdims=True) + RMS_EPS)


def rms_fwd(x, gain, name):
    S, D = x.shape
    tm = _tile(S, 512)

    def body(x_ref, g_ref, h_ref, ht_ref):
        xv = x_ref[...]
        h = xv * _rstd(xv) * g_ref[...]
        h_ref[...] = h.astype(BF16)
        ht_ref[...] = h.T.astype(BF16)

    return pl.pallas_call(
        body, name=name, grid=(S // tm,),
        in_specs=[pl.BlockSpec((tm, D), lambda i: (i, 0)), pl.BlockSpec((1, D), lambda i: (0, 0))],
        out_specs=[pl.BlockSpec((tm, D), lambda i: (i, 0)), pl.BlockSpec((D, tm), lambda i: (0, i))],
        out_shape=[jax.ShapeDtypeStruct((S, D), BF16), jax.ShapeDtypeStruct((D, S), BF16)],
        compiler_params=_params(40),
    )(x, gain)


def _rms_bwd_math(xv, gain, dh):
    r = _rstd(xv)
    xhat = xv * r
    dxhat = dh * gain
    dx = r * (dxhat - xhat * jnp.mean(dxhat * xhat, axis=-1, keepdims=True))
    return dx, dh * xhat


def rms_bwd(x, gain, dh, g, name):
    S, D = x.shape
    tm = _tile(S, 256)

    def body(x_ref, gain_ref, dh_ref, g_ref, go_ref, gb_ref, dg_ref):
        @pl.when(pl.program_id(0) == 0)
        def _():
            dg_ref[...] = jnp.zeros_like(dg_ref)

        dx, dgain = _rms_bwd_math(x_ref[...], gain_ref[...], dh_ref[...].astype(F32))
        gn = g_ref[...] + dx
        go_ref[...] = gn
        gb_ref[...] = gn.astype(BF16)
        dg_ref[...] += jnp.sum(dgain, axis=0, keepdims=True)

    row = pl.BlockSpec((tm, D), lambda i: (i, 0))
    vec = pl.BlockSpec((1, D), lambda i: (0, 0))
    return pl.pallas_call(
        body, name=name, grid=(S // tm,),
        in_specs=[row, vec, row, row], out_specs=[row, row, vec],
        out_shape=[jax.ShapeDtypeStruct((S, D), F32), jax.ShapeDtypeStruct((S, D), BF16),
                   jax.ShapeDtypeStruct((1, D), F32)],
        compiler_params=_params(48),
    )(x, gain, dh, g)


def final_loss(x, gain, target, name):
    S, D = x.shape
    tm = _tile(S, 256)

    def body(x_ref, gain_ref, t_ref, loss_ref, go_ref, gb_ref, dg_ref):
        @pl.when(pl.program_id(0) == 0)
        def _():
            dg_ref[...] = jnp.zeros_like(dg_ref)
            loss_ref[...] = jnp.zeros_like(loss_ref)

        xv, gain_v = x_ref[...], gain_ref[...]
        err = xv * _rstd(xv) * gain_v - t_ref[...]
        loss_ref[...] += jnp.sum(jnp.sum(err * err, axis=-1, keepdims=True), axis=0, keepdims=True) * (0.5 / D)
        dx, dgain = _rms_bwd_math(xv, gain_v, err * (1.0 / D))
        go_ref[...] = dx
        gb_ref[...] = dx.astype(BF16)
        dg_ref[...] += jnp.sum(dgain, axis=0, keepdims=True)

    row = pl.BlockSpec((tm, D), lambda i: (i, 0))
    vec = pl.BlockSpec((1, D), lambda i: (0, 0))
    return pl.pallas_call(
        body, name=name, grid=(S // tm,),
        in_specs=[row, vec, row],
        out_specs=[pl.BlockSpec((1, LANES), lambda i: (0, 0)), row, row, vec],
        out_shape=[jax.ShapeDtypeStruct((1, LANES), F32), jax.ShapeDtypeStruct((S, D), F32),
                   jax.ShapeDtypeStruct((S, D), BF16), jax.ShapeDtypeStruct((1, D), F32)],
        compiler_params=_params(48),
    )(x, gain, target)


def _blk(arr, width, tm, stacked):
    if stacked:
        return pl.BlockSpec((None, tm, width), lambda j, i: (j, i, 0))
    return pl.BlockSpec((tm, width), lambda j, i: (i, j))


def ffn_up(h, wg, wu, name, comm=None):
    S, D = h.shape
    J, _, Fb = wg.shape
    tm = _tile(S, 1024)

    def body(h_ref, wg_ref, wu_ref, da_ref, db_ref, hid_ref):
        hv = h_ref[...]
        a = _dot(hv, wg_ref[...])
        b = _dot(hv, wu_ref[...])
        sg = jax.nn.sigmoid(a)
        silu = a * sg
        da_ref[...] = (b * (sg * (1.0 + a * (1.0 - sg)))).astype(BF16)
        db_ref[...] = silu.astype(BF16)
        hid_ref[...] = (silu * b).astype(BF16)

    w_spec = pl.BlockSpec((None, D, Fb), lambda j, i: (j, 0, 0))
    o_spec = pl.BlockSpec((None, tm, Fb), lambda j, i: (j, i, 0))
    shp = jax.ShapeDtypeStruct((J, S, Fb), BF16)
    return _call(
        body, name=name, grid=(J, S // tm),
        in_specs=[pl.BlockSpec((tm, D), lambda j, i: (i, 0)), w_spec, w_spec],
        out_specs=[o_spec, o_spec, o_spec], out_shape=[shp, shp, shp], args=(h, wg, wu), comm=comm)


def proj_cols(h, w, name, comm=None):
    S, D = h.shape
    J, _, Wb = w.shape
    tm = _tile(S, 1024)
    kb = 2 if Wb % LANES == 0 and J % 2 == 0 else 1

    def body(h_ref, w_ref, z_ref):
        wv = jnp.concatenate([w_ref[k] for k in range(kb)], axis=1)
        z_ref[...] = _dot(h_ref[...], wv).astype(BF16)

    (z,), got = _call(
        body, name=name, grid=(J // kb, S // tm),
        in_specs=[pl.BlockSpec((tm, D), lambda j, i: (i, 0)), pl.BlockSpec((kb, D, Wb), lambda j, i: (j, 0, 0))],
        out_specs=[pl.BlockSpec((tm, kb * Wb), lambda j, i: (i, j))],
        out_shape=[jax.ShapeDtypeStruct((S, J * Wb), BF16)], args=(h, w), comm=comm)
    return z, got


def _norm_outs(S, D, tm, gain):
    if gain is None:
        return [], [], []
    return ([pl.BlockSpec((1, D), lambda i: (0, 0))],
            [pl.BlockSpec((tm, D), lambda i: (i, 0)), pl.BlockSpec((D, tm), lambda i: (0, i))],
            [jax.ShapeDtypeStruct((S, D), BF16), jax.ShapeDtypeStruct((D, S), BF16)])


def _write_norm(xo, gain_ref, h_ref, ht_ref):
    h = xo * _rstd(xo) * gain_ref[...]
    h_ref[...] = h.astype(BF16)
    ht_ref[...] = h.T.astype(BF16)


def ffn_down(hid, wd, x, name, comm=None, next_gain=None):
    J, S, Fb = hid.shape
    D = wd.shape[2]
    tm = _tile(S, 256)

    def body(hid_ref, wd_ref, x_ref, *rest):
        acc = _dot(hid_ref[0], wd_ref[0])
        for j in range(1, J):
            acc = acc + _dot(hid_ref[j], wd_ref[j])
        xo = x_ref[...] + FFN_RESIDUAL * acc
        if next_gain is None:
            rest[0][...] = xo
        else:
            gain_ref, o_ref, h_ref, ht_ref = rest
            o_ref[...] = xo
            _write_norm(xo, gain_ref, h_ref, ht_ref)

    row = pl.BlockSpec((tm, D), lambda i: (i, 0))
    g_in, n_specs, n_shapes = _norm_outs(S, D, tm, next_gain)
    outs, got = _call(
        body, name=name, grid=(S // tm,),
        in_specs=[pl.BlockSpec((J, tm, Fb), lambda i: (0, i, 0)),
                  pl.BlockSpec((J, Fb, D), lambda i: (0, 0, 0), pipeline_mode=pl.Buffered(1)), row] + g_in,
        out_specs=[row] + n_specs, out_shape=[jax.ShapeDtypeStruct((S, D), F32)] + n_shapes,
        args=(hid, wd, x) + (() if next_gain is None else (next_gain,)), vmem=56, comm=comm)
    return outs, got


def ffn_dhid(gb, wd, a, b, name):
    S, D = gb.shape
    J, Fb, _ = wd.shape
    tm = _tile(S, 1024)

    parts = 4 if tm % 64 == 0 else 1

    def body(g_ref, wd_ref, a_ref, b_ref, da_ref, db_ref):
        wdv = wd_ref[...]
        for part in range(parts):
            rows = pl.ds(part * (tm // parts), tm // parts)
            dhid = _dot(g_ref[rows, :], wdv, NT) * FFN_RESIDUAL
            da_ref[rows, :] = (dhid * a_ref[rows, :].astype(F32)).astype(BF16)
            db_ref[rows, :] = (dhid * b_ref[rows, :].astype(F32)).astype(BF16)

    o_spec = pl.BlockSpec((None, tm, Fb), lambda j, i: (j, i, 0))
    shp = jax.ShapeDtypeStruct((J, S, Fb), BF16)
    return pl.pallas_call(
        body, name=name, grid=(J, S // tm),
        in_specs=[pl.BlockSpec((tm, D), lambda j, i: (i, 0)), pl.BlockSpec((None, Fb, D), lambda j, i: (j, 0, 0)),
                  o_spec, o_spec],
        out_specs=[o_spec, o_spec], out_shape=[shp, shp], compiler_params=_params(48),
    )(gb, wd, a, b)


def back_proj(ds, ws, stacked, name, comm=None):
    n = len(ds)
    J, D, Wb = ws[0].shape
    S = ds[0].shape[1] if stacked else ds[0].shape[0]
    kb = J if n == 1 else 2
    nj = J // kb
    tm = _tile(S, 256 if n == 1 else 512)

    def body(*refs):
        d_refs, w_refs, o_ref = refs[:n], refs[n:2 * n], refs[2 * n]
        acc = None
        for d_ref, w_ref in zip(d_refs, w_refs):
            for k in range(kb):
                dk = d_ref[k] if stacked else d_ref[:, k * Wb:(k + 1) * Wb]
                t = _dot(dk, w_ref[k], NT)
                acc = t if acc is None else acc + t
        if nj == 1:
            o_ref[...] = acc.astype(BF16)
        else:
            acc_ref = refs[2 * n + 1]
            j = pl.program_id(1)

            @pl.when(j == 0)
            def _():
                acc_ref[...] = acc

            @pl.when((j > 0) & (j < nj - 1))
            def _():
                acc_ref[...] += acc

            @pl.when(j == nj - 1)
            def _():
                o_ref[...] = (acc_ref[...] + acc).astype(BF16)

    if stacked:
        d_spec = pl.BlockSpec((kb, tm, Wb), lambda i, j: (j, i, 0))
    else:
        d_spec = pl.BlockSpec((tm, kb * Wb), lambda i, j: (i, j))
    w_spec = pl.BlockSpec((kb, D, Wb), lambda i, j: (j, 0, 0), pipeline_mode=pl.Buffered(1) if nj == 1 else None)
    (dh,), got = _call(
        body, name=name, grid=(S // tm, nj),
        in_specs=[d_spec] * n + [w_spec] * n,
        out_specs=[pl.BlockSpec((tm, D), lambda i, j: (i, 0))], out_shape=[jax.ShapeDtypeStruct((S, D), BF16)],
        scratch=[] if nj == 1 else [pltpu.VMEM((tm, D), F32)], args=(*ds, *ws), vmem=56, comm=comm)
    return dh, got


def grad_lhs(ht, d, stacked, name, comm=None):
    D, S = ht.shape
    if stacked:
        J, _, Wb = d.shape
    else:
        J, Wb = NDEV, d.shape[1] // NDEV
    kb = 2 if not stacked and Wb % LANES == 0 and J % 2 == 0 else 1
    tk = _tile(S, 2048 // kb)
    nk = S // tk

    def body(ht_ref, d_ref, o_ref, acc_ref):
        i = pl.program_id(1)

        @pl.when(i == 0)
        def _():
            acc_ref[...] = jnp.zeros_like(acc_ref)

        acc_ref[...] += _dot(ht_ref[...], d_ref[...])

        @pl.when(i == nk - 1)
        def _():
            for k in range(kb):
                o_ref[k] = acc_ref[:, k * Wb:(k + 1) * Wb].astype(BF16)

    d_spec = pl.BlockSpec((None, tk, Wb), lambda j, i: (j, i, 0)) if stacked else pl.BlockSpec((tk, kb * Wb), lambda j, i: (i, j))
    (dw,), got = _call(
        body, name=name, grid=(J // kb, nk),
        in_specs=[pl.BlockSpec((D, tk), lambda j, i: (0, i)), d_spec],
        out_specs=[pl.BlockSpec((kb, D, Wb), lambda j, i: (j, 0, 0))],
        out_shape=[jax.ShapeDtypeStruct((J, D, Wb), BF16)],
        scratch=[pltpu.VMEM((D, kb * Wb), F32)], args=(ht, d), vmem=56, comm=comm)
    return dw, got


def grad_shared_rhs(a, gb, nblk, scale, name):
    S, D = gb.shape
    stacked = nblk is None
    if stacked:
        J, _, Wb = a.shape
    else:
        J, Wb = nblk, a.shape[1] // nblk
    tk = _tile(S, 2048 if Wb <= 768 else 1024)
    nk = S // tk

    def body(a_ref, g_ref, o_ref, acc_ref):
        i = pl.program_id(1)

        @pl.when(i == 0)
        def _():
            acc_ref[...] = jnp.zeros_like(acc_ref)

        acc_ref[...] += _dot(a_ref[...], g_ref[...], TN)

        @pl.when(i == nk - 1)
        def _():
            o_ref[...] = (acc_ref[...] * scale).astype(BF16)

    return pl.pallas_call(
        body, name=name, grid=(J, nk),
        in_specs=[_blk(a, Wb, tk, stacked), pl.BlockSpec((tk, D), lambda j, i: (i, 0))],
        out_specs=pl.BlockSpec((None, Wb, D), lambda j, i: (j, 0, 0)),
        out_shape=jax.ShapeDtypeStruct((J, Wb, D), BF16),
        scratch_shapes=[pltpu.VMEM((Wb, D), F32)], compiler_params=_params(56),
    )(a, gb)


def out_proj(ya, ycp, wo, x, next_gain, name):
    S, A = ya.shape
    Dm, D = wo.shape
    tm = _tile(S, 256)

    def body(ya_ref, ycp_ref, wo_ref, x_ref, gain_ref, o_ref, h_ref, ht_ref):
        xo = x_ref[...] + _dot(ya_ref[...], wo_ref[0:A, :]) + _dot(ycp_ref[...], wo_ref[A:Dm, :])
        o_ref[...] = xo
        _write_norm(xo, gain_ref, h_ref, ht_ref)

    row = pl.BlockSpec((tm, D), lambda i: (i, 0))
    g_in, n_specs, n_shapes = _norm_outs(S, D, tm, next_gain)
    return pl.pallas_call(
        body, name=name, grid=(S // tm,),
        in_specs=[pl.BlockSpec((tm, A), lambda i: (i, 0)), pl.BlockSpec((tm, Dm - A), lambda i: (i, 0)),
                  pl.BlockSpec((Dm, D), lambda i: (0, 0), pipeline_mode=pl.Buffered(1)), row] + g_in,
        out_specs=[row] + n_specs, out_shape=[jax.ShapeDtypeStruct((S, D), F32)] + n_shapes,
        compiler_params=_params(48),
    )(ya, ycp, wo, x, next_gain)


def out_proj_bwd(gb, wo, A, name):
    S, D = gb.shape
    Dm = wo.shape[0]
    tm = _tile(S, 512)

    def body(g_ref, wo_ref, dya_ref, dycp_ref):
        gv = g_ref[...]
        dya_ref[...] = _dot(gv, wo_ref[0:A, :], NT).astype(BF16)
        dycp_ref[...] = _dot(gv, wo_ref[A:Dm, :], NT).astype(BF16)

    return pl.pallas_call(
        body, name=name, grid=(S // tm,),
        in_specs=[pl.BlockSpec((tm, D), lambda i: (i, 0)), pl.BlockSpec((Dm, D), lambda i: (0, 0))],
        out_specs=[pl.BlockSpec((tm, A), lambda i: (i, 0)), pl.BlockSpec((tm, Dm - A), lambda i: (i, 0))],
        out_shape=[jax.ShapeDtypeStruct((S, A), BF16), jax.ShapeDtypeStruct((S, Dm - A), BF16)],
        compiler_params=_params(48),
    )(gb, wo)


ATT_TILE = BLK * max(DILATIONS)


def _head0(shape):
    return lax.broadcasted_iota(jnp.int32, shape, 1) < HEAD_DIM


def _pair_col(v, sel):
    return jnp.max(jnp.where(sel, v, -jnp.inf), axis=-1, keepdims=True)


def _stack_heads(x, h0):
    zero = jnp.zeros_like(x)
    return jnp.concatenate([jnp.where(h0, x, zero), jnp.where(h0, zero, x)], axis=0)


def _band(lo, hi):
    qi = lax.broadcasted_iota(jnp.int32, (BLK, 2 * BLK), 0)
    kj = lax.broadcasted_iota(jnp.int32, (BLK, 2 * BLK), 1)
    return (kj >= qi) & (kj <= qi + BLK) & (kj >= lo) & (kj < hi)


def _deinterleave(dst, src, d, rows, dst_stride, dst_off=0, src_off=0, cast=None):
    for r in range(d):
        v = src[pl.ds(src_off + r, rows, stride=d), :] if d > 1 else src[pl.ds(src_off, rows), :]
        dst[pl.ds(r * dst_stride + dst_off, rows), :] = v if cast is None else v.astype(cast)


def _interleave(dst, src, d, rows, src_stride, src_off=0, add=False):
    for r in range(d):
        v = src[pl.ds(r * src_stride + src_off, rows), :]
        idx = pl.ds(r, rows, stride=d) if d > 1 else pl.ds(0, rows)
        dst[idx, :] = dst[idx, :] + v if add else v


def attn_fwd(z, A, name, comm=None):
    S = z.shape[0]
    T = ATT_TILE
    Hp, nt = A // LANES, S // T
    np_ = len(DILATIONS)

    def body(q_ref, k_ref, kp_ref, v_ref, vp_ref, y_ref, lse_ref, qn, kn, vn, qp, kp, vp, accp, mp, lp, *nat):
        accs, ms, ls = nat[:np_], nat[np_:2 * np_], nat[2 * np_:]
        i = pl.program_id(1)
        qn[...] = q_ref[...].astype(F32)
        kn[0:T, :] = kp_ref[...].astype(F32)
        kn[T:2 * T, :] = k_ref[...].astype(F32)
        vn[0:T, :] = vp_ref[...].astype(F32)
        vn[T:2 * T, :] = v_ref[...].astype(F32)
        h0 = _head0((BLK, LANES))
        for pi, d in enumerate(DILATIONS):
            Ld = T // d
            nblk = Ld // BLK
            _deinterleave(qp, qn, d, Ld, Ld, cast=BF16)
            for src, dst in ((kn, kp), (vn, vp)):
                _deinterleave(dst, src, d, Ld, 2 * Ld, cast=BF16)
                _deinterleave(dst, src, d, Ld, 2 * Ld, dst_off=Ld, src_off=T, cast=BF16)

            def unit(u, carry, nblk=nblk):
                r, n = u // nblk, u % nblk
                q0 = pl.multiple_of(u * BLK, BLK)
                k0 = pl.multiple_of((2 * r * nblk + nblk + n - 1) * BLK, BLK)
                qb = qp[pl.ds(q0, BLK), :]
                kw, vw = kp[pl.ds(k0, 2 * BLK), :], vp[pl.ds(k0, 2 * BLK), :]
                mask = _band(jnp.where((i == 0) & (n == 0), BLK, 0), 2 * BLK)
                s = _dot(_stack_heads(qb, h0), kw, NT) * ATTN_SCALE
                s = jnp.where(jnp.concatenate([mask, mask], axis=0), s, NEG_INF)
                mx = jnp.max(s, axis=-1, keepdims=True)
                p = jnp.exp(s - mx)
                l = jnp.sum(p, axis=-1, keepdims=True)
                o = _dot(p.astype(BF16), vw)
                accp[pl.ds(q0, BLK), :] = jnp.where(h0, o[:BLK], o[BLK:])
                mp[pl.ds(q0, BLK), :] = jnp.where(h0, mx[:BLK], mx[BLK:])
                lp[pl.ds(q0, BLK), :] = jnp.where(h0, l[:BLK], l[BLK:])
                return carry

            lax.fori_loop(0, T // BLK, unit, 0, unroll=True)
            for src, dst in ((accp, accs[pi]), (mp, ms[pi]), (lp, ls[pi])):
                _interleave(dst, src, d, Ld, Ld)
        mv = [m[...] for m in ms]
        mx = mv[0]
        for m in mv[1:]:
            mx = jnp.maximum(mx, m)
        ws = [jnp.exp(m - mx) for m in mv]
        l = sum(w * lr[...] for w, lr in zip(ws, ls))
        a = sum(w * ar[...] for w, ar in zip(ws, accs))
        y_ref[...] = (a / l).astype(BF16)
        lse_ref[...] = mx + jnp.log(l)

    kb, vb = A // LANES, 2 * A // LANES
    cur = lambda off: pl.BlockSpec((T, LANES), lambda h, i: (i, off + h))
    prev = lambda off: pl.BlockSpec((T, LANES), lambda h, i: (jnp.maximum(i - 1, 0), off + h))
    out = pl.BlockSpec((T, LANES), lambda h, i: (i, h))
    vm = lambda rows, dt: pltpu.VMEM((rows, LANES), dt)
    (y, lse), got = _call(
        body, name=name, grid=(Hp, nt),
        in_specs=[cur(0), cur(kb), prev(kb), cur(vb), prev(vb)], out_specs=[out, out],
        out_shape=[jax.ShapeDtypeStruct((S, A), BF16), jax.ShapeDtypeStruct((S, A), F32)],
        scratch=[vm(T, F32), vm(2 * T, F32), vm(2 * T, F32), vm(T, BF16), vm(2 * T, BF16), vm(2 * T, BF16),
                 vm(T, F32), vm(T, F32), vm(T, F32)] + [vm(T, F32)] * (3 * np_),
        args=(z, z, z, z, z), vmem=48, comm=comm)
    return y, lse, got


def attn_bwd(z, dy, y, lse, A, name, comm=None):
    S = z.shape[0]
    T = ATT_TILE
    Hp, nt = A // LANES, S // T

    def body(q_ref, k_ref, kp_ref, v_ref, vp_ref, do_ref, y_ref, ls_ref, dq_ref, dk_ref, dv_ref,
             qn, don, dln, kn, vn, qp, dop, lsp, dlp, kp, vp, dqp, dkp, dvp, dqa, dka, dva, dkc, dvc):
        step = pl.program_id(1)
        i = nt - 1 - step
        h0t = _head0((T, LANES))
        dof = do_ref[...].astype(F32)
        prod = dof * y_ref[...].astype(F32)
        d0 = jnp.sum(jnp.where(h0t, prod, 0.0), axis=-1, keepdims=True)
        d1 = jnp.sum(jnp.where(h0t, 0.0, prod), axis=-1, keepdims=True)
        qn[...] = q_ref[...].astype(F32)
        don[...] = dof
        dln[...] = jnp.where(h0t, d0, d1)
        kn[0:T, :] = kp_ref[...].astype(F32)
        kn[T:2 * T, :] = k_ref[...].astype(F32)
        vn[0:T, :] = vp_ref[...].astype(F32)
        vn[T:2 * T, :] = v_ref[...].astype(F32)

        @pl.when(step == 0)
        def _():
            dka[...] = jnp.zeros_like(dka)
            dva[...] = jnp.zeros_like(dva)

        @pl.when(step > 0)
        def _():
            dka[...] = dkc[...]
            dva[...] = dvc[...]

        h0 = _head0((BLK, LANES))
        for pi, d in enumerate(DILATIONS):
            Ld = T // d
            nblk = Ld // BLK
            for src, dst, cast in ((qn, qp, BF16), (don, dop, BF16), (dln, dlp, None)):
                _deinterleave(dst, src, d, Ld, Ld, cast=cast)
            _deinterleave(lsp, ls_ref, d, Ld, Ld)
            for src, dst in ((kn, kp), (vn, vp)):
                _deinterleave(dst, src, d, Ld, 2 * Ld, cast=BF16)
                _deinterleave(dst, src, d, Ld, 2 * Ld, dst_off=Ld, src_off=T, cast=BF16)
            dkp[...] = jnp.zeros_like(dkp)
            dvp[...] = jnp.zeros_like(dvp)

            def unit(u, carry, nblk=nblk):
                r, b = u // nblk, u % nblk
                q0 = pl.multiple_of(u * BLK, BLK)
                k0 = pl.multiple_of((2 * r * nblk + nblk + b - 1) * BLK, BLK)
                mask = _band(jnp.where((i == 0) & (b == 0), BLK, 0), 2 * BLK)
                qb, dob = qp[pl.ds(q0, BLK), :], dop[pl.ds(q0, BLK), :]
                lsb, dlb = lsp[pl.ds(q0, BLK), :], dlp[pl.ds(q0, BLK), :]
                kw, vw = kp[pl.ds(k0, 2 * BLK), :], vp[pl.ds(k0, 2 * BLK), :]
                qs, dos = _stack_heads(qb, h0), _stack_heads(dob, h0)
                lse = jnp.concatenate([_pair_col(lsb, h0), _pair_col(lsb, ~h0)], axis=0)
                delta = jnp.concatenate([_pair_col(dlb, h0), _pair_col(dlb, ~h0)], axis=0)
                s = _dot(qs, kw, NT) * ATTN_SCALE
                p = jnp.where(jnp.concatenate([mask, mask], axis=0), jnp.exp(s - lse), 0.0)
                ds = (p * (_dot(dos, vw, NT) - delta)).astype(BF16)
                dq = _dot(ds, kw) * ATTN_SCALE
                dqp[pl.ds(q0, BLK), :] = jnp.where(h0, dq[:BLK], dq[BLK:])
                dkp[pl.ds(k0, 2 * BLK), :] += _dot(ds, qs, TN) * ATTN_SCALE
                dvp[pl.ds(k0, 2 * BLK), :] += _dot(p.astype(BF16), dos, TN)
                return carry

            lax.fori_loop(0, T // BLK, unit, 0, unroll=True)
            _interleave(dqa, dqp, d, Ld, Ld, add=pi > 0)
            for acc, nxt, part in ((dka, dkc, dkp), (dva, dvc, dvp)):
                _interleave(acc, part, d, Ld, 2 * Ld, src_off=Ld, add=True)
                _interleave(nxt, part, d, Ld, 2 * Ld, add=pi > 0)
        dq_ref[...] = dqa[...].astype(BF16)
        dk_ref[...] = dka[...].astype(BF16)
        dv_ref[...] = dva[...].astype(BF16)

    kb, vb = A // LANES, 2 * A // LANES
    cur = lambda off: pl.BlockSpec((T, LANES), lambda h, s: (nt - 1 - s, off + h))
    prev = lambda off: pl.BlockSpec((T, LANES), lambda h, s: (jnp.maximum(nt - 2 - s, 0), off + h))
    vm = lambda rows, dt: pltpu.VMEM((rows, LANES), dt)
    shp = jax.ShapeDtypeStruct((S, A), BF16)
    return _call(
        body, name=name, grid=(Hp, nt),
        in_specs=[cur(0), cur(kb), prev(kb), cur(vb), prev(vb), cur(0), cur(0), cur(0)],
        out_specs=[cur(0)] * 3, out_shape=[shp] * 3,
        scratch=[vm(T, F32), vm(T, F32), vm(T, F32), vm(2 * T, F32), vm(2 * T, F32),
                 vm(T, BF16), vm(T, BF16), vm(T, F32), vm(T, F32), vm(2 * T, BF16), vm(2 * T, BF16),
                 vm(T, F32), vm(2 * T, F32), vm(2 * T, F32)] + [vm(T, F32)] * 5,
        args=(z, z, z, z, z, dy, y, lse), vmem=56, comm=comm)


def _shift_down(e, k):
    return pltpu.roll(e, k, 0)


def _shift_up(e, k):
    return pltpu.roll(e, e.shape[0] - k, 0)


def _causal_sums(e, g):
    for lvl in range(g + 1):
        e = e + _shift_down(e, 1 << lvl)
    return e


def _anticausal_sums(e, g):
    for lvl in range(g + 1):
        e = e + _shift_up(e, 1 << lvl)
    return e


def _mixer_specs(S, tm, A, C, P):
    assert (3 * A) % C == 0 and (3 * A + 3 * C) % P == 0 and tm % HALO == 0
    cb, pb = 3 * A // C, (3 * A + 3 * C) // P
    hb = tm // HALO
    last = S // HALO - 1
    col = lambda w, j: pl.BlockSpec((tm, w), lambda i: (i, j))
    prev = lambda w, j: pl.BlockSpec((HALO, w), lambda i: (jnp.maximum(i * hb - 1, 0), j))
    nxt = lambda w, j: pl.BlockSpec((HALO, w), lambda i: (jnp.minimum((i + 1) * hb, last), j))
    return cb, pb, col, prev, nxt


def _conv_taps(gc, ci, gch, cih, keep_prev):
    u = gc * ci
    e = jnp.concatenate([gch * cih * keep_prev, u], axis=0)
    return u, _shift_down(e, 1)[HALO:], _shift_down(e, 2)[HALO:]


def _pooled(xp, xph, keep_prev, pos, g):
    cols = slice(g * POOL_GROUP, (g + 1) * POOL_GROUP)
    x = xp[:, cols]
    e = jnp.concatenate([xph[:, cols] * keep_prev, x], axis=0)
    cnt = jnp.minimum(pos + 1, POOL_WINDOWS[g]).astype(F32)
    return _causal_sums(e, g)[HALO:] / cnt - x


def convpool_fwd(z, conv_w, pool_w, pool_scale, A, C, P, name):
    S = z.shape[0]
    tm = _tile(S, 512)
    cb, pb, col, prev, _ = _mixer_specs(S, tm, A, C, P)

    def body(gb_ref, gc_ref, ci_ref, xp_ref, gch_ref, cih_ref, xph_ref, cw_ref, pw_ref, ps_ref, y_ref):
        i = pl.program_id(0)
        keep_prev = jnp.where(i == 0, 0.0, 1.0)
        f = lambda r: r[...].astype(F32)
        u, u1, u2 = _conv_taps(f(gc_ref), f(ci_ref), f(gch_ref), f(cih_ref), keep_prev)
        cw = cw_ref[...]
        y_ref[:, 0:C] = (f(gb_ref) * (cw[0:1] * u2 + cw[1:2] * u1 + cw[2:3] * u)).astype(BF16)
        xp, xph = f(xp_ref), f(xph_ref)
        pos = i * tm + lax.broadcasted_iota(jnp.int32, (tm, 1), 0)
        for g in range(len(POOL_WINDOWS)):
            cols = slice(g * POOL_GROUP, (g + 1) * POOL_GROUP)
            lin = _dot(_pooled(xp, xph, keep_prev, pos, g).astype(BF16), pw_ref[g])
            y_ref[:, C + g * POOL_GROUP:C + (g + 1) * POOL_GROUP] = (lin * ps_ref[:, cols]).astype(BF16)

    full = lambda shape: pl.BlockSpec(shape, lambda i: (0,) * len(shape))
    return pl.pallas_call(
        body, name=name, grid=(S // tm,),
        in_specs=[col(C, cb), col(C, cb + 1), col(C, cb + 2), col(P, pb),
                  prev(C, cb + 1), prev(C, cb + 2), prev(P, pb),
                  full(conv_w.shape), full(pool_w.shape), full(pool_scale.shape)],
        out_specs=pl.BlockSpec((tm, C + P), lambda i: (i, 0)),
        out_shape=jax.ShapeDtypeStruct((S, C + P), BF16), compiler_params=_params(48),
    )(z, z, z, z, z, z, z, conv_w, pool_w, pool_scale)


def convpool_bwd(z, dycp, conv_w, pool_w, pool_scale, A, C, P, name):
    S = z.shape[0]
    tm = _tile(S, 512)
    nt = S // tm
    cb, pb, col, prev, nxt = _mixer_specs(S, tm, A, C, P)
    NG = len(POOL_WINDOWS)

    def body(gb_ref, gc_ref, ci_ref, xp_ref, gch_ref, cih_ref, xph_ref, gbn_ref, dy_ref, dyn_ref,
             cw_ref, pw_ref, ps_ref, dz_ref, dcw_ref, dpw_ref, dps_ref):
        i = pl.program_id(0)

        @pl.when(i == 0)
        def _():
            dcw_ref[...] = jnp.zeros_like(dcw_ref)
            dpw_ref[...] = jnp.zeros_like(dpw_ref)
            dps_ref[...] = jnp.zeros_like(dps_ref)

        keep_prev = jnp.where(i == 0, 0.0, 1.0)
        keep_next = jnp.where(i == nt - 1, 0.0, 1.0)
        f = lambda r: r[...].astype(F32)
        gb, gc, ci = f(gb_ref), f(gc_ref), f(ci_ref)
        u, u1, u2 = _conv_taps(gc, ci, f(gch_ref), f(cih_ref), keep_prev)
        cw = cw_ref[...]
        dy, dyn = f(dy_ref), f(dyn_ref) * keep_next
        dyc = dy[:, 0:C]
        dz_ref[:, 0:C] = (dyc * (cw[0:1] * u2 + cw[1:2] * u1 + cw[2:3] * u)).astype(BF16)
        dc = dyc * gb
        e = jnp.concatenate([dc, dyn[:, 0:C] * f(gbn_ref)], axis=0)
        du = cw[2:3] * dc + cw[1:2] * _shift_up(e, 1)[:tm] + cw[0:1] * _shift_up(e, 2)[:tm]
        dz_ref[:, C:2 * C] = (du * ci).astype(BF16)
        dz_ref[:, 2 * C:3 * C] = (du * gc).astype(BF16)
        dcw_ref[0:1, :] += jnp.sum(dc * u2, axis=0, keepdims=True)
        dcw_ref[1:2, :] += jnp.sum(dc * u1, axis=0, keepdims=True)
        dcw_ref[2:3, :] += jnp.sum(dc * u, axis=0, keepdims=True)

        xp, xph = f(xp_ref), f(xph_ref)
        pos = i * tm + lax.broadcasted_iota(jnp.int32, (tm, 1), 0)
        pos_e = i * tm + lax.broadcasted_iota(jnp.int32, (tm + HALO, 1), 0)
        for g in range(NG):
            cols = slice(g * POOL_GROUP, (g + 1) * POOL_GROUP)
            ycols = slice(C + g * POOL_GROUP, C + (g + 1) * POOL_GROUP)
            pooled = _pooled(xp, xph, keep_prev, pos, g).astype(BF16)
            pw = pw_ref[g]
            dyp = dy[:, ycols]
            dps_ref[:, cols] += jnp.sum(dyp * _dot(pooled, pw), axis=0, keepdims=True)
            dlin = (jnp.concatenate([dyp, dyn[:, ycols]], axis=0) * ps_ref[:, cols]).astype(BF16)
            dpw_ref[g] += _dot(pooled, dlin[:tm], TN)
            dpool = _dot(dlin, pw, NT)
            r = dpool / jnp.minimum(pos_e + 1, POOL_WINDOWS[g]).astype(F32)
            dz_ref[:, 3 * C + g * POOL_GROUP:3 * C + (g + 1) * POOL_GROUP] = (
                _anticausal_sums(r, g)[:tm] - dpool[:tm]).astype(BF16)

    full = lambda shape: pl.BlockSpec(shape, lambda i: (0,) * len(shape))
    return pl.pallas_call(
        body, name=name, grid=(nt,),
        in_specs=[col(C, cb), col(C, cb + 1), col(C, cb + 2), col(P, pb),
                  prev(C, cb + 1), prev(C, cb + 2), prev(P, pb), nxt(C, cb),
                  pl.BlockSpec((tm, C + P), lambda i: (i, 0)),
                  pl.BlockSpec((HALO, C + P), lambda i: (jnp.minimum((i + 1) * (tm // HALO), S // HALO - 1), 0)),
                  full(conv_w.shape), full(pool_w.shape), full(pool_scale.shape)],
        out_specs=[pl.BlockSpec((tm, 3 * C + P), lambda i: (i, 0)),
                   full(conv_w.shape), full(pool_w.shape), full(pool_scale.shape)],
        out_shape=[jax.ShapeDtypeStruct((S, 3 * C + P), BF16), jax.ShapeDtypeStruct(conv_w.shape, F32),
                   jax.ShapeDtypeStruct(pool_w.shape, F32), jax.ShapeDtypeStruct(pool_scale.shape, F32)],
        compiler_params=_params(48),
    )(z, z, z, z, z, z, z, z, dycp, dycp, conv_w, pool_w, pool_scale)


def adamw(lands, w, m, v, name):
    R, C = w.shape
    nl = len(lands)
    n, Rl, _ = lands[0].shape
    assert Rl * nl == R
    tr = _tile(Rl, 128 * 1024 // C)
    nb = Rl // tr
    c1 = 1.0 - ADAM_B1 ** ADAM_STEP
    c2 = 1.0 - ADAM_B2 ** ADAM_STEP

    def body(*refs):
        land_refs = refs[:nl]
        w_ref, m_ref, v_ref, g_ref, d_ref, mo_ref, vo_ref = refs[nl:]
        i = pl.program_id(0)
        for a, land_ref in enumerate(land_refs):
            @pl.when((i >= a * nb) & (i < (a + 1) * nb))
            def _():
                g = land_ref[0].astype(F32)
                for s in range(1, n):
                    g = g + land_ref[s].astype(F32)
                g_ref[...] = g
                mn = ADAM_B1 * m_ref[...] + (1.0 - ADAM_B1) * g
                vn = ADAM_B2 * v_ref[...] + (1.0 - ADAM_B2) * (g * g)
                mo_ref[...] = mn
                vo_ref[...] = vn
                d_ref[...] = -ADAM_LR * ((mn / c1) / (jnp.sqrt(vn / c2) + ADAM_EPS) + ADAM_WD * w_ref[...])

    land_specs = [pl.BlockSpec((n, tr, C), lambda i, a=a: (0, jnp.clip(i - a * nb, 0, nb - 1), 0))
                  for a in range(nl)]
    row = pl.BlockSpec((tr, C), lambda i: (i, 0))
    shp = jax.ShapeDtypeStruct((R, C), F32)
    return pl.pallas_call(
        body, name=name, grid=(nl * nb,),
        in_specs=land_specs + [row, row, row], out_specs=[row] * 4, out_shape=[shp] * 4,
        compiler_params=_params(48),
    )(*lands, w, m, v)


def _gather_comm(arrs):
    return ("gather", arrs) if arrs else None


def _ffn_bwd(g, gb, saved, gain, wg, wu, wd, tag):
    x, ht, a, b, hid = saved
    da, db = ffn_dhid(gb, wd, a, b, f"ffn_dhid_{tag}")
    dwd = grad_shared_rhs(hid, gb, None, FFN_RESIDUAL, f"ffn_dwd_{tag}")
    dwg, (land_wd,) = grad_lhs(ht, da, True, f"ffn_dwg_{tag}", ("exchange", [dwd]))
    dwu, (land_wg,) = grad_lhs(ht, db, True, f"ffn_dwu_{tag}", ("exchange", [dwg]))
    dh, (land_wu,) = back_proj([da, db], [wg, wu], True, f"ffn_dh_{tag}", ("exchange", [dwu]))
    g, gb, dgain = rms_bwd(x, gain, dh, g, f"rms_bwd_{tag}")
    return g, gb, dgain, (land_wg, land_wu, land_wd)


def _mixer_fwd(x, h, ht, next_gain, w_in, conv_w, pool_w, pool_scale, w_out, dims, tag, carry):
    A, C, P = dims
    z, got_a = proj_cols(h, w_in, f"in_proj_{tag}", _gather_comm(carry[:1]))
    ya, lse, got_b = attn_fwd(z, A, f"attn_fwd_{tag}", _gather_comm(carry[1:]))
    ycp = convpool_fwd(z, conv_w, pool_w, pool_scale, A, C, P, f"convpool_{tag}")
    xo, hn, htn = out_proj(ya, ycp, w_out, x, next_gain, f"out_proj_{tag}")
    return xo, hn, htn, (x, ht, z, lse, ya, ycp), got_a + got_b


def _mixer_bwd(g, gb, saved, gain, w_in, conv_w, pool_w, pool_scale, w_out, dims, tag):
    A, C, P = dims
    x, ht, z, lse, ya, ycp = saved
    S, D = x.shape
    dya, dycp = out_proj_bwd(gb, w_out, A, f"out_proj_bwd_{tag}")
    dwo_a = grad_shared_rhs(ya, gb, 1, 1.0, f"dwo_attn_{tag}")
    dwo_cp = grad_shared_rhs(ycp, gb, 1, 1.0, f"dwo_cp_{tag}")
    dwo = jnp.concatenate([dwo_a[0], dwo_cp[0]], axis=0).reshape(NDEV, -1, D)
    dqkv, (land_wo,) = attn_bwd(z, dya, ya, lse, A, f"attn_bwd_{tag}", ("exchange", [dwo]))
    dz_cp, dcw, dpw, dps = convpool_bwd(z, dycp, conv_w, pool_w, pool_scale, A, C, P, f"convpool_bwd_{tag}")
    dz = jnp.concatenate(dqkv + [dz_cp], axis=1)
    dwin, _ = grad_lhs(ht, dz, False, f"dwin_{tag}")
    dh, (land_win,) = back_proj([dz], [w_in], False, f"in_proj_bwd_{tag}", ("exchange", [dwin]))
    g, gb, dgain = rms_bwd(x, gain, dh, g, f"rms_bwd_{tag}")
    return g, gb, (dgain, dcw, dpw, dps), (land_win, land_wo)


def _pack(arrs):
    flat = [a.reshape(-1).astype(F32) for a in arrs]
    spans, off = [], 0
    for a in flat:
        spans.append((off, a.shape[0]))
        off += a.shape[0]
    rows = -(-off // (8 * LANES)) * 8
    buf = jnp.concatenate(flat + [jnp.zeros((rows * LANES - off,), F32)]).reshape(rows, LANES)
    return buf, spans


def kernel(x, ffn1_norm, ffn1_w_gate, ffn1_w_up, ffn1_w_down, mix_norm, w_in, conv_w, pool_w, pool_scale, w_out, ffn2_norm, ffn2_w_gate, ffn2_w_up, ffn2_w_down, final_norm, loss_target, m_ffn1_norm, m_ffn1_w_gate, m_ffn1_w_up, m_ffn1_w_down, m_mix_norm, m_w_in, m_conv_w, m_pool_w, m_pool_scale, m_w_out, m_ffn2_norm, m_ffn2_w_gate, m_ffn2_w_up, m_ffn2_w_down, m_final_norm, v_ffn1_norm, v_ffn1_w_gate, v_ffn1_w_up, v_ffn1_w_down, v_mix_norm, v_w_in, v_conv_w, v_pool_w, v_pool_scale, v_w_out, v_ffn2_norm, v_ffn2_w_gate, v_ffn2_w_up, v_ffn2_w_down, v_final_norm):
    depth = ffn1_norm.shape[0]
    S, D = x.shape[1], x.shape[2]
    Cb = conv_w.shape[2]
    C = Cb * NDEV
    P = pool_scale.shape[1]
    A = (w_in.shape[2] * NDEV - 3 * C - P) // 3
    dims = (A, C, P)
    me = 4 * lax.axis_index("x") + 2 * lax.axis_index("y") + lax.axis_index("c")

    big = dict(ffn1_w_gate=ffn1_w_gate, ffn1_w_up=ffn1_w_up, ffn1_w_down=ffn1_w_down, w_in=w_in, w_out=w_out,
               ffn2_w_gate=ffn2_w_gate, ffn2_w_up=ffn2_w_up, ffn2_w_down=ffn2_w_down)
    big_m = dict(ffn1_w_gate=m_ffn1_w_gate, ffn1_w_up=m_ffn1_w_up, ffn1_w_down=m_ffn1_w_down, w_in=m_w_in,
                 w_out=m_w_out, ffn2_w_gate=m_ffn2_w_gate, ffn2_w_up=m_ffn2_w_up, ffn2_w_down=m_ffn2_w_down)
    big_v = dict(ffn1_w_gate=v_ffn1_w_gate, ffn1_w_up=v_ffn1_w_up, ffn1_w_down=v_ffn1_w_down, w_in=v_w_in,
                 w_out=v_w_out, ffn2_w_gate=v_ffn2_w_gate, ffn2_w_up=v_ffn2_w_up, ffn2_w_down=v_ffn2_w_down)
    names = list(big)

    first_names = ["ffn1_w_gate", "ffn1_w_up"]
    mix_names = ["w_in", "w_out"]
    ffn2_names = ["ffn2_w_gate", "ffn2_w_up", "ffn2_w_down"]
    shards = lambda l, ns: [big[n][l].astype(BF16) for n in ns]
    full = [dict() for _ in range(depth)]
    full[0].update(zip(first_names, all_gather(shards(0, first_names), "gather_first")))
    conv_full = all_gather([jnp.pad(conv_w.reshape(-1, Cb), ((0, 2), (0, LANES - Cb)))], "gather_conv_w")[0]
    conv_full = jnp.transpose(conv_full[:, :depth * 3, :Cb].reshape(NDEV, depth, 3, Cb), (1, 2, 0, 3)).reshape(depth, 3, C)
    pool_w_bf = pool_w.astype(BF16)

    xs = x[0]
    saved = []
    h, ht = rms_fwd(xs, ffn1_norm[0:1], "rms_f1_l0")
    for l in range(depth):
        W = full[l]
        last = l + 1 == depth
        up1 = ["ffn1_w_down"] + (mix_names if l == 0 else [])
        (a, b, hid), got = ffn_up(h, W["ffn1_w_gate"], W["ffn1_w_up"], f"ffn_up_f1_l{l}", _gather_comm(shards(l, up1)))
        W.update(zip(up1, got))
        s1 = (xs, ht, a, b, hid)
        (xs, h, ht), _ = ffn_down(hid, W["ffn1_w_down"], xs, f"ffn_down_f1_l{l}", next_gain=mix_norm[l:l + 1])
        xs, h, ht, s2, got = _mixer_fwd(xs, h, ht, ffn2_norm[l:l + 1], W["w_in"], conv_full[l], pool_w_bf[l],
                                        pool_scale[l:l + 1], W["w_out"].reshape(-1, D), dims, f"mix_l{l}",
                                        shards(l, ffn2_names[:2]))
        W.update(zip(ffn2_names[:2], got))
        got = ffn_up(h, W["ffn2_w_gate"], W["ffn2_w_up"], f"ffn_up_f2_l{l}",
                     _gather_comm(shards(l, ffn2_names[2:]) + ([] if last else shards(l + 1, first_names))))
        (a, b, hid), (W["ffn2_w_down"], *got_next) = got
        s3 = (xs, ht, a, b, hid)
        outs, got_down = ffn_down(hid, W["ffn2_w_down"], xs, f"ffn_down_f2_l{l}",
                                  _gather_comm([] if last else shards(l + 1, mix_names)),
                                  next_gain=None if last else ffn1_norm[l + 1:l + 2])
        if last:
            xs, = outs
        else:
            xs, h, ht = outs
            full[l + 1].update(zip(first_names, got_next))
            full[l + 1].update(zip(mix_names, got_down))
        saved.append((s1, s2, s3))
    loss_part, g, gb, d_final = final_loss(xs, final_norm.reshape(1, D), loss_target[0], "final_loss")

    small = [None] * depth
    lands = [None] * depth
    for l in reversed(range(depth)):
        W = full[l]
        s1, s2, s3 = saved[l]
        g, gb, dn2, (lg2, lu2, ld2) = _ffn_bwd(g, gb, s3, ffn2_norm[l:l + 1], W["ffn2_w_gate"], W["ffn2_w_up"],
                                               W["ffn2_w_down"], f"f2_l{l}")
        g, gb, (dnm, dcw, dpw, dps), (lwin, lwo) = _mixer_bwd(
            g, gb, s2, mix_norm[l:l + 1], W["w_in"], conv_full[l], pool_w_bf[l], pool_scale[l:l + 1],
            W["w_out"].reshape(-1, D), dims, f"mix_l{l}")
        g, gb, dn1, (lg1, lu1, ld1) = _ffn_bwd(g, gb, s1, ffn1_norm[l:l + 1], W["ffn1_w_gate"], W["ffn1_w_up"],
                                               W["ffn1_w_down"], f"f1_l{l}")
        lands[l] = dict(ffn1_w_gate=lg1, ffn1_w_up=lu1, ffn1_w_down=ld1, w_in=lwin, w_out=lwo,
                        ffn2_w_gate=lg2, ffn2_w_up=lu2, ffn2_w_down=ld2)
        small[l] = (dn1, dnm, dcw, dpw, dps, dn2)

    res = {}
    for n in names:
        shp = big[n].shape
        two = lambda t: t.reshape(-1, shp[-1])
        outs = adamw([lands[l][n] for l in range(depth)], two(big[n]), two(big_m[n]), two(big_v[n]), f"adamw_{n}")
        res[n] = [o.reshape(shp) for o in outs]

    st = lambda i: jnp.stack([small[l][i] for l in range(depth)])
    small_g = dict(ffn1_norm=st(0), mix_norm=st(1), conv_w=st(2), pool_w=st(3), pool_scale=st(4), ffn2_norm=st(5),
                   final_norm=d_final)
    small_names = list(small_g)
    zeros_conv = jnp.zeros((depth, 3, C), F32)
    place = lambda t: lax.dynamic_update_slice(zeros_conv, t, (0, 0, me * Cb))
    small_w = dict(ffn1_norm=ffn1_norm, mix_norm=mix_norm, conv_w=place(conv_w), pool_w=pool_w, pool_scale=pool_scale,
                   ffn2_norm=ffn2_norm, final_norm=final_norm)
    small_m = dict(ffn1_norm=m_ffn1_norm, mix_norm=m_mix_norm, conv_w=place(m_conv_w), pool_w=m_pool_w,
                   pool_scale=m_pool_scale, ffn2_norm=m_ffn2_norm, final_norm=m_final_norm)
    small_v = dict(ffn1_norm=v_ffn1_norm, mix_norm=v_mix_norm, conv_w=place(v_conv_w), pool_w=v_pool_w,
                   pool_scale=v_pool_scale, ffn2_norm=v_ffn2_norm, final_norm=v_final_norm)
    gbuf, spans = _pack([small_g[n] for n in small_names] + [loss_part])
    wbuf, _ = _pack([small_w[n] for n in small_names] + [jnp.zeros((1, LANES), F32)])
    mbuf, _ = _pack([small_m[n] for n in small_names] + [jnp.zeros((1, LANES), F32)])
    vbuf, _ = _pack([small_v[n] for n in small_names] + [jnp.zeros((1, LANES), F32)])
    gathered = all_gather([gbuf], "gather_small_grads")[0]
    outs = adamw([gathered], wbuf, mbuf, vbuf, "adamw_small")
    for n, (off, size) in zip(small_names, spans):
        shp = small_w[n].shape
        vals = [o.reshape(-1)[off:off + size].reshape(shp) for o in outs]
        if n == "conv_w":
            vals = [lax.dynamic_slice(t, (0, 0, me * Cb), (depth, 3, Cb)) for t in vals]
        res[n] = vals
    loss = outs[0].reshape(-1)[spans[-1][0]]

    order = ["ffn1_norm", "ffn1_w_gate", "ffn1_w_up", "ffn1_w_down", "mix_norm", "w_in", "conv_w", "pool_w",
             "pool_scale", "w_out", "ffn2_norm", "ffn2_w_gate", "ffn2_w_up", "ffn2_w_down", "final_norm"]
    return (loss, g[None], *[res[n][0] for n in order], *[res[n][1] for n in order],
            *[res[n][2] for n in order], *[res[n][3] for n in order])
```

```python
import functools

import jax
import jax.numpy as jnp
from jax import lax
from jax.experimental import pallas as pl
from jax.experimental.pallas import tpu as pltpu

F32 = jnp.float32
BF16 = jnp.bfloat16
NDEV = 8
HEAD_DIM = 64
LANES = 128
BLK = 128
DILATIONS = (1, 4, 16)
POOL_WINDOWS = (2, 4, 8, 16)
POOL_GROUP = 128
HALO = 16
FFN_RESIDUAL = 0.5
RMS_EPS = 1e-6
NEG_INF = -1e30
ATTN_SCALE = HEAD_DIM ** -0.5
ADAM_LR, ADAM_B1, ADAM_B2, ADAM_EPS, ADAM_WD, ADAM_STEP = 0.001, 0.9, 0.999, 1e-08, 0.01, 10
VMEM_BYTES = 64 * 1024 * 1024
MESH = pl.DeviceIdType.MESH

NN = (((1,), (0,)), ((), ()))
NT = (((1,), (1,)), ((), ()))
TN = (((0,), (0,)), ((), ()))


def _dot(a, b, dims=NN):
    return lax.dot_general(a, b, dims, preferred_element_type=F32)


def _params(vmem_mb=48):
    return pltpu.CompilerParams(vmem_limit_bytes=min(vmem_mb * 1024 * 1024, VMEM_BYTES - 4 * 1024 * 1024))


def _tile(n, want):
    t = min(n, max(16, want // 16 * 16))
    while t > 16 and (n % t or t % 16):
        t -= 16
    return t if n % t == 0 else n


def _coords():
    return lax.axis_index("x"), lax.axis_index("y"), lax.axis_index("c")


def _comm_sems(n):
    return [pltpu.SemaphoreType.DMA((7 * n,)), pltpu.SemaphoreType.DMA((7 * n,)), pltpu.SemaphoreType.DMA((n,))]


def _gather_ops(ins, outs, sems):
    n = len(ins)
    send_sems, recv_sems, local_sems = sems
    x, y, c = _coords()
    me, sibling = (x, y, c), (x, y, 1 - c)
    chips = [(1 - x, y), (x, 1 - y), (1 - x, 1 - y)]

    def slot(out, p):
        return out.at[4 * p[0] + 2 * p[1] + p[2]]

    def copy(a, k, block, to, src=None):
        return pltpu.make_async_remote_copy(
            src_ref=slot(outs[a], block) if src is None else src, dst_ref=slot(outs[a], block),
            send_sem=send_sems.at[7 * a + k], recv_sem=recv_sems.at[7 * a + k],
            device_id=to, device_id_type=MESH)

    def own():
        mine = [pltpu.make_async_copy(ins[a], slot(outs[a], me), local_sems.at[a]) for a in range(n)]
        first = []
        for a in range(n):
            first.append(copy(a, 0, me, sibling, src=ins[a]))
            first += [copy(a, 1 + j, me, (*chip, c), src=ins[a]) for j, chip in enumerate(chips)]
        return mine, first

    def start():
        mine, first = own()
        for cp in mine + first:
            cp.start()

    def relay():
        for j, chip in enumerate(chips):
            for a in range(n):
                copy(a, 1 + j, (*chip, c), me).wait_recv()
                copy(a, 4 + j, (*chip, c), sibling).start()

    def finish():
        mine, first = own()
        passed = [copy(a, 4 + j, (*chip, c), sibling) for j, chip in enumerate(chips) for a in range(n)]
        for a in range(n):
            copy(a, 0, sibling, me).wait_recv()
            for j, chip in enumerate(chips):
                copy(a, 4 + j, (*chip, 1 - c), me).wait_recv()
        for cp in first + passed:
            cp.wait_send()
        for cp in mine:
            cp.wait()

    return start, relay, finish


def _exchange_ops(ins, outs, sems):
    n = len(ins)
    send_sems, recv_sems, local_sems = sems
    x, y, c = _coords()
    me = 4 * x + 2 * y + c

    def peer(k):
        return (1 - x if k & 4 else x, 1 - y if k & 2 else y, 1 - c if k & 1 else c)

    def copy(a, k):
        p = peer(k)
        return pltpu.make_async_remote_copy(
            src_ref=ins[a].at[4 * p[0] + 2 * p[1] + p[2]], dst_ref=outs[a].at[me],
            send_sem=send_sems.at[7 * a + k - 1], recv_sem=recv_sems.at[7 * a + k - 1],
            device_id=p, device_id_type=MESH)

    def landed(a, k):
        p = peer(k)
        return pltpu.make_async_remote_copy(
            src_ref=ins[a].at[me], dst_ref=outs[a].at[4 * p[0] + 2 * p[1] + p[2]],
            send_sem=send_sems.at[7 * a + k - 1], recv_sem=recv_sems.at[7 * a + k - 1],
            device_id=p, device_id_type=MESH)

    def own():
        mine = [pltpu.make_async_copy(ins[a].at[me], outs[a].at[me], local_sems.at[a]) for a in range(n)]
        return mine, [copy(a, k) for a in range(n) for k in range(1, NDEV)]

    def start():
        mine, sent = own()
        for cp in mine + sent:
            cp.start()

    def finish():
        mine, sent = own()
        for a in range(n):
            for k in range(1, NDEV):
                landed(a, k).wait_recv()
        for cp in sent:
            cp.wait_send()
        for cp in mine:
            cp.wait()

    return start, None, finish


_COMM = {"gather": (_gather_ops, lambda a: (NDEV,) + a.shape), "exchange": (_exchange_ops, lambda a: a.shape)}


def all_gather(arrs, name):
    n = len(arrs)

    def body(*refs):
        start, relay, finish = _gather_ops(refs[:n], refs[n:2 * n], refs[2 * n:])
        start()
        relay()
        finish()

    any_spec = pl.BlockSpec(memory_space=pl.ANY)
    return pl.pallas_call(
        body, name=name, out_shape=[jax.ShapeDtypeStruct((NDEV,) + a.shape, a.dtype) for a in arrs],
        in_specs=[any_spec] * n, out_specs=[any_spec] * n, scratch_shapes=_comm_sems(n),
    )(*arrs)


def _call(body, *, name, grid, in_specs, out_specs, out_shape, args, scratch=(), vmem=48, comm=None):
    if comm is None:
        outs = pl.pallas_call(
            body, name=name, grid=grid, in_specs=list(in_specs), out_specs=list(out_specs), out_shape=list(out_shape),
            scratch_shapes=list(scratch), compiler_params=_params(vmem))(*args)
        return list(outs), []
    kind, arrs = comm
    ops, shape_of = _COMM[kind]
    n, n_in, n_out, n_scr = len(arrs), len(in_specs), len(out_specs), len(scratch)
    n_steps = functools.reduce(lambda p, g: p * g, grid, 1)
    relay_at = max(n_steps - 1 - max(n_steps // 8, 1), 0)

    def carrier(*refs):
        ins, c_in = refs[:n_in], refs[n_in:n_in + n]
        o0 = n_in + n
        outs, c_out = refs[o0:o0 + n_out], refs[o0 + n_out:o0 + n_out + n]
        s0 = o0 + n_out + n
        scr, sems = refs[s0:s0 + n_scr], refs[s0 + n_scr:]
        ids = [pl.program_id(d) for d in range(len(grid))]
        first = functools.reduce(jnp.logical_and, [i == 0 for i in ids])
        last = functools.reduce(jnp.logical_and, [i == g - 1 for i, g in zip(ids, grid)])
        start, relay, finish = ops(c_in, c_out, sems)
        pl.when(first)(start)
        body(*ins, *outs, *scr)
        if relay is not None:
            step = functools.reduce(lambda acc, ig: acc * ig[1] + ig[0], zip(ids, grid), 0)
            pl.when(step == relay_at)(relay)
        pl.when(last)(finish)

    any_spec = pl.BlockSpec(memory_space=pl.ANY)
    res = pl.pallas_call(
        carrier, name=name, grid=grid, in_specs=list(in_specs) + [any_spec] * n,
        out_specs=list(out_specs) + [any_spec] * n,
        out_shape=list(out_shape) + [jax.ShapeDtypeStruct(shape_of(a), a.dtype) for a in arrs],
        scratch_shapes=list(scratch) + _comm_sems(n), compiler_params=_params(vmem))(*args, *arrs)
    return list(res[:n_out]), list(res[n_out:])


def _rstd(x):
    return lax.rsqrt(jnp.mean(x * x, axis=-1, keepdims=True) + RMS_EPS)


def rms_fwd(x, gain, name):
    S, D = x.shape
    tm = _tile(S, 512)

    def body(x_ref, g_ref, h_ref, ht_ref):
        xv = x_ref[...]
        h = xv * _rstd(xv) * g_ref[...]
        h_ref[...] = h.astype(BF16)
        ht_ref[...] = h.T.astype(BF16)

    return pl.pallas_call(
        body, name=name, grid=(S // tm,),
        in_specs=[pl.BlockSpec((tm, D), lambda i: (i, 0)), pl.BlockSpec((1, D), lambda i: (0, 0))],
        out_specs=[pl.BlockSpec((tm, D), lambda i: (i, 0)), pl.BlockSpec((D, tm), lambda i: (0, i))],
        out_shape=[jax.ShapeDtypeStruct((S, D), BF16), jax.ShapeDtypeStruct((D, S), BF16)],
        compiler_params=_params(40),
    )(x, gain)


def _rms_bwd_math(xv, gain, dh):
    r = _rstd(xv)
    xhat = xv * r
    dxhat = dh * gain
    dx = r * (dxhat - xhat * jnp.mean(dxhat * xhat, axis=-1, keepdims=True))
    return dx, dh * xhat


def rms_bwd(x, gain, dh, g, name):
    S, D = x.shape
    tm = _tile(S, 256)

    def body(x_ref, gain_ref, dh_ref, g_ref, go_ref, gb_ref, dg_ref):
        @pl.when(pl.program_id(0) == 0)
        def _():
            dg_ref[...] = jnp.zeros_like(dg_ref)

        dx, dgain = _rms_bwd_math(x_ref[...], gain_ref[...], dh_ref[...].astype(F32))
        gn = g_ref[...] + dx
        go_ref[...] = gn
        gb_ref[...] = gn.astype(BF16)
        dg_ref[...] += jnp.sum(dgain, axis=0, keepdims=True)

    row = pl.BlockSpec((tm, D), lambda i: (i, 0))
    vec = pl.BlockSpec((1, D), lambda i: (0, 0))
    return pl.pallas_call(
        body, name=name, grid=(S // tm,),
        in_specs=[row, vec, row, row], out_specs=[row, row, vec],
        out_shape=[jax.ShapeDtypeStruct((S, D), F32), jax.ShapeDtypeStruct((S, D), BF16),
                   jax.ShapeDtypeStruct((1, D), F32)],
        compiler_params=_params(48),
    )(x, gain, dh, g)


def final_loss(x, gain, target, name):
    S, D = x.shape
    tm = _tile(S, 256)

    def body(x_ref, gain_ref, t_ref, loss_ref, go_ref, gb_ref, dg_ref):
        @pl.when(pl.program_id(0) == 0)
        def _():
            dg_ref[...] = jnp.zeros_like(dg_ref)
            loss_ref[...] = jnp.zeros_like(loss_ref)

        xv, gain_v = x_ref[...], gain_ref[...]
        err = xv * _rstd(xv) * gain_v - t_ref[...]
        loss_ref[...] += jnp.sum(jnp.sum(err * err, axis=-1, keepdims=True), axis=0, keepdims=True) * (0.5 / D)
        dx, dgain = _rms_bwd_math(xv, gain_v, err * (1.0 / D))
        go_ref[...] = dx
        gb_ref[...] = dx.astype(BF16)
        dg_ref[...] += jnp.sum(dgain, axis=0, keepdims=True)

    row = pl.BlockSpec((tm, D), lambda i: (i, 0))
    vec = pl.BlockSpec((1, D), lambda i: (0, 0))
    return pl.pallas_call(
        body, name=name, grid=(S // tm,),
        in_specs=[row, vec, row],
        out_specs=[pl.BlockSpec((1, LANES), lambda i: (0, 0)), row, row, vec],
        out_shape=[jax.ShapeDtypeStruct((1, LANES), F32), jax.ShapeDtypeStruct((S, D), F32),
                   jax.ShapeDtypeStruct((S, D), BF16), jax.ShapeDtypeStruct((1, D), F32)],
        compiler_params=_params(48),
    )(x, gain, target)


def _blk(arr, width, tm, stacked):
    if stacked:
        return pl.BlockSpec((None, tm, width), lambda j, i: (j, i, 0))
    return pl.BlockSpec((tm, width), lambda j, i: (i, j))


def ffn_up(h, wg, wu, name, comm=None):
    S, D = h.shape
    J, _, Fb = wg.shape
    tm = _tile(S, 1024)

    def body(h_ref, wg_ref, wu_ref, da_ref, db_ref, hid_ref):
        hv = h_ref[...]
        a = _dot(hv, wg_ref[...])
        b = _dot(hv, wu_ref[...])
        sg = jax.nn.sigmoid(a)
        silu = a * sg
        da_ref[...] = (b * (sg * (1.0 + a * (1.0 - sg)))).astype(BF16)
        db_ref[...] = silu.astype(BF16)
        hid_ref[...] = (silu * b).astype(BF16)

    w_spec = pl.BlockSpec((None, D, Fb), lambda j, i: (j, 0, 0))
    o_spec = pl.BlockSpec((None, tm, Fb), lambda j, i: (j, i, 0))
    shp = jax.ShapeDtypeStruct((J, S, Fb), BF16)
    return _call(
        body, name=name, grid=(J, S // tm),
        in_specs=[pl.BlockSpec((tm, D), lambda j, i: (i, 0)), w_spec, w_spec],
        out_specs=[o_spec, o_spec, o_spec], out_shape=[shp, shp, shp], args=(h, wg, wu), comm=comm)


def proj_cols(h, w, name, comm=None):
    S, D = h.shape
    J, _, Wb = w.shape
    tm = _tile(S, 1024)
    kb = 2 if Wb % LANES == 0 and J % 2 == 0 else 1

    def body(h_ref, w_ref, z_ref):
        wv = jnp.concatenate([w_ref[k] for k in range(kb)], axis=1)
        z_ref[...] = _dot(h_ref[...], wv).astype(BF16)

    (z,), got = _call(
        body, name=name, grid=(J // kb, S // tm),
        in_specs=[pl.BlockSpec((tm, D), lambda j, i: (i, 0)), pl.BlockSpec((kb, D, Wb), lambda j, i: (j, 0, 0))],
        out_specs=[pl.BlockSpec((tm, kb * Wb), lambda j, i: (i, j))],
        out_shape=[jax.ShapeDtypeStruct((S, J * Wb), BF16)], args=(h, w), comm=comm)
    return z, got


def _norm_outs(S, D, tm, gain):
    if gain is None:
        return [], [], []
    return ([pl.BlockSpec((1, D), lambda i: (0, 0))],
            [pl.BlockSpec((tm, D), lambda i: (i, 0)), pl.BlockSpec((D, tm), lambda i: (0, i))],
            [jax.ShapeDtypeStruct((S, D), BF16), jax.ShapeDtypeStruct((D, S), BF16)])


def _write_norm(xo, gain_ref, h_ref, ht_ref):
    h = xo * _rstd(xo) * gain_ref[...]
    h_ref[...] = h.astype(BF16)
    ht_ref[...] = h.T.astype(BF16)


def ffn_down(hid, wd, x, name, comm=None, next_gain=None):
    J, S, Fb = hid.shape
    D = wd.shape[2]
    tm = _tile(S, 256)

    def body(hid_ref, wd_ref, x_ref, *rest):
        acc = _dot(hid_ref[0], wd_ref[0])
        for j in range(1, J):
            acc = acc + _dot(hid_ref[j], wd_ref[j])
        xo = x_ref[...] + FFN_RESIDUAL * acc
        if next_gain is None:
            rest[0][...] = xo
        else:
            gain_ref, o_ref, h_ref, ht_ref = rest
            o_ref[...] = xo
            _write_norm(xo, gain_ref, h_ref, ht_ref)

    row = pl.BlockSpec((tm, D), lambda i: (i, 0))
    g_in, n_specs, n_shapes = _norm_outs(S, D, tm, next_gain)
    outs, got = _call(
        body, name=name, grid=(S // tm,),
        in_specs=[pl.BlockSpec((J, tm, Fb), lambda i: (0, i, 0)),
                  pl.BlockSpec((J, Fb, D), lambda i: (0, 0, 0), pipeline_mode=pl.Buffered(1)), row] + g_in,
        out_specs=[row] + n_specs, out_shape=[jax.ShapeDtypeStruct((S, D), F32)] + n_shapes,
        args=(hid, wd, x) + (() if next_gain is None else (next_gain,)), vmem=56, comm=comm)
    return outs, got


def ffn_dhid(gb, wd, a, b, name):
    S, D = gb.shape
    J, Fb, _ = wd.shape
    tm = _tile(S, 1024)

    parts = 4 if tm % 64 == 0 else 1

    def body(g_ref, wd_ref, a_ref, b_ref, da_ref, db_ref):
        wdv = wd_ref[...]
        for part in range(parts):
            rows = pl.ds(part * (tm // parts), tm // parts)
            dhid = _dot(g_ref[rows, :], wdv, NT) * FFN_RESIDUAL
            da_ref[rows, :] = (dhid * a_ref[rows, :].astype(F32)).astype(BF16)
            db_ref[rows, :] = (dhid * b_ref[rows, :].astype(F32)).astype(BF16)

    o_spec = pl.BlockSpec((None, tm, Fb), lambda j, i: (j, i, 0))
    shp = jax.ShapeDtypeStruct((J, S, Fb), BF16)
    return pl.pallas_call(
        body, name=name, grid=(J, S // tm),
        in_specs=[pl.BlockSpec((tm, D), lambda j, i: (i, 0)), pl.BlockSpec((None, Fb, D), lambda j, i: (j, 0, 0)),
                  o_spec, o_spec],
        out_specs=[o_spec, o_spec], out_shape=[shp, shp], compiler_params=_params(48),
    )(gb, wd, a, b)


def back_proj(ds, ws, stacked, name, comm=None):
    n = len(ds)
    J, D, Wb = ws[0].shape
    S = ds[0].shape[1] if stacked else ds[0].shape[0]
    kb = J if n == 1 else 2
    nj = J // kb
    tm = _tile(S, 256 if n == 1 else 512)

    def body(*refs):
        d_refs, w_refs, o_ref = refs[:n], refs[n:2 * n], refs[2 * n]
        acc = None
        for d_ref, w_ref in zip(d_refs, w_refs):
            for k in range(kb):
                dk = d_ref[k] if stacked else d_ref[:, k * Wb:(k + 1) * Wb]
                t = _dot(dk, w_ref[k], NT)
                acc = t if acc is None else acc + t
        if nj == 1:
            o_ref[...] = acc.astype(BF16)
        else:
            acc_ref = refs[2 * n + 1]
            j = pl.program_id(1)

            @pl.when(j == 0)
            def _():
                acc_ref[...] = acc

            @pl.when((j > 0) & (j < nj - 1))
            def _():
                acc_ref[...] += acc

            @pl.when(j == nj - 1)
            def _():
                o_ref[...] = (acc_ref[...] + acc).astype(BF16)

    if stacked:
        d_spec = pl.BlockSpec((kb, tm, Wb), lambda i, j: (j, i, 0))
    else:
        d_spec = pl.BlockSpec((tm, kb * Wb), lambda i, j: (i, j))
    w_spec = pl.BlockSpec((kb, D, Wb), lambda i, j: (j, 0, 0), pipeline_mode=pl.Buffered(1) if nj == 1 else None)
    (dh,), got = _call(
        body, name=name, grid=(S // tm, nj),
        in_specs=[d_spec] * n + [w_spec] * n,
        out_specs=[pl.BlockSpec((tm, D), lambda i, j: (i, 0))], out_shape=[jax.ShapeDtypeStruct((S, D), BF16)],
        scratch=[] if nj == 1 else [pltpu.VMEM((tm, D), F32)], args=(*ds, *ws), vmem=56, comm=comm)
    return dh, got


def grad_lhs(ht, d, stacked, name, comm=None):
    D, S = ht.shape
    if stacked:
        J, _, Wb = d.shape
    else:
        J, Wb = NDEV, d.shape[1] // NDEV
    kb = 2 if not stacked and Wb % LANES == 0 and J % 2 == 0 else 1
    tk = _tile(S, 2048 // kb)
    nk = S // tk

    def body(ht_ref, d_ref, o_ref, acc_ref):
        i = pl.program_id(1)

        @pl.when(i == 0)
        def _():
            acc_ref[...] = jnp.zeros_like(acc_ref)

        acc_ref[...] += _dot(ht_ref[...], d_ref[...])

        @pl.when(i == nk - 1)
        def _():
            for k in range(kb):
                o_ref[k] = acc_ref[:, k * Wb:(k + 1) * Wb].astype(BF16)

    d_spec = pl.BlockSpec((None, tk, Wb), lambda j, i: (j, i, 0)) if stacked else pl.BlockSpec((tk, kb * Wb), lambda j, i: (i, j))
    (dw,), got = _call(
        body, name=name, grid=(J // kb, nk),
        in_specs=[pl.BlockSpec((D, tk), lambda j, i: (0, i)), d_spec],
        out_specs=[pl.BlockSpec((kb, D, Wb), lambda j, i: (j, 0, 0))],
        out_shape=[jax.ShapeDtypeStruct((J, D, Wb), BF16)],
        scratch=[pltpu.VMEM((D, kb * Wb), F32)], args=(ht, d), vmem=56, comm=comm)
    return dw, got


def grad_shared_rhs(a, gb, nblk, scale, name):
    S, D = gb.shape
    stacked = nblk is None
    if stacked:
        J, _, Wb = a.shape
    else:
        J, Wb = nblk, a.shape[1] // nblk
    tk = _tile(S, 2048 if Wb <= 768 else 1024)
    nk = S // tk

    def body(a_ref, g_ref, o_ref, acc_ref):
        i = pl.program_id(1)

        @pl.when(i == 0)
        def _():
            acc_ref[...] = jnp.zeros_like(acc_ref)

        acc_ref[...] += _dot(a_ref[...], g_ref[...], TN)

        @pl.when(i == nk - 1)
        def _():
            o_ref[...] = (acc_ref[...] * scale).astype(BF16)

    return pl.pallas_call(
        body, name=name, grid=(J, nk),
        in_specs=[_blk(a, Wb, tk, stacked), pl.BlockSpec((tk, D), lambda j, i: (i, 0))],
        out_specs=pl.BlockSpec((None, Wb, D), lambda j, i: (j, 0, 0)),
        out_shape=jax.ShapeDtypeStruct((J, Wb, D), BF16),
        scratch_shapes=[pltpu.VMEM((Wb, D), F32)], compiler_params=_params(56),
    )(a, gb)


def out_proj(ya, ycp, wo, x, next_gain, name):
    S, A = ya.shape
    Dm, D = wo.shape
    tm = _tile(S, 256)

    def body(ya_ref, ycp_ref, wo_ref, x_ref, gain_ref, o_ref, h_ref, ht_ref):
        xo = x_ref[...] + _dot(ya_ref[...], wo_ref[0:A, :]) + _dot(ycp_ref[...], wo_ref[A:Dm, :])
        o_ref[...] = xo
        _write_norm(xo, gain_ref, h_ref, ht_ref)

    row = pl.BlockSpec((tm, D), lambda i: (i, 0))
    g_in, n_specs, n_shapes = _norm_outs(S, D, tm, next_gain)
    return pl.pallas_call(
        body, name=name, grid=(S // tm,),
        in_specs=[pl.BlockSpec((tm, A), lambda i: (i, 0)), pl.BlockSpec((tm, Dm - A), lambda i: (i, 0)),
                  pl.BlockSpec((Dm, D), lambda i: (0, 0), pipeline_mode=pl.Buffered(1)), row] + g_in,
        out_specs=[row] + n_specs, out_shape=[jax.ShapeDtypeStruct((S, D), F32)] + n_shapes,
        compiler_params=_params(48),
    )(ya, ycp, wo, x, next_gain)


def out_proj_bwd(gb, wo, A, name):
    S, D = gb.shape
    Dm = wo.shape[0]
    tm = _tile(S, 512)

    def body(g_ref, wo_ref, dya_ref, dycp_ref):
        gv = g_ref[...]
        dya_ref[...] = _dot(gv, wo_ref[0:A, :], NT).astype(BF16)
        dycp_ref[...] = _dot(gv, wo_ref[A:Dm, :], NT).astype(BF16)

    return pl.pallas_call(
        body, name=name, grid=(S // tm,),
        in_specs=[pl.BlockSpec((tm, D), lambda i: (i, 0)), pl.BlockSpec((Dm, D), lambda i: (0, 0))],
        out_specs=[pl.BlockSpec((tm, A), lambda i: (i, 0)), pl.BlockSpec((tm, Dm - A), lambda i: (i, 0))],
        out_shape=[jax.ShapeDtypeStruct((S, A), BF16), jax.ShapeDtypeStruct((S, Dm - A), BF16)],
        compiler_params=_params(48),
    )(gb, wo)


ATT_TILE = BLK * max(DILATIONS)


def _head0(shape):
    return lax.broadcasted_iota(jnp.int32, shape, 1) < HEAD_DIM


def _pair_col(v, sel):
    return jnp.max(jnp.where(sel, v, -jnp.inf), axis=-1, keepdims=True)


def _stack_heads(x, h0):
    zero = jnp.zeros_like(x)
    return jnp.concatenate([jnp.where(h0, x, zero), jnp.where(h0, zero, x)], axis=0)


def _band(lo, hi):
    qi = lax.broadcasted_iota(jnp.int32, (BLK, 2 * BLK), 0)
    kj = lax.broadcasted_iota(jnp.int32, (BLK, 2 * BLK), 1)
    return (kj >= qi) & (kj <= qi + BLK) & (kj >= lo) & (kj < hi)


def _deinterleave(dst, src, d, rows, dst_stride, dst_off=0, src_off=0, cast=None):
    for r in range(d):
        v = src[pl.ds(src_off + r, rows, stride=d), :] if d > 1 else src[pl.ds(src_off, rows), :]
        dst[pl.ds(r * dst_stride + dst_off, rows), :] = v if cast is None else v.astype(cast)


def _interleave(dst, src, d, rows, src_stride, src_off=0, add=False):
    for r in range(d):
        v = src[pl.ds(r * src_stride + src_off, rows), :]
        idx = pl.ds(r, rows, stride=d) if d > 1 else pl.ds(0, rows)
        dst[idx, :] = dst[idx, :] + v if add else v


def attn_fwd(z, A, name, comm=None):
    S = z.shape[0]
    T = ATT_TILE
    Hp, nt = A // LANES, S // T
    np_ = len(DILATIONS)

    def body(q_ref, k_ref, kp_ref, v_ref, vp_ref, y_ref, lse_ref, qn, kn, vn, qp, kp, vp, accp, mp, lp, *nat):
        accs, ms, ls = nat[:np_], nat[np_:2 * np_], nat[2 * np_:]
        i = pl.program_id(1)
        qn[...] = q_ref[...].astype(F32)
        kn[0:T, :] = kp_ref[...].astype(F32)
        kn[T:2 * T, :] = k_ref[...].astype(F32)
        vn[0:T, :] = vp_ref[...].astype(F32)
        vn[T:2 * T, :] = v_ref[...].astype(F32)
        h0 = _head0((BLK, LANES))
        for pi, d in enumerate(DILATIONS):
            Ld = T // d
            nblk = Ld // BLK
            _deinterleave(qp, qn, d, Ld, Ld, cast=BF16)
            for src, dst in ((kn, kp), (vn, vp)):
                _deinterleave(dst, src, d, Ld, 2 * Ld, cast=BF16)
                _deinterleave(dst, src, d, Ld, 2 * Ld, dst_off=Ld, src_off=T, cast=BF16)

            def unit(u, carry, nblk=nblk):
                r, n = u // nblk, u % nblk
                q0 = pl.multiple_of(u * BLK, BLK)
                k0 = pl.multiple_of((2 * r * nblk + nblk + n - 1) * BLK, BLK)
                qb = qp[pl.ds(q0, BLK), :]
                kw, vw = kp[pl.ds(k0, 2 * BLK), :], vp[pl.ds(k0, 2 * BLK), :]
                mask = _band(jnp.where((i == 0) & (n == 0), BLK, 0), 2 * BLK)
                s = _dot(_stack_heads(qb, h0), kw, NT) * ATTN_SCALE
                s = jnp.where(jnp.concatenate([mask, mask], axis=0), s, NEG_INF)
                mx = jnp.max(s, axis=-1, keepdims=True)
                p = jnp.exp(s - mx)
                l = jnp.sum(p, axis=-1, keepdims=True)
                o = _dot(p.astype(BF16), vw)
                accp[pl.ds(q0, BLK), :] = jnp.where(h0, o[:BLK], o[BLK:])
                mp[pl.ds(q0, BLK), :] = jnp.where(h0, mx[:BLK], mx[BLK:])
                lp[pl.ds(q0, BLK), :] = jnp.where(h0, l[:BLK], l[BLK:])
                return carry

            lax.fori_loop(0, T // BLK, unit, 0, unroll=True)
            for src, dst in ((accp, accs[pi]), (mp, ms[pi]), (lp, ls[pi])):
                _interleave(dst, src, d, Ld, Ld)
        mv = [m[...] for m in ms]
        mx = mv[0]
        for m in mv[1:]:
            mx = jnp.maximum(mx, m)
        ws = [jnp.exp(m - mx) for m in mv]
        l = sum(w * lr[...] for w, lr in zip(ws, ls))
        a = sum(w * ar[...] for w, ar in zip(ws, accs))
        y_ref[...] = (a / l).astype(BF16)
        lse_ref[...] = mx + jnp.log(l)

    kb, vb = A // LANES, 2 * A // LANES
    cur = lambda off: pl.BlockSpec((T, LANES), lambda h, i: (i, off + h))
    prev = lambda off: pl.BlockSpec((T, LANES), lambda h, i: (jnp.maximum(i - 1, 0), off + h))
    out = pl.BlockSpec((T, LANES), lambda h, i: (i, h))
    vm = lambda rows, dt: pltpu.VMEM((rows, LANES), dt)
    (y, lse), got = _call(
        body, name=name, grid=(Hp, nt),
        in_specs=[cur(0), cur(kb), prev(kb), cur(vb), prev(vb)], out_specs=[out, out],
        out_shape=[jax.ShapeDtypeStruct((S, A), BF16), jax.ShapeDtypeStruct((S, A), F32)],
        scratch=[vm(T, F32), vm(2 * T, F32), vm(2 * T, F32), vm(T, BF16), vm(2 * T, BF16), vm(2 * T, BF16),
                 vm(T, F32), vm(T, F32), vm(T, F32)] + [vm(T, F32)] * (3 * np_),
        args=(z, z, z, z, z), vmem=48, comm=comm)
    return y, lse, got


def attn_bwd(z, dy, y, lse, A, name, comm=None):
    S = z.shape[0]
    T = ATT_TILE
    Hp, nt = A // LANES, S // T

    def body(q_ref, k_ref, kp_ref, v_ref, vp_ref, do_ref, y_ref, ls_ref, dq_ref, dk_ref, dv_ref,
             qn, don, dln, kn, vn, qp, dop, lsp, dlp, kp, vp, dqp, dkp, dvp, dqa, dka, dva, dkc, dvc):
        step = pl.program_id(1)
        i = nt - 1 - step
        h0t = _head0((T, LANES))
        dof = do_ref[...].astype(F32)
        prod = dof * y_ref[...].astype(F32)
        d0 = jnp.sum(jnp.where(h0t, prod, 0.0), axis=-1, keepdims=True)
        d1 = jnp.sum(jnp.where(h0t, 0.0, prod), axis=-1, keepdims=True)
        qn[...] = q_ref[...].astype(F32)
        don[...] = dof
        dln[...] = jnp.where(h0t, d0, d1)
        kn[0:T, :] = kp_ref[...].astype(F32)
        kn[T:2 * T, :] = k_ref[...].astype(F32)
        vn[0:T, :] = vp_ref[...].astype(F32)
        vn[T:2 * T, :] = v_ref[...].astype(F32)

        @pl.when(step == 0)
        def _():
            dka[...] = jnp.zeros_like(dka)
            dva[...] = jnp.zeros_like(dva)

        @pl.when(step > 0)
        def _():
            dka[...] = dkc[...]
            dva[...] = dvc[...]

        h0 = _head0((BLK, LANES))
        for pi, d in enumerate(DILATIONS):
            Ld = T // d
            nblk = Ld // BLK
            for src, dst, cast in ((qn, qp, BF16), (don, dop, BF16), (dln, dlp, None)):
                _deinterleave(dst, src, d, Ld, Ld, cast=cast)
            _deinterleave(lsp, ls_ref, d, Ld, Ld)
            for src, dst in ((kn, kp), (vn, vp)):
                _deinterleave(dst, src, d, Ld, 2 * Ld, cast=BF16)
                _deinterleave(dst, src, d, Ld, 2 * Ld, dst_off=Ld, src_off=T, cast=BF16)
            dkp[...] = jnp.zeros_like(dkp)
            dvp[...] = jnp.zeros_like(dvp)

            def unit(u, carry, nblk=nblk):
                r, b = u // nblk, u % nblk
                q0 = pl.multiple_of(u * BLK, BLK)
                k0 = pl.multiple_of((2 * r * nblk + nblk + b - 1) * BLK, BLK)
                mask = _band(jnp.where((i == 0) & (b == 0), BLK, 0), 2 * BLK)
                qb, dob = qp[pl.ds(q0, BLK), :], dop[pl.ds(q0, BLK), :]
                lsb, dlb = lsp[pl.ds(q0, BLK), :], dlp[pl.ds(q0, BLK), :]
                kw, vw = kp[pl.ds(k0, 2 * BLK), :], vp[pl.ds(k0, 2 * BLK), :]
                qs, dos = _stack_heads(qb, h0), _stack_heads(dob, h0)
                lse = jnp.concatenate([_pair_col(lsb, h0), _pair_col(lsb, ~h0)], axis=0)
                delta = jnp.concatenate([_pair_col(dlb, h0), _pair_col(dlb, ~h0)], axis=0)
                s = _dot(qs, kw, NT) * ATTN_SCALE
                p = jnp.where(jnp.concatenate([mask, mask], axis=0), jnp.exp(s - lse), 0.0)
                ds = (p * (_dot(dos, vw, NT) - delta)).astype(BF16)
                dq = _dot(ds, kw) * ATTN_SCALE
                dqp[pl.ds(q0, BLK), :] = jnp.where(h0, dq[:BLK], dq[BLK:])
                dkp[pl.ds(k0, 2 * BLK), :] += _dot(ds, qs, TN) * ATTN_SCALE
                dvp[pl.ds(k0, 2 * BLK), :] += _dot(p.astype(BF16), dos, TN)
                return carry

            lax.fori_loop(0, T // BLK, unit, 0, unroll=True)
            _interleave(dqa, dqp, d, Ld, Ld, add=pi > 0)
            for acc, nxt, part in ((dka, dkc, dkp), (dva, dvc, dvp)):
                _interleave(acc, part, d, Ld, 2 * Ld, src_off=Ld, add=True)
                _interleave(nxt, part, d, Ld, 2 * Ld, add=pi > 0)
        dq_ref[...] = dqa[...].astype(BF16)
        dk_ref[...] = dka[...].astype(BF16)
        dv_ref[...] = dva[...].astype(BF16)

    kb, vb = A // LANES, 2 * A // LANES
    cur = lambda off: pl.BlockSpec((T, LANES), lambda h, s: (nt - 1 - s, off + h))
    prev = lambda off: pl.BlockSpec((T, LANES), lambda h, s: (jnp.maximum(nt - 2 - s, 0), off + h))
    vm = lambda rows, dt: pltpu.VMEM((rows, LANES), dt)
    shp = jax.ShapeDtypeStruct((S, A), BF16)
    return _call(
        body, name=name, grid=(Hp, nt),
        in_specs=[cur(0), cur(kb), prev(kb), cur(vb), prev(vb), cur(0), cur(0), cur(0)],
        out_specs=[cur(0)] * 3, out_shape=[shp] * 3,
        scratch=[vm(T, F32), vm(T, F32), vm(T, F32), vm(2 * T, F32), vm(2 * T, F32),
                 vm(T, BF16), vm(T, BF16), vm(T, F32), vm(T, F32), vm(2 * T, BF16), vm(2 * T, BF16),
                 vm(T, F32), vm(2 * T, F32), vm(2 * T, F32)] + [vm(T, F32)] * 5,
        args=(z, z, z, z, z, dy, y, lse), vmem=56, comm=comm)


def _shift_down(e, k):
    return pltpu.roll(e, k, 0)


def _shift_up(e, k):
    return pltpu.roll(e, e.shape[0] - k, 0)


def _causal_sums(e, g):
    for lvl in range(g + 1):
        e = e + _shift_down(e, 1 << lvl)
    return e


def _anticausal_sums(e, g):
    for lvl in range(g + 1):
        e = e + _shift_up(e, 1 << lvl)
    return e


def _mixer_specs(S, tm, A, C, P):
    assert (3 * A) % C == 0 and (3 * A + 3 * C) % P == 0 and tm % HALO == 0
    cb, pb = 3 * A // C, (3 * A + 3 * C) // P
    hb = tm // HALO
    last = S // HALO - 1
    col = lambda w, j: pl.BlockSpec((tm, w), lambda i: (i, j))
    prev = lambda w, j: pl.BlockSpec((HALO, w), lambda i: (jnp.maximum(i * hb - 1, 0), j))
    nxt = lambda w, j: pl.BlockSpec((HALO, w), lambda i: (jnp.minimum((i + 1) * hb, last), j))
    return cb, pb, col, prev, nxt


def _conv_taps(gc, ci, gch, cih, keep_prev):
    u = gc * ci
    e = jnp.concatenate([gch * cih * keep_prev, u], axis=0)
    return u, _shift_down(e, 1)[HALO:], _shift_down(e, 2)[HALO:]


def _pooled(xp, xph, keep_prev, pos, g):
    cols = slice(g * POOL_GROUP, (g + 1) * POOL_GROUP)
    x = xp[:, cols]
    e = jnp.concatenate([xph[:, cols] * keep_prev, x], axis=0)
    cnt = jnp.minimum(pos + 1, POOL_WINDOWS[g]).astype(F32)
    return _causal_sums(e, g)[HALO:] / cnt - x


def convpool_fwd(z, conv_w, pool_w, pool_scale, A, C, P, name):
    S = z.shape[0]
    tm = _tile(S, 512)
    cb, pb, col, prev, _ = _mixer_specs(S, tm, A, C, P)

    def body(gb_ref, gc_ref, ci_ref, xp_ref, gch_ref, cih_ref, xph_ref, cw_ref, pw_ref, ps_ref, y_ref):
        i = pl.program_id(0)
        keep_prev = jnp.where(i == 0, 0.0, 1.0)
        f = lambda r: r[...].astype(F32)
        u, u1, u2 = _conv_taps(f(gc_ref), f(ci_ref), f(gch_ref), f(cih_ref), keep_prev)
        cw = cw_ref[...]
        y_ref[:, 0:C] = (f(gb_ref) * (cw[0:1] * u2 + cw[1:2] * u1 + cw[2:3] * u)).astype(BF16)
        xp, xph = f(xp_ref), f(xph_ref)
        pos = i * tm + lax.broadcasted_iota(jnp.int32, (tm, 1), 0)
        for g in range(len(POOL_WINDOWS)):
            cols = slice(g * POOL_GROUP, (g + 1) * POOL_GROUP)
            lin = _dot(_pooled(xp, xph, keep_prev, pos, g).astype(BF16), pw_ref[g])
            y_ref[:, C + g * POOL_GROUP:C + (g + 1) * POOL_GROUP] = (lin * ps_ref[:, cols]).astype(BF16)

    full = lambda shape: pl.BlockSpec(shape, lambda i: (0,) * len(shape))
    return pl.pallas_call(
        body, name=name, grid=(S // tm,),
        in_specs=[col(C, cb), col(C, cb + 1), col(C, cb + 2), col(P, pb),
                  prev(C, cb + 1), prev(C, cb + 2), prev(P, pb),
                  full(conv_w.shape), full(pool_w.shape), full(pool_scale.shape)],
        out_specs=pl.BlockSpec((tm, C + P), lambda i: (i, 0)),
        out_shape=jax.ShapeDtypeStruct((S, C + P), BF16), compiler_params=_params(48),
    )(z, z, z, z, z, z, z, conv_w, pool_w, pool_scale)


def convpool_bwd(z, dycp, conv_w, pool_w, pool_scale, A, C, P, name):
    S = z.shape[0]
    tm = _tile(S, 512)
    nt = S // tm
    cb, pb, col, prev, nxt = _mixer_specs(S, tm, A, C, P)
    NG = len(POOL_WINDOWS)

    def body(gb_ref, gc_ref, ci_ref, xp_ref, gch_ref, cih_ref, xph_ref, gbn_ref, dy_ref, dyn_ref,
             cw_ref, pw_ref, ps_ref, dz_ref, dcw_ref, dpw_ref, dps_ref):
        i = pl.program_id(0)

        @pl.when(i == 0)
        def _():
            dcw_ref[...] = jnp.zeros_like(dcw_ref)
            dpw_ref[...] = jnp.zeros_like(dpw_ref)
            dps_ref[...] = jnp.zeros_like(dps_ref)

        keep_prev = jnp.where(i == 0, 0.0, 1.0)
        keep_next = jnp.where(i == nt - 1, 0.0, 1.0)
        f = lambda r: r[...].astype(F32)
        gb, gc, ci = f(gb_ref), f(gc_ref), f(ci_ref)
        u, u1, u2 = _conv_taps(gc, ci, f(gch_ref), f(cih_ref), keep_prev)
        cw = cw_ref[...]
        dy, dyn = f(dy_ref), f(dyn_ref) * keep_next
        dyc = dy[:, 0:C]
        dz_ref[:, 0:C] = (dyc * (cw[0:1] * u2 + cw[1:2] * u1 + cw[2:3] * u)).astype(BF16)
        dc = dyc * gb
        e = jnp.concatenate([dc, dyn[:, 0:C] * f(gbn_ref)], axis=0)
        du = cw[2:3] * dc + cw[1:2] * _shift_up(e, 1)[:tm] + cw[0:1] * _shift_up(e, 2)[:tm]
        dz_ref[:, C:2 * C] = (du * ci).astype(BF16)
        dz_ref[:, 2 * C:3 * C] = (du * gc).astype(BF16)
        dcw_ref[0:1, :] += jnp.sum(dc * u2, axis=0, keepdims=True)
        dcw_ref[1:2, :] += jnp.sum(dc * u1, axis=0, keepdims=True)
        dcw_ref[2:3, :] += jnp.sum(dc * u, axis=0, keepdims=True)

        xp, xph = f(xp_ref), f(xph_ref)
        pos = i * tm + lax.broadcasted_iota(jnp.int32, (tm, 1), 0)
        pos_e = i * tm + lax.broadcasted_iota(jnp.int32, (tm + HALO, 1), 0)
        for g in range(NG):
            cols = slice(g * POOL_GROUP, (g + 1) * POOL_GROUP)
            ycols = slice(C + g * POOL_GROUP, C + (g + 1) * POOL_GROUP)
            pooled = _pooled(xp, xph, keep_prev, pos, g).astype(BF16)
            pw = pw_ref[g]
            dyp = dy[:, ycols]
            dps_ref[:, cols] += jnp.sum(dyp * _dot(pooled, pw), axis=0, keepdims=True)
            dlin = (jnp.concatenate([dyp, dyn[:, ycols]], axis=0) * ps_ref[:, cols]).astype(BF16)
            dpw_ref[g] += _dot(pooled, dlin[:tm], TN)
            dpool = _dot(dlin, pw, NT)
            r = dpool / jnp.minimum(pos_e + 1, POOL_WINDOWS[g]).astype(F32)
            dz_ref[:, 3 * C + g * POOL_GROUP:3 * C + (g + 1) * POOL_GROUP] = (
                _anticausal_sums(r, g)[:tm] - dpool[:tm]).astype(BF16)

    full = lambda shape: pl.BlockSpec(shape, lambda i: (0,) * len(shape))
    return pl.pallas_call(
        body, name=name, grid=(nt,),
        in_specs=[col(C, cb), col(C, cb + 1), col(C, cb + 2), col(P, pb),
                  prev(C, cb + 1), prev(C, cb + 2), prev(P, pb), nxt(C, cb),
                  pl.BlockSpec((tm, C + P), lambda i: (i, 0)),
                  pl.BlockSpec((HALO, C + P), lambda i: (jnp.minimum((i + 1) * (tm // HALO), S // HALO - 1), 0)),
                  full(conv_w.shape), full(pool_w.shape), full(pool_scale.shape)],
        out_specs=[pl.BlockSpec((tm, 3 * C + P), lambda i: (i, 0)),
                   full(conv_w.shape), full(pool_w.shape), full(pool_scale.shape)],
        out_shape=[jax.ShapeDtypeStruct((S, 3 * C + P), BF16), jax.ShapeDtypeStruct(conv_w.shape, F32),
                   jax.ShapeDtypeStruct(pool_w.shape, F32), jax.ShapeDtypeStruct(pool_scale.shape, F32)],
        compiler_params=_params(48),
    )(z, z, z, z, z, z, z, z, dycp, dycp, conv_w, pool_w, pool_scale)


def adamw(lands, w, m, v, name):
    R, C = w.shape
    nl = len(lands)
    n, Rl, _ = lands[0].shape
    assert Rl * nl == R
    tr = _tile(Rl, 128 * 1024 // C)
    nb = Rl // tr
    c1 = 1.0 - ADAM_B1 ** ADAM_STEP
    c2 = 1.0 - ADAM_B2 ** ADAM_STEP

    def body(*refs):
        land_refs = refs[:nl]
        w_ref, m_ref, v_ref, g_ref, d_ref, mo_ref, vo_ref = refs[nl:]
        i = pl.program_id(0)
        for a, land_ref in enumerate(land_refs):
            @pl.when((i >= a * nb) & (i < (a + 1) * nb))
            def _():
                g = land_ref[0].astype(F32)
                for s in range(1, n):
                    g = g + land_ref[s].astype(F32)
                g_ref[...] = g
                mn = ADAM_B1 * m_ref[...] + (1.0 - ADAM_B1) * g
                vn = ADAM_B2 * v_ref[...] + (1.0 - ADAM_B2) * (g * g)
                mo_ref[...] = mn
                vo_ref[...] = vn
                d_ref[...] = -ADAM_LR * ((mn / c1) / (jnp.sqrt(vn / c2) + ADAM_EPS) + ADAM_WD * w_ref[...])

    land_specs = [pl.BlockSpec((n, tr, C), lambda i, a=a: (0, jnp.clip(i - a * nb, 0, nb - 1), 0))
                  for a in range(nl)]
    row = pl.BlockSpec((tr, C), lambda i: (i, 0))
    shp = jax.ShapeDtypeStruct((R, C), F32)
    return pl.pallas_call(
        body, name=name, grid=(nl * nb,),
        in_specs=land_specs + [row, row, row], out_specs=[row] * 4, out_shape=[shp] * 4,
        compiler_params=_params(48),
    )(*lands, w, m, v)


def _gather_comm(arrs):
    return ("gather", arrs) if arrs else None


def _ffn_bwd(g, gb, saved, gain, wg, wu, wd, tag, defer_wu=False):
    x, ht, a, b, hid = saved
    da, db = ffn_dhid(gb, wd, a, b, f"ffn_dhid_{tag}")
    dwd = grad_shared_rhs(hid, gb, None, FFN_RESIDUAL, f"ffn_dwd_{tag}")
    dwg, (land_wd,) = grad_lhs(ht, da, True, f"ffn_dwg_{tag}", ("exchange", [dwd]))
    if defer_wu:
        land_wu, _ = grad_lhs(ht, db, True, f"ffn_dwu_{tag}")
        dh, (land_wg,) = back_proj([da, db], [wg, wu], True, f"ffn_dh_{tag}", ("exchange", [dwg]))
    else:
        dwu, (land_wg,) = grad_lhs(ht, db, True, f"ffn_dwu_{tag}", ("exchange", [dwg]))
        dh, (land_wu,) = back_proj([da, db], [wg, wu], True, f"ffn_dh_{tag}", ("exchange", [dwu]))
    g, gb, dgain = rms_bwd(x, gain, dh, g, f"rms_bwd_{tag}")
    return g, gb, dgain, (land_wg, land_wu, land_wd)


def _mixer_fwd(x, h, ht, next_gain, w_in, conv_w, pool_w, pool_scale, w_out, dims, tag, carry):
    A, C, P = dims
    z, got_a = proj_cols(h, w_in, f"in_proj_{tag}", _gather_comm(carry[:1]))
    ya, lse, got_b = attn_fwd(z, A, f"attn_fwd_{tag}", _gather_comm(carry[1:]))
    ycp = convpool_fwd(z, conv_w, pool_w, pool_scale, A, C, P, f"convpool_{tag}")
    xo, hn, htn = out_proj(ya, ycp, w_out, x, next_gain, f"out_proj_{tag}")
    return xo, hn, htn, (x, ht, z, lse, ya, ycp), got_a + got_b


def _mixer_bwd(g, gb, saved, gain, w_in, conv_w, pool_w, pool_scale, w_out, dims, tag, also_send):
    A, C, P = dims
    x, ht, z, lse, ya, ycp = saved
    S, D = x.shape
    dya, dycp = out_proj_bwd(gb, w_out, A, f"out_proj_bwd_{tag}")
    dwo_a = grad_shared_rhs(ya, gb, 1, 1.0, f"dwo_attn_{tag}")
    dwo_cp = grad_shared_rhs(ycp, gb, 1, 1.0, f"dwo_cp_{tag}")
    dwo = jnp.concatenate([dwo_a[0], dwo_cp[0]], axis=0).reshape(NDEV, -1, D)
    dqkv, (land_wo, land_also) = attn_bwd(z, dya, ya, lse, A, f"attn_bwd_{tag}", ("exchange", [dwo, also_send]))
    dz_cp, dcw, dpw, dps = convpool_bwd(z, dycp, conv_w, pool_w, pool_scale, A, C, P, f"convpool_bwd_{tag}")
    dz = jnp.concatenate(dqkv + [dz_cp], axis=1)
    dwin, _ = grad_lhs(ht, dz, False, f"dwin_{tag}")
    dh, (land_win,) = back_proj([dz], [w_in], False, f"in_proj_bwd_{tag}", ("exchange", [dwin]))
    g, gb, dgain = rms_bwd(x, gain, dh, g, f"rms_bwd_{tag}")
    return g, gb, (dgain, dcw, dpw, dps), (land_win, land_wo, land_also)


def _pack(arrs):
    flat = [a.reshape(-1).astype(F32) for a in arrs]
    spans, off = [], 0
    for a in flat:
        spans.append((off, a.shape[0]))
        off += a.shape[0]
    rows = -(-off // (8 * LANES)) * 8
    buf = jnp.concatenate(flat + [jnp.zeros((rows * LANES - off,), F32)]).reshape(rows, LANES)
    return buf, spans


def kernel(x, ffn1_norm, ffn1_w_gate, ffn1_w_up, ffn1_w_down, mix_norm, w_in, conv_w, pool_w, pool_scale, w_out, ffn2_norm, ffn2_w_gate, ffn2_w_up, ffn2_w_down, final_norm, loss_target, m_ffn1_norm, m_ffn1_w_gate, m_ffn1_w_up, m_ffn1_w_down, m_mix_norm, m_w_in, m_conv_w, m_pool_w, m_pool_scale, m_w_out, m_ffn2_norm, m_ffn2_w_gate, m_ffn2_w_up, m_ffn2_w_down, m_final_norm, v_ffn1_norm, v_ffn1_w_gate, v_ffn1_w_up, v_ffn1_w_down, v_mix_norm, v_w_in, v_conv_w, v_pool_w, v_pool_scale, v_w_out, v_ffn2_norm, v_ffn2_w_gate, v_ffn2_w_up, v_ffn2_w_down, v_final_norm):
    depth = ffn1_norm.shape[0]
    S, D = x.shape[1], x.shape[2]
    Cb = conv_w.shape[2]
    C = Cb * NDEV
    P = pool_scale.shape[1]
    A = (w_in.shape[2] * NDEV - 3 * C - P) // 3
    dims = (A, C, P)
    me = 4 * lax.axis_index("x") + 2 * lax.axis_index("y") + lax.axis_index("c")

    big = dict(ffn1_w_gate=ffn1_w_gate, ffn1_w_up=ffn1_w_up, ffn1_w_down=ffn1_w_down, w_in=w_in, w_out=w_out,
               ffn2_w_gate=ffn2_w_gate, ffn2_w_up=ffn2_w_up, ffn2_w_down=ffn2_w_down)
    big_m = dict(ffn1_w_gate=m_ffn1_w_gate, ffn1_w_up=m_ffn1_w_up, ffn1_w_down=m_ffn1_w_down, w_in=m_w_in,
                 w_out=m_w_out, ffn2_w_gate=m_ffn2_w_gate, ffn2_w_up=m_ffn2_w_up, ffn2_w_down=m_ffn2_w_down)
    big_v = dict(ffn1_w_gate=v_ffn1_w_gate, ffn1_w_up=v_ffn1_w_up, ffn1_w_down=v_ffn1_w_down, w_in=v_w_in,
                 w_out=v_w_out, ffn2_w_gate=v_ffn2_w_gate, ffn2_w_up=v_ffn2_w_up, ffn2_w_down=v_ffn2_w_down)
    names = list(big)

    first_names = ["ffn1_w_gate", "ffn1_w_up"]
    mix_names = ["w_in", "w_out"]
    ffn2_names = ["ffn2_w_gate", "ffn2_w_up", "ffn2_w_down"]
    shards = lambda l, ns: [big[n][l].astype(BF16) for n in ns]
    full = [dict() for _ in range(depth)]
    full[0].update(zip(first_names, all_gather(shards(0, first_names), "gather_first")))
    conv_full = all_gather([jnp.pad(conv_w.reshape(-1, Cb), ((0, 2), (0, LANES - Cb)))], "gather_conv_w")[0]
    conv_full = jnp.transpose(conv_full[:, :depth * 3, :Cb].reshape(NDEV, depth, 3, Cb), (1, 2, 0, 3)).reshape(depth, 3, C)
    pool_w_bf = pool_w.astype(BF16)

    xs = x[0]
    saved = []
    h, ht = rms_fwd(xs, ffn1_norm[0:1], "rms_f1_l0")
    for l in range(depth):
        W = full[l]
        last = l + 1 == depth
        up1 = ["ffn1_w_down"] + (mix_names if l == 0 else [])
        (a, b, hid), got = ffn_up(h, W["ffn1_w_gate"], W["ffn1_w_up"], f"ffn_up_f1_l{l}", _gather_comm(shards(l, up1)))
        W.update(zip(up1, got))
        s1 = (xs, ht, a, b, hid)
        (xs, h, ht), _ = ffn_down(hid, W["ffn1_w_down"], xs, f"ffn_down_f1_l{l}", next_gain=mix_norm[l:l + 1])
        xs, h, ht, s2, got = _mixer_fwd(xs, h, ht, ffn2_norm[l:l + 1], W["w_in"], conv_full[l], pool_w_bf[l],
                                        pool_scale[l:l + 1], W["w_out"].reshape(-1, D), dims, f"mix_l{l}",
                                        shards(l, ffn2_names[:2]))
        W.update(zip(ffn2_names[:2], got))
        got = ffn_up(h, W["ffn2_w_gate"], W["ffn2_w_up"], f"ffn_up_f2_l{l}",
                     _gather_comm(shards(l, ffn2_names[2:]) + ([] if last else shards(l + 1, first_names))))
        (a, b, hid), (W["ffn2_w_down"], *got_next) = got
        s3 = (xs, ht, a, b, hid)
        outs, got_down = ffn_down(hid, W["ffn2_w_down"], xs, f"ffn_down_f2_l{l}",
                                  _gather_comm([] if last else shards(l + 1, mix_names)),
                                  next_gain=None if last else ffn1_norm[l + 1:l + 2])
        if last:
            xs, = outs
        else:
            xs, h, ht = outs
            full[l + 1].update(zip(first_names, got_next))
            full[l + 1].update(zip(mix_names, got_down))
        saved.append((s1, s2, s3))
    loss_part, g, gb, d_final = final_loss(xs, final_norm.reshape(1, D), loss_target[0], "final_loss")

    small = [None] * depth
    lands = [None] * depth
    for l in reversed(range(depth)):
        W = full[l]
        s1, s2, s3 = saved[l]
        g, gb, dn2, (lg2, dwu2, ld2) = _ffn_bwd(g, gb, s3, ffn2_norm[l:l + 1], W["ffn2_w_gate"], W["ffn2_w_up"],
                                                W["ffn2_w_down"], f"f2_l{l}", defer_wu=True)
        g, gb, (dnm, dcw, dpw, dps), (lwin, lwo, lu2) = _mixer_bwd(
            g, gb, s2, mix_norm[l:l + 1], W["w_in"], conv_full[l], pool_w_bf[l], pool_scale[l:l + 1],
            W["w_out"].reshape(-1, D), dims, f"mix_l{l}", dwu2)
        g, gb, dn1, (lg1, lu1, ld1) = _ffn_bwd(g, gb, s1, ffn1_norm[l:l + 1], W["ffn1_w_gate"], W["ffn1_w_up"],
                                               W["ffn1_w_down"], f"f1_l{l}")
        lands[l] = dict(ffn1_w_gate=lg1, ffn1_w_up=lu1, ffn1_w_down=ld1, w_in=lwin, w_out=lwo,
                        ffn2_w_gate=lg2, ffn2_w_up=lu2, ffn2_w_down=ld2)
        small[l] = (dn1, dnm, dcw, dpw, dps, dn2)

    res = {}
    for n in names:
        shp = big[n].shape
        two = lambda t: t.reshape(-1, shp[-1])
        outs = adamw([lands[l][n] for l in range(depth)], two(big[n]), two(big_m[n]), two(big_v[n]), f"adamw_{n}")
        res[n] = [o.reshape(shp) for o in outs]

    st = lambda i: jnp.stack([small[l][i] for l in range(depth)])
    small_g = dict(ffn1_norm=st(0), mix_norm=st(1), conv_w=st(2), pool_w=st(3), pool_scale=st(4), ffn2_norm=st(5),
                   final_norm=d_final)
    small_names = list(small_g)
    zeros_conv = jnp.zeros((depth, 3, C), F32)
    place = lambda t: lax.dynamic_update_slice(zeros_conv, t, (0, 0, me * Cb))
    small_w = dict(ffn1_norm=ffn1_norm, mix_norm=mix_norm, conv_w=place(conv_w), pool_w=pool_w, pool_scale=pool_scale,
                   ffn2_norm=ffn2_norm, final_norm=final_norm)
    small_m = dict(ffn1_norm=m_ffn1_norm, mix_norm=m_mix_norm, conv_w=place(m_conv_w), pool_w=m_pool_w,
                   pool_scale=m_pool_scale, ffn2_norm=m_ffn2_norm, final_norm=m_final_norm)
    small_v = dict(ffn1_norm=v_ffn1_norm, mix_norm=v_mix_norm, conv_w=place(v_conv_w), pool_w=v_pool_w,
                   pool_scale=v_pool_scale, ffn2_norm=v_ffn2_norm, final_norm=v_final_norm)
    gbuf, spans = _pack([small_g[n] for n in small_names] + [loss_part])
    wbuf, _ = _pack([small_w[n] for n in small_names] + [jnp.zeros((1, LANES), F32)])
    mbuf, _ = _pack([small_m[n] for n in small_names] + [jnp.zeros((1, LANES), F32)])
    vbuf, _ = _pack([small_v[n] for n in small_names] + [jnp.zeros((1, LANES), F32)])
    gathered = all_gather([gbuf], "gather_small_grads")[0]
    outs = adamw([gathered], wbuf, mbuf, vbuf, "adamw_small")
    for n, (off, size) in zip(small_names, spans):
        shp = small_w[n].shape
        vals = [o.reshape(-1)[off:off + size].reshape(shp) for o in outs]
        if n == "conv_w":
            vals = [lax.dynamic_slice(t, (0, 0, me * Cb), (depth, 3, Cb)) for t in vals]
        res[n] = vals
    loss = outs[0].reshape(-1)[spans[-1][0]]

    order = ["ffn1_norm", "ffn1_w_gate", "ffn1_w_up", "ffn1_w_down", "mix_norm", "w_in", "conv_w", "pool_w",
             "pool_scale", "w_out", "ffn2_norm", "ffn2_w_gate", "ffn2_w_up", "ffn2_w_down", "final_norm"]
    return (loss, g[None], *[res[n][0] for n in order], *[res[n][1] for n in order],
            *[res[n][2] for n in order], *[res[n][3] for n in order])
```

```python
import functools

import jax
import jax.numpy as jnp
from jax import lax
from jax.experimental import pallas as pl
from jax.experimental.pallas import tpu as pltpu

F32 = jnp.float32
BF16 = jnp.bfloat16
NDEV = 8
HEAD_DIM = 64
LANES = 128
BLK = 128
DILATIONS = (1, 4, 16)
POOL_WINDOWS = (2, 4, 8, 16)
POOL_GROUP = 128
HALO = 16
FFN_RESIDUAL = 0.5
RMS_EPS = 1e-6
NEG_INF = -1e30
ATTN_SCALE = HEAD_DIM ** -0.5
ADAM_LR, ADAM_B1, ADAM_B2, ADAM_EPS, ADAM_WD, ADAM_STEP = 0.001, 0.9, 0.999, 1e-08, 0.01, 10
VMEM_BYTES = 64 * 1024 * 1024
MESH = pl.DeviceIdType.MESH

NN = (((1,), (0,)), ((), ()))
NT = (((1,), (1,)), ((), ()))
TN = (((0,), (0,)), ((), ()))


def _dot(a, b, dims=NN):
    return lax.dot_general(a, b, dims, preferred_element_type=F32)


def _params(vmem_mb=48):
    return pltpu.CompilerParams(vmem_limit_bytes=min(vmem_mb * 1024 * 1024, VMEM_BYTES - 4 * 1024 * 1024))


def _tile(n, want):
    t = min(n, max(16, want // 16 * 16))
    while t > 16 and (n % t or t % 16):
        t -= 16
    return t if n % t == 0 else n


def _coords():
    return lax.axis_index("x"), lax.axis_index("y"), lax.axis_index("c")


def _comm_sems(n):
    return [pltpu.SemaphoreType.DMA((7 * n,)), pltpu.SemaphoreType.DMA((7 * n,)), pltpu.SemaphoreType.DMA((n,))]


def _gather_ops(ins, outs, sems):
    n = len(ins)
    send_sems, recv_sems, local_sems = sems
    x, y, c = _coords()
    me, sibling = (x, y, c), (x, y, 1 - c)
    chips = [(1 - x, y), (x, 1 - y), (1 - x, 1 - y)]

    def slot(out, p):
        return out.at[4 * p[0] + 2 * p[1] + p[2]]

    def copy(a, k, block, to, src=None):
        return pltpu.make_async_remote_copy(
            src_ref=slot(outs[a], block) if src is None else src, dst_ref=slot(outs[a], block),
            send_sem=send_sems.at[7 * a + k], recv_sem=recv_sems.at[7 * a + k],
            device_id=to, device_id_type=MESH)

    def own():
        mine = [pltpu.make_async_copy(ins[a], slot(outs[a], me), local_sems.at[a]) for a in range(n)]
        first = []
        for a in range(n):
            first.append(copy(a, 0, me, sibling, src=ins[a]))
            first += [copy(a, 1 + j, me, (*chip, c), src=ins[a]) for j, chip in enumerate(chips)]
        return mine, first

    def start():
        mine, first = own()
        for cp in mine + first:
            cp.start()

    def relay():
        for j, chip in enumerate(chips):
            for a in range(n):
                copy(a, 1 + j, (*chip, c), me).wait_recv()
                copy(a, 4 + j, (*chip, c), sibling).start()

    def finish():
        mine, first = own()
        passed = [copy(a, 4 + j, (*chip, c), sibling) for j, chip in enumerate(chips) for a in range(n)]
        for a in range(n):
            copy(a, 0, sibling, me).wait_recv()
            for j, chip in enumerate(chips):
                copy(a, 4 + j, (*chip, 1 - c), me).wait_recv()
        for cp in first + passed:
            cp.wait_send()
        for cp in mine:
            cp.wait()

    return start, relay, finish


def _exchange_ops(ins, outs, sems):
    n = len(ins)
    send_sems, recv_sems, local_sems = sems
    x, y, c = _coords()
    me = 4 * x + 2 * y + c

    def peer(k):
        return (1 - x if k & 4 else x, 1 - y if k & 2 else y, 1 - c if k & 1 else c)

    def copy(a, k):
        p = peer(k)
        return pltpu.make_async_remote_copy(
            src_ref=ins[a].at[4 * p[0] + 2 * p[1] + p[2]], dst_ref=outs[a].at[me],
            send_sem=send_sems.at[7 * a + k - 1], recv_sem=recv_sems.at[7 * a + k - 1],
            device_id=p, device_id_type=MESH)

    def landed(a, k):
        p = peer(k)
        return pltpu.make_async_remote_copy(
            src_ref=ins[a].at[me], dst_ref=outs[a].at[4 * p[0] + 2 * p[1] + p[2]],
            send_sem=send_sems.at[7 * a + k - 1], recv_sem=recv_sems.at[7 * a + k - 1],
            device_id=p, device_id_type=MESH)

    def own():
        mine = [pltpu.make_async_copy(ins[a].at[me], outs[a].at[me], local_sems.at[a]) for a in range(n)]
        return mine, [copy(a, k) for a in range(n) for k in range(1, NDEV)]

    def start():
        mine, sent = own()
        for cp in mine + sent:
            cp.start()

    def finish():
        mine, sent = own()
        for a in range(n):
            for k in range(1, NDEV):
                landed(a, k).wait_recv()
        for cp in sent:
            cp.wait_send()
        for cp in mine:
            cp.wait()

    return start, None, finish


_COMM = {"gather": (_gather_ops, lambda a: (NDEV,) + a.shape), "exchange": (_exchange_ops, lambda a: a.shape)}


def all_gather(arrs, name):
    n = len(arrs)

    def body(*refs):
        start, relay, finish = _gather_ops(refs[:n], refs[n:2 * n], refs[2 * n:])
        start()
        relay()
        finish()

    any_spec = pl.BlockSpec(memory_space=pl.ANY)
    return pl.pallas_call(
        body, name=name, out_shape=[jax.ShapeDtypeStruct((NDEV,) + a.shape, a.dtype) for a in arrs],
        in_specs=[any_spec] * n, out_specs=[any_spec] * n, scratch_shapes=_comm_sems(n),
    )(*arrs)


def _call(body, *, name, grid, in_specs, out_specs, out_shape, args, scratch=(), vmem=48, comm=None):
    if comm is None:
        outs = pl.pallas_call(
            body, name=name, grid=grid, in_specs=list(in_specs), out_specs=list(out_specs), out_shape=list(out_shape),
            scratch_shapes=list(scratch), compiler_params=_params(vmem))(*args)
        return list(outs), []
    kind, arrs = comm
    ops, shape_of = _COMM[kind]
    n, n_in, n_out, n_scr = len(arrs), len(in_specs), len(out_specs), len(scratch)
    n_steps = functools.reduce(lambda p, g: p * g, grid, 1)
    relay_at = max(n_steps - 1 - max(n_steps // 8, 1), 0)

    def carrier(*refs):
        ins, c_in = refs[:n_in], refs[n_in:n_in + n]
        o0 = n_in + n
        outs, c_out = refs[o0:o0 + n_out], refs[o0 + n_out:o0 + n_out + n]
        s0 = o0 + n_out + n
        scr, sems = refs[s0:s0 + n_scr], refs[s0 + n_scr:]
        ids = [pl.program_id(d) for d in range(len(grid))]
        first = functools.reduce(jnp.logical_and, [i == 0 for i in ids])
        last = functools.reduce(jnp.logical_and, [i == g - 1 for i, g in zip(ids, grid)])
        start, relay, finish = ops(c_in, c_out, sems)
        pl.when(first)(start)
        body(*ins, *outs, *scr)
        if relay is not None:
            step = functools.reduce(lambda acc, ig: acc * ig[1] + ig[0], zip(ids, grid), 0)
            pl.when(step == relay_at)(relay)
        pl.when(last)(finish)

    any_spec = pl.BlockSpec(memory_space=pl.ANY)
    res = pl.pallas_call(
        carrier, name=name, grid=grid, in_specs=list(in_specs) + [any_spec] * n,
        out_specs=list(out_specs) + [any_spec] * n,
        out_shape=list(out_shape) + [jax.ShapeDtypeStruct(shape_of(a), a.dtype) for a in arrs],
        scratch_shapes=list(scratch) + _comm_sems(n), compiler_params=_params(vmem))(*args, *arrs)
    return list(res[:n_out]), list(res[n_out:])


def _rstd(x):
    return lax.rsqrt(jnp.mean(x * x, axis=-1, keepdims=True) + RMS_EPS)


def rms_fwd(x, gain, name, comm=None):
    S, D = x.shape
    tm = _tile(S, 512)

    def body(x_ref, g_ref, h_ref, ht_ref):
        xv = x_ref[...]
        h = xv * _rstd(xv) * g_ref[...]
        h_ref[...] = h.astype(BF16)
        ht_ref[...] = h.T.astype(BF16)

    return _call(
        body, name=name, grid=(S // tm,),
        in_specs=[pl.BlockSpec((tm, D), lambda i: (i, 0)), pl.BlockSpec((1, D), lambda i: (0, 0))],
        out_specs=[pl.BlockSpec((tm, D), lambda i: (i, 0)), pl.BlockSpec((D, tm), lambda i: (0, i))],
        out_shape=[jax.ShapeDtypeStruct((S, D), BF16), jax.ShapeDtypeStruct((D, S), BF16)],
        args=(x, gain), vmem=40, comm=comm)


def _rms_bwd_math(xv, gain, dh):
    r = _rstd(xv)
    xhat = xv * r
    dxhat = dh * gain
    dx = r * (dxhat - xhat * jnp.mean(dxhat * xhat, axis=-1, keepdims=True))
    return dx, dh * xhat


def rms_bwd(x, gain, dh, g, name):
    S, D = x.shape
    tm = _tile(S, 256)

    def body(x_ref, gain_ref, dh_ref, g_ref, go_ref, gb_ref, dg_ref):
        @pl.when(pl.program_id(0) == 0)
        def _():
            dg_ref[...] = jnp.zeros_like(dg_ref)

        dx, dgain = _rms_bwd_math(x_ref[...], gain_ref[...], dh_ref[...].astype(F32))
        gn = g_ref[...] + dx
        go_ref[...] = gn
        gb_ref[...] = gn.astype(BF16)
        dg_ref[...] += jnp.sum(dgain, axis=0, keepdims=True)

    row = pl.BlockSpec((tm, D), lambda i: (i, 0))
    vec = pl.BlockSpec((1, D), lambda i: (0, 0))
    return pl.pallas_call(
        body, name=name, grid=(S // tm,),
        in_specs=[row, vec, row, row], out_specs=[row, row, vec],
        out_shape=[jax.ShapeDtypeStruct((S, D), F32), jax.ShapeDtypeStruct((S, D), BF16),
                   jax.ShapeDtypeStruct((1, D), F32)],
        compiler_params=_params(48),
    )(x, gain, dh, g)


def final_loss(x, gain, target, name):
    S, D = x.shape
    tm = _tile(S, 256)

    def body(x_ref, gain_ref, t_ref, loss_ref, go_ref, gb_ref, dg_ref):
        @pl.when(pl.program_id(0) == 0)
        def _():
            dg_ref[...] = jnp.zeros_like(dg_ref)
            loss_ref[...] = jnp.zeros_like(loss_ref)

        xv, gain_v = x_ref[...], gain_ref[...]
        err = xv * _rstd(xv) * gain_v - t_ref[...]
        loss_ref[...] += jnp.sum(jnp.sum(err * err, axis=-1, keepdims=True), axis=0, keepdims=True) * (0.5 / D)
        dx, dgain = _rms_bwd_math(xv, gain_v, err * (1.0 / D))
        go_ref[...] = dx
        gb_ref[...] = dx.astype(BF16)
        dg_ref[...] += jnp.sum(dgain, axis=0, keepdims=True)

    row = pl.BlockSpec((tm, D), lambda i: (i, 0))
    vec = pl.BlockSpec((1, D), lambda i: (0, 0))
    return pl.pallas_call(
        body, name=name, grid=(S // tm,),
        in_specs=[row, vec, row],
        out_specs=[pl.BlockSpec((1, LANES), lambda i: (0, 0)), row, row, vec],
        out_shape=[jax.ShapeDtypeStruct((1, LANES), F32), jax.ShapeDtypeStruct((S, D), F32),
                   jax.ShapeDtypeStruct((S, D), BF16), jax.ShapeDtypeStruct((1, D), F32)],
        compiler_params=_params(48),
    )(x, gain, target)


def _blk(arr, width, tm, stacked):
    if stacked:
        return pl.BlockSpec((None, tm, width), lambda j, i: (j, i, 0))
    return pl.BlockSpec((tm, width), lambda j, i: (i, j))


def ffn_up(h, wg, wu, name, comm=None):
    S, D = h.shape
    J, _, Fb = wg.shape
    tm = _tile(S, 1024)

    def body(h_ref, wg_ref, wu_ref, da_ref, db_ref, hid_ref):
        hv = h_ref[...]
        a = _dot(hv, wg_ref[...])
        b = _dot(hv, wu_ref[...])
        sg = jax.nn.sigmoid(a)
        silu = a * sg
        da_ref[...] = (b * (sg * (1.0 + a * (1.0 - sg)))).astype(BF16)
        db_ref[...] = silu.astype(BF16)
        hid_ref[...] = (silu * b).astype(BF16)

    w_spec = pl.BlockSpec((None, D, Fb), lambda j, i: (j, 0, 0))
    o_spec = pl.BlockSpec((None, tm, Fb), lambda j, i: (j, i, 0))
    shp = jax.ShapeDtypeStruct((J, S, Fb), BF16)
    return _call(
        body, name=name, grid=(J, S // tm),
        in_specs=[pl.BlockSpec((tm, D), lambda j, i: (i, 0)), w_spec, w_spec],
        out_specs=[o_spec, o_spec, o_spec], out_shape=[shp, shp, shp], args=(h, wg, wu), comm=comm)


def proj_cols(h, w, name, comm=None):
    S, D = h.shape
    J, _, Wb = w.shape
    tm = _tile(S, 1024)
    kb = 2 if Wb % LANES == 0 and J % 2 == 0 else 1

    def body(h_ref, w_ref, z_ref):
        wv = jnp.concatenate([w_ref[k] for k in range(kb)], axis=1)
        z_ref[...] = _dot(h_ref[...], wv).astype(BF16)

    (z,), got = _call(
        body, name=name, grid=(J // kb, S // tm),
        in_specs=[pl.BlockSpec((tm, D), lambda j, i: (i, 0)), pl.BlockSpec((kb, D, Wb), lambda j, i: (j, 0, 0))],
        out_specs=[pl.BlockSpec((tm, kb * Wb), lambda j, i: (i, j))],
        out_shape=[jax.ShapeDtypeStruct((S, J * Wb), BF16)], args=(h, w), comm=comm)
    return z, got


def _norm_outs(S, D, tm, gain):
    if gain is None:
        return [], [], []
    return ([pl.BlockSpec((1, D), lambda i: (0, 0))],
            [pl.BlockSpec((tm, D), lambda i: (i, 0)), pl.BlockSpec((D, tm), lambda i: (0, i))],
            [jax.ShapeDtypeStruct((S, D), BF16), jax.ShapeDtypeStruct((D, S), BF16)])


def _write_norm(xo, gain_ref, h_ref, ht_ref):
    h = xo * _rstd(xo) * gain_ref[...]
    h_ref[...] = h.astype(BF16)
    ht_ref[...] = h.T.astype(BF16)


def ffn_down(hid, wd, x, name, comm=None, next_gain=None):
    J, S, Fb = hid.shape
    D = wd.shape[2]
    tm = _tile(S, 256)

    def body(hid_ref, wd_ref, x_ref, *rest):
        acc = _dot(hid_ref[0], wd_ref[0])
        for j in range(1, J):
            acc = acc + _dot(hid_ref[j], wd_ref[j])
        xo = x_ref[...] + FFN_RESIDUAL * acc
        if next_gain is None:
            rest[0][...] = xo
        else:
            gain_ref, o_ref, h_ref, ht_ref = rest
            o_ref[...] = xo
            _write_norm(xo, gain_ref, h_ref, ht_ref)

    row = pl.BlockSpec((tm, D), lambda i: (i, 0))
    g_in, n_specs, n_shapes = _norm_outs(S, D, tm, next_gain)
    outs, got = _call(
        body, name=name, grid=(S // tm,),
        in_specs=[pl.BlockSpec((J, tm, Fb), lambda i: (0, i, 0)),
                  pl.BlockSpec((J, Fb, D), lambda i: (0, 0, 0), pipeline_mode=pl.Buffered(1)), row] + g_in,
        out_specs=[row] + n_specs, out_shape=[jax.ShapeDtypeStruct((S, D), F32)] + n_shapes,
        args=(hid, wd, x) + (() if next_gain is None else (next_gain,)), vmem=56, comm=comm)
    return outs, got


def ffn_dhid(gb, wd, a, b, name):
    S, D = gb.shape
    J, Fb, _ = wd.shape
    tm = _tile(S, 1024)

    parts = 4 if tm % 64 == 0 else 1

    def body(g_ref, wd_ref, a_ref, b_ref, da_ref, db_ref):
        wdv = wd_ref[...]
        for part in range(parts):
            rows = pl.ds(part * (tm // parts), tm // parts)
            dhid = _dot(g_ref[rows, :], wdv, NT) * FFN_RESIDUAL
            da_ref[rows, :] = (dhid * a_ref[rows, :].astype(F32)).astype(BF16)
            db_ref[rows, :] = (dhid * b_ref[rows, :].astype(F32)).astype(BF16)

    o_spec = pl.BlockSpec((None, tm, Fb), lambda j, i: (j, i, 0))
    shp = jax.ShapeDtypeStruct((J, S, Fb), BF16)
    return pl.pallas_call(
        body, name=name, grid=(J, S // tm),
        in_specs=[pl.BlockSpec((tm, D), lambda j, i: (i, 0)), pl.BlockSpec((None, Fb, D), lambda j, i: (j, 0, 0)),
                  o_spec, o_spec],
        out_specs=[o_spec, o_spec], out_shape=[shp, shp], compiler_params=_params(48),
    )(gb, wd, a, b)


def back_proj(ds, ws, stacked, name, comm=None):
    n = len(ds)
    J, D, Wb = ws[0].shape
    S = ds[0].shape[1] if stacked else ds[0].shape[0]
    kb = J if n == 1 else 2
    nj = J // kb
    tm = _tile(S, 256 if n == 1 else 512)

    def body(*refs):
        d_refs, w_refs, o_ref = refs[:n], refs[n:2 * n], refs[2 * n]
        acc = None
        for d_ref, w_ref in zip(d_refs, w_refs):
            for k in range(kb):
                dk = d_ref[k] if stacked else d_ref[:, k * Wb:(k + 1) * Wb]
                t = _dot(dk, w_ref[k], NT)
                acc = t if acc is None else acc + t
        if nj == 1:
            o_ref[...] = acc.astype(BF16)
        else:
            acc_ref = refs[2 * n + 1]
            j = pl.program_id(1)

            @pl.when(j == 0)
            def _():
                acc_ref[...] = acc

            @pl.when((j > 0) & (j < nj - 1))
            def _():
                acc_ref[...] += acc

            @pl.when(j == nj - 1)
            def _():
                o_ref[...] = (acc_ref[...] + acc).astype(BF16)

    if stacked:
        d_spec = pl.BlockSpec((kb, tm, Wb), lambda i, j: (j, i, 0))
    else:
        d_spec = pl.BlockSpec((tm, kb * Wb), lambda i, j: (i, j))
    w_spec = pl.BlockSpec((kb, D, Wb), lambda i, j: (j, 0, 0), pipeline_mode=pl.Buffered(1) if nj == 1 else None)
    (dh,), got = _call(
        body, name=name, grid=(S // tm, nj),
        in_specs=[d_spec] * n + [w_spec] * n,
        out_specs=[pl.BlockSpec((tm, D), lambda i, j: (i, 0))], out_shape=[jax.ShapeDtypeStruct((S, D), BF16)],
        scratch=[] if nj == 1 else [pltpu.VMEM((tm, D), F32)], args=(*ds, *ws), vmem=56, comm=comm)
    return dh, got


def grad_lhs(ht, d, stacked, name, comm=None):
    D, S = ht.shape
    if stacked:
        J, _, Wb = d.shape
    else:
        J, Wb = NDEV, d.shape[1] // NDEV
    kb = 2 if not stacked and Wb % LANES == 0 and J % 2 == 0 else 1
    tk = _tile(S, 2048 // kb)
    nk = S // tk

    def body(ht_ref, d_ref, o_ref, acc_ref):
        i = pl.program_id(1)

        @pl.when(i == 0)
        def _():
            acc_ref[...] = jnp.zeros_like(acc_ref)

        acc_ref[...] += _dot(ht_ref[...], d_ref[...])

        @pl.when(i == nk - 1)
        def _():
            for k in range(kb):
                o_ref[k] = acc_ref[:, k * Wb:(k + 1) * Wb].astype(BF16)

    d_spec = pl.BlockSpec((None, tk, Wb), lambda j, i: (j, i, 0)) if stacked else pl.BlockSpec((tk, kb * Wb), lambda j, i: (i, j))
    (dw,), got = _call(
        body, name=name, grid=(J // kb, nk),
        in_specs=[pl.BlockSpec((D, tk), lambda j, i: (0, i)), d_spec],
        out_specs=[pl.BlockSpec((kb, D, Wb), lambda j, i: (j, 0, 0))],
        out_shape=[jax.ShapeDtypeStruct((J, D, Wb), BF16)],
        scratch=[pltpu.VMEM((D, kb * Wb), F32)], args=(ht, d), vmem=56, comm=comm)
    return dw, got


def grad_shared_rhs(a, gb, nblk, scale, name):
    S, D = gb.shape
    stacked = nblk is None
    if stacked:
        J, _, Wb = a.shape
    else:
        J, Wb = nblk, a.shape[1] // nblk
    tk = _tile(S, 2048 if Wb <= 768 else 1024)
    nk = S // tk

    def body(a_ref, g_ref, o_ref, acc_ref):
        i = pl.program_id(1)

        @pl.when(i == 0)
        def _():
            acc_ref[...] = jnp.zeros_like(acc_ref)

        acc_ref[...] += _dot(a_ref[...], g_ref[...], TN)

        @pl.when(i == nk - 1)
        def _():
            o_ref[...] = (acc_ref[...] * scale).astype(BF16)

    return pl.pallas_call(
        body, name=name, grid=(J, nk),
        in_specs=[_blk(a, Wb, tk, stacked), pl.BlockSpec((tk, D), lambda j, i: (i, 0))],
        out_specs=pl.BlockSpec((None, Wb, D), lambda j, i: (j, 0, 0)),
        out_shape=jax.ShapeDtypeStruct((J, Wb, D), BF16),
        scratch_shapes=[pltpu.VMEM((Wb, D), F32)], compiler_params=_params(56),
    )(a, gb)


def out_proj(ya, ycp, wo, x, next_gain, name):
    S, A = ya.shape
    Dm, D = wo.shape
    tm = _tile(S, 256)

    def body(ya_ref, ycp_ref, wo_ref, x_ref, gain_ref, o_ref, h_ref, ht_ref):
        xo = x_ref[...] + _dot(ya_ref[...], wo_ref[0:A, :]) + _dot(ycp_ref[...], wo_ref[A:Dm, :])
        o_ref[...] = xo
        _write_norm(xo, gain_ref, h_ref, ht_ref)

    row = pl.BlockSpec((tm, D), lambda i: (i, 0))
    g_in, n_specs, n_shapes = _norm_outs(S, D, tm, next_gain)
    return pl.pallas_call(
        body, name=name, grid=(S // tm,),
        in_specs=[pl.BlockSpec((tm, A), lambda i: (i, 0)), pl.BlockSpec((tm, Dm - A), lambda i: (i, 0)),
                  pl.BlockSpec((Dm, D), lambda i: (0, 0), pipeline_mode=pl.Buffered(1)), row] + g_in,
        out_specs=[row] + n_specs, out_shape=[jax.ShapeDtypeStruct((S, D), F32)] + n_shapes,
        compiler_params=_params(48),
    )(ya, ycp, wo, x, next_gain)


def out_proj_bwd(gb, wo, A, name):
    S, D = gb.shape
    Dm = wo.shape[0]
    tm = _tile(S, 512)

    def body(g_ref, wo_ref, dya_ref, dycp_ref):
        gv = g_ref[...]
        dya_ref[...] = _dot(gv, wo_ref[0:A, :], NT).astype(BF16)
        dycp_ref[...] = _dot(gv, wo_ref[A:Dm, :], NT).astype(BF16)

    return pl.pallas_call(
        body, name=name, grid=(S // tm,),
        in_specs=[pl.BlockSpec((tm, D), lambda i: (i, 0)), pl.BlockSpec((Dm, D), lambda i: (0, 0))],
        out_specs=[pl.BlockSpec((tm, A), lambda i: (i, 0)), pl.BlockSpec((tm, Dm - A), lambda i: (i, 0))],
        out_shape=[jax.ShapeDtypeStruct((S, A), BF16), jax.ShapeDtypeStruct((S, Dm - A), BF16)],
        compiler_params=_params(48),
    )(gb, wo)


ATT_TILE = BLK * max(DILATIONS)


def _head0(shape):
    return lax.broadcasted_iota(jnp.int32, shape, 1) < HEAD_DIM


def _pair_col(v, sel):
    return jnp.max(jnp.where(sel, v, -jnp.inf), axis=-1, keepdims=True)


def _stack_heads(x, h0):
    zero = jnp.zeros_like(x)
    return jnp.concatenate([jnp.where(h0, x, zero), jnp.where(h0, zero, x)], axis=0)


def _band(lo, hi):
    qi = lax.broadcasted_iota(jnp.int32, (BLK, 2 * BLK), 0)
    kj = lax.broadcasted_iota(jnp.int32, (BLK, 2 * BLK), 1)
    return (kj >= qi) & (kj <= qi + BLK) & (kj >= lo) & (kj < hi)


def _deinterleave(dst, src, d, rows, dst_stride, dst_off=0, src_off=0, cast=None):
    for r in range(d):
        v = src[pl.ds(src_off + r, rows, stride=d), :] if d > 1 else src[pl.ds(src_off, rows), :]
        dst[pl.ds(r * dst_stride + dst_off, rows), :] = v if cast is None else v.astype(cast)


def _interleave(dst, src, d, rows, src_stride, src_off=0, add=False):
    for r in range(d):
        v = src[pl.ds(r * src_stride + src_off, rows), :]
        idx = pl.ds(r, rows, stride=d) if d > 1 else pl.ds(0, rows)
        dst[idx, :] = dst[idx, :] + v if add else v


def attn_fwd(z, A, name, comm=None):
    S = z.shape[0]
    T = ATT_TILE
    Hp, nt = A // LANES, S // T
    np_ = len(DILATIONS)

    def body(q_ref, k_ref, kp_ref, v_ref, vp_ref, y_ref, lse_ref, qn, kn, vn, qp, kp, vp, accp, mp, lp, *nat):
        accs, ms, ls = nat[:np_], nat[np_:2 * np_], nat[2 * np_:]
        i = pl.program_id(1)
        qn[...] = q_ref[...].astype(F32)
        kn[0:T, :] = kp_ref[...].astype(F32)
        kn[T:2 * T, :] = k_ref[...].astype(F32)
        vn[0:T, :] = vp_ref[...].astype(F32)
        vn[T:2 * T, :] = v_ref[...].astype(F32)
        h0 = _head0((BLK, LANES))
        for pi, d in enumerate(DILATIONS):
            Ld = T // d
            nblk = Ld // BLK
            _deinterleave(qp, qn, d, Ld, Ld, cast=BF16)
            for src, dst in ((kn, kp), (vn, vp)):
                _deinterleave(dst, src, d, Ld, 2 * Ld, cast=BF16)
                _deinterleave(dst, src, d, Ld, 2 * Ld, dst_off=Ld, src_off=T, cast=BF16)

            def unit(u, carry, nblk=nblk):
                r, n = u // nblk, u % nblk
                q0 = pl.multiple_of(u * BLK, BLK)
                k0 = pl.multiple_of((2 * r * nblk + nblk + n - 1) * BLK, BLK)
                qb = qp[pl.ds(q0, BLK), :]
                kw, vw = kp[pl.ds(k0, 2 * BLK), :], vp[pl.ds(k0, 2 * BLK), :]
                mask = _band(jnp.where((i == 0) & (n == 0), BLK, 0), 2 * BLK)
                s = _dot(_stack_heads(qb, h0), kw, NT) * ATTN_SCALE
                s = jnp.where(jnp.concatenate([mask, mask], axis=0), s, NEG_INF)
                mx = jnp.max(s, axis=-1, keepdims=True)
                p = jnp.exp(s - mx)
                l = jnp.sum(p, axis=-1, keepdims=True)
                o = _dot(p.astype(BF16), vw)
                accp[pl.ds(q0, BLK), :] = jnp.where(h0, o[:BLK], o[BLK:])
                mp[pl.ds(q0, BLK), :] = jnp.where(h0, mx[:BLK], mx[BLK:])
                lp[pl.ds(q0, BLK), :] = jnp.where(h0, l[:BLK], l[BLK:])
                return carry

            lax.fori_loop(0, T // BLK, unit, 0, unroll=True)
            for src, dst in ((accp, accs[pi]), (mp, ms[pi]), (lp, ls[pi])):
                _interleave(dst, src, d, Ld, Ld)
        mv = [m[...] for m in ms]
        mx = mv[0]
        for m in mv[1:]:
            mx = jnp.maximum(mx, m)
        ws = [jnp.exp(m - mx) for m in mv]
        l = sum(w * lr[...] for w, lr in zip(ws, ls))
        a = sum(w * ar[...] for w, ar in zip(ws, accs))
        y_ref[...] = (a / l).astype(BF16)
        lse_ref[...] = mx + jnp.log(l)

    kb, vb = A // LANES, 2 * A // LANES
    cur = lambda off: pl.BlockSpec((T, LANES), lambda h, i: (i, off + h))
    prev = lambda off: pl.BlockSpec((T, LANES), lambda h, i: (jnp.maximum(i - 1, 0), off + h))
    out = pl.BlockSpec((T, LANES), lambda h, i: (i, h))
    vm = lambda rows, dt: pltpu.VMEM((rows, LANES), dt)
    (y, lse), got = _call(
        body, name=name, grid=(Hp, nt),
        in_specs=[cur(0), cur(kb), prev(kb), cur(vb), prev(vb)], out_specs=[out, out],
        out_shape=[jax.ShapeDtypeStruct((S, A), BF16), jax.ShapeDtypeStruct((S, A), F32)],
        scratch=[vm(T, F32), vm(2 * T, F32), vm(2 * T, F32), vm(T, BF16), vm(2 * T, BF16), vm(2 * T, BF16),
                 vm(T, F32), vm(T, F32), vm(T, F32)] + [vm(T, F32)] * (3 * np_),
        args=(z, z, z, z, z), vmem=48, comm=comm)
    return y, lse, got


def attn_bwd(z, dy, y, lse, A, name, comm=None):
    S = z.shape[0]
    T = ATT_TILE
    Hp, nt = A // LANES, S // T

    def body(q_ref, k_ref, kp_ref, v_ref, vp_ref, do_ref, y_ref, ls_ref, dq_ref, dk_ref, dv_ref,
             qn, don, dln, kn, vn, qp, dop, lsp, dlp, kp, vp, dqp, dkp, dvp, dqa, dka, dva, dkc, dvc):
        step = pl.program_id(1)
        i = nt - 1 - step
        h0t = _head0((T, LANES))
        dof = do_ref[...].astype(F32)
        prod = dof * y_ref[...].astype(F32)
        d0 = jnp.sum(jnp.where(h0t, prod, 0.0), axis=-1, keepdims=True)
        d1 = jnp.sum(jnp.where(h0t, 0.0, prod), axis=-1, keepdims=True)
        qn[...] = q_ref[...].astype(F32)
        don[...] = dof
        dln[...] = jnp.where(h0t, d0, d1)
        kn[0:T, :] = kp_ref[...].astype(F32)
        kn[T:2 * T, :] = k_ref[...].astype(F32)
        vn[0:T, :] = vp_ref[...].astype(F32)
        vn[T:2 * T, :] = v_ref[...].astype(F32)

        @pl.when(step == 0)
        def _():
            dka[...] = jnp.zeros_like(dka)
            dva[...] = jnp.zeros_like(dva)

        @pl.when(step > 0)
        def _():
            dka[...] = dkc[...]
            dva[...] = dvc[...]

        h0 = _head0((BLK, LANES))
        for pi, d in enumerate(DILATIONS):
            Ld = T // d
            nblk = Ld // BLK
            for src, dst, cast in ((qn, qp, BF16), (don, dop, BF16), (dln, dlp, None)):
                _deinterleave(dst, src, d, Ld, Ld, cast=cast)
            _deinterleave(lsp, ls_ref, d, Ld, Ld)
            for src, dst in ((kn, kp), (vn, vp)):
                _deinterleave(dst, src, d, Ld, 2 * Ld, cast=BF16)
                _deinterleave(dst, src, d, Ld, 2 * Ld, dst_off=Ld, src_off=T, cast=BF16)
            dkp[...] = jnp.zeros_like(dkp)
            dvp[...] = jnp.zeros_like(dvp)

            def unit(u, carry, nblk=nblk):
                r, b = u // nblk, u % nblk
                q0 = pl.multiple_of(u * BLK, BLK)
                k0 = pl.multiple_of((2 * r * nblk + nblk + b - 1) * BLK, BLK)
                mask = _band(jnp.where((i == 0) & (b == 0), BLK, 0), 2 * BLK)
                qb, dob = qp[pl.ds(q0, BLK), :], dop[pl.ds(q0, BLK), :]
                lsb, dlb = lsp[pl.ds(q0, BLK), :], dlp[pl.ds(q0, BLK), :]
                kw, vw = kp[pl.ds(k0, 2 * BLK), :], vp[pl.ds(k0, 2 * BLK), :]
                qs, dos = _stack_heads(qb, h0), _stack_heads(dob, h0)
                lse = jnp.concatenate([_pair_col(lsb, h0), _pair_col(lsb, ~h0)], axis=0)
                delta = jnp.concatenate([_pair_col(dlb, h0), _pair_col(dlb, ~h0)], axis=0)
                s = _dot(qs, kw, NT) * ATTN_SCALE
                p = jnp.where(jnp.concatenate([mask, mask], axis=0), jnp.exp(s - lse), 0.0)
                ds = (p * (_dot(dos, vw, NT) - delta)).astype(BF16)
                dq = _dot(ds, kw) * ATTN_SCALE
                dqp[pl.ds(q0, BLK), :] = jnp.where(h0, dq[:BLK], dq[BLK:])
                dkp[pl.ds(k0, 2 * BLK), :] += _dot(ds, qs, TN) * ATTN_SCALE
                dvp[pl.ds(k0, 2 * BLK), :] += _dot(p.astype(BF16), dos, TN)
                return carry

            lax.fori_loop(0, T // BLK, unit, 0, unroll=True)
            _interleave(dqa, dqp, d, Ld, Ld, add=pi > 0)
            for acc, nxt, part in ((dka, dkc, dkp), (dva, dvc, dvp)):
                _interleave(acc, part, d, Ld, 2 * Ld, src_off=Ld, add=True)
                _interleave(nxt, part, d, Ld, 2 * Ld, add=pi > 0)
        dq_ref[...] = dqa[...].astype(BF16)
        dk_ref[...] = dka[...].astype(BF16)
        dv_ref[...] = dva[...].astype(BF16)

    kb, vb = A // LANES, 2 * A // LANES
    cur = lambda off: pl.BlockSpec((T, LANES), lambda h, s: (nt - 1 - s, off + h))
    prev = lambda off: pl.BlockSpec((T, LANES), lambda h, s: (jnp.maximum(nt - 2 - s, 0), off + h))
    vm = lambda rows, dt: pltpu.VMEM((rows, LANES), dt)
    shp = jax.ShapeDtypeStruct((S, A), BF16)
    return _call(
        body, name=name, grid=(Hp, nt),
        in_specs=[cur(0), cur(kb), prev(kb), cur(vb), prev(vb), cur(0), cur(0), cur(0)],
        out_specs=[cur(0)] * 3, out_shape=[shp] * 3,
        scratch=[vm(T, F32), vm(T, F32), vm(T, F32), vm(2 * T, F32), vm(2 * T, F32),
                 vm(T, BF16), vm(T, BF16), vm(T, F32), vm(T, F32), vm(2 * T, BF16), vm(2 * T, BF16),
                 vm(T, F32), vm(2 * T, F32), vm(2 * T, F32)] + [vm(T, F32)] * 5,
        args=(z, z, z, z, z, dy, y, lse), vmem=56, comm=comm)


def _shift_down(e, k):
    return pltpu.roll(e, k, 0)


def _shift_up(e, k):
    return pltpu.roll(e, e.shape[0] - k, 0)


def _causal_sums(e, g):
    for lvl in range(g + 1):
        e = e + _shift_down(e, 1 << lvl)
    return e


def _anticausal_sums(e, g):
    for lvl in range(g + 1):
        e = e + _shift_up(e, 1 << lvl)
    return e


def _mixer_specs(S, tm, A, C, P):
    assert (3 * A) % C == 0 and (3 * A + 3 * C) % P == 0 and tm % HALO == 0
    cb, pb = 3 * A // C, (3 * A + 3 * C) // P
    hb = tm // HALO
    last = S // HALO - 1
    col = lambda w, j: pl.BlockSpec((tm, w), lambda i: (i, j))
    prev = lambda w, j: pl.BlockSpec((HALO, w), lambda i: (jnp.maximum(i * hb - 1, 0), j))
    nxt = lambda w, j: pl.BlockSpec((HALO, w), lambda i: (jnp.minimum((i + 1) * hb, last), j))
    return cb, pb, col, prev, nxt


def _conv_taps(gc, ci, gch, cih, keep_prev):
    u = gc * ci
    e = jnp.concatenate([gch * cih * keep_prev, u], axis=0)
    return u, _shift_down(e, 1)[HALO:], _shift_down(e, 2)[HALO:]


def _pooled(xp, xph, keep_prev, pos, g):
    cols = slice(g * POOL_GROUP, (g + 1) * POOL_GROUP)
    x = xp[:, cols]
    e = jnp.concatenate([xph[:, cols] * keep_prev, x], axis=0)
    cnt = jnp.minimum(pos + 1, POOL_WINDOWS[g]).astype(F32)
    return _causal_sums(e, g)[HALO:] / cnt - x


def convpool_fwd(z, conv_w, pool_w, pool_scale, A, C, P, name):
    S = z.shape[0]
    tm = _tile(S, 512)
    cb, pb, col, prev, _ = _mixer_specs(S, tm, A, C, P)

    def body(gb_ref, gc_ref, ci_ref, xp_ref, gch_ref, cih_ref, xph_ref, cw_ref, pw_ref, ps_ref, y_ref):
        i = pl.program_id(0)
        keep_prev = jnp.where(i == 0, 0.0, 1.0)
        f = lambda r: r[...].astype(F32)
        u, u1, u2 = _conv_taps(f(gc_ref), f(ci_ref), f(gch_ref), f(cih_ref), keep_prev)
        cw = cw_ref[...]
        y_ref[:, 0:C] = (f(gb_ref) * (cw[0:1] * u2 + cw[1:2] * u1 + cw[2:3] * u)).astype(BF16)
        xp, xph = f(xp_ref), f(xph_ref)
        pos = i * tm + lax.broadcasted_iota(jnp.int32, (tm, 1), 0)
        for g in range(len(POOL_WINDOWS)):
            cols = slice(g * POOL_GROUP, (g + 1) * POOL_GROUP)
            lin = _dot(_pooled(xp, xph, keep_prev, pos, g).astype(BF16), pw_ref[g])
            y_ref[:, C + g * POOL_GROUP:C + (g + 1) * POOL_GROUP] = (lin * ps_ref[:, cols]).astype(BF16)

    full = lambda shape: pl.BlockSpec(shape, lambda i: (0,) * len(shape))
    return pl.pallas_call(
        body, name=name, grid=(S // tm,),
        in_specs=[col(C, cb), col(C, cb + 1), col(C, cb + 2), col(P, pb),
                  prev(C, cb + 1), prev(C, cb + 2), prev(P, pb),
                  full(conv_w.shape), full(pool_w.shape), full(pool_scale.shape)],
        out_specs=pl.BlockSpec((tm, C + P), lambda i: (i, 0)),
        out_shape=jax.ShapeDtypeStruct((S, C + P), BF16), compiler_params=_params(48),
    )(z, z, z, z, z, z, z, conv_w, pool_w, pool_scale)


def convpool_bwd(z, dycp, conv_w, pool_w, pool_scale, A, C, P, name):
    S = z.shape[0]
    tm = _tile(S, 512)
    nt = S // tm
    cb, pb, col, prev, nxt = _mixer_specs(S, tm, A, C, P)
    NG = len(POOL_WINDOWS)

    def body(gb_ref, gc_ref, ci_ref, xp_ref, gch_ref, cih_ref, xph_ref, gbn_ref, dy_ref, dyn_ref,
             cw_ref, pw_ref, ps_ref, dz_ref, dcw_ref, dpw_ref, dps_ref):
        i = pl.program_id(0)

        @pl.when(i == 0)
        def _():
            dcw_ref[...] = jnp.zeros_like(dcw_ref)
            dpw_ref[...] = jnp.zeros_like(dpw_ref)
            dps_ref[...] = jnp.zeros_like(dps_ref)

        keep_prev = jnp.where(i == 0, 0.0, 1.0)
        keep_next = jnp.where(i == nt - 1, 0.0, 1.0)
        f = lambda r: r[...].astype(F32)
        gb, gc, ci = f(gb_ref), f(gc_ref), f(ci_ref)
        u, u1, u2 = _conv_taps(gc, ci, f(gch_ref), f(cih_ref), keep_prev)
        cw = cw_ref[...]
        dy, dyn = f(dy_ref), f(dyn_ref) * keep_next
        dyc = dy[:, 0:C]
        dz_ref[:, 0:C] = (dyc * (cw[0:1] * u2 + cw[1:2] * u1 + cw[2:3] * u)).astype(BF16)
        dc = dyc * gb
        e = jnp.concatenate([dc, dyn[:, 0:C] * f(gbn_ref)], axis=0)
        du = cw[2:3] * dc + cw[1:2] * _shift_up(e, 1)[:tm] + cw[0:1] * _shift_up(e, 2)[:tm]
        dz_ref[:, C:2 * C] = (du * ci).astype(BF16)
        dz_ref[:, 2 * C:3 * C] = (du * gc).astype(BF16)
        dcw_ref[0:1, :] += jnp.sum(dc * u2, axis=0, keepdims=True)
        dcw_ref[1:2, :] += jnp.sum(dc * u1, axis=0, keepdims=True)
        dcw_ref[2:3, :] += jnp.sum(dc * u, axis=0, keepdims=True)

        xp, xph = f(xp_ref), f(xph_ref)
        pos = i * tm + lax.broadcasted_iota(jnp.int32, (tm, 1), 0)
        pos_e = i * tm + lax.broadcasted_iota(jnp.int32, (tm + HALO, 1), 0)
        for g in range(NG):
            cols = slice(g * POOL_GROUP, (g + 1) * POOL_GROUP)
            ycols = slice(C + g * POOL_GROUP, C + (g + 1) * POOL_GROUP)
            pooled = _pooled(xp, xph, keep_prev, pos, g).astype(BF16)
            pw = pw_ref[g]
            dyp = dy[:, ycols]
            dps_ref[:, cols] += jnp.sum(dyp * _dot(pooled, pw), axis=0, keepdims=True)
            dlin = (jnp.concatenate([dyp, dyn[:, ycols]], axis=0) * ps_ref[:, cols]).astype(BF16)
            dpw_ref[g] += _dot(pooled, dlin[:tm], TN)
            dpool = _dot(dlin, pw, NT)
            r = dpool / jnp.minimum(pos_e + 1, POOL_WINDOWS[g]).astype(F32)
            dz_ref[:, 3 * C + g * POOL_GROUP:3 * C + (g + 1) * POOL_GROUP] = (
                _anticausal_sums(r, g)[:tm] - dpool[:tm]).astype(BF16)

    full = lambda shape: pl.BlockSpec(shape, lambda i: (0,) * len(shape))
    return pl.pallas_call(
        body, name=name, grid=(nt,),
        in_specs=[col(C, cb), col(C, cb + 1), col(C, cb + 2), col(P, pb),
                  prev(C, cb + 1), prev(C, cb + 2), prev(P, pb), nxt(C, cb),
                  pl.BlockSpec((tm, C + P), lambda i: (i, 0)),
                  pl.BlockSpec((HALO, C + P), lambda i: (jnp.minimum((i + 1) * (tm // HALO), S // HALO - 1), 0)),
                  full(conv_w.shape), full(pool_w.shape), full(pool_scale.shape)],
        out_specs=[pl.BlockSpec((tm, 3 * C + P), lambda i: (i, 0)),
                   full(conv_w.shape), full(pool_w.shape), full(pool_scale.shape)],
        out_shape=[jax.ShapeDtypeStruct((S, 3 * C + P), BF16), jax.ShapeDtypeStruct(conv_w.shape, F32),
                   jax.ShapeDtypeStruct(pool_w.shape, F32), jax.ShapeDtypeStruct(pool_scale.shape, F32)],
        compiler_params=_params(48),
    )(z, z, z, z, z, z, z, z, dycp, dycp, conv_w, pool_w, pool_scale)


def adamw(lands, w, m, v, name):
    R, C = w.shape
    nl = len(lands)
    n, Rl, _ = lands[0].shape
    assert Rl * nl == R
    tr = _tile(Rl, 128 * 1024 // C)
    nb = Rl // tr
    c1 = 1.0 - ADAM_B1 ** ADAM_STEP
    c2 = 1.0 - ADAM_B2 ** ADAM_STEP

    def body(*refs):
        land_refs = refs[:nl]
        w_ref, m_ref, v_ref, g_ref, d_ref, mo_ref, vo_ref = refs[nl:]
        i = pl.program_id(0)
        for a, land_ref in enumerate(land_refs):
            @pl.when((i >= a * nb) & (i < (a + 1) * nb))
            def _():
                g = land_ref[0].astype(F32)
                for s in range(1, n):
                    g = g + land_ref[s].astype(F32)
                g_ref[...] = g
                mn = ADAM_B1 * m_ref[...] + (1.0 - ADAM_B1) * g
                vn = ADAM_B2 * v_ref[...] + (1.0 - ADAM_B2) * (g * g)
                mo_ref[...] = mn
                vo_ref[...] = vn
                d_ref[...] = -ADAM_LR * ((mn / c1) / (jnp.sqrt(vn / c2) + ADAM_EPS) + ADAM_WD * w_ref[...])

    land_specs = [pl.BlockSpec((n, tr, C), lambda i, a=a: (0, jnp.clip(i - a * nb, 0, nb - 1), 0))
                  for a in range(nl)]
    row = pl.BlockSpec((tr, C), lambda i: (i, 0))
    shp = jax.ShapeDtypeStruct((R, C), F32)
    return pl.pallas_call(
        body, name=name, grid=(nl * nb,),
        in_specs=land_specs + [row, row, row], out_specs=[row] * 4, out_shape=[shp] * 4,
        compiler_params=_params(48),
    )(*lands, w, m, v)


def _gather_comm(arrs):
    return ("gather", arrs) if arrs else None


def _ffn_bwd(g, gb, saved, gain, wg, wu, wd, tag, defer_wu=False):
    x, ht, a, b, hid = saved
    da, db = ffn_dhid(gb, wd, a, b, f"ffn_dhid_{tag}")
    dwd = grad_shared_rhs(hid, gb, None, FFN_RESIDUAL, f"ffn_dwd_{tag}")
    dwg, (land_wd,) = grad_lhs(ht, da, True, f"ffn_dwg_{tag}", ("exchange", [dwd]))
    if defer_wu:
        land_wu, _ = grad_lhs(ht, db, True, f"ffn_dwu_{tag}")
        dh, (land_wg,) = back_proj([da, db], [wg, wu], True, f"ffn_dh_{tag}", ("exchange", [dwg]))
    else:
        dwu, (land_wg,) = grad_lhs(ht, db, True, f"ffn_dwu_{tag}", ("exchange", [dwg]))
        dh, (land_wu,) = back_proj([da, db], [wg, wu], True, f"ffn_dh_{tag}", ("exchange", [dwu]))
    g, gb, dgain = rms_bwd(x, gain, dh, g, f"rms_bwd_{tag}")
    return g, gb, dgain, (land_wg, land_wu, land_wd)


def _mixer_fwd(x, h, ht, next_gain, w_in, conv_w, pool_w, pool_scale, w_out, dims, tag, carry):
    A, C, P = dims
    z, got_a = proj_cols(h, w_in, f"in_proj_{tag}", _gather_comm(carry[:1]))
    ya, lse, got_b = attn_fwd(z, A, f"attn_fwd_{tag}", _gather_comm(carry[1:]))
    ycp = convpool_fwd(z, conv_w, pool_w, pool_scale, A, C, P, f"convpool_{tag}")
    xo, hn, htn = out_proj(ya, ycp, w_out, x, next_gain, f"out_proj_{tag}")
    return xo, hn, htn, (x, ht, z, lse, ya, ycp), got_a + got_b


def _mixer_bwd(g, gb, saved, gain, w_in, conv_w, pool_w, pool_scale, w_out, dims, tag, also_send):
    A, C, P = dims
    x, ht, z, lse, ya, ycp = saved
    S, D = x.shape
    dya, dycp = out_proj_bwd(gb, w_out, A, f"out_proj_bwd_{tag}")
    dwo_a = grad_shared_rhs(ya, gb, 1, 1.0, f"dwo_attn_{tag}")
    dwo_cp = grad_shared_rhs(ycp, gb, 1, 1.0, f"dwo_cp_{tag}")
    dwo = jnp.concatenate([dwo_a[0], dwo_cp[0]], axis=0).reshape(NDEV, -1, D)
    dqkv, (land_wo, land_also) = attn_bwd(z, dya, ya, lse, A, f"attn_bwd_{tag}", ("exchange", [dwo, also_send]))
    dz_cp, dcw, dpw, dps = convpool_bwd(z, dycp, conv_w, pool_w, pool_scale, A, C, P, f"convpool_bwd_{tag}")
    dz = jnp.concatenate(dqkv + [dz_cp], axis=1)
    dwin, _ = grad_lhs(ht, dz, False, f"dwin_{tag}")
    dh, (land_win,) = back_proj([dz], [w_in], False, f"in_proj_bwd_{tag}", ("exchange", [dwin]))
    g, gb, dgain = rms_bwd(x, gain, dh, g, f"rms_bwd_{tag}")
    return g, gb, (dgain, dcw, dpw, dps), (land_win, land_wo, land_also)


def _pack(arrs):
    flat = [a.reshape(-1).astype(F32) for a in arrs]
    spans, off = [], 0
    for a in flat:
        spans.append((off, a.shape[0]))
        off += a.shape[0]
    rows = -(-off // (8 * LANES)) * 8
    buf = jnp.concatenate(flat + [jnp.zeros((rows * LANES - off,), F32)]).reshape(rows, LANES)
    return buf, spans


def kernel(x, ffn1_norm, ffn1_w_gate, ffn1_w_up, ffn1_w_down, mix_norm, w_in, conv_w, pool_w, pool_scale, w_out, ffn2_norm, ffn2_w_gate, ffn2_w_up, ffn2_w_down, final_norm, loss_target, m_ffn1_norm, m_ffn1_w_gate, m_ffn1_w_up, m_ffn1_w_down, m_mix_norm, m_w_in, m_conv_w, m_pool_w, m_pool_scale, m_w_out, m_ffn2_norm, m_ffn2_w_gate, m_ffn2_w_up, m_ffn2_w_down, m_final_norm, v_ffn1_norm, v_ffn1_w_gate, v_ffn1_w_up, v_ffn1_w_down, v_mix_norm, v_w_in, v_conv_w, v_pool_w, v_pool_scale, v_w_out, v_ffn2_norm, v_ffn2_w_gate, v_ffn2_w_up, v_ffn2_w_down, v_final_norm):
    depth = ffn1_norm.shape[0]
    S, D = x.shape[1], x.shape[2]
    Cb = conv_w.shape[2]
    C = Cb * NDEV
    P = pool_scale.shape[1]
    A = (w_in.shape[2] * NDEV - 3 * C - P) // 3
    dims = (A, C, P)
    me = 4 * lax.axis_index("x") + 2 * lax.axis_index("y") + lax.axis_index("c")

    big = dict(ffn1_w_gate=ffn1_w_gate, ffn1_w_up=ffn1_w_up, ffn1_w_down=ffn1_w_down, w_in=w_in, w_out=w_out,
               ffn2_w_gate=ffn2_w_gate, ffn2_w_up=ffn2_w_up, ffn2_w_down=ffn2_w_down)
    big_m = dict(ffn1_w_gate=m_ffn1_w_gate, ffn1_w_up=m_ffn1_w_up, ffn1_w_down=m_ffn1_w_down, w_in=m_w_in,
                 w_out=m_w_out, ffn2_w_gate=m_ffn2_w_gate, ffn2_w_up=m_ffn2_w_up, ffn2_w_down=m_ffn2_w_down)
    big_v = dict(ffn1_w_gate=v_ffn1_w_gate, ffn1_w_up=v_ffn1_w_up, ffn1_w_down=v_ffn1_w_down, w_in=v_w_in,
                 w_out=v_w_out, ffn2_w_gate=v_ffn2_w_gate, ffn2_w_up=v_ffn2_w_up, ffn2_w_down=v_ffn2_w_down)
    names = list(big)

    first_names = ["ffn1_w_gate", "ffn1_w_up"]
    mix_names = ["w_in", "w_out"]
    ffn2_names = ["ffn2_w_gate", "ffn2_w_up", "ffn2_w_down"]
    shards = lambda l, ns: [big[n][l].astype(BF16) for n in ns]
    full = [dict() for _ in range(depth)]
    conv_full = all_gather([jnp.pad(conv_w.reshape(-1, Cb), ((0, 2), (0, LANES - Cb)))], "gather_conv_w")[0]
    conv_full = jnp.transpose(conv_full[:, :depth * 3, :Cb].reshape(NDEV, depth, 3, Cb), (1, 2, 0, 3)).reshape(depth, 3, C)
    pool_w_bf = pool_w.astype(BF16)

    xs = x[0]
    saved = []
    (h, ht), got = rms_fwd(xs, ffn1_norm[0:1], "rms_f1_l0", ("gather", shards(0, first_names)))
    full[0].update(zip(first_names, got))
    for l in range(depth):
        W = full[l]
        last = l + 1 == depth
        up1 = ["ffn1_w_down"] + (mix_names if l == 0 else [])
        (a, b, hid), got = ffn_up(h, W["ffn1_w_gate"], W["ffn1_w_up"], f"ffn_up_f1_l{l}", _gather_comm(shards(l, up1)))
        W.update(zip(up1, got))
        s1 = (xs, ht, a, b, hid)
        (xs, h, ht), _ = ffn_down(hid, W["ffn1_w_down"], xs, f"ffn_down_f1_l{l}", next_gain=mix_norm[l:l + 1])
        xs, h, ht, s2, got = _mixer_fwd(xs, h, ht, ffn2_norm[l:l + 1], W["w_in"], conv_full[l], pool_w_bf[l],
                                        pool_scale[l:l + 1], W["w_out"].reshape(-1, D), dims, f"mix_l{l}",
                                        shards(l, ffn2_names[:2]))
        W.update(zip(ffn2_names[:2], got))
        got = ffn_up(h, W["ffn2_w_gate"], W["ffn2_w_up"], f"ffn_up_f2_l{l}",
                     _gather_comm(shards(l, ffn2_names[2:]) + ([] if last else shards(l + 1, first_names))))
        (a, b, hid), (W["ffn2_w_down"], *got_next) = got
        s3 = (xs, ht, a, b, hid)
        outs, got_down = ffn_down(hid, W["ffn2_w_down"], xs, f"ffn_down_f2_l{l}",
                                  _gather_comm([] if last else shards(l + 1, mix_names)),
                                  next_gain=None if last else ffn1_norm[l + 1:l + 2])
        if last:
            xs, = outs
        else:
            xs, h, ht = outs
            full[l + 1].update(zip(first_names, got_next))
            full[l + 1].update(zip(mix_names, got_down))
        saved.append((s1, s2, s3))
    loss_part, g, gb, d_final = final_loss(xs, final_norm.reshape(1, D), loss_target[0], "final_loss")

    small = [None] * depth
    lands = [None] * depth
    for l in reversed(range(depth)):
        W = full[l]
        s1, s2, s3 = saved[l]
        g, gb, dn2, (lg2, dwu2, ld2) = _ffn_bwd(g, gb, s3, ffn2_norm[l:l + 1], W["ffn2_w_gate"], W["ffn2_w_up"],
                                                W["ffn2_w_down"], f"f2_l{l}", defer_wu=True)
        g, gb, (dnm, dcw, dpw, dps), (lwin, lwo, lu2) = _mixer_bwd(
            g, gb, s2, mix_norm[l:l + 1], W["w_in"], conv_full[l], pool_w_bf[l], pool_scale[l:l + 1],
            W["w_out"].reshape(-1, D), dims, f"mix_l{l}", dwu2)
        g, gb, dn1, (lg1, lu1, ld1) = _ffn_bwd(g, gb, s1, ffn1_norm[l:l + 1], W["ffn1_w_gate"], W["ffn1_w_up"],
                                               W["ffn1_w_down"], f"f1_l{l}")
        lands[l] = dict(ffn1_w_gate=lg1, ffn1_w_up=lu1, ffn1_w_down=ld1, w_in=lwin, w_out=lwo,
                        ffn2_w_gate=lg2, ffn2_w_up=lu2, ffn2_w_down=ld2)
        small[l] = (dn1, dnm, dcw, dpw, dps, dn2)

    res = {}
    for n in names:
        shp = big[n].shape
        two = lambda t: t.reshape(-1, shp[-1])
        outs = adamw([lands[l][n] for l in range(depth)], two(big[n]), two(big_m[n]), two(big_v[n]), f"adamw_{n}")
        res[n] = [o.reshape(shp) for o in outs]

    st = lambda i: jnp.stack([small[l][i] for l in range(depth)])
    small_g = dict(ffn1_norm=st(0), mix_norm=st(1), conv_w=st(2), pool_w=st(3), pool_scale=st(4), ffn2_norm=st(5),
                   final_norm=d_final)
    small_names = list(small_g)
    zeros_conv = jnp.zeros((depth, 3, C), F32)
    place = lambda t: lax.dynamic_update_slice(zeros_conv, t, (0, 0, me * Cb))
    small_w = dict(ffn1_norm=ffn1_norm, mix_norm=mix_norm, conv_w=place(conv_w), pool_w=pool_w, pool_scale=pool_scale,
                   ffn2_norm=ffn2_norm, final_norm=final_norm)
    small_m = dict(ffn1_norm=m_ffn1_norm, mix_norm=m_mix_norm, conv_w=place(m_conv_w), pool_w=m_pool_w,
                   pool_scale=m_pool_scale, ffn2_norm=m_ffn2_norm, final_norm=m_final_norm)
    small_v = dict(ffn1_norm=v_ffn1_norm, mix_norm=v_mix_norm, conv_w=place(v_conv_w), pool_w=v_pool_w,
                   pool_scale=v_pool_scale, ffn2_norm=v_ffn2_norm, final_norm=v_final_norm)
    gbuf, spans = _pack([small_g[n] for n in small_names] + [loss_part])
    wbuf, _ = _pack([small_w[n] for n in small_names] + [jnp.zeros((1, LANES), F32)])
    mbuf, _ = _pack([small_m[n] for n in small_names] + [jnp.zeros((1, LANES), F32)])
    vbuf, _ = _pack([small_v[n] for n in small_names] + [jnp.zeros((1, LANES), F32)])
    gathered = all_gather([gbuf], "gather_small_grads")[0]
    outs = adamw([gathered], wbuf, mbuf, vbuf, "adamw_small")
    for n, (off, size) in zip(small_names, spans):
        shp = small_w[n].shape
        vals = [o.reshape(-1)[off:off + size].reshape(shp) for o in outs]
        if n == "conv_w":
            vals = [lax.dynamic_slice(t, (0, 0, me * Cb), (depth, 3, Cb)) for t in vals]
        res[n] = vals
    loss = outs[0].reshape(-1)[spans[-1][0]]

    order = ["ffn1_norm", "ffn1_w_gate", "ffn1_w_up", "ffn1_w_down", "mix_norm", "w_in", "conv_w", "pool_w",
             "pool_scale", "w_out", "ffn2_norm", "ffn2_w_gate", "ffn2_w_up", "ffn2_w_down", "final_norm"]
    return (loss, g[None], *[res[n][0] for n in order], *[res[n][1] for n in order],
            *[res[n][2] for n in order], *[res[n][3] for n in order])
```

```python
import functools

import jax
import jax.numpy as jnp
from jax import lax
from jax.experimental import pallas as pl
from jax.experimental.pallas import tpu as pltpu

F32 = jnp.float32
BF16 = jnp.bfloat16
NDEV = 8
HEAD_DIM = 64
LANES = 128
BLK = 128
DILATIONS = (1, 4, 16)
POOL_WINDOWS = (2, 4, 8, 16)
POOL_GROUP = 128
HALO = 16
FFN_RESIDUAL = 0.5
RMS_EPS = 1e-6
NEG_INF = -1e30
ATTN_SCALE = HEAD_DIM ** -0.5
ADAM_LR, ADAM_B1, ADAM_B2, ADAM_EPS, ADAM_WD, ADAM_STEP = 0.001, 0.9, 0.999, 1e-08, 0.01, 10
VMEM_BYTES = 64 * 1024 * 1024
MESH = pl.DeviceIdType.MESH

NN = (((1,), (0,)), ((), ()))
NT = (((1,), (1,)), ((), ()))
TN = (((0,), (0,)), ((), ()))


def _dot(a, b, dims=NN):
    return lax.dot_general(a, b, dims, preferred_element_type=F32)


def _params(vmem_mb=48):
    return pltpu.CompilerParams(vmem_limit_bytes=min(vmem_mb * 1024 * 1024, VMEM_BYTES - 4 * 1024 * 1024))


def _tile(n, want):
    t = min(n, max(16, want // 16 * 16))
    while t > 16 and (n % t or t % 16):
        t -= 16
    return t if n % t == 0 else n


def _coords():
    return lax.axis_index("x"), lax.axis_index("y"), lax.axis_index("c")


def _comm_sems(n):
    return [pltpu.SemaphoreType.DMA((7 * n,)), pltpu.SemaphoreType.DMA((7 * n,)), pltpu.SemaphoreType.DMA((n,))]


def _gather_ops(ins, outs, sems):
    n = len(ins)
    send_sems, recv_sems, local_sems = sems
    x, y, c = _coords()
    me, sibling = (x, y, c), (x, y, 1 - c)
    chips = [(1 - x, y), (x, 1 - y), (1 - x, 1 - y)]

    def slot(out, p):
        return out.at[4 * p[0] + 2 * p[1] + p[2]]

    def copy(a, k, block, to, src=None):
        return pltpu.make_async_remote_copy(
            src_ref=slot(outs[a], block) if src is None else src, dst_ref=slot(outs[a], block),
            send_sem=send_sems.at[7 * a + k], recv_sem=recv_sems.at[7 * a + k],
            device_id=to, device_id_type=MESH)

    def own():
        mine = [pltpu.make_async_copy(ins[a], slot(outs[a], me), local_sems.at[a]) for a in range(n)]
        first = []
        for a in range(n):
            first.append(copy(a, 0, me, sibling, src=ins[a]))
            first += [copy(a, 1 + j, me, (*chip, c), src=ins[a]) for j, chip in enumerate(chips)]
        return mine, first

    def start():
        mine, first = own()
        for cp in mine + first:
            cp.start()

    def relay():
        for j, chip in enumerate(chips):
            for a in range(n):
                copy(a, 1 + j, (*chip, c), me).wait_recv()
                copy(a, 4 + j, (*chip, c), sibling).start()

    def finish():
        mine, first = own()
        passed = [copy(a, 4 + j, (*chip, c), sibling) for j, chip in enumerate(chips) for a in range(n)]
        for a in range(n):
            copy(a, 0, sibling, me).wait_recv()
            for j, chip in enumerate(chips):
                copy(a, 4 + j, (*chip, 1 - c), me).wait_recv()
        for cp in first + passed:
            cp.wait_send()
        for cp in mine:
            cp.wait()

    return start, relay, finish


def _exchange_ops(ins, outs, sems):
    n = len(ins)
    send_sems, recv_sems, local_sems = sems
    x, y, c = _coords()
    me = 4 * x + 2 * y + c

    def peer(k):
        return (1 - x if k & 4 else x, 1 - y if k & 2 else y, 1 - c if k & 1 else c)

    def copy(a, k):
        p = peer(k)
        return pltpu.make_async_remote_copy(
            src_ref=ins[a].at[4 * p[0] + 2 * p[1] + p[2]], dst_ref=outs[a].at[me],
            send_sem=send_sems.at[7 * a + k - 1], recv_sem=recv_sems.at[7 * a + k - 1],
            device_id=p, device_id_type=MESH)

    def landed(a, k):
        p = peer(k)
        return pltpu.make_async_remote_copy(
            src_ref=ins[a].at[me], dst_ref=outs[a].at[4 * p[0] + 2 * p[1] + p[2]],
            send_sem=send_sems.at[7 * a + k - 1], recv_sem=recv_sems.at[7 * a + k - 1],
            device_id=p, device_id_type=MESH)

    def own():
        mine = [pltpu.make_async_copy(ins[a].at[me], outs[a].at[me], local_sems.at[a]) for a in range(n)]
        return mine, [copy(a, k) for a in range(n) for k in range(1, NDEV)]

    def start():
        mine, sent = own()
        for cp in mine + sent:
            cp.start()

    def finish():
        mine, sent = own()
        for a in range(n):
            for k in range(1, NDEV):
                landed(a, k).wait_recv()
        for cp in sent:
            cp.wait_send()
        for cp in mine:
            cp.wait()

    return start, None, finish


_COMM = {"gather": (_gather_ops, lambda a: (NDEV,) + a.shape), "exchange": (_exchange_ops, lambda a: a.shape)}


def all_gather(arrs, name):
    n = len(arrs)

    def body(*refs):
        start, relay, finish = _gather_ops(refs[:n], refs[n:2 * n], refs[2 * n:])
        start()
        relay()
        finish()

    any_spec = pl.BlockSpec(memory_space=pl.ANY)
    return pl.pallas_call(
        body, name=name, out_shape=[jax.ShapeDtypeStruct((NDEV,) + a.shape, a.dtype) for a in arrs],
        in_specs=[any_spec] * n, out_specs=[any_spec] * n, scratch_shapes=_comm_sems(n),
    )(*arrs)


def _call(body, *, name, grid, in_specs, out_specs, out_shape, args, scratch=(), vmem=48, comm=None):
    if comm is None:
        outs = pl.pallas_call(
            body, name=name, grid=grid, in_specs=list(in_specs), out_specs=list(out_specs), out_shape=list(out_shape),
            scratch_shapes=list(scratch), compiler_params=_params(vmem))(*args)
        return list(outs), []
    kind, arrs = comm
    ops, shape_of = _COMM[kind]
    n, n_in, n_out, n_scr = len(arrs), len(in_specs), len(out_specs), len(scratch)
    n_steps = functools.reduce(lambda p, g: p * g, grid, 1)
    relay_at = max(n_steps - 1 - max(n_steps // 8, 1), 0)

    def carrier(*refs):
        ins, c_in = refs[:n_in], refs[n_in:n_in + n]
        o0 = n_in + n
        outs, c_out = refs[o0:o0 + n_out], refs[o0 + n_out:o0 + n_out + n]
        s0 = o0 + n_out + n
        scr, sems = refs[s0:s0 + n_scr], refs[s0 + n_scr:]
        ids = [pl.program_id(d) for d in range(len(grid))]
        first = functools.reduce(jnp.logical_and, [i == 0 for i in ids])
        last = functools.reduce(jnp.logical_and, [i == g - 1 for i, g in zip(ids, grid)])
        start, relay, finish = ops(c_in, c_out, sems)
        pl.when(first)(start)
        body(*ins, *outs, *scr)
        if relay is not None:
            step = functools.reduce(lambda acc, ig: acc * ig[1] + ig[0], zip(ids, grid), 0)
            pl.when(step == relay_at)(relay)
        pl.when(last)(finish)

    any_spec = pl.BlockSpec(memory_space=pl.ANY)
    res = pl.pallas_call(
        carrier, name=name, grid=grid, in_specs=list(in_specs) + [any_spec] * n,
        out_specs=list(out_specs) + [any_spec] * n,
        out_shape=list(out_shape) + [jax.ShapeDtypeStruct(shape_of(a), a.dtype) for a in arrs],
        scratch_shapes=list(scratch) + _comm_sems(n), compiler_params=_params(vmem))(*args, *arrs)
    return list(res[:n_out]), list(res[n_out:])


def _rstd(x):
    return lax.rsqrt(jnp.mean(x * x, axis=-1, keepdims=True) + RMS_EPS)


def rms_fwd(x, gain, name, comm=None):
    S, D = x.shape
    tm = _tile(S, 512)

    def body(x_ref, g_ref, h_ref, ht_ref):
        xv = x_ref[...]
        h = xv * _rstd(xv) * g_ref[...]
        h_ref[...] = h.astype(BF16)
        ht_ref[...] = h.T.astype(BF16)

    return _call(
        body, name=name, grid=(S // tm,),
        in_specs=[pl.BlockSpec((tm, D), lambda i: (i, 0)), pl.BlockSpec((1, D), lambda i: (0, 0))],
        out_specs=[pl.BlockSpec((tm, D), lambda i: (i, 0)), pl.BlockSpec((D, tm), lambda i: (0, i))],
        out_shape=[jax.ShapeDtypeStruct((S, D), BF16), jax.ShapeDtypeStruct((D, S), BF16)],
        args=(x, gain), vmem=40, comm=comm)


def _rms_bwd_math(xv, gain, dh):
    r = _rstd(xv)
    xhat = xv * r
    dxhat = dh * gain
    dx = r * (dxhat - xhat * jnp.mean(dxhat * xhat, axis=-1, keepdims=True))
    return dx, dh * xhat


def rms_bwd(x, gain, dh, g, name):
    S, D = x.shape
    tm = _tile(S, 256)

    def body(x_ref, gain_ref, dh_ref, g_ref, go_ref, gb_ref, dg_ref):
        @pl.when(pl.program_id(0) == 0)
        def _():
            dg_ref[...] = jnp.zeros_like(dg_ref)

        dx, dgain = _rms_bwd_math(x_ref[...], gain_ref[...], dh_ref[...].astype(F32))
        gn = g_ref[...] + dx
        go_ref[...] = gn
        gb_ref[...] = gn.astype(BF16)
        dg_ref[...] += jnp.sum(dgain, axis=0, keepdims=True)

    row = pl.BlockSpec((tm, D), lambda i: (i, 0))
    vec = pl.BlockSpec((1, D), lambda i: (0, 0))
    return pl.pallas_call(
        body, name=name, grid=(S // tm,),
        in_specs=[row, vec, row, row], out_specs=[row, row, vec],
        out_shape=[jax.ShapeDtypeStruct((S, D), F32), jax.ShapeDtypeStruct((S, D), BF16),
                   jax.ShapeDtypeStruct((1, D), F32)],
        compiler_params=_params(48),
    )(x, gain, dh, g)


def final_loss(x, gain, target, name):
    S, D = x.shape
    tm = _tile(S, 256)

    def body(x_ref, gain_ref, t_ref, loss_ref, go_ref, gb_ref, dg_ref):
        @pl.when(pl.program_id(0) == 0)
        def _():
            dg_ref[...] = jnp.zeros_like(dg_ref)
            loss_ref[...] = jnp.zeros_like(loss_ref)

        xv, gain_v = x_ref[...], gain_ref[...]
        err = xv * _rstd(xv) * gain_v - t_ref[...]
        loss_ref[...] += jnp.sum(jnp.sum(err * err, axis=-1, keepdims=True), axis=0, keepdims=True) * (0.5 / D)
        dx, dgain = _rms_bwd_math(xv, gain_v, err * (1.0 / D))
        go_ref[...] = dx
        gb_ref[...] = dx.astype(BF16)
        dg_ref[...] += jnp.sum(dgain, axis=0, keepdims=True)

    row = pl.BlockSpec((tm, D), lambda i: (i, 0))
    vec = pl.BlockSpec((1, D), lambda i: (0, 0))
    return pl.pallas_call(
        body, name=name, grid=(S // tm,),
        in_specs=[row, vec, row],
        out_specs=[pl.BlockSpec((1, LANES), lambda i: (0, 0)), row, row, vec],
        out_shape=[jax.ShapeDtypeStruct((1, LANES), F32), jax.ShapeDtypeStruct((S, D), F32),
                   jax.ShapeDtypeStruct((S, D), BF16), jax.ShapeDtypeStruct((1, D), F32)],
        compiler_params=_params(48),
    )(x, gain, target)


def _blk(arr, width, tm, stacked):
    if stacked:
        return pl.BlockSpec((None, tm, width), lambda j, i: (j, i, 0))
    return pl.BlockSpec((tm, width), lambda j, i: (i, j))


def ffn_up(h, wg, wu, name, comm=None):
    S, D = h.shape
    J, _, Fb = wg.shape
    tm = _tile(S, 1024)

    def body(h_ref, wg_ref, wu_ref, da_ref, db_ref, hid_ref):
        hv = h_ref[...]
        a = _dot(hv, wg_ref[...])
        b = _dot(hv, wu_ref[...])
        sg = jax.nn.sigmoid(a)
        silu = a * sg
        da_ref[...] = (b * (sg * (1.0 + a * (1.0 - sg)))).astype(BF16)
        db_ref[...] = silu.astype(BF16)
        hid_ref[...] = (silu * b).astype(BF16)

    w_spec = pl.BlockSpec((None, D, Fb), lambda j, i: (j, 0, 0))
    o_spec = pl.BlockSpec((None, tm, Fb), lambda j, i: (j, i, 0))
    shp = jax.ShapeDtypeStruct((J, S, Fb), BF16)
    return _call(
        body, name=name, grid=(J, S // tm),
        in_specs=[pl.BlockSpec((tm, D), lambda j, i: (i, 0)), w_spec, w_spec],
        out_specs=[o_spec, o_spec, o_spec], out_shape=[shp, shp, shp], args=(h, wg, wu), comm=comm)


def proj_cols(h, w, name, comm=None):
    S, D = h.shape
    J, _, Wb = w.shape
    tm = _tile(S, 1024)
    kb = 2 if Wb % LANES == 0 and J % 2 == 0 else 1

    def body(h_ref, w_ref, z_ref):
        wv = jnp.concatenate([w_ref[k] for k in range(kb)], axis=1)
        z_ref[...] = _dot(h_ref[...], wv).astype(BF16)

    (z,), got = _call(
        body, name=name, grid=(J // kb, S // tm),
        in_specs=[pl.BlockSpec((tm, D), lambda j, i: (i, 0)), pl.BlockSpec((kb, D, Wb), lambda j, i: (j, 0, 0))],
        out_specs=[pl.BlockSpec((tm, kb * Wb), lambda j, i: (i, j))],
        out_shape=[jax.ShapeDtypeStruct((S, J * Wb), BF16)], args=(h, w), comm=comm)
    return z, got


def _norm_outs(S, D, tm, gain):
    if gain is None:
        return [], [], []
    return ([pl.BlockSpec((1, D), lambda i: (0, 0))],
            [pl.BlockSpec((tm, D), lambda i: (i, 0)), pl.BlockSpec((D, tm), lambda i: (0, i))],
            [jax.ShapeDtypeStruct((S, D), BF16), jax.ShapeDtypeStruct((D, S), BF16)])


def _write_norm(xo, gain_ref, h_ref, ht_ref):
    h = xo * _rstd(xo) * gain_ref[...]
    h_ref[...] = h.astype(BF16)
    ht_ref[...] = h.T.astype(BF16)


def ffn_down(hid, wd, x, name, comm=None, next_gain=None):
    J, S, Fb = hid.shape
    D = wd.shape[2]
    tm = _tile(S, 256)

    def body(hid_ref, wd_ref, x_ref, *rest):
        acc = _dot(hid_ref[0], wd_ref[0])
        for j in range(1, J):
            acc = acc + _dot(hid_ref[j], wd_ref[j])
        xo = x_ref[...] + FFN_RESIDUAL * acc
        if next_gain is None:
            rest[0][...] = xo
        else:
            gain_ref, o_ref, h_ref, ht_ref = rest
            o_ref[...] = xo
            _write_norm(xo, gain_ref, h_ref, ht_ref)

    row = pl.BlockSpec((tm, D), lambda i: (i, 0))
    g_in, n_specs, n_shapes = _norm_outs(S, D, tm, next_gain)
    outs, got = _call(
        body, name=name, grid=(S // tm,),
        in_specs=[pl.BlockSpec((J, tm, Fb), lambda i: (0, i, 0)),
                  pl.BlockSpec((J, Fb, D), lambda i: (0, 0, 0), pipeline_mode=pl.Buffered(1)), row] + g_in,
        out_specs=[row] + n_specs, out_shape=[jax.ShapeDtypeStruct((S, D), F32)] + n_shapes,
        args=(hid, wd, x) + (() if next_gain is None else (next_gain,)), vmem=56, comm=comm)
    return outs, got


def ffn_dhid(gb, wd, a, b, name):
    S, D = gb.shape
    J, Fb, _ = wd.shape
    tm = _tile(S, 1024)

    parts = 4 if tm % 64 == 0 else 1

    def body(g_ref, wd_ref, a_ref, b_ref, da_ref, db_ref):
        wdv = wd_ref[...]
        for part in range(parts):
            rows = pl.ds(part * (tm // parts), tm // parts)
            dhid = _dot(g_ref[rows, :], wdv, NT) * FFN_RESIDUAL
            da_ref[rows, :] = (dhid * a_ref[rows, :].astype(F32)).astype(BF16)
            db_ref[rows, :] = (dhid * b_ref[rows, :].astype(F32)).astype(BF16)

    o_spec = pl.BlockSpec((None, tm, Fb), lambda j, i: (j, i, 0))
    shp = jax.ShapeDtypeStruct((J, S, Fb), BF16)
    return pl.pallas_call(
        body, name=name, grid=(J, S // tm),
        in_specs=[pl.BlockSpec((tm, D), lambda j, i: (i, 0)), pl.BlockSpec((None, Fb, D), lambda j, i: (j, 0, 0)),
                  o_spec, o_spec],
        out_specs=[o_spec, o_spec], out_shape=[shp, shp], compiler_params=_params(48),
    )(gb, wd, a, b)


def back_proj(ds, ws, stacked, name, comm=None):
    n = len(ds)
    J, D, Wb = ws[0].shape
    S = ds[0].shape[1] if stacked else ds[0].shape[0]
    kb = J if n == 1 else 2
    nj = J // kb
    tm = _tile(S, 256 if n == 1 else 512)

    def body(*refs):
        d_refs, w_refs, o_ref = refs[:n], refs[n:2 * n], refs[2 * n]
        acc = None
        for d_ref, w_ref in zip(d_refs, w_refs):
            for k in range(kb):
                dk = d_ref[k] if stacked else d_ref[:, k * Wb:(k + 1) * Wb]
                t = _dot(dk, w_ref[k], NT)
                acc = t if acc is None else acc + t
        if nj == 1:
            o_ref[...] = acc.astype(BF16)
        else:
            acc_ref = refs[2 * n + 1]
            j = pl.program_id(1)

            @pl.when(j == 0)
            def _():
                acc_ref[...] = acc

            @pl.when((j > 0) & (j < nj - 1))
            def _():
                acc_ref[...] += acc

            @pl.when(j == nj - 1)
            def _():
                o_ref[...] = (acc_ref[...] + acc).astype(BF16)

    if stacked:
        d_spec = pl.BlockSpec((kb, tm, Wb), lambda i, j: (j, i, 0))
    else:
        d_spec = pl.BlockSpec((tm, kb * Wb), lambda i, j: (i, j))
    w_spec = pl.BlockSpec((kb, D, Wb), lambda i, j: (j, 0, 0), pipeline_mode=pl.Buffered(1) if nj == 1 else None)
    (dh,), got = _call(
        body, name=name, grid=(S // tm, nj),
        in_specs=[d_spec] * n + [w_spec] * n,
        out_specs=[pl.BlockSpec((tm, D), lambda i, j: (i, 0))], out_shape=[jax.ShapeDtypeStruct((S, D), BF16)],
        scratch=[] if nj == 1 else [pltpu.VMEM((tm, D), F32)], args=(*ds, *ws), vmem=56, comm=comm)
    return dh, got


def grad_lhs(ht, d, stacked, name, comm=None):
    D, S = ht.shape
    if stacked:
        J, _, Wb = d.shape
    else:
        J, Wb = NDEV, d.shape[1] // NDEV
    kb = 2 if not stacked and Wb % LANES == 0 and J % 2 == 0 else 1
    tk = _tile(S, 2048 // kb)
    nk = S // tk

    def body(ht_ref, d_ref, o_ref, acc_ref):
        i = pl.program_id(1)

        @pl.when(i == 0)
        def _():
            acc_ref[...] = jnp.zeros_like(acc_ref)

        acc_ref[...] += _dot(ht_ref[...], d_ref[...])

        @pl.when(i == nk - 1)
        def _():
            for k in range(kb):
                o_ref[k] = acc_ref[:, k * Wb:(k + 1) * Wb].astype(BF16)

    d_spec = pl.BlockSpec((None, tk, Wb), lambda j, i: (j, i, 0)) if stacked else pl.BlockSpec((tk, kb * Wb), lambda j, i: (i, j))
    (dw,), got = _call(
        body, name=name, grid=(J // kb, nk),
        in_specs=[pl.BlockSpec((D, tk), lambda j, i: (0, i)), d_spec],
        out_specs=[pl.BlockSpec((kb, D, Wb), lambda j, i: (j, 0, 0))],
        out_shape=[jax.ShapeDtypeStruct((J, D, Wb), BF16)],
        scratch=[pltpu.VMEM((D, kb * Wb), F32)], args=(ht, d), vmem=56, comm=comm)
    return dw, got


def grad_shared_rhs(a, gb, nblk, scale, name):
    S, D = gb.shape
    stacked = nblk is None
    if stacked:
        J, _, Wb = a.shape
    else:
        J, Wb = nblk, a.shape[1] // nblk
    tk = _tile(S, 2048 if Wb <= 768 else 1024)
    nk = S // tk

    def body(a_ref, g_ref, o_ref, acc_ref):
        i = pl.program_id(1)

        @pl.when(i == 0)
        def _():
            acc_ref[...] = jnp.zeros_like(acc_ref)

        acc_ref[...] += _dot(a_ref[...], g_ref[...], TN)

        @pl.when(i == nk - 1)
        def _():
            o_ref[...] = (acc_ref[...] * scale).astype(BF16)

    return pl.pallas_call(
        body, name=name, grid=(J, nk),
        in_specs=[_blk(a, Wb, tk, stacked), pl.BlockSpec((tk, D), lambda j, i: (i, 0))],
        out_specs=pl.BlockSpec((None, Wb, D), lambda j, i: (j, 0, 0)),
        out_shape=jax.ShapeDtypeStruct((J, Wb, D), BF16),
        scratch_shapes=[pltpu.VMEM((Wb, D), F32)], compiler_params=_params(56),
    )(a, gb)


def out_proj(ya, ycp, wo, x, next_gain, name):
    S, A = ya.shape
    Dm, D = wo.shape
    tm = _tile(S, 256)

    def body(ya_ref, ycp_ref, wo_ref, x_ref, gain_ref, o_ref, h_ref, ht_ref):
        xo = x_ref[...] + _dot(ya_ref[...], wo_ref[0:A, :]) + _dot(ycp_ref[...], wo_ref[A:Dm, :])
        o_ref[...] = xo
        _write_norm(xo, gain_ref, h_ref, ht_ref)

    row = pl.BlockSpec((tm, D), lambda i: (i, 0))
    g_in, n_specs, n_shapes = _norm_outs(S, D, tm, next_gain)
    return pl.pallas_call(
        body, name=name, grid=(S // tm,),
        in_specs=[pl.BlockSpec((tm, A), lambda i: (i, 0)), pl.BlockSpec((tm, Dm - A), lambda i: (i, 0)),
                  pl.BlockSpec((Dm, D), lambda i: (0, 0), pipeline_mode=pl.Buffered(1)), row] + g_in,
        out_specs=[row] + n_specs, out_shape=[jax.ShapeDtypeStruct((S, D), F32)] + n_shapes,
        compiler_params=_params(48),
    )(ya, ycp, wo, x, next_gain)


def out_proj_bwd(gb, wo, A, name):
    S, D = gb.shape
    Dm = wo.shape[0]
    tm = _tile(S, 512)

    def body(g_ref, wo_ref, dya_ref, dycp_ref):
        gv = g_ref[...]
        dya_ref[...] = _dot(gv, wo_ref[0:A, :], NT).astype(BF16)
        dycp_ref[...] = _dot(gv, wo_ref[A:Dm, :], NT).astype(BF16)

    return pl.pallas_call(
        body, name=name, grid=(S // tm,),
        in_specs=[pl.BlockSpec((tm, D), lambda i: (i, 0)), pl.BlockSpec((Dm, D), lambda i: (0, 0))],
        out_specs=[pl.BlockSpec((tm, A), lambda i: (i, 0)), pl.BlockSpec((tm, Dm - A), lambda i: (i, 0))],
        out_shape=[jax.ShapeDtypeStruct((S, A), BF16), jax.ShapeDtypeStruct((S, Dm - A), BF16)],
        compiler_params=_params(48),
    )(gb, wo)


ATT_TILE = BLK * max(DILATIONS)


def _head0(shape):
    return lax.broadcasted_iota(jnp.int32, shape, 1) < HEAD_DIM


def _pair_col(v, sel):
    return jnp.max(jnp.where(sel, v, -jnp.inf), axis=-1, keepdims=True)


def _stack_heads(x, h0):
    zero = jnp.zeros_like(x)
    return jnp.concatenate([jnp.where(h0, x, zero), jnp.where(h0, zero, x)], axis=0)


def _band(lo, hi):
    qi = lax.broadcasted_iota(jnp.int32, (BLK, 2 * BLK), 0)
    kj = lax.broadcasted_iota(jnp.int32, (BLK, 2 * BLK), 1)
    return (kj >= qi) & (kj <= qi + BLK) & (kj >= lo) & (kj < hi)


def _deinterleave(dst, src, d, rows, dst_stride, dst_off=0, src_off=0, cast=None):
    for r in range(d):
        v = src[pl.ds(src_off + r, rows, stride=d), :] if d > 1 else src[pl.ds(src_off, rows), :]
        dst[pl.ds(r * dst_stride + dst_off, rows), :] = v if cast is None else v.astype(cast)


def _interleave(dst, src, d, rows, src_stride, src_off=0, add=False):
    for r in range(d):
        v = src[pl.ds(r * src_stride + src_off, rows), :]
        idx = pl.ds(r, rows, stride=d) if d > 1 else pl.ds(0, rows)
        dst[idx, :] = dst[idx, :] + v if add else v


def attn_fwd(z, A, name, comm=None):
    S = z.shape[0]
    T = ATT_TILE
    Hp, nt = A // LANES, S // T
    np_ = len(DILATIONS)

    def body(q_ref, k_ref, kp_ref, v_ref, vp_ref, y_ref, lse_ref, qn, kn, vn, qp, kp, vp, accp, mp, lp, *nat):
        accs, ms, ls = nat[:np_], nat[np_:2 * np_], nat[2 * np_:]
        i = pl.program_id(1)
        qn[...] = q_ref[...].astype(F32)
        kn[0:T, :] = kp_ref[...].astype(F32)
        kn[T:2 * T, :] = k_ref[...].astype(F32)
        vn[0:T, :] = vp_ref[...].astype(F32)
        vn[T:2 * T, :] = v_ref[...].astype(F32)
        h0 = _head0((BLK, LANES))
        for pi, d in enumerate(DILATIONS):
            Ld = T // d
            nblk = Ld // BLK
            _deinterleave(qp, qn, d, Ld, Ld, cast=BF16)
            for src, dst in ((kn, kp), (vn, vp)):
                _deinterleave(dst, src, d, Ld, 2 * Ld, cast=BF16)
                _deinterleave(dst, src, d, Ld, 2 * Ld, dst_off=Ld, src_off=T, cast=BF16)

            def unit(u, carry, nblk=nblk):
                r, n = u // nblk, u % nblk
                q0 = pl.multiple_of(u * BLK, BLK)
                k0 = pl.multiple_of((2 * r * nblk + nblk + n - 1) * BLK, BLK)
                qb = qp[pl.ds(q0, BLK), :]
                kw, vw = kp[pl.ds(k0, 2 * BLK), :], vp[pl.ds(k0, 2 * BLK), :]
                mask = _band(jnp.where((i == 0) & (n == 0), BLK, 0), 2 * BLK)
                s = _dot(_stack_heads(qb, h0), kw, NT) * ATTN_SCALE
                s = jnp.where(jnp.concatenate([mask, mask], axis=0), s, NEG_INF)
                mx = jnp.max(s, axis=-1, keepdims=True)
                p = jnp.exp(s - mx)
                l = jnp.sum(p, axis=-1, keepdims=True)
                o = _dot(p.astype(BF16), vw)
                accp[pl.ds(q0, BLK), :] = jnp.where(h0, o[:BLK], o[BLK:])
                mp[pl.ds(q0, BLK), :] = jnp.where(h0, mx[:BLK], mx[BLK:])
                lp[pl.ds(q0, BLK), :] = jnp.where(h0, l[:BLK], l[BLK:])
                return carry

            lax.fori_loop(0, T // BLK, unit, 0, unroll=True)
            for src, dst in ((accp, accs[pi]), (mp, ms[pi]), (lp, ls[pi])):
                _interleave(dst, src, d, Ld, Ld)
        mv = [m[...] for m in ms]
        mx = mv[0]
        for m in mv[1:]:
            mx = jnp.maximum(mx, m)
        ws = [jnp.exp(m - mx) for m in mv]
        l = sum(w * lr[...] for w, lr in zip(ws, ls))
        a = sum(w * ar[...] for w, ar in zip(ws, accs))
        y_ref[...] = (a / l).astype(BF16)
        lse_ref[...] = mx + jnp.log(l)

    kb, vb = A // LANES, 2 * A // LANES
    cur = lambda off: pl.BlockSpec((T, LANES), lambda h, i: (i, off + h))
    prev = lambda off: pl.BlockSpec((T, LANES), lambda h, i: (jnp.maximum(i - 1, 0), off + h))
    out = pl.BlockSpec((T, LANES), lambda h, i: (i, h))
    vm = lambda rows, dt: pltpu.VMEM((rows, LANES), dt)
    (y, lse), got = _call(
        body, name=name, grid=(Hp, nt),
        in_specs=[cur(0), cur(kb), prev(kb), cur(vb), prev(vb)], out_specs=[out, out],
        out_shape=[jax.ShapeDtypeStruct((S, A), BF16), jax.ShapeDtypeStruct((S, A), F32)],
        scratch=[vm(T, F32), vm(2 * T, F32), vm(2 * T, F32), vm(T, BF16), vm(2 * T, BF16), vm(2 * T, BF16),
                 vm(T, F32), vm(T, F32), vm(T, F32)] + [vm(T, F32)] * (3 * np_),
        args=(z, z, z, z, z), vmem=48, comm=comm)
    return y, lse, got


def attn_bwd(z, dy, y, lse, A, name, comm=None):
    S = z.shape[0]
    T = ATT_TILE
    Hp, nt = A // LANES, S // T

    def body(q_ref, k_ref, kp_ref, v_ref, vp_ref, do_ref, y_ref, ls_ref, dq_ref, dk_ref, dv_ref,
             qn, don, dln, kn, vn, qp, dop, lsp, dlp, kp, vp, dqp, dkp, dvp, dqa, dka, dva, dkc, dvc):
        step = pl.program_id(1)
        i = nt - 1 - step
        h0t = _head0((T, LANES))
        dof = do_ref[...].astype(F32)
        prod = dof * y_ref[...].astype(F32)
        d0 = jnp.sum(jnp.where(h0t, prod, 0.0), axis=-1, keepdims=True)
        d1 = jnp.sum(jnp.where(h0t, 0.0, prod), axis=-1, keepdims=True)
        qn[...] = q_ref[...].astype(F32)
        don[...] = dof
        dln[...] = jnp.where(h0t, d0, d1)
        kn[0:T, :] = kp_ref[...].astype(F32)
        kn[T:2 * T, :] = k_ref[...].astype(F32)
        vn[0:T, :] = vp_ref[...].astype(F32)
        vn[T:2 * T, :] = v_ref[...].astype(F32)

        @pl.when(step == 0)
        def _():
            dka[...] = jnp.zeros_like(dka)
            dva[...] = jnp.zeros_like(dva)

        @pl.when(step > 0)
        def _():
            dka[...] = dkc[...]
            dva[...] = dvc[...]

        h0 = _head0((BLK, LANES))
        for pi, d in enumerate(DILATIONS):
            Ld = T // d
            nblk = Ld // BLK
            for src, dst, cast in ((qn, qp, BF16), (don, dop, BF16), (dln, dlp, None)):
                _deinterleave(dst, src, d, Ld, Ld, cast=cast)
            _deinterleave(lsp, ls_ref, d, Ld, Ld)
            for src, dst in ((kn, kp), (vn, vp)):
                _deinterleave(dst, src, d, Ld, 2 * Ld, cast=BF16)
                _deinterleave(dst, src, d, Ld, 2 * Ld, dst_off=Ld, src_off=T, cast=BF16)
            dkp[...] = jnp.zeros_like(dkp)
            dvp[...] = jnp.zeros_like(dvp)

            def unit(u, carry, nblk=nblk):
                r, b = u // nblk, u % nblk
                q0 = pl.multiple_of(u * BLK, BLK)
                k0 = pl.multiple_of((2 * r * nblk + nblk + b - 1) * BLK, BLK)
                mask = _band(jnp.where((i == 0) & (b == 0), BLK, 0), 2 * BLK)
                qb, dob = qp[pl.ds(q0, BLK), :], dop[pl.ds(q0, BLK), :]
                lsb, dlb = lsp[pl.ds(q0, BLK), :], dlp[pl.ds(q0, BLK), :]
                kw, vw = kp[pl.ds(k0, 2 * BLK), :], vp[pl.ds(k0, 2 * BLK), :]
                qs, dos = _stack_heads(qb, h0), _stack_heads(dob, h0)
                lse = jnp.concatenate([_pair_col(lsb, h0), _pair_col(lsb, ~h0)], axis=0)
                delta = jnp.concatenate([_pair_col(dlb, h0), _pair_col(dlb, ~h0)], axis=0)
                dq = None
                for half in range(2):
                    cols = slice(half * BLK, (half + 1) * BLK)
                    kh, vh = kw[cols, :], vw[cols, :]
                    mh = mask[:, cols]
                    s = _dot(qs, kh, NT) * ATTN_SCALE
                    p = jnp.where(jnp.concatenate([mh, mh], axis=0), jnp.exp(s - lse), 0.0)
                    ds = (p * (_dot(dos, vh, NT) - delta)).astype(BF16)
                    dqh = _dot(ds, kh)
                    dq = dqh if dq is None else dq + dqh
                    kr = pl.ds(k0 + half * BLK, BLK)
                    dkp[kr, :] += _dot(ds, qs, TN) * ATTN_SCALE
                    dvp[kr, :] += _dot(p.astype(BF16), dos, TN)
                dq = dq * ATTN_SCALE
                dqp[pl.ds(q0, BLK), :] = jnp.where(h0, dq[:BLK], dq[BLK:])
                return carry

            lax.fori_loop(0, T // BLK, unit, 0, unroll=True)
            _interleave(dqa, dqp, d, Ld, Ld, add=pi > 0)
            for acc, nxt, part in ((dka, dkc, dkp), (dva, dvc, dvp)):
                _interleave(acc, part, d, Ld, 2 * Ld, src_off=Ld, add=True)
                _interleave(nxt, part, d, Ld, 2 * Ld, add=pi > 0)
        dq_ref[...] = dqa[...].astype(BF16)
        dk_ref[...] = dka[...].astype(BF16)
        dv_ref[...] = dva[...].astype(BF16)

    kb, vb = A // LANES, 2 * A // LANES
    cur = lambda off: pl.BlockSpec((T, LANES), lambda h, s: (nt - 1 - s, off + h))
    prev = lambda off: pl.BlockSpec((T, LANES), lambda h, s: (jnp.maximum(nt - 2 - s, 0), off + h))
    vm = lambda rows, dt: pltpu.VMEM((rows, LANES), dt)
    shp = jax.ShapeDtypeStruct((S, A), BF16)
    return _call(
        body, name=name, grid=(Hp, nt),
        in_specs=[cur(0), cur(kb), prev(kb), cur(vb), prev(vb), cur(0), cur(0), cur(0)],
        out_specs=[cur(0)] * 3, out_shape=[shp] * 3,
        scratch=[vm(T, F32), vm(T, F32), vm(T, F32), vm(2 * T, F32), vm(2 * T, F32),
                 vm(T, BF16), vm(T, BF16), vm(T, F32), vm(T, F32), vm(2 * T, BF16), vm(2 * T, BF16),
                 vm(T, F32), vm(2 * T, F32), vm(2 * T, F32)] + [vm(T, F32)] * 5,
        args=(z, z, z, z, z, dy, y, lse), vmem=56, comm=comm)


def _shift_down(e, k):
    return pltpu.roll(e, k, 0)


def _shift_up(e, k):
    return pltpu.roll(e, e.shape[0] - k, 0)


def _causal_sums(e, g):
    for lvl in range(g + 1):
        e = e + _shift_down(e, 1 << lvl)
    return e


def _anticausal_sums(e, g):
    for lvl in range(g + 1):
        e = e + _shift_up(e, 1 << lvl)
    return e


def _mixer_specs(S, tm, A, C, P):
    assert (3 * A) % C == 0 and (3 * A + 3 * C) % P == 0 and tm % HALO == 0
    cb, pb = 3 * A // C, (3 * A + 3 * C) // P
    hb = tm // HALO
    last = S // HALO - 1
    col = lambda w, j: pl.BlockSpec((tm, w), lambda i: (i, j))
    prev = lambda w, j: pl.BlockSpec((HALO, w), lambda i: (jnp.maximum(i * hb - 1, 0), j))
    nxt = lambda w, j: pl.BlockSpec((HALO, w), lambda i: (jnp.minimum((i + 1) * hb, last), j))
    return cb, pb, col, prev, nxt


def _conv_taps(gc, ci, gch, cih, keep_prev):
    u = gc * ci
    e = jnp.concatenate([gch * cih * keep_prev, u], axis=0)
    return u, _shift_down(e, 1)[HALO:], _shift_down(e, 2)[HALO:]


def _pooled(xp, xph, keep_prev, pos, g):
    cols = slice(g * POOL_GROUP, (g + 1) * POOL_GROUP)
    x = xp[:, cols]
    e = jnp.concatenate([xph[:, cols] * keep_prev, x], axis=0)
    cnt = jnp.minimum(pos + 1, POOL_WINDOWS[g]).astype(F32)
    return _causal_sums(e, g)[HALO:] / cnt - x


def convpool_fwd(z, conv_w, pool_w, pool_scale, A, C, P, name):
    S = z.shape[0]
    tm = _tile(S, 512)
    cb, pb, col, prev, _ = _mixer_specs(S, tm, A, C, P)

    def body(gb_ref, gc_ref, ci_ref, xp_ref, gch_ref, cih_ref, xph_ref, cw_ref, pw_ref, ps_ref, y_ref):
        i = pl.program_id(0)
        keep_prev = jnp.where(i == 0, 0.0, 1.0)
        f = lambda r: r[...].astype(F32)
        u, u1, u2 = _conv_taps(f(gc_ref), f(ci_ref), f(gch_ref), f(cih_ref), keep_prev)
        cw = cw_ref[...]
        y_ref[:, 0:C] = (f(gb_ref) * (cw[0:1] * u2 + cw[1:2] * u1 + cw[2:3] * u)).astype(BF16)
        xp, xph = f(xp_ref), f(xph_ref)
        pos = i * tm + lax.broadcasted_iota(jnp.int32, (tm, 1), 0)
        for g in range(len(POOL_WINDOWS)):
            cols = slice(g * POOL_GROUP, (g + 1) * POOL_GROUP)
            lin = _dot(_pooled(xp, xph, keep_prev, pos, g).astype(BF16), pw_ref[g])
            y_ref[:, C + g * POOL_GROUP:C + (g + 1) * POOL_GROUP] = (lin * ps_ref[:, cols]).astype(BF16)

    full = lambda shape: pl.BlockSpec(shape, lambda i: (0,) * len(shape))
    return pl.pallas_call(
        body, name=name, grid=(S // tm,),
        in_specs=[col(C, cb), col(C, cb + 1), col(C, cb + 2), col(P, pb),
                  prev(C, cb + 1), prev(C, cb + 2), prev(P, pb),
                  full(conv_w.shape), full(pool_w.shape), full(pool_scale.shape)],
        out_specs=pl.BlockSpec((tm, C + P), lambda i: (i, 0)),
        out_shape=jax.ShapeDtypeStruct((S, C + P), BF16), compiler_params=_params(48),
    )(z, z, z, z, z, z, z, conv_w, pool_w, pool_scale)


def convpool_bwd(z, dycp, conv_w, pool_w, pool_scale, A, C, P, name):
    S = z.shape[0]
    tm = _tile(S, 512)
    nt = S // tm
    cb, pb, col, prev, nxt = _mixer_specs(S, tm, A, C, P)
    NG = len(POOL_WINDOWS)

    def body(gb_ref, gc_ref, ci_ref, xp_ref, gch_ref, cih_ref, xph_ref, gbn_ref, dy_ref, dyn_ref,
             cw_ref, pw_ref, ps_ref, dz_ref, dcw_ref, dpw_ref, dps_ref):
        i = pl.program_id(0)

        @pl.when(i == 0)
        def _():
            dcw_ref[...] = jnp.zeros_like(dcw_ref)
            dpw_ref[...] = jnp.zeros_like(dpw_ref)
            dps_ref[...] = jnp.zeros_like(dps_ref)

        keep_prev = jnp.where(i == 0, 0.0, 1.0)
        keep_next = jnp.where(i == nt - 1, 0.0, 1.0)
        f = lambda r: r[...].astype(F32)
        gb, gc, ci = f(gb_ref), f(gc_ref), f(ci_ref)
        u, u1, u2 = _conv_taps(gc, ci, f(gch_ref), f(cih_ref), keep_prev)
        cw = cw_ref[...]
        dy, dyn = f(dy_ref), f(dyn_ref) * keep_next
        dyc = dy[:, 0:C]
        dz_ref[:, 0:C] = (dyc * (cw[0:1] * u2 + cw[1:2] * u1 + cw[2:3] * u)).astype(BF16)
        dc = dyc * gb
        e = jnp.concatenate([dc, dyn[:, 0:C] * f(gbn_ref)], axis=0)
        du = cw[2:3] * dc + cw[1:2] * _shift_up(e, 1)[:tm] + cw[0:1] * _shift_up(e, 2)[:tm]
        dz_ref[:, C:2 * C] = (du * ci).astype(BF16)
        dz_ref[:, 2 * C:3 * C] = (du * gc).astype(BF16)
        dcw_ref[0:1, :] += jnp.sum(dc * u2, axis=0, keepdims=True)
        dcw_ref[1:2, :] += jnp.sum(dc * u1, axis=0, keepdims=True)
        dcw_ref[2:3, :] += jnp.sum(dc * u, axis=0, keepdims=True)

        xp, xph = f(xp_ref), f(xph_ref)
        pos = i * tm + lax.broadcasted_iota(jnp.int32, (tm, 1), 0)
        pos_e = i * tm + lax.broadcasted_iota(jnp.int32, (tm + HALO, 1), 0)
        for g in range(NG):
            cols = slice(g * POOL_GROUP, (g + 1) * POOL_GROUP)
            ycols = slice(C + g * POOL_GROUP, C + (g + 1) * POOL_GROUP)
            pooled = _pooled(xp, xph, keep_prev, pos, g).astype(BF16)
            pw = pw_ref[g]
            dyp = dy[:, ycols]
            dps_ref[:, cols] += jnp.sum(dyp * _dot(pooled, pw), axis=0, keepdims=True)
            dlin = (jnp.concatenate([dyp, dyn[:, ycols]], axis=0) * ps_ref[:, cols]).astype(BF16)
            dpw_ref[g] += _dot(pooled, dlin[:tm], TN)
            dpool = _dot(dlin, pw, NT)
            r = dpool / jnp.minimum(pos_e + 1, POOL_WINDOWS[g]).astype(F32)
            dz_ref[:, 3 * C + g * POOL_GROUP:3 * C + (g + 1) * POOL_GROUP] = (
                _anticausal_sums(r, g)[:tm] - dpool[:tm]).astype(BF16)

    full = lambda shape: pl.BlockSpec(shape, lambda i: (0,) * len(shape))
    return pl.pallas_call(
        body, name=name, grid=(nt,),
        in_specs=[col(C, cb), col(C, cb + 1), col(C, cb + 2), col(P, pb),
                  prev(C, cb + 1), prev(C, cb + 2), prev(P, pb), nxt(C, cb),
                  pl.BlockSpec((tm, C + P), lambda i: (i, 0)),
                  pl.BlockSpec((HALO, C + P), lambda i: (jnp.minimum((i + 1) * (tm // HALO), S // HALO - 1), 0)),
                  full(conv_w.shape), full(pool_w.shape), full(pool_scale.shape)],
        out_specs=[pl.BlockSpec((tm, 3 * C + P), lambda i: (i, 0)),
                   full(conv_w.shape), full(pool_w.shape), full(pool_scale.shape)],
        out_shape=[jax.ShapeDtypeStruct((S, 3 * C + P), BF16), jax.ShapeDtypeStruct(conv_w.shape, F32),
                   jax.ShapeDtypeStruct(pool_w.shape, F32), jax.ShapeDtypeStruct(pool_scale.shape, F32)],
        compiler_params=_params(48),
    )(z, z, z, z, z, z, z, z, dycp, dycp, conv_w, pool_w, pool_scale)


def adamw(lands, w, m, v, name):
    R, C = w.shape
    nl = len(lands)
    n, Rl, _ = lands[0].shape
    assert Rl * nl == R
    tr = _tile(Rl, 128 * 1024 // C)
    nb = Rl // tr
    c1 = 1.0 - ADAM_B1 ** ADAM_STEP
    c2 = 1.0 - ADAM_B2 ** ADAM_STEP

    def body(*refs):
        land_refs = refs[:nl]
        w_ref, m_ref, v_ref, g_ref, d_ref, mo_ref, vo_ref = refs[nl:]
        i = pl.program_id(0)
        for a, land_ref in enumerate(land_refs):
            @pl.when((i >= a * nb) & (i < (a + 1) * nb))
            def _():
                g = land_ref[0].astype(F32)
                for s in range(1, n):
                    g = g + land_ref[s].astype(F32)
                g_ref[...] = g
                mn = ADAM_B1 * m_ref[...] + (1.0 - ADAM_B1) * g
                vn = ADAM_B2 * v_ref[...] + (1.0 - ADAM_B2) * (g * g)
                mo_ref[...] = mn
                vo_ref[...] = vn
                d_ref[...] = -ADAM_LR * ((mn / c1) / (jnp.sqrt(vn / c2) + ADAM_EPS) + ADAM_WD * w_ref[...])

    land_specs = [pl.BlockSpec((n, tr, C), lambda i, a=a: (0, jnp.clip(i - a * nb, 0, nb - 1), 0))
                  for a in range(nl)]
    row = pl.BlockSpec((tr, C), lambda i: (i, 0))
    shp = jax.ShapeDtypeStruct((R, C), F32)
    return pl.pallas_call(
        body, name=name, grid=(nl * nb,),
        in_specs=land_specs + [row, row, row], out_specs=[row] * 4, out_shape=[shp] * 4,
        compiler_params=_params(48),
    )(*lands, w, m, v)


def _gather_comm(arrs):
    return ("gather", arrs) if arrs else None


def _ffn_bwd(g, gb, saved, gain, wg, wu, wd, tag, defer_wu=False):
    x, ht, a, b, hid = saved
    da, db = ffn_dhid(gb, wd, a, b, f"ffn_dhid_{tag}")
    dwd = grad_shared_rhs(hid, gb, None, FFN_RESIDUAL, f"ffn_dwd_{tag}")
    dwg, (land_wd,) = grad_lhs(ht, da, True, f"ffn_dwg_{tag}", ("exchange", [dwd]))
    if defer_wu:
        land_wu, _ = grad_lhs(ht, db, True, f"ffn_dwu_{tag}")
        dh, (land_wg,) = back_proj([da, db], [wg, wu], True, f"ffn_dh_{tag}", ("exchange", [dwg]))
    else:
        dwu, (land_wg,) = grad_lhs(ht, db, True, f"ffn_dwu_{tag}", ("exchange", [dwg]))
        dh, (land_wu,) = back_proj([da, db], [wg, wu], True, f"ffn_dh_{tag}", ("exchange", [dwu]))
    g, gb, dgain = rms_bwd(x, gain, dh, g, f"rms_bwd_{tag}")
    return g, gb, dgain, (land_wg, land_wu, land_wd)


def _mixer_fwd(x, h, ht, next_gain, w_in, conv_w, pool_w, pool_scale, w_out, dims, tag, carry):
    A, C, P = dims
    z, got_a = proj_cols(h, w_in, f"in_proj_{tag}", _gather_comm(carry[:1]))
    ya, lse, got_b = attn_fwd(z, A, f"attn_fwd_{tag}", _gather_comm(carry[1:]))
    ycp = convpool_fwd(z, conv_w, pool_w, pool_scale, A, C, P, f"convpool_{tag}")
    xo, hn, htn = out_proj(ya, ycp, w_out, x, next_gain, f"out_proj_{tag}")
    return xo, hn, htn, (x, ht, z, lse, ya, ycp), got_a + got_b


def _mixer_bwd(g, gb, saved, gain, w_in, conv_w, pool_w, pool_scale, w_out, dims, tag, also_send):
    A, C, P = dims
    x, ht, z, lse, ya, ycp = saved
    S, D = x.shape
    dya, dycp = out_proj_bwd(gb, w_out, A, f"out_proj_bwd_{tag}")
    dwo_a = grad_shared_rhs(ya, gb, 1, 1.0, f"dwo_attn_{tag}")
    dwo_cp = grad_shared_rhs(ycp, gb, 1, 1.0, f"dwo_cp_{tag}")
    dwo = jnp.concatenate([dwo_a[0], dwo_cp[0]], axis=0).reshape(NDEV, -1, D)
    dqkv, (land_wo, land_also) = attn_bwd(z, dya, ya, lse, A, f"attn_bwd_{tag}", ("exchange", [dwo, also_send]))
    dz_cp, dcw, dpw, dps = convpool_bwd(z, dycp, conv_w, pool_w, pool_scale, A, C, P, f"convpool_bwd_{tag}")
    dz = jnp.concatenate(dqkv + [dz_cp], axis=1)
    dwin, _ = grad_lhs(ht, dz, False, f"dwin_{tag}")
    dh, (land_win,) = back_proj([dz], [w_in], False, f"in_proj_bwd_{tag}", ("exchange", [dwin]))
    g, gb, dgain = rms_bwd(x, gain, dh, g, f"rms_bwd_{tag}")
    return g, gb, (dgain, dcw, dpw, dps), (land_win, land_wo, land_also)


def _pack(arrs):
    flat = [a.reshape(-1).astype(F32) for a in arrs]
    spans, off = [], 0
    for a in flat:
        spans.append((off, a.shape[0]))
        off += a.shape[0]
    rows = -(-off // (8 * LANES)) * 8
    buf = jnp.concatenate(flat + [jnp.zeros((rows * LANES - off,), F32)]).reshape(rows, LANES)
    return buf, spans


def kernel(x, ffn1_norm, ffn1_w_gate, ffn1_w_up, ffn1_w_down, mix_norm, w_in, conv_w, pool_w, pool_scale, w_out, ffn2_norm, ffn2_w_gate, ffn2_w_up, ffn2_w_down, final_norm, loss_target, m_ffn1_norm, m_ffn1_w_gate, m_ffn1_w_up, m_ffn1_w_down, m_mix_norm, m_w_in, m_conv_w, m_pool_w, m_pool_scale, m_w_out, m_ffn2_norm, m_ffn2_w_gate, m_ffn2_w_up, m_ffn2_w_down, m_final_norm, v_ffn1_norm, v_ffn1_w_gate, v_ffn1_w_up, v_ffn1_w_down, v_mix_norm, v_w_in, v_conv_w, v_pool_w, v_pool_scale, v_w_out, v_ffn2_norm, v_ffn2_w_gate, v_ffn2_w_up, v_ffn2_w_down, v_final_norm):
    depth = ffn1_norm.shape[0]
    S, D = x.shape[1], x.shape[2]
    Cb = conv_w.shape[2]
    C = Cb * NDEV
    P = pool_scale.shape[1]
    A = (w_in.shape[2] * NDEV - 3 * C - P) // 3
    dims = (A, C, P)
    me = 4 * lax.axis_index("x") + 2 * lax.axis_index("y") + lax.axis_index("c")

    big = dict(ffn1_w_gate=ffn1_w_gate, ffn1_w_up=ffn1_w_up, ffn1_w_down=ffn1_w_down, w_in=w_in, w_out=w_out,
               ffn2_w_gate=ffn2_w_gate, ffn2_w_up=ffn2_w_up, ffn2_w_down=ffn2_w_down)
    big_m = dict(ffn1_w_gate=m_ffn1_w_gate, ffn1_w_up=m_ffn1_w_up, ffn1_w_down=m_ffn1_w_down, w_in=m_w_in,
                 w_out=m_w_out, ffn2_w_gate=m_ffn2_w_gate, ffn2_w_up=m_ffn2_w_up, ffn2_w_down=m_ffn2_w_down)
    big_v = dict(ffn1_w_gate=v_ffn1_w_gate, ffn1_w_up=v_ffn1_w_up, ffn1_w_down=v_ffn1_w_down, w_in=v_w_in,
                 w_out=v_w_out, ffn2_w_gate=v_ffn2_w_gate, ffn2_w_up=v_ffn2_w_up, ffn2_w_down=v_ffn2_w_down)
    names = list(big)

    first_names = ["ffn1_w_gate", "ffn1_w_up"]
    mix_names = ["w_in", "w_out"]
    ffn2_names = ["ffn2_w_gate", "ffn2_w_up", "ffn2_w_down"]
    shards = lambda l, ns: [big[n][l].astype(BF16) for n in ns]
    full = [dict() for _ in range(depth)]
    conv_full = all_gather([jnp.pad(conv_w.reshape(-1, Cb), ((0, 2), (0, LANES - Cb)))], "gather_conv_w")[0]
    conv_full = jnp.transpose(conv_full[:, :depth * 3, :Cb].reshape(NDEV, depth, 3, Cb), (1, 2, 0, 3)).reshape(depth, 3, C)
    pool_w_bf = pool_w.astype(BF16)

    xs = x[0]
    saved = []
    (h, ht), got = rms_fwd(xs, ffn1_norm[0:1], "rms_f1_l0", ("gather", shards(0, first_names)))
    full[0].update(zip(first_names, got))
    for l in range(depth):
        W = full[l]
        last = l + 1 == depth
        up1 = ["ffn1_w_down"] + (mix_names if l == 0 else [])
        (a, b, hid), got = ffn_up(h, W["ffn1_w_gate"], W["ffn1_w_up"], f"ffn_up_f1_l{l}", _gather_comm(shards(l, up1)))
        W.update(zip(up1, got))
        s1 = (xs, ht, a, b, hid)
        (xs, h, ht), _ = ffn_down(hid, W["ffn1_w_down"], xs, f"ffn_down_f1_l{l}", next_gain=mix_norm[l:l + 1])
        xs, h, ht, s2, got = _mixer_fwd(xs, h, ht, ffn2_norm[l:l + 1], W["w_in"], conv_full[l], pool_w_bf[l],
                                        pool_scale[l:l + 1], W["w_out"].reshape(-1, D), dims, f"mix_l{l}",
                                        shards(l, ffn2_names[:2]))
        W.update(zip(ffn2_names[:2], got))
        got = ffn_up(h, W["ffn2_w_gate"], W["ffn2_w_up"], f"ffn_up_f2_l{l}",
                     _gather_comm(shards(l, ffn2_names[2:]) + ([] if last else shards(l + 1, first_names))))
        (a, b, hid), (W["ffn2_w_down"], *got_next) = got
        s3 = (xs, ht, a, b, hid)
        outs, got_down = ffn_down(hid, W["ffn2_w_down"], xs, f"ffn_down_f2_l{l}",
                                  _gather_comm([] if last else shards(l + 1, mix_names)),
                                  next_gain=None if last else ffn1_norm[l + 1:l + 2])
        if last:
            xs, = outs
        else:
            xs, h, ht = outs
            full[l + 1].update(zip(first_names, got_next))
            full[l + 1].update(zip(mix_names, got_down))
        saved.append((s1, s2, s3))
    loss_part, g, gb, d_final = final_loss(xs, final_norm.reshape(1, D), loss_target[0], "final_loss")

    small = [None] * depth
    lands = [None] * depth
    for l in reversed(range(depth)):
        W = full[l]
        s1, s2, s3 = saved[l]
        g, gb, dn2, (lg2, dwu2, ld2) = _ffn_bwd(g, gb, s3, ffn2_norm[l:l + 1], W["ffn2_w_gate"], W["ffn2_w_up"],
                                                W["ffn2_w_down"], f"f2_l{l}", defer_wu=True)
        g, gb, (dnm, dcw, dpw, dps), (lwin, lwo, lu2) = _mixer_bwd(
            g, gb, s2, mix_norm[l:l + 1], W["w_in"], conv_full[l], pool_w_bf[l], pool_scale[l:l + 1],
            W["w_out"].reshape(-1, D), dims, f"mix_l{l}", dwu2)
        g, gb, dn1, (lg1, lu1, ld1) = _ffn_bwd(g, gb, s1, ffn1_norm[l:l + 1], W["ffn1_w_gate"], W["ffn1_w_up"],
                                               W["ffn1_w_down"], f"f1_l{l}")
        lands[l] = dict(ffn1_w_gate=lg1, ffn1_w_up=lu1, ffn1_w_down=ld1, w_in=lwin, w_out=lwo,
                        ffn2_w_gate=lg2, ffn2_w_up=lu2, ffn2_w_down=ld2)
        small[l] = (dn1, dnm, dcw, dpw, dps, dn2)

    res = {}
    for n in names:
        shp = big[n].shape
        two = lambda t: t.reshape(-1, shp[-1])
        outs = adamw([lands[l][n] for l in range(depth)], two(big[n]), two(big_m[n]), two(big_v[n]), f"adamw_{n}")
        res[n] = [o.reshape(shp) for o in outs]

    st = lambda i: jnp.stack([small[l][i] for l in range(depth)])
    small_g = dict(ffn1_norm=st(0), mix_norm=st(1), conv_w=st(2), pool_w=st(3), pool_scale=st(4), ffn2_norm=st(5),
                   final_norm=d_final)
    small_names = list(small_g)
    zeros_conv = jnp.zeros((depth, 3, C), F32)
    place = lambda t: lax.dynamic_update_slice(zeros_conv, t, (0, 0, me * Cb))
    small_w = dict(ffn1_norm=ffn1_norm, mix_norm=mix_norm, conv_w=place(conv_w), pool_w=pool_w, pool_scale=pool_scale,
                   ffn2_norm=ffn2_norm, final_norm=final_norm)
    small_m = dict(ffn1_norm=m_ffn1_norm, mix_norm=m_mix_norm, conv_w=place(m_conv_w), pool_w=m_pool_w,
                   pool_scale=m_pool_scale, ffn2_norm=m_ffn2_norm, final_norm=m_final_norm)
    small_v = dict(ffn1_norm=v_ffn1_norm, mix_norm=v_mix_norm, conv_w=place(v_conv_w), pool_w=v_pool_w,
                   pool_scale=v_pool_scale, ffn2_norm=v_ffn2_norm, final_norm=v_final_norm)
    gbuf, spans = _pack([small_g[n] for n in small_names] + [loss_part])
    wbuf, _ = _pack([small_w[n] for n in small_names] + [jnp.zeros((1, LANES), F32)])
    mbuf, _ = _pack([small_m[n] for n in small_names] + [jnp.zeros((1, LANES), F32)])
    vbuf, _ = _pack([small_v[n] for n in small_names] + [jnp.zeros((1, LANES), F32)])
    gathered = all_gather([gbuf], "gather_small_grads")[0]
    outs = adamw([gathered], wbuf, mbuf, vbuf, "adamw_small")
    for n, (off, size) in zip(small_names, spans):
        shp = small_w[n].shape
        vals = [o.reshape(-1)[off:off + size].reshape(shp) for o in outs]
        if n == "conv_w":
            vals = [lax.dynamic_slice(t, (0, 0, me * Cb), (depth, 3, Cb)) for t in vals]
        res[n] = vals
    loss = outs[0].reshape(-1)[spans[-1][0]]

    order = ["ffn1_norm", "ffn1_w_gate", "ffn1_w_up", "ffn1_w_down", "mix_norm", "w_in", "conv_w", "pool_w",
             "pool_scale", "w_out", "ffn2_norm", "ffn2_w_gate", "ffn2_w_up", "ffn2_w_down", "final_norm"]
    return (loss, g[None], *[res[n][0] for n in order], *[res[n][1] for n in order],
            *[res[n][2] for n in order], *[res[n][3] for n in order])
```
